```python
import math
import jax, jax.numpy as jnp
from jax import lax
import numpy as np

D_MODEL = 1024
BATCH = 8
SEQ = 8192
DEPTH = 4

POOL_WINDOWS = (2, 4, 8, 16)
POOL_GROUP = D_MODEL // 8
POOL_WIDTH = POOL_GROUP * len(POOL_WINDOWS)
SSM_D_INNER = D_MODEL
SSM_HEAD_DIM = 64
SSM_HEADS = SSM_D_INNER // SSM_HEAD_DIM
SSM_GROUPS = 2
SSM_STATE = 128
SSM_CONV = 4
SSM_CHUNK = 128
SSM_CONV_DIM = SSM_D_INNER + 2 * SSM_GROUPS * SSM_STATE
EVEN_IN = POOL_WIDTH + SSM_D_INNER + SSM_CONV_DIM + SSM_HEADS
EVEN_MIX = POOL_WIDTH + SSM_D_INNER
FOX_HEADS = 8
FOX_HEAD_DIM = 64
FOX_WIDTH = FOX_HEADS * FOX_HEAD_DIM
MLA_HEADS = 8
MLA_NOPE = 64
MLA_ROPE = 32
MLA_V = 64
MLA_Q_RANK = 512
MLA_KV_RANK = 256
ROPE_THETA = 10000.0
ODD_IN = 3 * FOX_WIDTH + FOX_HEADS + MLA_Q_RANK + MLA_KV_RANK + MLA_ROPE
ODD_MIX = FOX_WIDTH + MLA_HEADS * MLA_V
ATTN_BLOCK = 128
D_FF = -(-8 * D_MODEL // (3 * 256)) * 256
DEEPNORM_ALPHA = (2 * DEPTH) ** 0.25
DEEPNORM_BETA = (8 * DEPTH) ** -0.25
N_EVEN = (DEPTH + 1) // 2
N_ODD = DEPTH // 2
LN_EPS = 1e-5
RMS_EPS = 1e-6

kernel_name = "hybrid_pool_ssd_fox_mla_deepnorm"

F32 = jnp.float32


def layer_norm(x, g, b):
    xf = x.astype(F32)
    mu = jnp.mean(xf, axis=-1, keepdims=True)
    var = jnp.mean(jnp.square(xf - mu), axis=-1, keepdims=True)
    return ((xf - mu) * lax.rsqrt(var + LN_EPS) * g + b).astype(x.dtype)


def rms_norm(x, g):
    xf = x.astype(F32)
    return (xf * lax.rsqrt(jnp.mean(jnp.square(xf), axis=-1, keepdims=True) + RMS_EPS) * g).astype(x.dtype)


def rope(x, pos):
    half = x.shape[-1] // 2
    freqs = jnp.power(ROPE_THETA, -jnp.arange(half, dtype=F32) / half)
    ang = pos.astype(F32)[:, None] * freqs[None, :]
    cos, sin = jnp.cos(ang), jnp.sin(ang)
    x1 = x[..., :half].astype(F32)
    x2 = x[..., half:].astype(F32)
    return jnp.concatenate([x1 * cos - x2 * sin, x2 * cos + x1 * sin], axis=-1).astype(x.dtype)


def multiscale_pool(u, pool_w, pool_scale):
    b, s, _ = u.shape
    ng = len(POOL_WINDOWS)
    ug = u.reshape(b, s, ng, POOL_GROUP).astype(F32)
    cs = jnp.cumsum(ug, axis=1)
    pos = jnp.arange(s)
    means = []
    for g, w in enumerate(POOL_WINDOWS):
        c = cs[:, :, g]
        lagged = jnp.pad(c, ((0, 0), (w, 0), (0, 0)))[:, :s]
        count = jnp.minimum(pos + 1, w).astype(F32)[None, :, None]
        means.append((c - lagged) / count)
    diff = (jnp.stack(means, axis=2) - ug).astype(u.dtype)
    y = jnp.einsum('bsgc,gcd->bsgd', diff, pool_w).reshape(b, s, POOL_WIDTH)
    return y * pool_scale


def causal_dwconv(u, w, bias):
    c = u.shape[-1]
    out = lax.conv_general_dilated(
        u, w[:, None, :].astype(u.dtype), window_strides=(1,),
        padding=[(SSM_CONV - 1, 0)], dimension_numbers=('NWC', 'WIO', 'NWC'),
        feature_group_count=c)
    return out + bias


def mamba2_ssd(z, xbc, dt, conv_w, conv_b, dt_bias, a_log, d_skip, norm_w):
    b, s, _ = z.shape
    G, E, P, N, L = SSM_GROUPS, SSM_HEADS // SSM_GROUPS, SSM_HEAD_DIM, SSM_STATE, SSM_CHUNK
    nc = s // L
    xbc = jax.nn.silu(causal_dwconv(xbc, conv_w, conv_b))
    xs, bm, cm = jnp.split(xbc, [SSM_D_INNER, SSM_D_INNER + G * N], axis=-1)
    dt = jax.nn.softplus(dt.astype(F32) + dt_bias)
    a = -jnp.exp(a_log.astype(F32))
    xh = xs.reshape(b, nc, L, G, E, P).astype(F32)
    dtc = dt.reshape(b, nc, L, G, E)
    xdt = xh * dtc[..., None]
    da = dtc * a.reshape(G, E)
    bc = bm.reshape(b, nc, L, G, N).astype(F32)
    cc = cm.reshape(b, nc, L, G, N).astype(F32)
    acs = jnp.cumsum(da, axis=2).transpose(0, 1, 3, 4, 2)
    seg = acs[..., :, None] - acs[..., None, :]
    tri = jnp.tril(jnp.ones((L, L), dtype=bool))
    lmat = jnp.exp(jnp.where(tri, seg, -jnp.inf))
    cb = jnp.einsum('bclgn,bcsgn->bcgls', cc, bc)
    y_diag = jnp.einsum('bcgls,bcgels,bcsgep->bclgep', cb, lmat, xdt)
    decay_states = jnp.exp(acs[..., -1:] - acs)
    states = jnp.einsum('bclgn,bcgel,bclgep->bcgepn', bc, decay_states, xdt)
    chunk_decay = jnp.exp(acs[..., -1])

    def step(h, inp):
        st, dec = inp
        return h * dec[..., None, None] + st, h

    h0 = jnp.zeros((b, G, E, P, N), F32)
    _, prev = lax.scan(step, h0, (jnp.moveaxis(states, 1, 0), jnp.moveaxis(chunk_decay, 1, 0)))
    prev = jnp.moveaxis(prev, 0, 1)
    y_off = jnp.einsum('bclgn,bcgepn,bcgel->bclgep', cc, prev, jnp.exp(acs))
    y = y_diag + y_off + xh * d_skip.astype(F32).reshape(G, E)[:, :, None]
    y = y.reshape(b, s, SSM_D_INNER) * jax.nn.silu(z.astype(F32))
    return rms_norm(y, norm_w).astype(z.dtype)


def block_causal_attention(q, k, v, log_decay_cum):
    b, h, s, dk = q.shape
    nb = s // ATTN_BLOCK
    scale = dk ** -0.5
    qb = q.reshape(b, h, nb, ATTN_BLOCK, dk).transpose(2, 0, 1, 3, 4)
    fb = None if log_decay_cum is None else log_decay_cum.reshape(b, h, nb, ATTN_BLOCK).transpose(2, 0, 1, 3)
    kpos = jnp.arange(s)

    def one_block(args):
        qi, fi, i = args
        sc = jnp.einsum('bhqd,bhkd->bhqk', qi, k).astype(F32) * scale
        if fi is not None:
            sc = sc + fi[..., :, None] - log_decay_cum[:, :, None, :]
        qpos = i * ATTN_BLOCK + jnp.arange(ATTN_BLOCK)
        sc = jnp.where(kpos[None, :] <= qpos[:, None], sc, -jnp.inf)
        p = jax.nn.softmax(sc, axis=-1).astype(v.dtype)
        return jnp.einsum('bhqk,bhkd->bhqd', p, v)

    o = lax.map(one_block, (qb, fb, jnp.arange(nb)))
    return o.transpose(1, 0, 3, 2, 4).reshape(b, s, h * v.shape[-1])


def even_mixer(x, w_in, pool_w, pool_scale, conv_w, conv_b, dt_bias, a_log, d_skip, norm_w, w_out):
    proj = x @ w_in
    c1 = POOL_WIDTH
    c2 = c1 + SSM_D_INNER
    c3 = c2 + SSM_CONV_DIM
    u, z, xbc, dt = jnp.split(proj, [c1, c2, c3], axis=-1)
    y_pool = multiscale_pool(u, pool_w, pool_scale)
    y_ssm = mamba2_ssd(z, xbc, dt, conv_w, conv_b, dt_bias, a_log, d_skip, norm_w)
    return jnp.concatenate([y_pool, y_ssm], axis=-1) @ w_out


def odd_mixer(x, w_in, fgate_b, q_norm_w, w_uq, kv_norm_w, w_ukv, w_out):
    b, s, _ = x.shape
    proj = x @ w_in
    cuts = np.cumsum([FOX_WIDTH, FOX_WIDTH, FOX_WIDTH, FOX_HEADS, MLA_Q_RANK, MLA_KV_RANK]).tolist()
    qf, kf, vf, fl, cq, ckv, kr = jnp.split(proj, cuts, axis=-1)

    def heads(t, nh):
        return t.reshape(b, s, nh, -1).transpose(0, 2, 1, 3)

    log_f = jax.nn.log_sigmoid(fl.astype(F32) + fgate_b)
    fcum = jnp.cumsum(log_f, axis=1).transpose(0, 2, 1)
    o_fox = block_causal_attention(heads(qf, FOX_HEADS), heads(kf, FOX_HEADS), heads(vf, FOX_HEADS), fcum)

    pos = jnp.arange(s)
    q = heads(rms_norm(cq, q_norm_w) @ w_uq, MLA_HEADS)
    q = jnp.concatenate([q[..., :MLA_NOPE], rope(q[..., MLA_NOPE:], pos)], axis=-1)
    kv = heads(rms_norm(ckv, kv_norm_w) @ w_ukv, MLA_HEADS)
    k_rope = jnp.broadcast_to(rope(kr[:, None], pos), (b, MLA_HEADS, s, MLA_ROPE))
    k = jnp.concatenate([kv[..., :MLA_NOPE], k_rope], axis=-1)
    o_mla = block_causal_attention(q, k, kv[..., MLA_NOPE:], None)
    return jnp.concatenate([o_fox, o_mla], axis=-1) @ w_out


def swiglu(x, w_gate, w_up, w_down):
    return (jax.nn.silu(x @ w_gate) * (x @ w_up)) @ w_down


def _fwd_setup_inputs(seed: int = 0) -> dict:
    key = jax.random.key(seed)
    ks = iter(jax.random.split(key, 40))

    def nrm(shape, std):
        return jax.random.normal(next(ks), shape, F32) * std

    def gain(shape):
        return 1.0 + 0.1 * jax.random.normal(next(ks), shape, F32)

    dt0 = jnp.exp(jax.random.uniform(next(ks), (N_EVEN, SSM_HEADS), F32, math.log(1e-3), math.log(1e-1)))
    return {
        "x": jax.random.normal(next(ks), (BATCH, SEQ, D_MODEL), F32),
        "even_w_in": nrm((N_EVEN, D_MODEL, EVEN_IN), D_MODEL ** -0.5),
        "pool_w": nrm((N_EVEN, len(POOL_WINDOWS), POOL_GROUP, POOL_GROUP), POOL_GROUP ** -0.5),
        "pool_scale": gain((N_EVEN, POOL_WIDTH)),
        "conv_w": nrm((N_EVEN, SSM_CONV, SSM_CONV_DIM), SSM_CONV ** -0.5),
        "conv_b": nrm((N_EVEN, SSM_CONV_DIM), 0.01),
        "dt_bias": dt0 + jnp.log(-jnp.expm1(-dt0)),
        "a_log": jnp.log(jax.random.uniform(next(ks), (N_EVEN, SSM_HEADS), F32, 1.0, 16.0)),
        "d_skip": gain((N_EVEN, SSM_HEADS)),
        "ssm_norm_w": gain((N_EVEN, SSM_D_INNER)),
        "even_w_out": nrm((N_EVEN, EVEN_MIX, D_MODEL), EVEN_MIX ** -0.5 * DEEPNORM_BETA),
        "odd_w_in": nrm((N_ODD, D_MODEL, ODD_IN), D_MODEL ** -0.5),
        "fgate_b": jax.random.uniform(next(ks), (N_ODD, FOX_HEADS), F32, 1.0, 5.0),
        "q_norm_w": gain((N_ODD, MLA_Q_RANK)),
        "w_uq": nrm((N_ODD, MLA_Q_RANK, MLA_HEADS * (MLA_NOPE + MLA_ROPE)), MLA_Q_RANK ** -0.5),
        "kv_norm_w": gain((N_ODD, MLA_KV_RANK)),
        "w_ukv": nrm((N_ODD, MLA_KV_RANK, MLA_HEADS * (MLA_NOPE + MLA_V)), MLA_KV_RANK ** -0.5),
        "odd_w_out": nrm((N_ODD, ODD_MIX, D_MODEL), ODD_MIX ** -0.5 * DEEPNORM_BETA),
        "ffn_w_gate": nrm((DEPTH, D_MODEL, D_FF), D_MODEL ** -0.5),
        "ffn_w_up": nrm((DEPTH, D_MODEL, D_FF), D_MODEL ** -0.5),
        "ffn_w_down": nrm((DEPTH, D_FF, D_MODEL), D_FF ** -0.5 * DEEPNORM_BETA),
        "ln_mix_g": gain((DEPTH, D_MODEL)),
        "ln_mix_b": nrm((DEPTH, D_MODEL), 0.02),
        "ln_ffn_g": gain((DEPTH, D_MODEL)),
        "ln_ffn_b": nrm((DEPTH, D_MODEL), 0.02),
    }


def _fwd_reference(x, even_w_in, pool_w, pool_scale, conv_w, conv_b, dt_bias, a_log, d_skip, ssm_norm_w,
              even_w_out, odd_w_in, fgate_b, q_norm_w, w_uq, kv_norm_w, w_ukv, odd_w_out,
              ffn_w_gate, ffn_w_up, ffn_w_down, ln_mix_g, ln_mix_b, ln_ffn_g, ln_ffn_b):
    for l in range(DEPTH):
        i = l // 2
        if l % 2 == 0:
            h = even_mixer(x, even_w_in[i], pool_w[i], pool_scale[i], conv_w[i], conv_b[i], dt_bias[i],
                           a_log[i], d_skip[i], ssm_norm_w[i], even_w_out[i])
        else:
            h = odd_mixer(x, odd_w_in[i], fgate_b[i], q_norm_w[i], w_uq[i], kv_norm_w[i], w_ukv[i], odd_w_out[i])
        x = layer_norm(DEEPNORM_ALPHA * x + h, ln_mix_g[l], ln_mix_b[l])
        x = layer_norm(DEEPNORM_ALPHA * x + swiglu(x, ffn_w_gate[l], ffn_w_up[l], ffn_w_down[l]),
                       ln_ffn_g[l], ln_ffn_b[l])
    return x


import jax as _jax
import jax.numpy as _jnp

TWIN_FORMAT = 'train_step'
FWD_PARAMS = ['x', 'even_w_in', 'pool_w', 'pool_scale', 'conv_w', 'conv_b', 'dt_bias', 'a_log', 'd_skip', 'ssm_norm_w', 'even_w_out', 'odd_w_in', 'fgate_b', 'q_norm_w', 'w_uq', 'kv_norm_w', 'w_ukv', 'odd_w_out', 'ffn_w_gate', 'ffn_w_up', 'ffn_w_down', 'ln_mix_g', 'ln_mix_b', 'ln_ffn_g', 'ln_ffn_b']
TWIN_WEIGHTS = ['even_w_in', 'pool_w', 'pool_scale', 'conv_w', 'conv_b', 'dt_bias', 'a_log', 'd_skip', 'ssm_norm_w', 'even_w_out', 'odd_w_in', 'fgate_b', 'q_norm_w', 'w_uq', 'kv_norm_w', 'w_ukv', 'odd_w_out', 'ffn_w_gate', 'ffn_w_up', 'ffn_w_down', 'ln_mix_g', 'ln_mix_b', 'ln_ffn_g', 'ln_ffn_b']
TWIN_DIFF_INPUT = 'x'
TWIN_INPUTS = ['x', 'even_w_in', 'pool_w', 'pool_scale', 'conv_w', 'conv_b', 'dt_bias', 'a_log', 'd_skip', 'ssm_norm_w', 'even_w_out', 'odd_w_in', 'fgate_b', 'q_norm_w', 'w_uq', 'kv_norm_w', 'w_ukv', 'odd_w_out', 'ffn_w_gate', 'ffn_w_up', 'ffn_w_down', 'ln_mix_g', 'ln_mix_b', 'ln_ffn_g', 'ln_ffn_b', 'loss_target', 'm_even_w_in', 'm_pool_w', 'm_pool_scale', 'm_conv_w', 'm_conv_b', 'm_dt_bias', 'm_a_log', 'm_d_skip', 'm_ssm_norm_w', 'm_even_w_out', 'm_odd_w_in', 'm_fgate_b', 'm_q_norm_w', 'm_w_uq', 'm_kv_norm_w', 'm_w_ukv', 'm_odd_w_out', 'm_ffn_w_gate', 'm_ffn_w_up', 'm_ffn_w_down', 'm_ln_mix_g', 'm_ln_mix_b', 'm_ln_ffn_g', 'm_ln_ffn_b', 'v_even_w_in', 'v_pool_w', 'v_pool_scale', 'v_conv_w', 'v_conv_b', 'v_dt_bias', 'v_a_log', 'v_d_skip', 'v_ssm_norm_w', 'v_even_w_out', 'v_odd_w_in', 'v_fgate_b', 'v_q_norm_w', 'v_w_uq', 'v_kv_norm_w', 'v_w_ukv', 'v_odd_w_out', 'v_ffn_w_gate', 'v_ffn_w_up', 'v_ffn_w_down', 'v_ln_mix_g', 'v_ln_mix_b', 'v_ln_ffn_g', 'v_ln_ffn_b']
TWIN_OUTPUTS = ['loss', 'grad_x', 'grad_even_w_in', 'grad_pool_w', 'grad_pool_scale', 'grad_conv_w', 'grad_conv_b', 'grad_dt_bias', 'grad_a_log', 'grad_d_skip', 'grad_ssm_norm_w', 'grad_even_w_out', 'grad_odd_w_in', 'grad_fgate_b', 'grad_q_norm_w', 'grad_w_uq', 'grad_kv_norm_w', 'grad_w_ukv', 'grad_odd_w_out', 'grad_ffn_w_gate', 'grad_ffn_w_up', 'grad_ffn_w_down', 'grad_ln_mix_g', 'grad_ln_mix_b', 'grad_ln_ffn_g', 'grad_ln_ffn_b', 'delta_even_w_in', 'delta_pool_w', 'delta_pool_scale', 'delta_conv_w', 'delta_conv_b', 'delta_dt_bias', 'delta_a_log', 'delta_d_skip', 'delta_ssm_norm_w', 'delta_even_w_out', 'delta_odd_w_in', 'delta_fgate_b', 'delta_q_norm_w', 'delta_w_uq', 'delta_kv_norm_w', 'delta_w_ukv', 'delta_odd_w_out', 'delta_ffn_w_gate', 'delta_ffn_w_up', 'delta_ffn_w_down', 'delta_ln_mix_g', 'delta_ln_mix_b', 'delta_ln_ffn_g', 'delta_ln_ffn_b', 'new_m_even_w_in', 'new_m_pool_w', 'new_m_pool_scale', 'new_m_conv_w', 'new_m_conv_b', 'new_m_dt_bias', 'new_m_a_log', 'new_m_d_skip', 'new_m_ssm_norm_w', 'new_m_even_w_out', 'new_m_odd_w_in', 'new_m_fgate_b', 'new_m_q_norm_w', 'new_m_w_uq', 'new_m_kv_norm_w', 'new_m_w_ukv', 'new_m_odd_w_out', 'new_m_ffn_w_gate', 'new_m_ffn_w_up', 'new_m_ffn_w_down', 'new_m_ln_mix_g', 'new_m_ln_mix_b', 'new_m_ln_ffn_g', 'new_m_ln_ffn_b', 'new_v_even_w_in', 'new_v_pool_w', 'new_v_pool_scale', 'new_v_conv_w', 'new_v_conv_b', 'new_v_dt_bias', 'new_v_a_log', 'new_v_d_skip', 'new_v_ssm_norm_w', 'new_v_even_w_out', 'new_v_odd_w_in', 'new_v_fgate_b', 'new_v_q_norm_w', 'new_v_w_uq', 'new_v_kv_norm_w', 'new_v_w_ukv', 'new_v_odd_w_out', 'new_v_ffn_w_gate', 'new_v_ffn_w_up', 'new_v_ffn_w_down', 'new_v_ln_mix_g', 'new_v_ln_mix_b', 'new_v_ln_ffn_g', 'new_v_ln_ffn_b']
TWIN_LEAF_KINDS = {'loss': 'loss', 'grad_x': 'grad_x', 'grad_even_w_in': 'grad_w', 'grad_pool_w': 'grad_w', 'grad_pool_scale': 'grad_w', 'grad_conv_w': 'grad_w', 'grad_conv_b': 'grad_w', 'grad_dt_bias': 'grad_w', 'grad_a_log': 'grad_w', 'grad_d_skip': 'grad_w', 'grad_ssm_norm_w': 'grad_w', 'grad_even_w_out': 'grad_w', 'grad_odd_w_in': 'grad_w', 'grad_fgate_b': 'grad_w', 'grad_q_norm_w': 'grad_w', 'grad_w_uq': 'grad_w', 'grad_kv_norm_w': 'grad_w', 'grad_w_ukv': 'grad_w', 'grad_odd_w_out': 'grad_w', 'grad_ffn_w_gate': 'grad_w', 'grad_ffn_w_up': 'grad_w', 'grad_ffn_w_down': 'grad_w', 'grad_ln_mix_g': 'grad_w', 'grad_ln_mix_b': 'grad_w', 'grad_ln_ffn_g': 'grad_w', 'grad_ln_ffn_b': 'grad_w', 'delta_even_w_in': 'delta_w', 'delta_pool_w': 'delta_w', 'delta_pool_scale': 'delta_w', 'delta_conv_w': 'delta_w', 'delta_conv_b': 'delta_w', 'delta_dt_bias': 'delta_w', 'delta_a_log': 'delta_w', 'delta_d_skip': 'delta_w', 'delta_ssm_norm_w': 'delta_w', 'delta_even_w_out': 'delta_w', 'delta_odd_w_in': 'delta_w', 'delta_fgate_b': 'delta_w', 'delta_q_norm_w': 'delta_w', 'delta_w_uq': 'delta_w', 'delta_kv_norm_w': 'delta_w', 'delta_w_ukv': 'delta_w', 'delta_odd_w_out': 'delta_w', 'delta_ffn_w_gate': 'delta_w', 'delta_ffn_w_up': 'delta_w', 'delta_ffn_w_down': 'delta_w', 'delta_ln_mix_g': 'delta_w', 'delta_ln_mix_b': 'delta_w', 'delta_ln_ffn_g': 'delta_w', 'delta_ln_ffn_b': 'delta_w', 'new_m_even_w_in': 'new_m', 'new_m_pool_w': 'new_m', 'new_m_pool_scale': 'new_m', 'new_m_conv_w': 'new_m', 'new_m_conv_b': 'new_m', 'new_m_dt_bias': 'new_m', 'new_m_a_log': 'new_m', 'new_m_d_skip': 'new_m', 'new_m_ssm_norm_w': 'new_m', 'new_m_even_w_out': 'new_m', 'new_m_odd_w_in': 'new_m', 'new_m_fgate_b': 'new_m', 'new_m_q_norm_w': 'new_m', 'new_m_w_uq': 'new_m', 'new_m_kv_norm_w': 'new_m', 'new_m_w_ukv': 'new_m', 'new_m_odd_w_out': 'new_m', 'new_m_ffn_w_gate': 'new_m', 'new_m_ffn_w_up': 'new_m', 'new_m_ffn_w_down': 'new_m', 'new_m_ln_mix_g': 'new_m', 'new_m_ln_mix_b': 'new_m', 'new_m_ln_ffn_g': 'new_m', 'new_m_ln_ffn_b': 'new_m', 'new_v_even_w_in': 'new_v', 'new_v_pool_w': 'new_v', 'new_v_pool_scale': 'new_v', 'new_v_conv_w': 'new_v', 'new_v_conv_b': 'new_v', 'new_v_dt_bias': 'new_v', 'new_v_a_log': 'new_v', 'new_v_d_skip': 'new_v', 'new_v_ssm_norm_w': 'new_v', 'new_v_even_w_out': 'new_v', 'new_v_odd_w_in': 'new_v', 'new_v_fgate_b': 'new_v', 'new_v_q_norm_w': 'new_v', 'new_v_w_uq': 'new_v', 'new_v_kv_norm_w': 'new_v', 'new_v_w_ukv': 'new_v', 'new_v_odd_w_out': 'new_v', 'new_v_ffn_w_gate': 'new_v', 'new_v_ffn_w_up': 'new_v', 'new_v_ffn_w_down': 'new_v', 'new_v_ln_mix_g': 'new_v', 'new_v_ln_mix_b': 'new_v', 'new_v_ln_ffn_g': 'new_v', 'new_v_ln_ffn_b': 'new_v'}


def _forward(args):
    return _fwd_reference(*[args[k] for k in FWD_PARAMS])


def _output_shape():
    def fwd():
        inp = _fwd_setup_inputs(0)
        return _fwd_reference(*[inp[k] for k in FWD_PARAMS])
    out = _jax.eval_shape(fwd)
    return out.shape, out.dtype

N_MICROBATCH = 1
ADAM_LR = 0.001
ADAM_B1 = 0.9
ADAM_B2 = 0.999
ADAM_EPS = 1e-08
ADAM_WD = 0.01
ADAM_STEP = 10
PER_EXAMPLE_BATCH_AXIS = {'x': 0, 'loss_target': 0}
SHARED_INPUTS = []
_WEIGHT_DTYPES = {'even_w_in': _jnp.float32, 'pool_w': _jnp.float32, 'pool_scale': _jnp.float32, 'conv_w': _jnp.float32, 'conv_b': _jnp.float32, 'dt_bias': _jnp.float32, 'a_log': _jnp.float32, 'd_skip': _jnp.float32, 'ssm_norm_w': _jnp.float32, 'even_w_out': _jnp.float32, 'odd_w_in': _jnp.float32, 'fgate_b': _jnp.float32, 'q_norm_w': _jnp.float32, 'w_uq': _jnp.float32, 'kv_norm_w': _jnp.float32, 'w_ukv': _jnp.float32, 'odd_w_out': _jnp.float32, 'ffn_w_gate': _jnp.float32, 'ffn_w_up': _jnp.float32, 'ffn_w_down': _jnp.float32, 'ln_mix_g': _jnp.float32, 'ln_mix_b': _jnp.float32, 'ln_ffn_g': _jnp.float32, 'ln_ffn_b': _jnp.float32}
MOMENT_SCALE = {'even_w_in': 5.389352e-02, 'pool_w': 6.159194e-02, 'pool_scale': 6.535864e-02, 'conv_w': 5.254207e-02, 'conv_b': 1.067312e-01, 'dt_bias': 1.393351e-01, 'a_log': 6.446151e-01, 'd_skip': 3.038608e-01, 'ssm_norm_w': 6.906074e-02, 'even_w_out': 1.993651e-01, 'odd_w_in': 1.970757e-02, 'fgate_b': 8.991487e-02, 'q_norm_w': 1.234365e-02, 'w_uq': 1.006226e-02, 'kv_norm_w': 2.709134e-02, 'w_ukv': 1.344111e-02, 'odd_w_out': 4.857928e-02, 'ffn_w_gate': 2.475202e-02, 'ffn_w_up': 2.520670e-02, 'ffn_w_down': 9.974515e-02, 'ln_mix_g': 1.373922e+01, 'ln_mix_b': 1.354029e+00, 'ln_ffn_g': 3.820147e+01, 'ln_ffn_b': 2.508134e+00}


def _to_microbatches(a, axis):
    t = _jnp.moveaxis(a, axis, 0)
    t = t.reshape((N_MICROBATCH, t.shape[0] // N_MICROBATCH) + t.shape[1:])
    return _jnp.moveaxis(t, 1, axis + 1)


def setup_inputs(seed: int = 0) -> dict:
    inp = _fwd_setup_inputs(seed)
    key = _jax.random.fold_in(_jax.random.key(seed), 7919)
    shape, _ = _output_shape()
    out = dict(inp)
    out["loss_target"] = _jax.random.normal(_jax.random.fold_in(key, 0), shape, _jnp.float32)
    for i, name in enumerate(TWIN_WEIGHTS):
        w = inp[name].astype(_jnp.float32)
        if MOMENT_SCALE is None:
            s = _jnp.sqrt(_jnp.mean(_jnp.square(w)) + 1e-30)
        else:
            s = MOMENT_SCALE[name]
        km, kv = _jax.random.split(_jax.random.fold_in(key, i + 1))
        out[name] = w
        out["m_" + name] = s * _jax.random.normal(km, w.shape, _jnp.float32)
        out["v_" + name] = (s * s) * _jax.random.uniform(kv, w.shape, _jnp.float32, 0.5, 1.5)
    if N_MICROBATCH > 1:
        for name, axis in PER_EXAMPLE_BATCH_AXIS.items():
            out[name] = _to_microbatches(out[name], axis)
    return {'x': out['x'], 'even_w_in': out['even_w_in'], 'pool_w': out['pool_w'], 'pool_scale': out['pool_scale'], 'conv_w': out['conv_w'], 'conv_b': out['conv_b'], 'dt_bias': out['dt_bias'], 'a_log': out['a_log'], 'd_skip': out['d_skip'], 'ssm_norm_w': out['ssm_norm_w'], 'even_w_out': out['even_w_out'], 'odd_w_in': out['odd_w_in'], 'fgate_b': out['fgate_b'], 'q_norm_w': out['q_norm_w'], 'w_uq': out['w_uq'], 'kv_norm_w': out['kv_norm_w'], 'w_ukv': out['w_ukv'], 'odd_w_out': out['odd_w_out'], 'ffn_w_gate': out['ffn_w_gate'], 'ffn_w_up': out['ffn_w_up'], 'ffn_w_down': out['ffn_w_down'], 'ln_mix_g': out['ln_mix_g'], 'ln_mix_b': out['ln_mix_b'], 'ln_ffn_g': out['ln_ffn_g'], 'ln_ffn_b': out['ln_ffn_b'], 'loss_target': out['loss_target'], 'm_even_w_in': out['m_even_w_in'], 'm_pool_w': out['m_pool_w'], 'm_pool_scale': out['m_pool_scale'], 'm_conv_w': out['m_conv_w'], 'm_conv_b': out['m_conv_b'], 'm_dt_bias': out['m_dt_bias'], 'm_a_log': out['m_a_log'], 'm_d_skip': out['m_d_skip'], 'm_ssm_norm_w': out['m_ssm_norm_w'], 'm_even_w_out': out['m_even_w_out'], 'm_odd_w_in': out['m_odd_w_in'], 'm_fgate_b': out['m_fgate_b'], 'm_q_norm_w': out['m_q_norm_w'], 'm_w_uq': out['m_w_uq'], 'm_kv_norm_w': out['m_kv_norm_w'], 'm_w_ukv': out['m_w_ukv'], 'm_odd_w_out': out['m_odd_w_out'], 'm_ffn_w_gate': out['m_ffn_w_gate'], 'm_ffn_w_up': out['m_ffn_w_up'], 'm_ffn_w_down': out['m_ffn_w_down'], 'm_ln_mix_g': out['m_ln_mix_g'], 'm_ln_mix_b': out['m_ln_mix_b'], 'm_ln_ffn_g': out['m_ln_ffn_g'], 'm_ln_ffn_b': out['m_ln_ffn_b'], 'v_even_w_in': out['v_even_w_in'], 'v_pool_w': out['v_pool_w'], 'v_pool_scale': out['v_pool_scale'], 'v_conv_w': out['v_conv_w'], 'v_conv_b': out['v_conv_b'], 'v_dt_bias': out['v_dt_bias'], 'v_a_log': out['v_a_log'], 'v_d_skip': out['v_d_skip'], 'v_ssm_norm_w': out['v_ssm_norm_w'], 'v_even_w_out': out['v_even_w_out'], 'v_odd_w_in': out['v_odd_w_in'], 'v_fgate_b': out['v_fgate_b'], 'v_q_norm_w': out['v_q_norm_w'], 'v_w_uq': out['v_w_uq'], 'v_kv_norm_w': out['v_kv_norm_w'], 'v_w_ukv': out['v_w_ukv'], 'v_odd_w_out': out['v_odd_w_out'], 'v_ffn_w_gate': out['v_ffn_w_gate'], 'v_ffn_w_up': out['v_ffn_w_up'], 'v_ffn_w_down': out['v_ffn_w_down'], 'v_ln_mix_g': out['v_ln_mix_g'], 'v_ln_mix_b': out['v_ln_mix_b'], 'v_ln_ffn_g': out['v_ln_ffn_g'], 'v_ln_ffn_b': out['v_ln_ffn_b']}


def _loss(weights, diff, rest, loss_target):
    with _jax.named_scope("forward"):
        args = {**rest, TWIN_DIFF_INPUT: diff, **{k: w.astype(_WEIGHT_DTYPES[k]) for k, w in weights.items()}}
        y = _forward(args)
    with _jax.named_scope("loss_head"):
        err = _jnp.square(y.astype(_jnp.float32) - loss_target)
        return 0.5 * _jnp.sum(_jnp.mean(err, axis=-1)) if err.ndim else 0.5 * err


def _adamw(w, g, m, v):
    m = ADAM_B1 * m + (1.0 - ADAM_B1) * g
    v = ADAM_B2 * v + (1.0 - ADAM_B2) * _jnp.square(g)
    m_hat = m / (1.0 - ADAM_B1 ** ADAM_STEP)
    v_hat = v / (1.0 - ADAM_B2 ** ADAM_STEP)
    delta = -ADAM_LR * (m_hat / (_jnp.sqrt(v_hat) + ADAM_EPS) + ADAM_WD * w)
    return delta, m, v


def reference(x, even_w_in, pool_w, pool_scale, conv_w, conv_b, dt_bias, a_log, d_skip, ssm_norm_w, even_w_out, odd_w_in, fgate_b, q_norm_w, w_uq, kv_norm_w, w_ukv, odd_w_out, ffn_w_gate, ffn_w_up, ffn_w_down, ln_mix_g, ln_mix_b, ln_ffn_g, ln_ffn_b, loss_target, m_even_w_in, m_pool_w, m_pool_scale, m_conv_w, m_conv_b, m_dt_bias, m_a_log, m_d_skip, m_ssm_norm_w, m_even_w_out, m_odd_w_in, m_fgate_b, m_q_norm_w, m_w_uq, m_kv_norm_w, m_w_ukv, m_odd_w_out, m_ffn_w_gate, m_ffn_w_up, m_ffn_w_down, m_ln_mix_g, m_ln_mix_b, m_ln_ffn_g, m_ln_ffn_b, v_even_w_in, v_pool_w, v_pool_scale, v_conv_w, v_conv_b, v_dt_bias, v_a_log, v_d_skip, v_ssm_norm_w, v_even_w_out, v_odd_w_in, v_fgate_b, v_q_norm_w, v_w_uq, v_kv_norm_w, v_w_ukv, v_odd_w_out, v_ffn_w_gate, v_ffn_w_up, v_ffn_w_down, v_ln_mix_g, v_ln_mix_b, v_ln_ffn_g, v_ln_ffn_b):
    given = dict(x=x, even_w_in=even_w_in, pool_w=pool_w, pool_scale=pool_scale, conv_w=conv_w, conv_b=conv_b, dt_bias=dt_bias, a_log=a_log, d_skip=d_skip, ssm_norm_w=ssm_norm_w, even_w_out=even_w_out, odd_w_in=odd_w_in, fgate_b=fgate_b, q_norm_w=q_norm_w, w_uq=w_uq, kv_norm_w=kv_norm_w, w_ukv=w_ukv, odd_w_out=odd_w_out, ffn_w_gate=ffn_w_gate, ffn_w_up=ffn_w_up, ffn_w_down=ffn_w_down, ln_mix_g=ln_mix_g, ln_mix_b=ln_mix_b, ln_ffn_g=ln_ffn_g, ln_ffn_b=ln_ffn_b, loss_target=loss_target, m_even_w_in=m_even_w_in, m_pool_w=m_pool_w, m_pool_scale=m_pool_scale, m_conv_w=m_conv_w, m_conv_b=m_conv_b, m_dt_bias=m_dt_bias, m_a_log=m_a_log, m_d_skip=m_d_skip, m_ssm_norm_w=m_ssm_norm_w, m_even_w_out=m_even_w_out, m_odd_w_in=m_odd_w_in, m_fgate_b=m_fgate_b, m_q_norm_w=m_q_norm_w, m_w_uq=m_w_uq, m_kv_norm_w=m_kv_norm_w, m_w_ukv=m_w_ukv, m_odd_w_out=m_odd_w_out, m_ffn_w_gate=m_ffn_w_gate, m_ffn_w_up=m_ffn_w_up, m_ffn_w_down=m_ffn_w_down, m_ln_mix_g=m_ln_mix_g, m_ln_mix_b=m_ln_mix_b, m_ln_ffn_g=m_ln_ffn_g, m_ln_ffn_b=m_ln_ffn_b, v_even_w_in=v_even_w_in, v_pool_w=v_pool_w, v_pool_scale=v_pool_scale, v_conv_w=v_conv_w, v_conv_b=v_conv_b, v_dt_bias=v_dt_bias, v_a_log=v_a_log, v_d_skip=v_d_skip, v_ssm_norm_w=v_ssm_norm_w, v_even_w_out=v_even_w_out, v_odd_w_in=v_odd_w_in, v_fgate_b=v_fgate_b, v_q_norm_w=v_q_norm_w, v_w_uq=v_w_uq, v_kv_norm_w=v_kv_norm_w, v_w_ukv=v_w_ukv, v_odd_w_out=v_odd_w_out, v_ffn_w_gate=v_ffn_w_gate, v_ffn_w_up=v_ffn_w_up, v_ffn_w_down=v_ffn_w_down, v_ln_mix_g=v_ln_mix_g, v_ln_mix_b=v_ln_mix_b, v_ln_ffn_g=v_ln_ffn_g, v_ln_ffn_b=v_ln_ffn_b)
    weights = {n: given[n] for n in TWIN_WEIGHTS}
    shared = {n: given[n] for n in SHARED_INPUTS}
    per_example = {n: given[n] for n in ['x']}
    grad_fn = _jax.value_and_grad(_loss, argnums=(0, 1))

    def one_microbatch(ex, loss_target):
        ex = dict(ex)
        diff = ex.pop(TWIN_DIFF_INPUT)
        return grad_fn(weights, diff, {**shared, **ex}, loss_target)

    if N_MICROBATCH == 1:
        loss, (grad_w, grad_x) = one_microbatch(per_example, given["loss_target"])
    else:
        def body(carry, xs):
            loss_sum, grad_sum = carry
            l_k, (gw_k, gx_k) = one_microbatch(xs[0], xs[1])
            with _jax.named_scope("update"):
                return (loss_sum + l_k, _jax.tree.map(_jnp.add, grad_sum, gw_k)), gx_k

        init = (_jnp.zeros((), _jnp.float32), _jax.tree.map(_jnp.zeros_like, weights))
        (loss, grad_w), grad_x = _jax.lax.scan(body, init, (per_example, given["loss_target"]))
    with _jax.named_scope("update"):
        delta_w, new_m, new_v = {}, {}, {}
        for n in TWIN_WEIGHTS:
            delta_w[n], new_m[n], new_v[n] = _adamw(weights[n], grad_w[n], given["m_" + n], given["v_" + n])
    return (loss, grad_x, *[grad_w[n] for n in TWIN_WEIGHTS], *[delta_w[n] for n in TWIN_WEIGHTS],
            *[new_m[n] for n in TWIN_WEIGHTS], *[new_v[n] for n in TWIN_WEIGHTS])
```

```python
import functools
import math

import jax
import jax.numpy as jnp
from jax import lax
from jax.experimental import pallas as pl
from jax.experimental.pallas import tpu as pltpu

F32 = jnp.float32
BF16 = jnp.bfloat16
HI = lax.Precision.HIGHEST
MESH_ID = pl.DeviceIdType.MESH

VMEM_LIMIT_BYTES = 56 * 1024 * 1024
LANES = 128

D_MODEL = 1024
DEPTH = 4
POOL_WINDOWS = (2, 4, 8, 16)
POOL_GROUP = 128
POOL_WIDTH = 512
SSM_D_INNER = 1024
SSM_HEAD_DIM = 64
SSM_HEADS = 16
SSM_GROUPS = 2
SSM_STATE = 128
SSM_CONV = 4
SSM_CHUNK = 128
SSM_CONV_DIM = 1536
EVEN_IN = 3088
EVEN_IN_PAD = 3200
FOX_HEADS = 8
FOX_WIDTH = 512
MLA_HEADS = 8
MLA_NOPE = 64
MLA_ROPE = 32
MLA_V = 64
MLA_Q_RANK = 512
MLA_KV_RANK = 256
ROPE_THETA = 10000.0
ODD_IN = 2344
ODD_IN_PAD = 2560
D_FF = 2816
ALPHA = (2 * DEPTH) ** 0.25
LN_EPS = 1e-5
RMS_EPS = 1e-6
ADAM_LR = 0.001
ADAM_B1 = 0.9
ADAM_B2 = 0.999
ADAM_EPS = 1e-08
ADAM_WD = 0.01
ADAM_STEP = 10
NEG_BIG = -1e30

ATTN_TILE = 512
ROW_TILE = 256


def _cparams(sem=None):
    return pltpu.CompilerParams(dimension_semantics=sem, vmem_limit_bytes=VMEM_LIMIT_BYTES)


def _pick(d, prefs):
    for p in prefs:
        if d % p == 0:
            return p
    return d


_MN_PREFS = (1024, 512, 640, 1408, 768, 384, 256, 128)
_K_PREFS = (512, 640, 1408, 768, 256, 128)


def _mm(a, b, mode, out_dtype, name, extra=None, alpha=1.0):
    if mode == "nn":
        (m, k), (k2, n) = a.shape, b.shape
    elif mode == "nt":
        (m, k), (n, k2) = a.shape, b.shape
    else:
        (k, m), (k2, n) = a.shape, b.shape
    assert k == k2, (a.shape, b.shape, mode)
    tm, tn, tk = _pick(m, _MN_PREFS), _pick(n, _MN_PREFS), _pick(k, _K_PREFS)
    nk = k // tk
    if mode == "nn":
        a_spec = pl.BlockSpec((tm, tk), lambda i, j, kk: (i, kk))
        b_spec = pl.BlockSpec((tk, tn), lambda i, j, kk: (kk, j))
        dims = (((1,), (0,)), ((), ()))
    elif mode == "nt":
        a_spec = pl.BlockSpec((tm, tk), lambda i, j, kk: (i, kk))
        b_spec = pl.BlockSpec((tn, tk), lambda i, j, kk: (j, kk))
        dims = (((1,), (1,)), ((), ()))
    else:
        a_spec = pl.BlockSpec((tk, tm), lambda i, j, kk: (kk, i))
        b_spec = pl.BlockSpec((tk, tn), lambda i, j, kk: (kk, j))
        dims = (((0,), (0,)), ((), ()))
    o_spec = pl.BlockSpec((tm, tn), lambda i, j, kk: (i, j))
    has_extra = extra is not None

    def body(*refs):
        if has_extra:
            a_ref, b_ref, e_ref, o_ref, acc = refs
        else:
            a_ref, b_ref, o_ref, acc = refs
        kk = pl.program_id(2)

        @pl.when(kk == 0)
        def _():
            acc[...] = jnp.zeros_like(acc)

        acc[...] += lax.dot_general(a_ref[...].astype(BF16), b_ref[...].astype(BF16), dims,
                                    preferred_element_type=F32)

        @pl.when(kk == nk - 1)
        def _():
            r = acc[...]
            if has_extra:
                r = r + alpha * e_ref[...].astype(F32)
            o_ref[...] = r.astype(o_ref.dtype)

    ins = [a, b] + ([extra] if has_extra else [])
    specs = [a_spec, b_spec] + ([o_spec] if has_extra else [])
    return pl.pallas_call(
        body, name=name, grid=(m // tm, n // tn, nk), in_specs=specs, out_specs=o_spec,
        out_shape=jax.ShapeDtypeStruct((m, n), out_dtype),
        scratch_shapes=[pltpu.VMEM((tm, tn), F32)],
        compiler_params=_cparams(("parallel", "parallel", "arbitrary")),
    )(*ins)


def _full_spec(shape):
    nd = len(shape)
    return pl.BlockSpec(tuple(shape), lambda i, _nd=nd: (0,) * _nd)


def _scan_fwd(f, name, n, tiles, ctiles, params, cparams, carry_shapes, out_defs):
    rows = tiles[0].shape[0]
    tt = rows // n
    nt, nct, npar, ncp, ncar, nout = len(tiles), len(ctiles), len(params), len(cparams), len(carry_shapes), len(out_defs)

    def body(*refs):
        pos = 0
        t_refs = refs[pos:pos + nt]; pos += nt
        ct_refs = refs[pos:pos + nct]; pos += nct
        p_refs = refs[pos:pos + npar]; pos += npar
        cp_refs = refs[pos:pos + ncp]; pos += ncp
        o_refs = refs[pos:pos + nout]; pos += nout
        cs_refs = refs[pos:pos + ncar]; pos += ncar
        c_scr = refs[pos:pos + ncar]
        i = pl.program_id(0)

        @pl.when(i == 0)
        def _():
            for c in c_scr:
                c[...] = jnp.zeros_like(c)

        carry = tuple(c[...] for c in c_scr)
        for s, c in zip(cs_refs, carry):
            s[0] = c
        new_carry, outs = f(carry, tuple(r[...] for r in t_refs), tuple(r[...] for r in ct_refs),
                            tuple(r[...] for r in p_refs), tuple(r[...] for r in cp_refs), i)
        for o_ref, o in zip(o_refs, outs):
            o_ref[...] = o
        for c, v in zip(c_scr, new_carry):
            c[...] = v

    tile_spec = lambda arr: pl.BlockSpec((tt, arr.shape[1]), lambda i: (i, 0))
    in_specs = ([tile_spec(t) for t in tiles] + [tile_spec(t) for t in ctiles]
                + [_full_spec(p.shape) for p in params] + [_full_spec(p.shape) for p in cparams])
    out_specs = ([pl.BlockSpec((tt, c), lambda i: (i, 0)) for c, _ in out_defs]
                 + [pl.BlockSpec((1,) + tuple(s), lambda i: (i, 0, 0)) for s in carry_shapes])
    out_shape = ([jax.ShapeDtypeStruct((rows, c), dt) for c, dt in out_defs]
                 + [jax.ShapeDtypeStruct((n,) + tuple(s), F32) for s in carry_shapes])
    res = pl.pallas_call(
        body, name=name, grid=(n,), in_specs=in_specs, out_specs=out_specs, out_shape=out_shape,
        scratch_shapes=[pltpu.VMEM(tuple(s), F32) for s in carry_shapes],
        compiler_params=_cparams(("arbitrary",)),
    )(*tiles, *ctiles, *params, *cparams)
    return list(res[:nout]), list(res[nout:])


def _scan_bwd(f, name, n, tiles, ctiles, params, cparams, carries, douts):
    rows = tiles[0].shape[0]
    tt = rows // n
    nt, nct, npar, ncp, ncar, nout = len(tiles), len(ctiles), len(params), len(cparams), len(carries), len(douts)

    def body(*refs):
        pos = 0
        t_refs = refs[pos:pos + nt]; pos += nt
        ct_refs = refs[pos:pos + nct]; pos += nct
        p_refs = refs[pos:pos + npar]; pos += npar
        cp_refs = refs[pos:pos + ncp]; pos += ncp
        cs_refs = refs[pos:pos + ncar]; pos += ncar
        do_refs = refs[pos:pos + nout]; pos += nout
        dt_refs = refs[pos:pos + nt]; pos += nt
        dp_refs = refs[pos:pos + npar]; pos += npar
        dc_scr = refs[pos:pos + ncar]
        i = pl.program_id(0)

        @pl.when(i == 0)
        def _():
            for c in dc_scr:
                c[...] = jnp.zeros_like(c)
            for d in dp_refs:
                d[...] = jnp.zeros_like(d)

        ctv = tuple(r[...] for r in ct_refs)
        cpv = tuple(r[...] for r in cp_refs)

        def g(c, t, p):
            return f(c, t, ctv, p, cpv, n - 1 - i)

        _, vjp = jax.vjp(g, tuple(s[0] for s in cs_refs), tuple(r[...] for r in t_refs),
                         tuple(r[...] for r in p_refs))
        dc, dt, dp = vjp((tuple(c[...] for c in dc_scr), tuple(r[...] for r in do_refs)))
        for r, v in zip(dt_refs, dt):
            r[...] = v
        for r, v in zip(dp_refs, dp):
            r[...] += v
        for c, v in zip(dc_scr, dc):
            c[...] = v

    rev_tile = lambda arr: pl.BlockSpec((tt, arr.shape[1]), lambda i: (n - 1 - i, 0))
    in_specs = ([rev_tile(t) for t in tiles] + [rev_tile(t) for t in ctiles]
                + [_full_spec(p.shape) for p in params] + [_full_spec(p.shape) for p in cparams]
                + [pl.BlockSpec((1,) + tuple(c.shape[1:]), lambda i: (n - 1 - i, 0, 0)) for c in carries]
                + [rev_tile(d) for d in douts])
    out_specs = [rev_tile(t) for t in tiles] + [_full_spec(p.shape) for p in params]
    out_shape = ([jax.ShapeDtypeStruct(t.shape, t.dtype) for t in tiles]
                 + [jax.ShapeDtypeStruct(p.shape, F32) for p in params])
    res = pl.pallas_call(
        body, name=name, grid=(n,), in_specs=in_specs, out_specs=out_specs, out_shape=out_shape,
        scratch_shapes=[pltpu.VMEM(tuple(c.shape[1:]), F32) for c in carries],
        compiler_params=_cparams(("arbitrary",)),
    )(*tiles, *ctiles, *params, *cparams, *carries, *douts)
    return list(res[:nt]), list(res[nt:])


def _silu(x):
    return x * jax.nn.sigmoid(x)


def _softplus(x):
    return jnp.maximum(x, 0.0) + jnp.log(1.0 + jnp.exp(-jnp.abs(x)))


def _f_ln(carry, tiles, ctiles, params, cparams, idx):
    (r,), (g, b) = tiles, params
    mu = jnp.mean(r, axis=-1, keepdims=True)
    xc = r - mu
    var = jnp.mean(xc * xc, axis=-1, keepdims=True)
    return (), (xc * lax.rsqrt(var + LN_EPS) * g + b,)


def _f_act(carry, tiles, ctiles, params, cparams, idx):
    (gu,) = tiles
    g = gu[:, :D_FF].astype(F32)
    u = gu[:, D_FF:].astype(F32)
    return (), ((_silu(g) * u).astype(BF16),)


def _f_pool(carry, tiles, ctiles, params, cparams, idx):
    (prev,), (u,), (pool_w, pool_scale) = carry, tiles, params
    tt = u.shape[0]
    halo = prev.shape[0]
    ext = jnp.concatenate([prev, u], axis=0)
    pos = idx * tt + lax.broadcasted_iota(jnp.int32, (tt, 1), 0)
    ys = []
    for g, w in enumerate(POOL_WINDOWS):
        lo, hi = g * POOL_GROUP, (g + 1) * POOL_GROUP
        eg = ext[:, lo:hi]
        s = eg[halo:halo + tt]
        for j in range(1, w):
            s = s + eg[halo - j:halo - j + tt]
        count = jnp.minimum(pos + 1, w).astype(F32)
        diff = s / count - u[:, lo:hi]
        ys.append(jnp.dot(diff.astype(BF16), pool_w[g].astype(BF16), preferred_element_type=F32))
    y = jnp.concatenate(ys, axis=1) * pool_scale
    return (u[tt - halo:, :],), (y.astype(BF16),)


def _f_conv(carry, tiles, ctiles, params, cparams, idx):
    (prev,), (xbc,), (conv_w, conv_b) = carry, tiles, params
    tt = xbc.shape[0]
    halo = prev.shape[0]
    ext = jnp.concatenate([prev, xbc], axis=0)
    acc = jnp.zeros_like(xbc) + conv_b
    for k in range(SSM_CONV):
        off = halo - (SSM_CONV - 1) + k
        acc = acc + ext[off:off + tt] * conv_w[k:k + 1, :]
    return (xbc[tt - halo:, :],), (_silu(acc),)


def _f_ssd(carry, tiles, ctiles, params, cparams, idx):
    (state,), (xa, dtr, z), (dt_bias, a_log, d_skip, norm_w) = carry, tiles, params
    L, P, N, E = SSM_CHUNK, SSM_HEAD_DIM, SSM_STATE, SSM_HEADS // SSM_GROUPS
    dt = _softplus(dtr + dt_bias)
    da = dt * (-jnp.exp(a_log))
    r = lax.broadcasted_iota(jnp.int32, (L, L), 0)
    c = lax.broadcasted_iota(jnp.int32, (L, L), 1)
    tri = c <= r
    acs = jnp.dot(tri.astype(F32), da, precision=HI, preferred_element_type=F32)
    acs_t = acs.T
    ys, new_states = [], []
    for g in range(SSM_GROUPS):
        bg = xa[:, SSM_D_INNER + g * N:SSM_D_INNER + (g + 1) * N].astype(BF16)
        cg = xa[:, SSM_D_INNER + (SSM_GROUPS + g) * N:SSM_D_INNER + (SSM_GROUPS + g + 1) * N].astype(BF16)
        cb = lax.dot_general(cg, bg, (((1,), (1,)), ((), ())), preferred_element_type=F32)
        for e in range(E):
            h = g * E + e
            xh = xa[:, h * P:(h + 1) * P]
            col = acs[:, h:h + 1]
            row = acs_t[h:h + 1, :]
            last = acs[L - 1:L, h:h + 1]
            lmat = jnp.exp(jnp.where(tri, col - row, NEG_BIG))
            xdt_f = xh * dt[:, h:h + 1]
            y_diag = jnp.dot((cb * lmat).astype(BF16), xdt_f.astype(BF16), preferred_element_type=F32)
            xdec = (xdt_f * jnp.exp(last - col)).astype(BF16)
            st = lax.dot_general(xdec, bg, (((0,), (0,)), ((), ())), preferred_element_type=F32)
            prev = state[h * P:(h + 1) * P, :]
            y_off = lax.dot_general(cg, prev.astype(BF16), (((1,), (1,)), ((), ())),
                                    preferred_element_type=F32) * jnp.exp(col)
            new_states.append(prev * jnp.exp(last) + st)
            ys.append(y_diag + y_off + xh * d_skip[:, h:h + 1])
    y = jnp.concatenate(ys, axis=1) * _silu(z)
    y = y * lax.rsqrt(jnp.mean(y * y, axis=-1, keepdims=True) + RMS_EPS) * norm_w
    return (jnp.concatenate(new_states, axis=0),), (y.astype(BF16),)


def _f_fox_gate(carry, tiles, ctiles, params, cparams, idx):
    (run,), (fl,), (fb,) = carry, tiles, params
    tt = fl.shape[0]
    logf = -_softplus(-(fl + fb))
    r = lax.broadcasted_iota(jnp.int32, (tt, tt), 0)
    c = lax.broadcasted_iota(jnp.int32, (tt, tt), 1)
    cum = jnp.dot((c <= r).astype(F32), logf, precision=HI, preferred_element_type=F32) + run[0:1, :]
    return (jnp.broadcast_to(cum[tt - 1:tt, :], run.shape),), (cum,)


def _rms(x, w):
    return x * lax.rsqrt(jnp.mean(x * x, axis=-1, keepdims=True) + RMS_EPS) * w


def _f_mla_prep(carry, tiles, ctiles, params, cparams, idx):
    (cq, ckv, kr), (ck, sk), (qw, kvw), (rk,) = tiles, ctiles, params, cparams
    rot = jnp.dot(kr, rk, precision=HI, preferred_element_type=F32)
    return (), (_rms(cq, qw).astype(BF16), _rms(ckv, kvw).astype(BF16), (kr * ck + rot * sk).astype(BF16))


def _f_rope_q(carry, tiles, ctiles, params, cparams, idx):
    (q,), (cos, sin) = tiles, ctiles
    nope = MLA_HEADS * MLA_NOPE
    x1 = q[:, nope:nope + LANES]
    x2 = q[:, nope + LANES:]
    return (), (jnp.concatenate([q[:, :nope], x1 * cos - x2 * sin, x2 * cos + x1 * sin], axis=1).astype(BF16),)


def _scores(q, k, fq, fk, scale, qi, ki, tq, tk):
    s = lax.dot_general(q, k, (((1,), (1,)), ((), ())), preferred_element_type=F32) * scale
    if fq is not None:
        s = s + fq - fk
    rows = qi * tq + lax.broadcasted_iota(jnp.int32, (tq, tk), 0)
    cols = ki * tk + lax.broadcasted_iota(jnp.int32, (tq, tk), 1)
    return jnp.where(cols <= rows, s, NEG_BIG)


def _attn_fwd(q, k, v, fq, fk, scale, name):
    H, T, dk = q.shape
    dv = v.shape[2]
    tq = tk = min(ATTN_TILE, T)
    nq, nk = T // tq, T // tk
    decay = fq is not None

    def body(*refs):
        if decay:
            q_ref, k_ref, v_ref, fq_ref, fk_ref, o_ref, lse_ref, m_s, l_s, acc = refs
        else:
            q_ref, k_ref, v_ref, o_ref, lse_ref, m_s, l_s, acc = refs
        qi, ki = pl.program_id(1), pl.program_id(2)

        @pl.when(ki == 0)
        def _():
            m_s[...] = jnp.full_like(m_s, NEG_BIG)
            l_s[...] = jnp.zeros_like(l_s)
            acc[...] = jnp.zeros_like(acc)

        @pl.when(ki <= qi)
        def _():
            s = _scores(q_ref[0], k_ref[0], fq_ref[0] if decay else None, fk_ref[0] if decay else None,
                        scale, qi, ki, tq, tk)
            m_new = jnp.maximum(m_s[...], jnp.max(s, axis=-1, keepdims=True))
            p = jnp.exp(s - m_new)
            corr = jnp.exp(m_s[...] - m_new)
            l_s[...] = corr * l_s[...] + jnp.sum(p, axis=-1, keepdims=True)
            acc[...] = corr * acc[...] + jnp.dot(p.astype(BF16), v_ref[0], preferred_element_type=F32)
            m_s[...] = m_new

        @pl.when(ki == nk - 1)
        def _():
            o_ref[0] = (acc[...] / l_s[...]).astype(o_ref.dtype)
            lse_ref[0] = m_s[...] + jnp.log(l_s[...])

    qspec = lambda d: pl.BlockSpec((1, tq, d), lambda h, qi, ki: (h, qi, 0))
    kspec = lambda d: pl.BlockSpec((1, tk, d), lambda h, qi, ki: (h, jnp.minimum(ki, qi), 0))
    in_specs = [qspec(dk), kspec(dk), kspec(dv)]
    ins = [q, k, v]
    if decay:
        in_specs += [qspec(1), pl.BlockSpec((1, 1, tk), lambda h, qi, ki: (h, 0, jnp.minimum(ki, qi)))]
        ins += [fq, fk]
    return pl.pallas_call(
        body, name=name, grid=(H, nq, nk), in_specs=in_specs,
        out_specs=[qspec(dv), qspec(1)],
        out_shape=[jax.ShapeDtypeStruct((H, T, dv), BF16), jax.ShapeDtypeStruct((H, T, 1), F32)],
        scratch_shapes=[pltpu.VMEM((tq, 1), F32), pltpu.VMEM((tq, 1), F32), pltpu.VMEM((tq, dv), F32)],
        compiler_params=_cparams(("parallel", "parallel", "arbitrary")),
    )(*ins)


def _attn_bwd_dq(q, k, v, o, do, lse, fq, fk, scale, name):
    H, T, dk = q.shape
    dv = v.shape[2]
    tq = tk = min(ATTN_TILE, T)
    nq, nk = T // tq, T // tk
    decay = fq is not None

    def body(*refs):
        if decay:
            q_ref, k_ref, v_ref, o_ref, do_ref, lse_ref, fq_ref, fk_ref, dq_ref, dl_ref, acc, dl, leak = refs
        else:
            q_ref, k_ref, v_ref, o_ref, do_ref, lse_ref, dq_ref, dl_ref, acc, dl, leak = refs
        qi, ki = pl.program_id(1), pl.program_id(2)

        @pl.when(ki == 0)
        def _():
            acc[...] = jnp.zeros_like(acc)
            leak[...] = jnp.zeros_like(leak)
            dl[...] = jnp.sum(do_ref[0].astype(F32) * o_ref[0].astype(F32), axis=-1, keepdims=True)

        @pl.when(ki <= qi)
        def _():
            s = _scores(q_ref[0], k_ref[0], fq_ref[0] if decay else None, fk_ref[0] if decay else None,
                        scale, qi, ki, tq, tk)
            p = jnp.exp(s - lse_ref[0])
            dp = lax.dot_general(do_ref[0], v_ref[0], (((1,), (1,)), ((), ())), preferred_element_type=F32)
            ds = p * (dp - dl[...])
            leak[...] += jnp.sum(ds, axis=-1, keepdims=True)
            acc[...] += jnp.dot(ds.astype(BF16), k_ref[0], preferred_element_type=F32)

        @pl.when(ki == nk - 1)
        def _():
            dq_ref[0] = (acc[...] * scale).astype(dq_ref.dtype)
            dl_ref[0] = dl[...] + leak[...]

    qspec = lambda d: pl.BlockSpec((1, tq, d), lambda h, qi, ki: (h, qi, 0))
    kspec = lambda d: pl.BlockSpec((1, tk, d), lambda h, qi, ki: (h, jnp.minimum(ki, qi), 0))
    in_specs = [qspec(dk), kspec(dk), kspec(dv), qspec(dv), qspec(dv), qspec(1)]
    ins = [q, k, v, o, do, lse]
    if decay:
        in_specs += [qspec(1), pl.BlockSpec((1, 1, tk), lambda h, qi, ki: (h, 0, jnp.minimum(ki, qi)))]
        ins += [fq, fk]
    return pl.pallas_call(
        body, name=name, grid=(H, nq, nk), in_specs=in_specs, out_specs=[qspec(dk), qspec(1)],
        out_shape=[jax.ShapeDtypeStruct((H, T, dk), BF16), jax.ShapeDtypeStruct((H, T, 1), F32)],
        scratch_shapes=[pltpu.VMEM((tq, dk), F32), pltpu.VMEM((tq, 1), F32), pltpu.VMEM((tq, 1), F32)],
        compiler_params=_cparams(("parallel", "parallel", "arbitrary")),
    )(*ins)


def _attn_bwd_dkv(q, k, v, delta, do, lse, fq, fk, scale, name):
    H, T, dk = q.shape
    dv = v.shape[2]
    tq = tk = min(ATTN_TILE, T)
    nq, nk = T // tq, T // tk
    decay = fq is not None

    def body(*refs):
        if decay:
            q_ref, k_ref, v_ref, dl_ref, do_ref, lse_ref, fq_ref, fk_ref, dk_ref, dv_ref, df_ref, dk_s, dv_s, df_s = refs
        else:
            q_ref, k_ref, v_ref, dl_ref, do_ref, lse_ref, dk_ref, dv_ref, dk_s, dv_s = refs
        ki, qi = pl.program_id(1), pl.program_id(2)

        @pl.when(qi == 0)
        def _():
            dk_s[...] = jnp.zeros_like(dk_s)
            dv_s[...] = jnp.zeros_like(dv_s)
            if decay:
                df_s[...] = jnp.zeros_like(df_s)

        @pl.when(qi >= ki)
        def _():
            s = _scores(q_ref[0], k_ref[0], fq_ref[0] if decay else None, fk_ref[0] if decay else None,
                        scale, qi, ki, tq, tk)
            p = jnp.exp(s - lse_ref[0])
            do_v = do_ref[0]
            delta = dl_ref[0]
            dv_s[...] += lax.dot_general(p.astype(BF16), do_v, (((0,), (0,)), ((), ())), preferred_element_type=F32)
            dp = lax.dot_general(do_v, v_ref[0], (((1,), (1,)), ((), ())), preferred_element_type=F32)
            ds = p * (dp - delta)
            dk_s[...] += lax.dot_general(ds.astype(BF16), q_ref[0], (((0,), (0,)), ((), ())),
                                         preferred_element_type=F32)
            if decay:
                df_s[...] -= jnp.sum(ds, axis=0, keepdims=True)

        @pl.when(qi == nq - 1)
        def _():
            dk_ref[0] = (dk_s[...] * scale).astype(dk_ref.dtype)
            dv_ref[0] = dv_s[...].astype(dv_ref.dtype)
            if decay:
                df_ref[0] = df_s[...]

    qspec = lambda d: pl.BlockSpec((1, tq, d), lambda h, ki, qi: (h, jnp.maximum(qi, ki), 0))
    kspec = lambda d: pl.BlockSpec((1, tk, d), lambda h, ki, qi: (h, ki, 0))
    in_specs = [qspec(dk), kspec(dk), kspec(dv), qspec(1), qspec(dv), qspec(1)]
    ins = [q, k, v, delta, do, lse]
    out_specs = [kspec(dk), kspec(dv)]
    out_shape = [jax.ShapeDtypeStruct((H, T, dk), BF16), jax.ShapeDtypeStruct((H, T, dv), BF16)]
    scratch = [pltpu.VMEM((tk, dk), F32), pltpu.VMEM((tk, dv), F32)]
    if decay:
        fkspec = pl.BlockSpec((1, 1, tk), lambda h, ki, qi: (h, 0, ki))
        in_specs += [qspec(1), fkspec]
        ins += [fq, fk]
        out_specs.append(fkspec)
        out_shape.append(jax.ShapeDtypeStruct((H, 1, T), F32))
        scratch.append(pltpu.VMEM((1, tk), F32))
    return pl.pallas_call(
        body, name=name, grid=(H, nk, nq), in_specs=in_specs, out_specs=out_specs, out_shape=out_shape,
        scratch_shapes=scratch, compiler_params=_cparams(("parallel", "parallel", "arbitrary")),
    )(*ins)


def _loss_head(y, target, name):
    rows, d = y.shape
    tt = _pick(rows, (512, 256, 128))

    def body(y_ref, t_ref, dy_ref, part_ref):
        @pl.when(pl.program_id(0) == 0)
        def _():
            part_ref[...] = jnp.zeros_like(part_ref)

        err = y_ref[...] - t_ref[...]
        dy_ref[...] = err * (1.0 / d)
        sq = jnp.sum(err * err, axis=0, keepdims=True)
        folded = sq[:, :LANES]
        for j in range(1, d // LANES):
            folded = folded + sq[:, j * LANES:(j + 1) * LANES]
        part_ref[...] += folded

    spec = pl.BlockSpec((tt, d), lambda i: (i, 0))
    return pl.pallas_call(
        body, name=name, grid=(rows // tt,), in_specs=[spec, spec],
        out_specs=[spec, pl.BlockSpec((1, LANES), lambda i: (0, 0))],
        out_shape=[jax.ShapeDtypeStruct((rows, d), F32), jax.ShapeDtypeStruct((1, LANES), F32)],
        compiler_params=_cparams(("arbitrary",)),
    )(y, target)


def _adamw(w, g, m, v, name):
    shape = w.shape
    cols = shape[-1]
    rows = math.prod(shape[:-1])
    w2, g2, m2, v2 = (a.reshape(rows, cols) for a in (w, g, m, v))
    tr = rows
    for cand in (512, 256, 128, 64, 32, 16, 8):
        if rows % cand == 0:
            tr = cand
            break
    c1 = 1.0 / (1.0 - ADAM_B1 ** ADAM_STEP)
    c2 = 1.0 / (1.0 - ADAM_B2 ** ADAM_STEP)

    def body(w_ref, g_ref, m_ref, v_ref, d_ref, nm_ref, nv_ref):
        gv = g_ref[...]
        nm = ADAM_B1 * m_ref[...] + (1.0 - ADAM_B1) * gv
        nv = ADAM_B2 * v_ref[...] + (1.0 - ADAM_B2) * gv * gv
        d_ref[...] = -ADAM_LR * ((nm * c1) / (jnp.sqrt(nv * c2) + ADAM_EPS) + ADAM_WD * w_ref[...])
        nm_ref[...] = nm
        nv_ref[...] = nv

    spec = pl.BlockSpec((tr, cols), lambda i: (i, 0))
    sds = jax.ShapeDtypeStruct((rows, cols), F32)
    d, nm, nv = pl.pallas_call(
        body, name=name, grid=(rows // tr,), in_specs=[spec] * 4, out_specs=[spec] * 3, out_shape=[sds] * 3,
        compiler_params=_cparams(("parallel",)),
    )(w2, g2, m2, v2)
    return d.reshape(shape), nm.reshape(shape), nv.reshape(shape)


_ANY = pl.BlockSpec(memory_space=pl.ANY)


def _my_xyc():
    return lax.axis_index("x"), lax.axis_index("y"), lax.axis_index("c")


def _chip_allgather(shard, name):
    def body(x_ref, out_ref, send_sems, recv_sems, loc_sem):
        x, y, c = _my_xyc()
        myq = 2 * x + y
        local = pltpu.make_async_copy(x_ref, out_ref.at[myq], loc_sem)
        local.start()
        chips = [(1 - x, y), (x, 1 - y), (1 - x, 1 - y)]
        sends = []
        for j, (px, py) in enumerate(chips):
            cp = pltpu.make_async_remote_copy(src_ref=x_ref, dst_ref=out_ref.at[myq], send_sem=send_sems.at[j],
                                              recv_sem=recv_sems.at[j], device_id=(px, py, c), device_id_type=MESH_ID)
            cp.start()
            sends.append(cp)
        for j, (px, py) in enumerate(chips):
            pltpu.make_async_remote_copy(src_ref=x_ref, dst_ref=out_ref.at[2 * px + py], send_sem=send_sems.at[j],
                                         recv_sem=recv_sems.at[j], device_id=(px, py, c),
                                         device_id_type=MESH_ID).wait_recv()
        for cp in sends:
            cp.wait_send()
        local.wait()

    return pl.pallas_call(
        body, name=name, in_specs=[_ANY], out_specs=_ANY,
        out_shape=jax.ShapeDtypeStruct((4,) + shard.shape, shard.dtype),
        scratch_shapes=[pltpu.SemaphoreType.DMA((3,)), pltpu.SemaphoreType.DMA((3,)), pltpu.SemaphoreType.DMA],
    )(shard)


def _sibling_swap_halves(g, name):
    q, rows, cols = g.shape
    rh = rows // 2

    def body(g_ref, out_ref, send_sem, recv_sem):
        x, y, c = _my_xyc()
        src = g_ref.at[:, pl.ds((1 - c) * rh, rh), :]
        cp = pltpu.make_async_remote_copy(src_ref=src, dst_ref=out_ref, send_sem=send_sem, recv_sem=recv_sem,
                                          device_id=(x, y, 1 - c), device_id_type=MESH_ID)
        cp.start()
        cp.wait()

    return pl.pallas_call(
        body, name=name, in_specs=[_ANY], out_specs=_ANY,
        out_shape=jax.ShapeDtypeStruct((q, rh, cols), g.dtype),
        scratch_shapes=[pltpu.SemaphoreType.DMA, pltpu.SemaphoreType.DMA],
    )(g)


def _add_own_half(g, other, name):
    q, rows, cols = g.shape
    rh = rows // 2
    tr = _pick(rh, (512, 256, 128, 64, 32, 16, 8))
    nb = rh // tr
    cidx = lax.axis_index("c").astype(jnp.int32).reshape(1)

    def body(c_ref, g_ref, o_ref, out_ref):
        out_ref[...] = g_ref[...] + o_ref[...]

    grid_spec = pltpu.PrefetchScalarGridSpec(
        num_scalar_prefetch=1, grid=(q, nb),
        in_specs=[pl.BlockSpec((1, tr, cols), lambda a, i, c_ref: (a, c_ref[0] * nb + i, 0)),
                  pl.BlockSpec((1, tr, cols), lambda a, i, c_ref: (a, i, 0))],
        out_specs=pl.BlockSpec((1, tr, cols), lambda a, i, c_ref: (a, i, 0)))
    return pl.pallas_call(
        body, name=name, grid_spec=grid_spec, out_shape=jax.ShapeDtypeStruct((q, rh, cols), F32),
        compiler_params=_cparams(("parallel", "parallel")),
    )(cidx, g, other)


def _chip_exchange(s, name):
    def body(s_ref, out_ref, send_sems, recv_sems, loc_sem):
        x, y, c = _my_xyc()
        myq = 2 * x + y
        local = pltpu.make_async_copy(s_ref.at[myq], out_ref.at[myq], loc_sem)
        local.start()
        chips = [(1 - x, y), (x, 1 - y), (1 - x, 1 - y)]
        sends = []
        for j, (px, py) in enumerate(chips):
            cp = pltpu.make_async_remote_copy(src_ref=s_ref.at[2 * px + py], dst_ref=out_ref.at[myq],
                                              send_sem=send_sems.at[j], recv_sem=recv_sems.at[j],
                                              device_id=(px, py, c), device_id_type=MESH_ID)
            cp.start()
            sends.append(cp)
        for j, (px, py) in enumerate(chips):
            pltpu.make_async_remote_copy(src_ref=s_ref.at[myq], dst_ref=out_ref.at[2 * px + py],
                                         send_sem=send_sems.at[j], recv_sem=recv_sems.at[j],
                                         device_id=(px, py, c), device_id_type=MESH_ID).wait_recv()
        for cp in sends:
            cp.wait_send()
        local.wait()

    return pl.pallas_call(
        body, name=name, in_specs=[_ANY], out_specs=_ANY, out_shape=jax.ShapeDtypeStruct(s.shape, s.dtype),
        scratch_shapes=[pltpu.SemaphoreType.DMA((3,)), pltpu.SemaphoreType.DMA((3,)), pltpu.SemaphoreType.DMA],
    )(s)


def _sum_leading(a, name):
    q, rows, cols = a.shape
    tr = _pick(rows, (512, 256, 128, 64, 32, 16, 8))

    def body(a_ref, out_ref):
        acc = a_ref[0]
        for j in range(1, q):
            acc = acc + a_ref[j]
        out_ref[...] = acc

    return pl.pallas_call(
        body, name=name, grid=(rows // tr,), in_specs=[pl.BlockSpec((q, tr, cols), lambda i: (0, i, 0))],
        out_specs=pl.BlockSpec((tr, cols), lambda i: (i, 0)), out_shape=jax.ShapeDtypeStruct((rows, cols), F32),
        compiler_params=_cparams(("parallel",)),
    )(a)


def _sibling_join_halves(r, name):
    rh, cols = r.shape

    def body(r_ref, out_ref, send_sem, recv_sem, loc_sem):
        x, y, c = _my_xyc()
        local = pltpu.make_async_copy(r_ref, out_ref.at[pl.ds(c * rh, rh), :], loc_sem)
        local.start()
        cp = pltpu.make_async_remote_copy(src_ref=r_ref, dst_ref=out_ref.at[pl.ds(c * rh, rh), :], send_sem=send_sem,
                                          recv_sem=recv_sem, device_id=(x, y, 1 - c), device_id_type=MESH_ID)
        cp.start()
        pltpu.make_async_remote_copy(src_ref=r_ref, dst_ref=out_ref.at[pl.ds((1 - c) * rh, rh), :], send_sem=send_sem,
                                     recv_sem=recv_sem, device_id=(x, y, 1 - c), device_id_type=MESH_ID).wait_recv()
        cp.wait_send()
        local.wait()

    return pl.pallas_call(
        body, name=name, in_specs=[_ANY], out_specs=_ANY, out_shape=jax.ShapeDtypeStruct((2 * rh, cols), r.dtype),
        scratch_shapes=[pltpu.SemaphoreType.DMA, pltpu.SemaphoreType.DMA, pltpu.SemaphoreType.DMA],
    )(r)


def _all_reduce_small(vec, name):
    rows, cols = vec.shape

    def body(v_ref, out_ref, buf, send_sems, recv_sems):
        x, y, c = _my_xyc()
        me = 4 * x + 2 * y + c
        buf[me] = v_ref[...]
        sends = []
        for kk in range(1, 8):
            peer = (1 - x if kk & 4 else x, 1 - y if kk & 2 else y, 1 - c if kk & 1 else c)
            cp = pltpu.make_async_remote_copy(src_ref=v_ref, dst_ref=buf.at[me], send_sem=send_sems.at[kk - 1],
                                              recv_sem=recv_sems.at[kk - 1], device_id=peer, device_id_type=MESH_ID)
            cp.start()
            sends.append(cp)
        for kk in range(1, 8):
            px, py, pc = (1 - x if kk & 4 else x, 1 - y if kk & 2 else y, 1 - c if kk & 1 else c)
            pltpu.make_async_remote_copy(src_ref=v_ref, dst_ref=buf.at[4 * px + 2 * py + pc],
                                         send_sem=send_sems.at[kk - 1], recv_sem=recv_sems.at[kk - 1],
                                         device_id=(px, py, pc), device_id_type=MESH_ID).wait_recv()
        for cp in sends:
            cp.wait_send()
        acc = buf[0]
        for j in range(1, 8):
            acc = acc + buf[j]
        out_ref[...] = acc

    vm = pl.BlockSpec(memory_space=pltpu.VMEM)
    return pl.pallas_call(
        body, name=name, in_specs=[vm], out_specs=vm, out_shape=jax.ShapeDtypeStruct((rows, cols), F32),
        scratch_shapes=[pltpu.VMEM((8, rows, cols), F32), pltpu.SemaphoreType.DMA((7,)), pltpu.SemaphoreType.DMA((7,))],
        compiler_params=pltpu.CompilerParams(vmem_limit_bytes=VMEM_LIMIT_BYTES),
    )(vec)


def _reduce_scatter(g):
    other = _sibling_swap_halves(g, "rs_swap_halves")
    s = _add_own_half(g, other, "rs_add_halves")
    recv = _chip_exchange(s, "rs_chip_exchange")
    r = _sum_leading(recv, "rs_sum_chips")
    return _sibling_join_halves(r, "rs_join_halves")


PACK_COLS = 1024
SHARDED = (("even_w_in", 2), ("conv_w", 2), ("even_w_out", 1), ("odd_w_in", 2), ("q_norm_w", 1), ("w_uq", 2),
           ("kv_norm_w", 1), ("w_ukv", 2), ("odd_w_out", 1), ("ffn_w_gate", 2), ("ffn_w_up", 2), ("ffn_w_down", 1))
SPLIT_HI_LO = ("conv_w", "q_norm_w", "kv_norm_w")
REPLICATED = ("pool_w", "pool_scale", "conv_b", "dt_bias", "a_log", "d_skip", "ssm_norm_w", "fgate_b",
              "ln_mix_g", "ln_mix_b", "ln_ffn_g", "ln_ffn_b")


def _pad_rows(flat, row_multiple):
    n = flat.shape[0]
    per = PACK_COLS * row_multiple
    total = -(-n // per) * per
    return jnp.pad(flat, (0, total - n)).reshape(total // PACK_COLS, PACK_COLS)


def _pack_weight_shards(shards):
    parts = []
    for name, _ in SHARDED:
        w = shards[name].reshape(-1)
        if name in SPLIT_HI_LO:
            hi = w.astype(BF16)
            parts += [hi, (w - hi.astype(F32)).astype(BF16)]
        else:
            parts.append(w.astype(BF16))
    return _pad_rows(jnp.concatenate(parts), 16)


def _unpack_gathered(gathered, shards):
    flat = gathered.reshape(4, -1)
    out, off = {}, 0
    for name, axis in SHARDED:
        shp = shards[name].shape
        size = math.prod(shp)
        if name in SPLIT_HI_LO:
            hi = flat[:, off:off + size].astype(F32)
            lo = flat[:, off + size:off + 2 * size].astype(F32)
            seg = hi + lo
            off += 2 * size
        else:
            seg = flat[:, off:off + size]
            off += size
        seg = seg.reshape((4,) + shp)
        out[name] = jnp.concatenate([seg[q] for q in range(4)], axis=axis)
    return out


def _pack_grads(grads, shards):
    blocks = []
    for q in range(4):
        parts = []
        for name, axis in SHARDED:
            width = shards[name].shape[axis]
            parts.append(lax.slice_in_dim(grads[name], q * width, (q + 1) * width, axis=axis).reshape(-1))
        blocks.append(_pad_rows(jnp.concatenate(parts), 16))
    return jnp.stack(blocks)


def _unpack_reduced(reduced, shards):
    flat = reduced.reshape(-1)
    out, off = {}, 0
    for name, _ in SHARDED:
        shp = shards[name].shape
        size = math.prod(shp)
        out[name] = flat[off:off + size].reshape(shp)
        off += size
    return out


def _to_heads(t, nh):
    rows, width = t.shape
    return t.reshape(rows, nh, width // nh).transpose(1, 0, 2)


def _from_heads(t):
    nh, rows, d = t.shape
    return t.transpose(1, 0, 2).reshape(rows, nh * d)


def _rope_tables(rows):
    half = MLA_ROPE // 2
    freqs = jnp.power(ROPE_THETA, -jnp.arange(half, dtype=F32) / half)
    ang = jnp.arange(rows, dtype=F32)[:, None] * freqs[None, :]
    cos, sin = jnp.cos(ang), jnp.sin(ang)
    cos_q, sin_q = jnp.tile(cos, (1, MLA_HEADS)), jnp.tile(sin, (1, MLA_HEADS))
    zeros = jnp.zeros((rows, LANES - MLA_ROPE), F32)
    cos_k = jnp.concatenate([cos, cos, zeros], axis=1)
    sin_k = jnp.concatenate([sin, sin, zeros], axis=1)
    r = lax.broadcasted_iota(jnp.int32, (LANES, LANES), 0)
    c = lax.broadcasted_iota(jnp.int32, (LANES, LANES), 1)
    rot = (jnp.where((r == c + half) & (c < half), -1.0, 0.0)
           + jnp.where((c == r + half) & (r < half), 1.0, 0.0)).astype(F32)
    return cos_q, sin_q, cos_k, sin_k, rot


def _pad_cols(a, width):
    return jnp.pad(a, ((0, 0), (0, width - a.shape[1])))


def _prep_even_w_in(w):
    return _pad_cols(w, EVEN_IN_PAD)


_ODD_CUT = (1536, 1544, 2312, 2344)


def _prep_odd_w_in(w):
    c0, c1, c2, c3 = _ODD_CUT
    return jnp.concatenate([w[:, :c0], w[:, c1:c2], _pad_cols(w[:, c2:c3], LANES), _pad_cols(w[:, c0:c1], LANES)],
                           axis=1)


def _unprep_odd_w_in(g):
    c0, c1, c2, c3 = _ODD_CUT
    n1 = c0 + (c2 - c1)
    return jnp.concatenate([g[:, :c0], g[:, n1 + LANES:n1 + LANES + (c1 - c0)], g[:, c0:n1],
                            g[:, n1:n1 + (c3 - c2)]], axis=1)


def _prep_w_uq(w):
    r = w.reshape(w.shape[0], MLA_HEADS, MLA_NOPE + MLA_ROPE)
    half = MLA_ROPE // 2
    return jnp.concatenate([r[:, :, :MLA_NOPE].reshape(w.shape[0], -1),
                            r[:, :, MLA_NOPE:MLA_NOPE + half].reshape(w.shape[0], -1),
                            r[:, :, MLA_NOPE + half:].reshape(w.shape[0], -1)], axis=1)


def _unprep_w_uq(g):
    rows = g.shape[0]
    half = MLA_ROPE // 2
    nope = MLA_HEADS * MLA_NOPE
    return jnp.concatenate([g[:, :nope].reshape(rows, MLA_HEADS, MLA_NOPE),
                            g[:, nope:nope + LANES].reshape(rows, MLA_HEADS, half),
                            g[:, nope + LANES:].reshape(rows, MLA_HEADS, half)], axis=2).reshape(rows, -1)


def _row(v, width=None):
    v = v.reshape(1, -1)
    return v if width is None else _pad_cols(v, width)


def _even_forward(x_bf, w, i):
    rows = x_bf.shape[0]
    proj = _mm(x_bf, w["even_w_in"][i], "nn", F32, "even_proj")
    u, z, xbc, dtr = (proj[:, :512], proj[:, 512:1536], proj[:, 1536:3072], proj[:, 3072:3200])
    n_row = rows // ROW_TILE
    pool_p = (w["pool_w"][i], _row(w["pool_scale"][i]))
    (y_pool,), pool_c = _scan_fwd(_f_pool, "pool_fwd", n_row, [u], [], pool_p, [], [(16, POOL_WIDTH)],
                                  [(POOL_WIDTH, BF16)])
    conv_p = (w["conv_w"][i], _row(w["conv_b"][i]))
    (xa,), conv_c = _scan_fwd(_f_conv, "conv_fwd", n_row, [xbc], [], conv_p, [], [(8, SSM_CONV_DIM)],
                              [(SSM_CONV_DIM, F32)])
    ssd_p = (_row(w["dt_bias"][i], LANES), _row(w["a_log"][i], LANES), _row(w["d_skip"][i], LANES),
             _row(w["ssm_norm_w"][i]))
    (y_ssm,), ssd_c = _scan_fwd(_f_ssd, "ssd_fwd", rows // SSM_CHUNK, [xa, dtr, z], [], ssd_p, [],
                                [(SSM_D_INNER, SSM_STATE)], [(SSM_D_INNER, BF16)])
    mix = jnp.concatenate([y_pool, y_ssm], axis=1)
    saved = dict(u=u, z=z, xbc=xbc, dtr=dtr, xa=xa, mix=mix, pool_p=pool_p, pool_c=pool_c, conv_p=conv_p,
                 conv_c=conv_c, ssd_p=ssd_p, ssd_c=ssd_c)
    return mix, saved


def _even_backward(dmix, x_bf, sv, w, i, d_r1):
    rows = x_bf.shape[0]
    n_row = rows // ROW_TILE
    dy_pool, dy_ssm = dmix[:, :POOL_WIDTH], dmix[:, POOL_WIDTH:]
    (du,), (g_pool_w, g_pool_scale) = _scan_bwd(_f_pool, "pool_bwd", n_row, [sv["u"]], [], sv["pool_p"], [],
                                                sv["pool_c"], [dy_pool])
    (dxa, ddtr, dz), (g_dt_bias, g_a_log, g_d_skip, g_norm_w) = _scan_bwd(
        _f_ssd, "ssd_bwd", rows // SSM_CHUNK, [sv["xa"], sv["dtr"], sv["z"]], [], sv["ssd_p"], [], sv["ssd_c"],
        [dy_ssm])
    (dxbc,), (g_conv_w, g_conv_b) = _scan_bwd(_f_conv, "conv_bwd", n_row, [sv["xbc"]], [], sv["conv_p"], [],
                                              sv["conv_c"], [dxa])
    dproj = jnp.concatenate([du, dz, dxbc, ddtr], axis=1).astype(BF16)
    g_w_in = _mm(x_bf, dproj, "tn", F32, "even_dw_in")[:, :EVEN_IN]
    dx = _mm(dproj, w["even_w_in"][i], "nt", F32, "even_dx", extra=d_r1, alpha=ALPHA)
    grads = dict(even_w_in=g_w_in, pool_w=g_pool_w, pool_scale=g_pool_scale[0], conv_w=g_conv_w, conv_b=g_conv_b[0],
                 dt_bias=g_dt_bias[0, :SSM_HEADS], a_log=g_a_log[0, :SSM_HEADS], d_skip=g_d_skip[0, :SSM_HEADS],
                 ssm_norm_w=g_norm_w[0])
    return dx, grads


def _odd_forward(x_bf, w, i, tables):
    rows = x_bf.shape[0]
    cos_q, sin_q, cos_k, sin_k, rot = tables
    proj = _mm(x_bf, w["odd_w_in"][i], "nn", F32, "odd_proj")
    qf, kf, vf = (_to_heads(proj[:, j * 512:(j + 1) * 512].astype(BF16), FOX_HEADS) for j in range(3))
    cq, ckv = proj[:, 1536:2048], proj[:, 2048:2304]
    kr, fl = proj[:, 2304:2432], proj[:, 2432:2560]
    n_row = rows // ROW_TILE
    fox_p = (_row(w["fgate_b"][i], LANES),)
    n_gate = rows // min(ATTN_TILE, rows)
    (fcum,), fox_c = _scan_fwd(_f_fox_gate, "fox_gate_fwd", n_gate, [fl], [], fox_p, [], [(8, LANES)], [(LANES, F32)])
    fc_heads = fcum[:, :FOX_HEADS].T
    fq, fk = fc_heads[:, :, None], fc_heads[:, None, :]
    o_fox, lse_fox = _attn_fwd(qf, kf, vf, fq, fk, 64 ** -0.5, "fox_attn_fwd")

    prep_p = (_row(w["q_norm_w"][i]), _row(w["kv_norm_w"][i]))
    (cqn, ckvn, krr), _ = _scan_fwd(_f_mla_prep, "mla_prep_fwd", n_row, [cq, ckv, kr], [cos_k, sin_k], prep_p,
                                    [rot], [], [(MLA_Q_RANK, BF16), (MLA_KV_RANK, BF16), (LANES, BF16)])
    q_flat = _mm(cqn, w["w_uq"][i], "nn", F32, "mla_q_up")
    (q_rope,), _ = _scan_fwd(_f_rope_q, "rope_q_fwd", n_row, [q_flat], [cos_q, sin_q], [], [], [],
                             [(q_flat.shape[1], BF16)])
    kv = _mm(ckvn, w["w_ukv"][i], "nn", BF16, "mla_kv_up")
    nope = MLA_HEADS * MLA_NOPE
    half = MLA_ROPE // 2
    q_m = jnp.concatenate([_to_heads(q_rope[:, :nope], MLA_HEADS), _to_heads(q_rope[:, nope:nope + LANES], MLA_HEADS),
                           _to_heads(q_rope[:, nope + LANES:], MLA_HEADS)], axis=2)
    kv_h = _to_heads(kv, MLA_HEADS)
    k_rope = jnp.broadcast_to(krr[None, :, :MLA_ROPE], (MLA_HEADS, rows, MLA_ROPE))
    k_m = jnp.concatenate([kv_h[:, :, :MLA_NOPE], k_rope], axis=2)
    v_m = kv_h[:, :, MLA_NOPE:]
    o_mla, lse_mla = _attn_fwd(q_m, k_m, v_m, None, None, (MLA_NOPE + MLA_ROPE) ** -0.5, "mla_attn_fwd")
    mix = jnp.concatenate([_from_heads(o_fox), _from_heads(o_mla)], axis=1)
    saved = dict(qf=qf, kf=kf, vf=vf, fq=fq, fk=fk, o_fox=o_fox, lse_fox=lse_fox, fl=fl, fox_p=fox_p, fox_c=fox_c,
                 cq=cq, ckv=ckv, kr=kr, prep_p=prep_p, cqn=cqn, ckvn=ckvn, q_flat=q_flat, q_m=q_m, k_m=k_m, v_m=v_m,
                 o_mla=o_mla, lse_mla=lse_mla, mix=mix)
    return mix, saved


def _odd_backward(dmix, x_bf, sv, w, i, d_r1, tables):
    rows = x_bf.shape[0]
    cos_q, sin_q, cos_k, sin_k, rot = tables
    n_row = rows // ROW_TILE
    nope = MLA_HEADS * MLA_NOPE
    half = MLA_ROPE // 2
    do_fox = _to_heads(dmix[:, :FOX_WIDTH], FOX_HEADS)
    do_mla = _to_heads(dmix[:, FOX_WIDTH:], MLA_HEADS)
    fox_args = (sv["qf"], sv["kf"], sv["vf"], sv["o_fox"], do_fox, sv["lse_fox"], sv["fq"], sv["fk"], 64 ** -0.5)
    dqf, delta_fox = _attn_bwd_dq(*fox_args, "fox_attn_dq")
    dkf, dvf, dfk = _attn_bwd_dkv(*fox_args[:3], delta_fox, *fox_args[4:], "fox_attn_dkv")
    dfcum = _pad_cols(dfk[:, 0, :].T, LANES)
    n_gate = rows // min(ATTN_TILE, rows)
    (dfl,), (g_fb,) = _scan_bwd(_f_fox_gate, "fox_gate_bwd", n_gate, [sv["fl"]], [], sv["fox_p"], [], sv["fox_c"],
                                [dfcum])
    mla_args = (sv["q_m"], sv["k_m"], sv["v_m"], sv["o_mla"], do_mla, sv["lse_mla"], None, None,
                (MLA_NOPE + MLA_ROPE) ** -0.5)
    dq_m, delta_mla = _attn_bwd_dq(*mla_args, "mla_attn_dq")
    dk_m, dv_m = _attn_bwd_dkv(*mla_args[:3], delta_mla, *mla_args[4:], "mla_attn_dkv")
    dq_rope = jnp.concatenate([_from_heads(dq_m[:, :, :MLA_NOPE]), _from_heads(dq_m[:, :, MLA_NOPE:MLA_NOPE + half]),
                               _from_heads(dq_m[:, :, MLA_NOPE + half:])], axis=1)
    (dq_flat,), _ = _scan_bwd(_f_rope_q, "rope_q_bwd", n_row, [sv["q_flat"]], [cos_q, sin_q], [], [], [], [dq_rope])
    g_w_uq = _mm(sv["cqn"], dq_flat, "tn", F32, "mla_dw_uq")
    dcqn = _mm(dq_flat, w["w_uq"][i], "nt", BF16, "mla_dcqn")
    dkv = _from_heads(jnp.concatenate([dk_m[:, :, :MLA_NOPE], dv_m], axis=2))
    g_w_ukv = _mm(sv["ckvn"], dkv, "tn", F32, "mla_dw_ukv")
    dckvn = _mm(dkv, w["w_ukv"][i], "nt", BF16, "mla_dckvn")
    dkrr = _head_sum(dk_m, "mla_dk_rope_sum")
    (dcq, dckv, dkr), (g_qw, g_kvw) = _scan_bwd(_f_mla_prep, "mla_prep_bwd", n_row, [sv["cq"], sv["ckv"], sv["kr"]],
                                                [cos_k, sin_k], sv["prep_p"], [rot], [], [dcqn, dckvn, dkrr])
    dproj = jnp.concatenate([_from_heads(dqf).astype(F32), _from_heads(dkf).astype(F32), _from_heads(dvf).astype(F32),
                             dcq, dckv, dkr, dfl], axis=1).astype(BF16)
    g_w_in = _mm(x_bf, dproj, "tn", F32, "odd_dw_in")
    dx = _mm(dproj, w["odd_w_in"][i], "nt", F32, "odd_dx", extra=d_r1, alpha=ALPHA)
    grads = dict(odd_w_in=_unprep_odd_w_in(g_w_in), fgate_b=g_fb[0, :FOX_HEADS], q_norm_w=g_qw[0], kv_norm_w=g_kvw[0],
                 w_uq=_unprep_w_uq(g_w_uq), w_ukv=g_w_ukv)
    return dx, grads


def _head_sum(dk_m, name):
    H, T, dk = dk_m.shape
    tt = _pick(T, (512, 256, 128))

    def body(d_ref, o_ref):
        acc = d_ref[0].astype(F32)
        for h in range(1, H):
            acc = acc + d_ref[h].astype(F32)
        o_ref[...] = jnp.concatenate([acc[:, MLA_NOPE:], jnp.zeros((tt, LANES - MLA_ROPE), F32)], axis=1).astype(BF16)

    return pl.pallas_call(
        body, name=name, grid=(T // tt,), in_specs=[pl.BlockSpec((H, tt, dk), lambda i: (0, i, 0))],
        out_specs=pl.BlockSpec((tt, LANES), lambda i: (i, 0)), out_shape=jax.ShapeDtypeStruct((T, LANES), BF16),
        compiler_params=_cparams(("parallel",)),
    )(dk_m)


def _local_step(x, target, w, small):
    rows = x.shape[0]
    n_row = rows // ROW_TILE
    tables = _rope_tables(rows)
    saved = []
    x_f32 = x
    x_bf = x.astype(BF16)
    for l in range(DEPTH):
        i = l // 2
        if l % 2 == 0:
            mix, sv = _even_forward(x_bf, w, i)
            w_out = w["even_w_out"][i]
        else:
            mix, sv = _odd_forward(x_bf, w, i, tables)
            w_out = w["odd_w_out"][i]
        r1 = _mm(mix, w_out, "nn", F32, "mix_out_even" if l % 2 == 0 else "mix_out_odd", extra=x_f32, alpha=ALPHA)
        ln1_p = (_row(small["ln_mix_g"][l]), _row(small["ln_mix_b"][l]))
        (x_mid,), _ = _scan_fwd(_f_ln, "ln_fwd", n_row, [r1], [], ln1_p, [], [], [(D_MODEL, F32)])
        x_mid_bf = x_mid.astype(BF16)
        gu = _mm(x_mid_bf, w["ffn_w_gu"][l], "nn", BF16, "ffn_gu")
        (act,), _ = _scan_fwd(_f_act, "ffn_act_fwd", n_row, [gu], [], [], [], [], [(D_FF, BF16)])
        r2 = _mm(act, w["ffn_w_down"][l], "nn", F32, "ffn_down", extra=x_mid, alpha=ALPHA)
        ln2_p = (_row(small["ln_ffn_g"][l]), _row(small["ln_ffn_b"][l]))
        (x_out,), _ = _scan_fwd(_f_ln, "ln_fwd", n_row, [r2], [], ln2_p, [], [], [(D_MODEL, F32)])
        saved.append(dict(sv=sv, x_bf=x_bf, r1=r1, ln1_p=ln1_p, x_mid_bf=x_mid_bf, gu=gu, act=act, r2=r2, ln2_p=ln2_p,
                          w_out=w_out))
        x_f32, x_bf = x_out, x_out.astype(BF16)

    dy, loss_part = _loss_head(x_f32, target, "loss_head")
    loss = 0.5 * jnp.sum(loss_part) / D_MODEL

    layer_grads = []
    for l in reversed(range(DEPTH)):
        i = l // 2
        s = saved[l]
        (d_r2,), (g_ln2_g, g_ln2_b) = _scan_bwd(_f_ln, "ln_bwd", n_row, [s["r2"]], [], s["ln2_p"], [], [], [dy])
        g_down = _mm(s["act"], d_r2, "tn", F32, "ffn_dw_down")
        dact = _mm(d_r2, w["ffn_w_down"][l], "nt", BF16, "ffn_dact")
        (dgu,), _ = _scan_bwd(_f_act, "ffn_act_bwd", n_row, [s["gu"]], [], [], [], [], [dact])
        g_gu = _mm(s["x_mid_bf"], dgu, "tn", F32, "ffn_dw_gu")
        dx_mid = _mm(dgu, w["ffn_w_gu"][l], "nt", F32, "ffn_dx", extra=d_r2, alpha=ALPHA)
        (d_r1,), (g_ln1_g, g_ln1_b) = _scan_bwd(_f_ln, "ln_bwd", n_row, [s["r1"]], [], s["ln1_p"], [], [], [dx_mid])
        g_w_out = _mm(s["sv"]["mix"], d_r1, "tn", F32, "even_dw_out" if l % 2 == 0 else "odd_dw_out")
        dmix = _mm(d_r1, s["w_out"], "nt", BF16, "even_dmix" if l % 2 == 0 else "odd_dmix")
        if l % 2 == 0:
            dy, g = _even_backward(dmix, s["x_bf"], s["sv"], w, i, d_r1)
            g["even_w_out"] = g_w_out
        else:
            dy, g = _odd_backward(dmix, s["x_bf"], s["sv"], w, i, d_r1, tables)
            g["odd_w_out"] = g_w_out
        g.update(ffn_w_gate=g_gu[:, :D_FF], ffn_w_up=g_gu[:, D_FF:], ffn_w_down=g_down, ln_mix_g=g_ln1_g[0],
                 ln_mix_b=g_ln1_b[0], ln_ffn_g=g_ln2_g[0], ln_ffn_b=g_ln2_b[0])
        layer_grads.append((l, g))
    return loss, dy, layer_grads


EVEN_NAMES = ("even_w_in", "pool_w", "pool_scale", "conv_w", "conv_b", "dt_bias", "a_log", "d_skip", "ssm_norm_w",
              "even_w_out")
ODD_NAMES = ("odd_w_in", "fgate_b", "q_norm_w", "w_uq", "kv_norm_w", "w_ukv", "odd_w_out")
PER_LAYER_NAMES = ("ffn_w_gate", "ffn_w_up", "ffn_w_down", "ln_mix_g", "ln_mix_b", "ln_ffn_g", "ln_ffn_b")
WEIGHT_NAMES = EVEN_NAMES + ODD_NAMES + PER_LAYER_NAMES


def _stack_grads(layer_grads):
    by_layer = dict(layer_grads)
    out = {}
    for n in EVEN_NAMES:
        out[n] = jnp.stack([by_layer[l][n] for l in range(0, DEPTH, 2)])
    for n in ODD_NAMES:
        out[n] = jnp.stack([by_layer[l][n] for l in range(1, DEPTH, 2)])
    for n in PER_LAYER_NAMES:
        out[n] = jnp.stack([by_layer[l][n] for l in range(DEPTH)])
    return out


def _prepare_weights(full):
    w = {}
    w["even_w_in"] = [_prep_even_w_in(full["even_w_in"][i]) for i in range(2)]
    w["even_w_out"] = [full["even_w_out"][i] for i in range(2)]
    w["odd_w_in"] = [_prep_odd_w_in(full["odd_w_in"][i]) for i in range(2)]
    w["w_uq"] = [_prep_w_uq(full["w_uq"][i]) for i in range(2)]
    w["w_ukv"] = [full["w_ukv"][i] for i in range(2)]
    w["odd_w_out"] = [full["odd_w_out"][i] for i in range(2)]
    w["ffn_w_gu"] = [jnp.concatenate([full["ffn_w_gate"][l], full["ffn_w_up"][l]], axis=1) for l in range(DEPTH)]
    w["ffn_w_down"] = [full["ffn_w_down"][l] for l in range(DEPTH)]
    for n in ("conv_w", "q_norm_w", "kv_norm_w"):
        w[n] = full[n]
    return w


def _flatten_small(vals):
    flat = jnp.concatenate([vals[n].reshape(-1) for n in REPLICATED])
    n = flat.shape[0]
    per = LANES * 8
    total = -(-n // per) * per
    return jnp.pad(flat, (0, total - n)).reshape(total // LANES, LANES)


def _unflatten_small(mat, like):
    flat = mat.reshape(-1)
    out, off = {}, 0
    for n in REPLICATED:
        size = math.prod(like[n].shape)
        out[n] = flat[off:off + size].reshape(like[n].shape)
        off += size
    return out


def kernel(x, even_w_in, pool_w, pool_scale, conv_w, conv_b, dt_bias, a_log, d_skip, ssm_norm_w, even_w_out, odd_w_in, fgate_b, q_norm_w, w_uq, kv_norm_w, w_ukv, odd_w_out, ffn_w_gate, ffn_w_up, ffn_w_down, ln_mix_g, ln_mix_b, ln_ffn_g, ln_ffn_b, loss_target, m_even_w_in, m_pool_w, m_pool_scale, m_conv_w, m_conv_b, m_dt_bias, m_a_log, m_d_skip, m_ssm_norm_w, m_even_w_out, m_odd_w_in, m_fgate_b, m_q_norm_w, m_w_uq, m_kv_norm_w, m_w_ukv, m_odd_w_out, m_ffn_w_gate, m_ffn_w_up, m_ffn_w_down, m_ln_mix_g, m_ln_mix_b, m_ln_ffn_g, m_ln_ffn_b, v_even_w_in, v_pool_w, v_pool_scale, v_conv_w, v_conv_b, v_dt_bias, v_a_log, v_d_skip, v_ssm_norm_w, v_even_w_out, v_odd_w_in, v_fgate_b, v_q_norm_w, v_w_uq, v_kv_norm_w, v_w_ukv, v_odd_w_out, v_ffn_w_gate, v_ffn_w_up, v_ffn_w_down, v_ln_mix_g, v_ln_mix_b, v_ln_ffn_g, v_ln_ffn_b):
    weights = dict(even_w_in=even_w_in, pool_w=pool_w, pool_scale=pool_scale, conv_w=conv_w, conv_b=conv_b,
                   dt_bias=dt_bias, a_log=a_log, d_skip=d_skip, ssm_norm_w=ssm_norm_w, even_w_out=even_w_out,
                   odd_w_in=odd_w_in, fgate_b=fgate_b, q_norm_w=q_norm_w, w_uq=w_uq, kv_norm_w=kv_norm_w, w_ukv=w_ukv,
                   odd_w_out=odd_w_out, ffn_w_gate=ffn_w_gate, ffn_w_up=ffn_w_up, ffn_w_down=ffn_w_down,
                   ln_mix_g=ln_mix_g, ln_mix_b=ln_mix_b, ln_ffn_g=ln_ffn_g, ln_ffn_b=ln_ffn_b)
    m_in = dict(even_w_in=m_even_w_in, pool_w=m_pool_w, pool_scale=m_pool_scale, conv_w=m_conv_w, conv_b=m_conv_b,
                dt_bias=m_dt_bias, a_log=m_a_log, d_skip=m_d_skip, ssm_norm_w=m_ssm_norm_w, even_w_out=m_even_w_out,
                odd_w_in=m_odd_w_in, fgate_b=m_fgate_b, q_norm_w=m_q_norm_w, w_uq=m_w_uq, kv_norm_w=m_kv_norm_w,
                w_ukv=m_w_ukv, odd_w_out=m_odd_w_out, ffn_w_gate=m_ffn_w_gate, ffn_w_up=m_ffn_w_up,
                ffn_w_down=m_ffn_w_down, ln_mix_g=m_ln_mix_g, ln_mix_b=m_ln_mix_b, ln_ffn_g=m_ln_ffn_g,
                ln_ffn_b=m_ln_ffn_b)
    v_in = dict(even_w_in=v_even_w_in, pool_w=v_pool_w, pool_scale=v_pool_scale, conv_w=v_conv_w, conv_b=v_conv_b,
                dt_bias=v_dt_bias, a_log=v_a_log, d_skip=v_d_skip, ssm_norm_w=v_ssm_norm_w, even_w_out=v_even_w_out,
                odd_w_in=v_odd_w_in, fgate_b=v_fgate_b, q_norm_w=v_q_norm_w, w_uq=v_w_uq, kv_norm_w=v_kv_norm_w,
                w_ukv=v_w_ukv, odd_w_out=v_odd_w_out, ffn_w_gate=v_ffn_w_gate, ffn_w_up=v_ffn_w_up,
                ffn_w_down=v_ffn_w_down, ln_mix_g=v_ln_mix_g, ln_mix_b=v_ln_mix_b, ln_ffn_g=v_ln_ffn_g,
                ln_ffn_b=v_ln_ffn_b)
    shards = {n: weights[n] for n, _ in SHARDED}

    gathered = _chip_allgather(_pack_weight_shards(shards), "weights_allgather")
    w = _prepare_weights(_unpack_gathered(gathered, shards))
    small = {n: weights[n] for n in REPLICATED}
    w.update(small)

    loss_local, dx, layer_grads = _local_step(x[0], loss_target[0], w, small)
    grads_full = _stack_grads(layer_grads)

    reduced = _reduce_scatter(_pack_grads(grads_full, shards))
    grads = _unpack_reduced(reduced, shards)
    small_sum = _all_reduce_small(_flatten_small(grads_full), "small_grads_allreduce")
    grads.update(_unflatten_small(small_sum, small))
    loss = lax.psum(loss_local, ("x", "y", "c"))

    deltas, new_m, new_v = {}, {}, {}
    for n in WEIGHT_NAMES:
        deltas[n], new_m[n], new_v[n] = _adamw(weights[n], grads[n], m_in[n], v_in[n], "adamw_" + n)
    return (loss, dx[None], *[grads[n] for n in WEIGHT_NAMES], *[deltas[n] for n in WEIGHT_NAMES],
            *[new_m[n] for n in WEIGHT_NAMES], *[new_v[n] for n in WEIGHT_NAMES])
```

```python
import functools
import math

import numpy as np
import jax
import jax.numpy as jnp
from jax import lax
from jax.experimental import pallas as pl
from jax.experimental.pallas import tpu as pltpu

F32 = jnp.float32
BF16 = jnp.bfloat16
HI = lax.Precision.HIGHEST
MESH_ID = pl.DeviceIdType.MESH

VMEM_LIMIT_BYTES = 56 * 1024 * 1024
LANES = 128

D_MODEL = 1024
DEPTH = 4
POOL_WINDOWS = (2, 4, 8, 16)
POOL_GROUP = 128
POOL_WIDTH = 512
SSM_D_INNER = 1024
SSM_HEAD_DIM = 64
SSM_HEADS = 16
SSM_GROUPS = 2
SSM_STATE = 128
SSM_CONV = 4
SSM_CHUNK = 128
SSM_CONV_DIM = 1536
EVEN_IN = 3088
EVEN_IN_PAD = 3200
FOX_HEADS = 8
FOX_WIDTH = 512
MLA_HEADS = 8
MLA_NOPE = 64
MLA_ROPE = 32
MLA_V = 64
MLA_Q_RANK = 512
MLA_KV_RANK = 256
ROPE_THETA = 10000.0
ODD_IN = 2344
ODD_IN_PAD = 2560
D_FF = 2816
ALPHA = (2 * DEPTH) ** 0.25
LN_EPS = 1e-5
RMS_EPS = 1e-6
ADAM_LR = 0.001
ADAM_B1 = 0.9
ADAM_B2 = 0.999
ADAM_EPS = 1e-08
ADAM_WD = 0.01
ADAM_STEP = 10
NEG_BIG = -1e30

ATTN_TILE = 512
ATTN_WIDE = 2048
FOX_SCALE = 0.125
MLA_SCALE = (MLA_NOPE + MLA_ROPE) ** -0.5
ROW_TILE = 256


def _cparams(sem=None):
    return pltpu.CompilerParams(dimension_semantics=sem, vmem_limit_bytes=VMEM_LIMIT_BYTES)


def _pick(d, prefs):
    for p in prefs:
        if d % p == 0:
            return p
    return d


_M_PREFS = (2048, 1024, 512, 640, 1408, 768, 384, 256, 128)
_N_PREFS = (1024, 512, 640, 1408, 768, 384, 256, 128)
_K_PREFS = (1024, 512, 640, 1408, 768, 256, 128)
MM_MAX_ACC_ELEMS = 1408 * 1024


def _mm(a, b, mode, out_dtype, name, extra=None, alpha=1.0):
    if mode == "nn":
        (m, k), (k2, n) = a.shape, b.shape
    elif mode == "nt":
        (m, k), (n, k2) = a.shape, b.shape
    else:
        (k, m), (k2, n) = a.shape, b.shape
    assert k == k2, (a.shape, b.shape, mode)
    tn, tk = _pick(n, _N_PREFS), _pick(k, _K_PREFS)
    tm = _pick(m, tuple(p for p in _M_PREFS if p * tn <= MM_MAX_ACC_ELEMS))
    nk = k // tk
    if mode == "nn":
        a_spec = pl.BlockSpec((tm, tk), lambda i, j, kk: (i, kk))
        b_spec = pl.BlockSpec((tk, tn), lambda i, j, kk: (kk, j))
        dims = (((1,), (0,)), ((), ()))
    elif mode == "nt":
        a_spec = pl.BlockSpec((tm, tk), lambda i, j, kk: (i, kk))
        b_spec = pl.BlockSpec((tn, tk), lambda i, j, kk: (j, kk))
        dims = (((1,), (1,)), ((), ()))
    else:
        a_spec = pl.BlockSpec((tk, tm), lambda i, j, kk: (kk, i))
        b_spec = pl.BlockSpec((tk, tn), lambda i, j, kk: (kk, j))
        dims = (((0,), (0,)), ((), ()))
    o_spec = pl.BlockSpec((tm, tn), lambda i, j, kk: (i, j))
    has_extra = extra is not None

    def body(*refs):
        if has_extra:
            a_ref, b_ref, e_ref, o_ref, acc = refs
        else:
            a_ref, b_ref, o_ref, acc = refs
        kk = pl.program_id(2)

        @pl.when(kk == 0)
        def _():
            acc[...] = jnp.zeros_like(acc)

        acc[...] += lax.dot_general(a_ref[...].astype(BF16), b_ref[...].astype(BF16), dims,
                                    preferred_element_type=F32)

        @pl.when(kk == nk - 1)
        def _():
            r = acc[...]
            if has_extra:
                r = r + alpha * e_ref[...].astype(F32)
            o_ref[...] = r.astype(o_ref.dtype)

    ins = [a, b] + ([extra] if has_extra else [])
    specs = [a_spec, b_spec] + ([o_spec] if has_extra else [])
    return pl.pallas_call(
        body, name=name, grid=(m // tm, n // tn, nk), in_specs=specs, out_specs=o_spec,
        out_shape=jax.ShapeDtypeStruct((m, n), out_dtype),
        scratch_shapes=[pltpu.VMEM((tm, tn), F32)],
        compiler_params=_cparams(("parallel", "parallel", "arbitrary")),
    )(*ins)


def _full_spec(shape):
    nd = len(shape)
    return pl.BlockSpec(tuple(shape), lambda i, _nd=nd: (0,) * _nd)


def _scan_fwd(f, name, n, tiles, ctiles, params, cparams, carry_shapes, out_defs):
    rows = tiles[0].shape[0]
    tt = rows // n
    nt, nct, npar, ncp, ncar, nout = len(tiles), len(ctiles), len(params), len(cparams), len(carry_shapes), len(out_defs)

    def body(*refs):
        pos = 0
        t_refs = refs[pos:pos + nt]; pos += nt
        ct_refs = refs[pos:pos + nct]; pos += nct
        p_refs = refs[pos:pos + npar]; pos += npar
        cp_refs = refs[pos:pos + ncp]; pos += ncp
        o_refs = refs[pos:pos + nout]; pos += nout
        cs_refs = refs[pos:pos + ncar]; pos += ncar
        c_scr = refs[pos:pos + ncar]
        i = pl.program_id(0)

        @pl.when(i == 0)
        def _():
            for c in c_scr:
                c[...] = jnp.zeros_like(c)

        carry = tuple(c[...] for c in c_scr)
        for s, c in zip(cs_refs, carry):
            s[0] = c
        new_carry, outs = f(carry, tuple(r[...] for r in t_refs), tuple(r[...] for r in ct_refs),
                            tuple(r[...] for r in p_refs), tuple(r[...] for r in cp_refs), i)
        for o_ref, o in zip(o_refs, outs):
            o_ref[...] = o
        for c, v in zip(c_scr, new_carry):
            c[...] = v

    tile_spec = lambda arr: pl.BlockSpec((tt, arr.shape[1]), lambda i: (i, 0))
    in_specs = ([tile_spec(t) for t in tiles] + [tile_spec(t) for t in ctiles]
                + [_full_spec(p.shape) for p in params] + [_full_spec(p.shape) for p in cparams])
    out_specs = ([pl.BlockSpec((tt, c), lambda i: (i, 0)) for c, _ in out_defs]
                 + [pl.BlockSpec((1,) + tuple(s), lambda i: (i, 0, 0)) for s in carry_shapes])
    out_shape = ([jax.ShapeDtypeStruct((rows, c), dt) for c, dt in out_defs]
                 + [jax.ShapeDtypeStruct((n,) + tuple(s), F32) for s in carry_shapes])
    res = pl.pallas_call(
        body, name=name, grid=(n,), in_specs=in_specs, out_specs=out_specs, out_shape=out_shape,
        scratch_shapes=[pltpu.VMEM(tuple(s), F32) for s in carry_shapes],
        compiler_params=_cparams(("arbitrary",)),
    )(*tiles, *ctiles, *params, *cparams)
    return list(res[:nout]), list(res[nout:])


def _scan_bwd(f, name, n, tiles, ctiles, params, cparams, carries, douts):
    rows = tiles[0].shape[0]
    tt = rows // n
    nt, nct, npar, ncp, ncar, nout = len(tiles), len(ctiles), len(params), len(cparams), len(carries), len(douts)

    def body(*refs):
        pos = 0
        t_refs = refs[pos:pos + nt]; pos += nt
        ct_refs = refs[pos:pos + nct]; pos += nct
        p_refs = refs[pos:pos + npar]; pos += npar
        cp_refs = refs[pos:pos + ncp]; pos += ncp
        cs_refs = refs[pos:pos + ncar]; pos += ncar
        do_refs = refs[pos:pos + nout]; pos += nout
        dt_refs = refs[pos:pos + nt]; pos += nt
        dp_refs = refs[pos:pos + npar]; pos += npar
        dc_scr = refs[pos:pos + ncar]
        i = pl.program_id(0)

        @pl.when(i == 0)
        def _():
            for c in dc_scr:
                c[...] = jnp.zeros_like(c)
            for d in dp_refs:
                d[...] = jnp.zeros_like(d)

        ctv = tuple(r[...] for r in ct_refs)
        cpv = tuple(r[...] for r in cp_refs)

        def g(c, t, p):
            return f(c, t, ctv, p, cpv, n - 1 - i)

        _, vjp = jax.vjp(g, tuple(s[0] for s in cs_refs), tuple(r[...] for r in t_refs),
                         tuple(r[...] for r in p_refs))
        dc, dt, dp = vjp((tuple(c[...] for c in dc_scr), tuple(r[...] for r in do_refs)))
        for r, v in zip(dt_refs, dt):
            r[...] = v
        for r, v in zip(dp_refs, dp):
            r[...] += v
        for c, v in zip(dc_scr, dc):
            c[...] = v

    rev_tile = lambda arr: pl.BlockSpec((tt, arr.shape[1]), lambda i: (n - 1 - i, 0))
    in_specs = ([rev_tile(t) for t in tiles] + [rev_tile(t) for t in ctiles]
                + [_full_spec(p.shape) for p in params] + [_full_spec(p.shape) for p in cparams]
                + [pl.BlockSpec((1,) + tuple(c.shape[1:]), lambda i: (n - 1 - i, 0, 0)) for c in carries]
                + [rev_tile(d) for d in douts])
    out_specs = [rev_tile(t) for t in tiles] + [_full_spec(p.shape) for p in params]
    out_shape = ([jax.ShapeDtypeStruct(t.shape, t.dtype) for t in tiles]
                 + [jax.ShapeDtypeStruct(p.shape, F32) for p in params])
    res = pl.pallas_call(
        body, name=name, grid=(n,), in_specs=in_specs, out_specs=out_specs, out_shape=out_shape,
        scratch_shapes=[pltpu.VMEM(tuple(c.shape[1:]), F32) for c in carries],
        compiler_params=_cparams(("arbitrary",)),
    )(*tiles, *ctiles, *params, *cparams, *carries, *douts)
    return list(res[:nt]), list(res[nt:])


def _silu(x):
    return x * jax.nn.sigmoid(x)


def _softplus(x):
    return jnp.maximum(x, 0.0) + jnp.log(1.0 + jnp.exp(-jnp.abs(x)))


def _f_ln(carry, tiles, ctiles, params, cparams, idx):
    (r,), (g, b) = tiles, params
    mu = jnp.mean(r, axis=-1, keepdims=True)
    xc = r - mu
    var = jnp.mean(xc * xc, axis=-1, keepdims=True)
    return (), (xc * lax.rsqrt(var + LN_EPS) * g + b,)


def _f_act(carry, tiles, ctiles, params, cparams, idx):
    (gu,) = tiles
    g = gu[:, :D_FF].astype(F32)
    u = gu[:, D_FF:].astype(F32)
    return (), ((_silu(g) * u).astype(BF16),)


def _f_pool(carry, tiles, ctiles, params, cparams, idx):
    (prev,), (u,), (pool_w, pool_scale) = carry, tiles, params
    tt = u.shape[0]
    halo = prev.shape[0]
    ext = jnp.concatenate([prev, u], axis=0)
    pos = idx * tt + lax.broadcasted_iota(jnp.int32, (tt, 1), 0)
    ys = []
    for g, w in enumerate(POOL_WINDOWS):
        lo, hi = g * POOL_GROUP, (g + 1) * POOL_GROUP
        eg = ext[:, lo:hi]
        s = eg[halo:halo + tt]
        for j in range(1, w):
            s = s + eg[halo - j:halo - j + tt]
        count = jnp.minimum(pos + 1, w).astype(F32)
        diff = s / count - u[:, lo:hi]
        ys.append(jnp.dot(diff.astype(BF16), pool_w[g].astype(BF16), preferred_element_type=F32))
    y = jnp.concatenate(ys, axis=1) * pool_scale
    return (u[tt - halo:, :],), (y.astype(BF16),)


def _f_conv(carry, tiles, ctiles, params, cparams, idx):
    (prev,), (xbc,), (conv_w, conv_b) = carry, tiles, params
    tt = xbc.shape[0]
    halo = prev.shape[0]
    ext = jnp.concatenate([prev, xbc], axis=0)
    acc = jnp.zeros_like(xbc) + conv_b
    for k in range(SSM_CONV):
        off = halo - (SSM_CONV - 1) + k
        acc = acc + ext[off:off + tt] * conv_w[k:k + 1, :]
    return (xbc[tt - halo:, :],), (_silu(acc),)


def _f_ssd(carry, tiles, ctiles, params, cparams, idx):
    (state,), (xa, dtr, z), (dt_bias, a_log, d_skip, norm_w) = carry, tiles, params
    L, P, N, E = SSM_CHUNK, SSM_HEAD_DIM, SSM_STATE, SSM_HEADS // SSM_GROUPS
    dt = _softplus(dtr + dt_bias)
    da = dt * (-jnp.exp(a_log))
    r = lax.broadcasted_iota(jnp.int32, (L, L), 0)
    c = lax.broadcasted_iota(jnp.int32, (L, L), 1)
    tri = c <= r
    acs = jnp.dot(tri.astype(F32), da, precision=HI, preferred_element_type=F32)
    acs_t = acs.T
    ys, new_states = [], []
    for g in range(SSM_GROUPS):
        bg = xa[:, SSM_D_INNER + g * N:SSM_D_INNER + (g + 1) * N].astype(BF16)
        cg = xa[:, SSM_D_INNER + (SSM_GROUPS + g) * N:SSM_D_INNER + (SSM_GROUPS + g + 1) * N].astype(BF16)
        cb = lax.dot_general(cg, bg, (((1,), (1,)), ((), ())), preferred_element_type=F32)
        for e in range(E):
            h = g * E + e
            xh = xa[:, h * P:(h + 1) * P]
            col = acs[:, h:h + 1]
            row = acs_t[h:h + 1, :]
            last = acs[L - 1:L, h:h + 1]
            lmat = jnp.exp(jnp.where(tri, col - row, NEG_BIG))
            xdt_f = xh * dt[:, h:h + 1]
            y_diag = jnp.dot((cb * lmat).astype(BF16), xdt_f.astype(BF16), preferred_element_type=F32)
            xdec = (xdt_f * jnp.exp(last - col)).astype(BF16)
            st = lax.dot_general(xdec, bg, (((0,), (0,)), ((), ())), preferred_element_type=F32)
            prev = state[h * P:(h + 1) * P, :]
            y_off = lax.dot_general(cg, prev.astype(BF16), (((1,), (1,)), ((), ())),
                                    preferred_element_type=F32) * jnp.exp(col)
            new_states.append(prev * jnp.exp(last) + st)
            ys.append(y_diag + y_off + xh * d_skip[:, h:h + 1])
    y = jnp.concatenate(ys, axis=1) * _silu(z)
    y = y * lax.rsqrt(jnp.mean(y * y, axis=-1, keepdims=True) + RMS_EPS) * norm_w
    return (jnp.concatenate(new_states, axis=0),), (y.astype(BF16),)


def _f_fox_gate(carry, tiles, ctiles, params, cparams, idx):
    (run,), (fl,), (fb,) = carry, tiles, params
    tt = fl.shape[0]
    logf = -_softplus(-(fl + fb))
    r = lax.broadcasted_iota(jnp.int32, (tt, tt), 0)
    c = lax.broadcasted_iota(jnp.int32, (tt, tt), 1)
    cum = jnp.dot((c <= r).astype(F32), logf, precision=HI, preferred_element_type=F32) + run[0:1, :]
    return (jnp.broadcast_to(cum[tt - 1:tt, :], run.shape),), (cum,)


def _rms(x, w):
    return x * lax.rsqrt(jnp.mean(x * x, axis=-1, keepdims=True) + RMS_EPS) * w


def _f_mla_prep(carry, tiles, ctiles, params, cparams, idx):
    (cq, ckv, kr), (ck, sk), (qw, kvw), (rk,) = tiles, ctiles, params, cparams
    rot = jnp.dot(kr, rk, precision=HI, preferred_element_type=F32)
    return (), (_rms(cq, qw).astype(BF16), _rms(ckv, kvw).astype(BF16), (kr * ck + rot * sk).astype(BF16))


def _f_rope_q(carry, tiles, ctiles, params, cparams, idx):
    (q,), (cos, sin) = tiles, ctiles
    nope = MLA_HEADS * MLA_NOPE
    x1 = q[:, nope:nope + LANES]
    x2 = q[:, nope + LANES:]
    roped = jnp.concatenate([q[:, :nope], x1 * cos - x2 * sin, x2 * cos + x1 * sin], axis=1)
    return (), ((roped * MLA_SCALE).astype(BF16),)


def _scores(q, k, fq, fk, masked, qi, ki, tq, tk):
    s = lax.dot_general(q, k, (((1,), (1,)), ((), ())), preferred_element_type=F32)
    if fq is not None:
        s = s + fq - fk
    if masked:
        rows = qi * tq + lax.broadcasted_iota(jnp.int32, (tq, tk), 0)
        cols = ki * tk + lax.broadcasted_iota(jnp.int32, (tq, tk), 1)
        s = jnp.where(cols <= rows, s, NEG_BIG)
    return s


def _q_major_tables(T, tq, tk):
    r = tk // tq
    qi = np.concatenate([np.full(i // r + 1, i, np.int32) for i in range(T // tq)])
    ki = np.concatenate([np.arange(i // r + 1, dtype=np.int32) for i in range(T // tq)])
    return [jnp.asarray(a) for a in (qi, ki, (ki == 0).astype(np.int32), (ki == qi // r).astype(np.int32))]


def _k_major_tables(T, tq, tk):
    r = tq // tk
    nq = T // tq
    ki = np.concatenate([np.full(nq - i // r, i, np.int32) for i in range(T // tk)])
    qi = np.concatenate([np.arange(i // r, nq, dtype=np.int32) for i in range(T // tk)])
    return [jnp.asarray(a) for a in (ki, qi, (qi == ki // r).astype(np.int32), (qi == nq - 1).astype(np.int32))]


def _attn_tiles(T):
    return min(ATTN_TILE, T), min(ATTN_WIDE, T)


def _attn_fwd(q, k, v, fq, fk, name):
    H, T, dk = q.shape
    dv = v.shape[2]
    tq, tk = _attn_tiles(T)
    decay = fq is not None
    tables = _q_major_tables(T, tq, tk)

    def body(qi_ref, ki_ref, first_ref, last_ref, *refs):
        if decay:
            q_ref, k_ref, v_ref, fq_ref, fk_ref, o_ref, lse_ref, m_s, l_s, acc = refs
        else:
            q_ref, k_ref, v_ref, o_ref, lse_ref, m_s, l_s, acc = refs
        t = pl.program_id(1)
        qi, ki = qi_ref[t], ki_ref[t]

        @pl.when(first_ref[t] == 1)
        def _():
            m_s[...] = jnp.full_like(m_s, NEG_BIG)
            l_s[...] = jnp.zeros_like(l_s)
            acc[...] = jnp.zeros_like(acc)

        def step(masked):
            s = _scores(q_ref[0], k_ref[0], fq_ref[0] if decay else None, fk_ref[0] if decay else None,
                        masked, qi, ki, tq, tk)
            m_new = jnp.maximum(m_s[...], jnp.max(s, axis=-1, keepdims=True))
            p = jnp.exp(s - m_new)
            corr = jnp.exp(m_s[...] - m_new)
            l_s[...] = corr * l_s[...] + jnp.sum(p, axis=-1, keepdims=True)
            acc[...] = corr * acc[...] + jnp.dot(p.astype(BF16), v_ref[0], preferred_element_type=F32)
            m_s[...] = m_new

        @pl.when(last_ref[t] == 0)
        def _():
            step(False)

        @pl.when(last_ref[t] == 1)
        def _():
            step(True)
            o_ref[0] = (acc[...] / l_s[...]).astype(o_ref.dtype)
            lse_ref[0] = m_s[...] + jnp.log(l_s[...])

    qspec = lambda d: pl.BlockSpec((1, tq, d), lambda h, t, qi, ki, fi, la: (h, qi[t], 0))
    kspec = lambda d: pl.BlockSpec((1, tk, d), lambda h, t, qi, ki, fi, la: (h, ki[t], 0))
    in_specs = [qspec(dk), kspec(dk), kspec(dv)]
    ins = [q, k, v]
    if decay:
        in_specs += [qspec(1), pl.BlockSpec((1, 1, tk), lambda h, t, qi, ki, fi, la: (h, 0, ki[t]))]
        ins += [fq, fk]
    grid_spec = pltpu.PrefetchScalarGridSpec(
        num_scalar_prefetch=4, grid=(H, int(tables[0].shape[0])), in_specs=in_specs,
        out_specs=[qspec(dv), qspec(1)],
        scratch_shapes=[pltpu.VMEM((tq, 1), F32), pltpu.VMEM((tq, 1), F32), pltpu.VMEM((tq, dv), F32)])
    return pl.pallas_call(
        body, name=name, grid_spec=grid_spec,
        out_shape=[jax.ShapeDtypeStruct((H, T, dv), BF16), jax.ShapeDtypeStruct((H, T, 1), F32)],
        compiler_params=_cparams(("parallel", "arbitrary")),
    )(*tables, *ins)


def _attn_bwd_dq(q, k, v, o, do, lse, fq, fk, name):
    H, T, dk = q.shape
    dv = v.shape[2]
    tq, tk = _attn_tiles(T)
    decay = fq is not None
    tables = _q_major_tables(T, tq, tk)

    def body(qi_ref, ki_ref, first_ref, last_ref, *refs):
        if decay:
            q_ref, k_ref, v_ref, o_ref, do_ref, lse_ref, fq_ref, fk_ref, dq_ref, dl_ref, acc, dl, leak = refs
        else:
            q_ref, k_ref, v_ref, o_ref, do_ref, lse_ref, dq_ref, dl_ref, acc, dl, leak = refs
        t = pl.program_id(1)
        qi, ki = qi_ref[t], ki_ref[t]

        @pl.when(first_ref[t] == 1)
        def _():
            acc[...] = jnp.zeros_like(acc)
            leak[...] = jnp.zeros_like(leak)
            dl[...] = jnp.sum(do_ref[0].astype(F32) * o_ref[0].astype(F32), axis=-1, keepdims=True)

        def step(masked):
            s = _scores(q_ref[0], k_ref[0], fq_ref[0] if decay else None, fk_ref[0] if decay else None,
                        masked, qi, ki, tq, tk)
            p = jnp.exp(s - lse_ref[0])
            dp = lax.dot_general(do_ref[0], v_ref[0], (((1,), (1,)), ((), ())), preferred_element_type=F32)
            ds = p * (dp - dl[...])
            leak[...] += jnp.sum(ds, axis=-1, keepdims=True)
            acc[...] += jnp.dot(ds.astype(BF16), k_ref[0], preferred_element_type=F32)

        @pl.when(last_ref[t] == 0)
        def _():
            step(False)

        @pl.when(last_ref[t] == 1)
        def _():
            step(True)
            dq_ref[0] = acc[...].astype(dq_ref.dtype)
            dl_ref[0] = dl[...] + leak[...]

    qspec = lambda d: pl.BlockSpec((1, tq, d), lambda h, t, qi, ki, fi, la: (h, qi[t], 0))
    kspec = lambda d: pl.BlockSpec((1, tk, d), lambda h, t, qi, ki, fi, la: (h, ki[t], 0))
    in_specs = [qspec(dk), kspec(dk), kspec(dv), qspec(dv), qspec(dv), qspec(1)]
    ins = [q, k, v, o, do, lse]
    if decay:
        in_specs += [qspec(1), pl.BlockSpec((1, 1, tk), lambda h, t, qi, ki, fi, la: (h, 0, ki[t]))]
        ins += [fq, fk]
    grid_spec = pltpu.PrefetchScalarGridSpec(
        num_scalar_prefetch=4, grid=(H, int(tables[0].shape[0])), in_specs=in_specs,
        out_specs=[qspec(dk), qspec(1)],
        scratch_shapes=[pltpu.VMEM((tq, dk), F32), pltpu.VMEM((tq, 1), F32), pltpu.VMEM((tq, 1), F32)])
    return pl.pallas_call(
        body, name=name, grid_spec=grid_spec,
        out_shape=[jax.ShapeDtypeStruct((H, T, dk), BF16), jax.ShapeDtypeStruct((H, T, 1), F32)],
        compiler_params=_cparams(("parallel", "arbitrary")),
    )(*tables, *ins)


def _attn_bwd_dkv(q, k, v, delta, do, lse, fq, fk, name):
    H, T, dk = q.shape
    dv = v.shape[2]
    tk, tq = _attn_tiles(T)
    decay = fq is not None
    tables = _k_major_tables(T, tq, tk)

    def body(ki_ref, qi_ref, first_ref, last_ref, *refs):
        if decay:
            q_ref, k_ref, v_ref, dl_ref, do_ref, lse_ref, fq_ref, fk_ref, dk_ref, dv_ref, df_ref, dk_s, dv_s, df_s = refs
        else:
            q_ref, k_ref, v_ref, dl_ref, do_ref, lse_ref, dk_ref, dv_ref, dk_s, dv_s = refs
        t = pl.program_id(1)
        ki, qi = ki_ref[t], qi_ref[t]

        @pl.when(first_ref[t] == 1)
        def _():
            dk_s[...] = jnp.zeros_like(dk_s)
            dv_s[...] = jnp.zeros_like(dv_s)
            if decay:
                df_s[...] = jnp.zeros_like(df_s)

        def step(masked):
            s = _scores(q_ref[0], k_ref[0], fq_ref[0] if decay else None, fk_ref[0] if decay else None,
                        masked, qi, ki, tq, tk)
            p = jnp.exp(s - lse_ref[0])
            do_v = do_ref[0]
            delta = dl_ref[0]
            dv_s[...] += lax.dot_general(p.astype(BF16), do_v, (((0,), (0,)), ((), ())), preferred_element_type=F32)
            dp = lax.dot_general(do_v, v_ref[0], (((1,), (1,)), ((), ())), preferred_element_type=F32)
            ds = p * (dp - delta)
            dk_s[...] += lax.dot_general(ds.astype(BF16), q_ref[0], (((0,), (0,)), ((), ())),
                                         preferred_element_type=F32)
            if decay:
                df_s[...] -= jnp.sum(ds, axis=0, keepdims=True)

        @pl.when(first_ref[t] == 1)
        def _():
            step(True)

        @pl.when(first_ref[t] == 0)
        def _():
            step(False)

        @pl.when(last_ref[t] == 1)
        def _():
            dk_ref[0] = dk_s[...].astype(dk_ref.dtype)
            dv_ref[0] = dv_s[...].astype(dv_ref.dtype)
            if decay:
                df_ref[0] = df_s[...]

    qspec = lambda d: pl.BlockSpec((1, tq, d), lambda h, t, ki, qi, fi, la: (h, qi[t], 0))
    kspec = lambda d: pl.BlockSpec((1, tk, d), lambda h, t, ki, qi, fi, la: (h, ki[t], 0))
    in_specs = [qspec(dk), kspec(dk), kspec(dv), qspec(1), qspec(dv), qspec(1)]
    ins = [q, k, v, delta, do, lse]
    out_specs = [kspec(dk), kspec(dv)]
    out_shape = [jax.ShapeDtypeStruct((H, T, dk), BF16), jax.ShapeDtypeStruct((H, T, dv), BF16)]
    scratch = [pltpu.VMEM((tk, dk), F32), pltpu.VMEM((tk, dv), F32)]
    if decay:
        fkspec = pl.BlockSpec((1, 1, tk), lambda h, t, ki, qi, fi, la: (h, 0, ki[t]))
        in_specs += [qspec(1), fkspec]
        ins += [fq, fk]
        out_specs.append(fkspec)
        out_shape.append(jax.ShapeDtypeStruct((H, 1, T), F32))
        scratch.append(pltpu.VMEM((1, tk), F32))
    grid_spec = pltpu.PrefetchScalarGridSpec(
        num_scalar_prefetch=4, grid=(H, int(tables[0].shape[0])), in_specs=in_specs, out_specs=out_specs,
        scratch_shapes=scratch)
    return pl.pallas_call(
        body, name=name, grid_spec=grid_spec, out_shape=out_shape, compiler_params=_cparams(("parallel", "arbitrary")),
    )(*tables, *ins)


def _loss_head(y, target, name):
    rows, d = y.shape
    tt = _pick(rows, (512, 256, 128))

    def body(y_ref, t_ref, dy_ref, part_ref):
        @pl.when(pl.program_id(0) == 0)
        def _():
            part_ref[...] = jnp.zeros_like(part_ref)

        err = y_ref[...] - t_ref[...]
        dy_ref[...] = err * (1.0 / d)
        sq = jnp.sum(err * err, axis=0, keepdims=True)
        folded = sq[:, :LANES]
        for j in range(1, d // LANES):
            folded = folded + sq[:, j * LANES:(j + 1) * LANES]
        part_ref[...] += folded

    spec = pl.BlockSpec((tt, d), lambda i: (i, 0))
    return pl.pallas_call(
        body, name=name, grid=(rows // tt,), in_specs=[spec, spec],
        out_specs=[spec, pl.BlockSpec((1, LANES), lambda i: (0, 0))],
        out_shape=[jax.ShapeDtypeStruct((rows, d), F32), jax.ShapeDtypeStruct((1, LANES), F32)],
        compiler_params=_cparams(("arbitrary",)),
    )(y, target)


def _adamw(w, g, m, v, name):
    shape = w.shape
    cols = shape[-1]
    rows = math.prod(shape[:-1])
    w2, g2, m2, v2 = (a.reshape(rows, cols) for a in (w, g, m, v))
    tr = rows
    for cand in (512, 256, 128, 64, 32, 16, 8):
        if rows % cand == 0:
            tr = cand
            break
    c1 = 1.0 / (1.0 - ADAM_B1 ** ADAM_STEP)
    c2 = 1.0 / (1.0 - ADAM_B2 ** ADAM_STEP)

    def body(w_ref, g_ref, m_ref, v_ref, d_ref, nm_ref, nv_ref):
        gv = g_ref[...]
        nm = ADAM_B1 * m_ref[...] + (1.0 - ADAM_B1) * gv
        nv = ADAM_B2 * v_ref[...] + (1.0 - ADAM_B2) * gv * gv
        d_ref[...] = -ADAM_LR * ((nm * c1) / (jnp.sqrt(nv * c2) + ADAM_EPS) + ADAM_WD * w_ref[...])
        nm_ref[...] = nm
        nv_ref[...] = nv

    spec = pl.BlockSpec((tr, cols), lambda i: (i, 0))
    sds = jax.ShapeDtypeStruct((rows, cols), F32)
    d, nm, nv = pl.pallas_call(
        body, name=name, grid=(rows // tr,), in_specs=[spec] * 4, out_specs=[spec] * 3, out_shape=[sds] * 3,
        compiler_params=_cparams(("parallel",)),
    )(w2, g2, m2, v2)
    return d.reshape(shape), nm.reshape(shape), nv.reshape(shape)


_ANY = pl.BlockSpec(memory_space=pl.ANY)


def _my_xyc():
    return lax.axis_index("x"), lax.axis_index("y"), lax.axis_index("c")


def _chip_allgather(shard, name):
    def body(x_ref, out_ref, send_sems, recv_sems, loc_sem):
        x, y, c = _my_xyc()
        myq = 2 * x + y
        local = pltpu.make_async_copy(x_ref, out_ref.at[myq], loc_sem)
        local.start()
        chips = [(1 - x, y), (x, 1 - y), (1 - x, 1 - y)]
        sends = []
        for j, (px, py) in enumerate(chips):
            cp = pltpu.make_async_remote_copy(src_ref=x_ref, dst_ref=out_ref.at[myq], send_sem=send_sems.at[j],
                                              recv_sem=recv_sems.at[j], device_id=(px, py, c), device_id_type=MESH_ID)
            cp.start()
            sends.append(cp)
        for j, (px, py) in enumerate(chips):
            pltpu.make_async_remote_copy(src_ref=x_ref, dst_ref=out_ref.at[2 * px + py], send_sem=send_sems.at[j],
                                         recv_sem=recv_sems.at[j], device_id=(px, py, c),
                                         device_id_type=MESH_ID).wait_recv()
        for cp in sends:
            cp.wait_send()
        local.wait()

    return pl.pallas_call(
        body, name=name, in_specs=[_ANY], out_specs=_ANY,
        out_shape=jax.ShapeDtypeStruct((4,) + shard.shape, shard.dtype),
        scratch_shapes=[pltpu.SemaphoreType.DMA((3,)), pltpu.SemaphoreType.DMA((3,)), pltpu.SemaphoreType.DMA],
    )(shard)


def _sibling_swap_halves(g, name):
    q, rows, cols = g.shape
    rh = rows // 2

    def body(g_ref, out_ref, send_sem, recv_sem):
        x, y, c = _my_xyc()
        src = g_ref.at[:, pl.ds((1 - c) * rh, rh), :]
        cp = pltpu.make_async_remote_copy(src_ref=src, dst_ref=out_ref, send_sem=send_sem, recv_sem=recv_sem,
                                          device_id=(x, y, 1 - c), device_id_type=MESH_ID)
        cp.start()
        cp.wait()

    return pl.pallas_call(
        body, name=name, in_specs=[_ANY], out_specs=_ANY,
        out_shape=jax.ShapeDtypeStruct((q, rh, cols), g.dtype),
        scratch_shapes=[pltpu.SemaphoreType.DMA, pltpu.SemaphoreType.DMA],
    )(g)


def _add_own_half(g, other, name):
    q, rows, cols = g.shape
    rh = rows // 2
    tr = _pick(rh, (512, 256, 128, 64, 32, 16, 8))
    nb = rh // tr
    cidx = lax.axis_index("c").astype(jnp.int32).reshape(1)

    def body(c_ref, g_ref, o_ref, out_ref):
        out_ref[...] = g_ref[...] + o_ref[...]

    grid_spec = pltpu.PrefetchScalarGridSpec(
        num_scalar_prefetch=1, grid=(q, nb),
        in_specs=[pl.BlockSpec((1, tr, cols), lambda a, i, c_ref: (a, c_ref[0] * nb + i, 0)),
                  pl.BlockSpec((1, tr, cols), lambda a, i, c_ref: (a, i, 0))],
        out_specs=pl.BlockSpec((1, tr, cols), lambda a, i, c_ref: (a, i, 0)))
    return pl.pallas_call(
        body, name=name, grid_spec=grid_spec, out_shape=jax.ShapeDtypeStruct((q, rh, cols), F32),
        compiler_params=_cparams(("parallel", "parallel")),
    )(cidx, g, other)


def _chip_exchange(s, name):
    def body(s_ref, out_ref, send_sems, recv_sems, loc_sem):
        x, y, c = _my_xyc()
        myq = 2 * x + y
        local = pltpu.make_async_copy(s_ref.at[myq], out_ref.at[myq], loc_sem)
        local.start()
        chips = [(1 - x, y), (x, 1 - y), (1 - x, 1 - y)]
        sends = []
        for j, (px, py) in enumerate(chips):
            cp = pltpu.make_async_remote_copy(src_ref=s_ref.at[2 * px + py], dst_ref=out_ref.at[myq],
                                              send_sem=send_sems.at[j], recv_sem=recv_sems.at[j],
                                              device_id=(px, py, c), device_id_type=MESH_ID)
            cp.start()
            sends.append(cp)
        for j, (px, py) in enumerate(chips):
            pltpu.make_async_remote_copy(src_ref=s_ref.at[myq], dst_ref=out_ref.at[2 * px + py],
                                         send_sem=send_sems.at[j], recv_sem=recv_sems.at[j],
                                         device_id=(px, py, c), device_id_type=MESH_ID).wait_recv()
        for cp in sends:
            cp.wait_send()
        local.wait()

    return pl.pallas_call(
        body, name=name, in_specs=[_ANY], out_specs=_ANY, out_shape=jax.ShapeDtypeStruct(s.shape, s.dtype),
        scratch_shapes=[pltpu.SemaphoreType.DMA((3,)), pltpu.SemaphoreType.DMA((3,)), pltpu.SemaphoreType.DMA],
    )(s)


def _sum_leading(a, name):
    q, rows, cols = a.shape
    tr = _pick(rows, (512, 256, 128, 64, 32, 16, 8))

    def body(a_ref, out_ref):
        acc = a_ref[0]
        for j in range(1, q):
            acc = acc + a_ref[j]
        out_ref[...] = acc

    return pl.pallas_call(
        body, name=name, grid=(rows // tr,), in_specs=[pl.BlockSpec((q, tr, cols), lambda i: (0, i, 0))],
        out_specs=pl.BlockSpec((tr, cols), lambda i: (i, 0)), out_shape=jax.ShapeDtypeStruct((rows, cols), F32),
        compiler_params=_cparams(("parallel",)),
    )(a)


def _sibling_join_halves(r, name):
    rh, cols = r.shape

    def body(r_ref, out_ref, send_sem, recv_sem, loc_sem):
        x, y, c = _my_xyc()
        local = pltpu.make_async_copy(r_ref, out_ref.at[pl.ds(c * rh, rh), :], loc_sem)
        local.start()
        cp = pltpu.make_async_remote_copy(src_ref=r_ref, dst_ref=out_ref.at[pl.ds(c * rh, rh), :], send_sem=send_sem,
                                          recv_sem=recv_sem, device_id=(x, y, 1 - c), device_id_type=MESH_ID)
        cp.start()
        pltpu.make_async_remote_copy(src_ref=r_ref, dst_ref=out_ref.at[pl.ds((1 - c) * rh, rh), :], send_sem=send_sem,
                                     recv_sem=recv_sem, device_id=(x, y, 1 - c), device_id_type=MESH_ID).wait_recv()
        cp.wait_send()
        local.wait()

    return pl.pallas_call(
        body, name=name, in_specs=[_ANY], out_specs=_ANY, out_shape=jax.ShapeDtypeStruct((2 * rh, cols), r.dtype),
        scratch_shapes=[pltpu.SemaphoreType.DMA, pltpu.SemaphoreType.DMA, pltpu.SemaphoreType.DMA],
    )(r)


def _all_reduce_small(vec, name):
    rows, cols = vec.shape

    def body(v_ref, out_ref, buf, send_sems, recv_sems):
        x, y, c = _my_xyc()
        me = 4 * x + 2 * y + c
        buf[me] = v_ref[...]
        sends = []
        for kk in range(1, 8):
            peer = (1 - x if kk & 4 else x, 1 - y if kk & 2 else y, 1 - c if kk & 1 else c)
            cp = pltpu.make_async_remote_copy(src_ref=v_ref, dst_ref=buf.at[me], send_sem=send_sems.at[kk - 1],
                                              recv_sem=recv_sems.at[kk - 1], device_id=peer, device_id_type=MESH_ID)
            cp.start()
            sends.append(cp)
        for kk in range(1, 8):
            px, py, pc = (1 - x if kk & 4 else x, 1 - y if kk & 2 else y, 1 - c if kk & 1 else c)
            pltpu.make_async_remote_copy(src_ref=v_ref, dst_ref=buf.at[4 * px + 2 * py + pc],
                                         send_sem=send_sems.at[kk - 1], recv_sem=recv_sems.at[kk - 1],
                                         device_id=(px, py, pc), device_id_type=MESH_ID).wait_recv()
        for cp in sends:
            cp.wait_send()
        acc = buf[0]
        for j in range(1, 8):
            acc = acc + buf[j]
        out_ref[...] = acc

    vm = pl.BlockSpec(memory_space=pltpu.VMEM)
    return pl.pallas_call(
        body, name=name, in_specs=[vm], out_specs=vm, out_shape=jax.ShapeDtypeStruct((rows, cols), F32),
        scratch_shapes=[pltpu.VMEM((8, rows, cols), F32), pltpu.SemaphoreType.DMA((7,)), pltpu.SemaphoreType.DMA((7,))],
        compiler_params=pltpu.CompilerParams(vmem_limit_bytes=VMEM_LIMIT_BYTES),
    )(vec)


def _reduce_scatter(g):
    other = _sibling_swap_halves(g, "rs_swap_halves")
    s = _add_own_half(g, other, "rs_add_halves")
    recv = _chip_exchange(s, "rs_chip_exchange")
    r = _sum_leading(recv, "rs_sum_chips")
    return _sibling_join_halves(r, "rs_join_halves")


PACK_COLS = 1024
SHARDED = (("even_w_in", 2), ("conv_w", 2), ("even_w_out", 1), ("odd_w_in", 2), ("q_norm_w", 1), ("w_uq", 2),
           ("kv_norm_w", 1), ("w_ukv", 2), ("odd_w_out", 1), ("ffn_w_gate", 2), ("ffn_w_up", 2), ("ffn_w_down", 1))
SPLIT_HI_LO = ("conv_w", "q_norm_w", "kv_norm_w")
REPLICATED = ("pool_w", "pool_scale", "conv_b", "dt_bias", "a_log", "d_skip", "ssm_norm_w", "fgate_b",
              "ln_mix_g", "ln_mix_b", "ln_ffn_g", "ln_ffn_b")


PACK_ROW_MULTIPLE = 1024


def _pad_rows(flat, row_multiple):
    n = flat.shape[0]
    per = PACK_COLS * row_multiple
    total = -(-n // per) * per
    return jnp.pad(flat, (0, total - n)).reshape(total // PACK_COLS, PACK_COLS)


def _pack_weight_shards(shards):
    parts = []
    for name, _ in SHARDED:
        w = shards[name].reshape(-1)
        if name in SPLIT_HI_LO:
            hi = w.astype(BF16)
            parts += [hi, (w - hi.astype(F32)).astype(BF16)]
        else:
            parts.append(w.astype(BF16))
    return _pad_rows(jnp.concatenate(parts), PACK_ROW_MULTIPLE)


def _unpack_gathered(gathered, shards):
    flat = gathered.reshape(4, -1)
    out, off = {}, 0
    for name, axis in SHARDED:
        shp = shards[name].shape
        size = math.prod(shp)
        if name in SPLIT_HI_LO:
            hi = flat[:, off:off + size].astype(F32)
            lo = flat[:, off + size:off + 2 * size].astype(F32)
            seg = hi + lo
            off += 2 * size
        else:
            seg = flat[:, off:off + size]
            off += size
        seg = seg.reshape((4,) + shp)
        out[name] = jnp.concatenate([seg[q] for q in range(4)], axis=axis)
    return out


def _pack_grads(grads, shards):
    blocks = []
    for q in range(4):
        parts = []
        for name, axis in SHARDED:
            width = shards[name].shape[axis]
            parts.append(lax.slice_in_dim(grads[name], q * width, (q + 1) * width, axis=axis).reshape(-1))
        blocks.append(_pad_rows(jnp.concatenate(parts), PACK_ROW_MULTIPLE))
    return jnp.stack(blocks)


def _unpack_reduced(reduced, shards):
    flat = reduced.reshape(-1)
    out, off = {}, 0
    for name, _ in SHARDED:
        shp = shards[name].shape
        size = math.prod(shp)
        out[name] = flat[off:off + size].reshape(shp)
        off += size
    return out


def _to_heads(t, nh):
    rows, width = t.shape
    return t.reshape(rows, nh, width // nh).transpose(1, 0, 2)


def _from_heads(t):
    nh, rows, d = t.shape
    return t.transpose(1, 0, 2).reshape(rows, nh * d)


def _rope_tables(rows):
    half = MLA_ROPE // 2
    freqs = jnp.power(ROPE_THETA, -jnp.arange(half, dtype=F32) / half)
    ang = jnp.arange(rows, dtype=F32)[:, None] * freqs[None, :]
    cos, sin = jnp.cos(ang), jnp.sin(ang)
    cos_q, sin_q = jnp.tile(cos, (1, MLA_HEADS)), jnp.tile(sin, (1, MLA_HEADS))
    zeros = jnp.zeros((rows, LANES - MLA_ROPE), F32)
    cos_k = jnp.concatenate([cos, cos, zeros], axis=1)
    sin_k = jnp.concatenate([sin, sin, zeros], axis=1)
    r = lax.broadcasted_iota(jnp.int32, (LANES, LANES), 0)
    c = lax.broadcasted_iota(jnp.int32, (LANES, LANES), 1)
    rot = (jnp.where((r == c + half) & (c < half), -1.0, 0.0)
           + jnp.where((c == r + half) & (r < half), 1.0, 0.0)).astype(F32)
    return cos_q, sin_q, cos_k, sin_k, rot


def _pad_cols(a, width):
    return jnp.pad(a, ((0, 0), (0, width - a.shape[1])))


def _prep_even_w_in(w):
    return _pad_cols(w, EVEN_IN_PAD)


_ODD_CUT = (1536, 1544, 2312, 2344)


def _prep_odd_w_in(w):
    c0, c1, c2, c3 = _ODD_CUT
    return jnp.concatenate([w[:, :c0], w[:, c1:c2], _pad_cols(w[:, c2:c3], LANES), _pad_cols(w[:, c0:c1], LANES)],
                           axis=1)


def _unprep_odd_w_in(g):
    c0, c1, c2, c3 = _ODD_CUT
    n1 = c0 + (c2 - c1)
    return jnp.concatenate([g[:, :c0], g[:, n1 + LANES:n1 + LANES + (c1 - c0)], g[:, c0:n1],
                            g[:, n1:n1 + (c3 - c2)]], axis=1)


def _prep_w_uq(w):
    r = w.reshape(w.shape[0], MLA_HEADS, MLA_NOPE + MLA_ROPE)
    half = MLA_ROPE // 2
    return jnp.concatenate([r[:, :, :MLA_NOPE].reshape(w.shape[0], -1),
                            r[:, :, MLA_NOPE:MLA_NOPE + half].reshape(w.shape[0], -1),
                            r[:, :, MLA_NOPE + half:].reshape(w.shape[0], -1)], axis=1)


def _unprep_w_uq(g):
    rows = g.shape[0]
    half = MLA_ROPE // 2
    nope = MLA_HEADS * MLA_NOPE
    return jnp.concatenate([g[:, :nope].reshape(rows, MLA_HEADS, MLA_NOPE),
                            g[:, nope:nope + LANES].reshape(rows, MLA_HEADS, half),
                            g[:, nope + LANES:].reshape(rows, MLA_HEADS, half)], axis=2).reshape(rows, -1)


def _row(v, width=None):
    v = v.reshape(1, -1)
    return v if width is None else _pad_cols(v, width)


def _even_forward(x_bf, w, i):
    rows = x_bf.shape[0]
    proj = _mm(x_bf, w["even_w_in"][i], "nn", F32, "even_proj")
    u, z, xbc, dtr = (proj[:, :512], proj[:, 512:1536], proj[:, 1536:3072], proj[:, 3072:3200])
    n_row = rows // ROW_TILE
    pool_p = (w["pool_w"][i], _row(w["pool_scale"][i]))
    (y_pool,), pool_c = _scan_fwd(_f_pool, "pool_fwd", n_row, [u], [], pool_p, [], [(16, POOL_WIDTH)],
                                  [(POOL_WIDTH, BF16)])
    conv_p = (w["conv_w"][i], _row(w["conv_b"][i]))
    (xa,), conv_c = _scan_fwd(_f_conv, "conv_fwd", n_row, [xbc], [], conv_p, [], [(8, SSM_CONV_DIM)],
                              [(SSM_CONV_DIM, F32)])
    ssd_p = (_row(w["dt_bias"][i], LANES), _row(w["a_log"][i], LANES), _row(w["d_skip"][i], LANES),
             _row(w["ssm_norm_w"][i]))
    (y_ssm,), ssd_c = _scan_fwd(_f_ssd, "ssd_fwd", rows // SSM_CHUNK, [xa, dtr, z], [], ssd_p, [],
                                [(SSM_D_INNER, SSM_STATE)], [(SSM_D_INNER, BF16)])
    mix = jnp.concatenate([y_pool, y_ssm], axis=1)
    saved = dict(u=u, z=z, xbc=xbc, dtr=dtr, xa=xa, mix=mix, pool_p=pool_p, pool_c=pool_c, conv_p=conv_p,
                 conv_c=conv_c, ssd_p=ssd_p, ssd_c=ssd_c)
    return mix, saved


def _even_backward(dmix, x_bf, sv, w, i, d_r1):
    rows = x_bf.shape[0]
    n_row = rows // ROW_TILE
    dy_pool, dy_ssm = dmix[:, :POOL_WIDTH], dmix[:, POOL_WIDTH:]
    (du,), (g_pool_w, g_pool_scale) = _scan_bwd(_f_pool, "pool_bwd", n_row, [sv["u"]], [], sv["pool_p"], [],
                                                sv["pool_c"], [dy_pool])
    (dxa, ddtr, dz), (g_dt_bias, g_a_log, g_d_skip, g_norm_w) = _scan_bwd(
        _f_ssd, "ssd_bwd", rows // SSM_CHUNK, [sv["xa"], sv["dtr"], sv["z"]], [], sv["ssd_p"], [], sv["ssd_c"],
        [dy_ssm])
    (dxbc,), (g_conv_w, g_conv_b) = _scan_bwd(_f_conv, "conv_bwd", n_row, [sv["xbc"]], [], sv["conv_p"], [],
                                              sv["conv_c"], [dxa])
    dproj = jnp.concatenate([du, dz, dxbc, ddtr], axis=1).astype(BF16)
    g_w_in = _mm(x_bf, dproj, "tn", F32, "even_dw_in")[:, :EVEN_IN]
    dx = _mm(dproj, w["even_w_in"][i], "nt", F32, "even_dx", extra=d_r1, alpha=ALPHA)
    grads = dict(even_w_in=g_w_in, pool_w=g_pool_w, pool_scale=g_pool_scale[0], conv_w=g_conv_w, conv_b=g_conv_b[0],
                 dt_bias=g_dt_bias[0, :SSM_HEADS], a_log=g_a_log[0, :SSM_HEADS], d_skip=g_d_skip[0, :SSM_HEADS],
                 ssm_norm_w=g_norm_w[0])
    return dx, grads


def _odd_forward(x_bf, w, i, tables):
    rows = x_bf.shape[0]
    cos_q, sin_q, cos_k, sin_k, rot = tables
    proj = _mm(x_bf, w["odd_w_in"][i], "nn", F32, "odd_proj")
    qf, kf, vf = (_to_heads((proj[:, j * 512:(j + 1) * 512] * sc).astype(BF16), FOX_HEADS)
                  for j, sc in enumerate((FOX_SCALE, 1.0, 1.0)))
    cq, ckv = proj[:, 1536:2048], proj[:, 2048:2304]
    kr, fl = proj[:, 2304:2432], proj[:, 2432:2560]
    n_row = rows // ROW_TILE
    fox_p = (_row(w["fgate_b"][i], LANES),)
    n_gate = rows // min(ATTN_TILE, rows)
    (fcum,), fox_c = _scan_fwd(_f_fox_gate, "fox_gate_fwd", n_gate, [fl], [], fox_p, [], [(8, LANES)], [(LANES, F32)])
    fc_heads = fcum[:, :FOX_HEADS].T
    fq, fk = fc_heads[:, :, None], fc_heads[:, None, :]
    o_fox, lse_fox = _attn_fwd(qf, kf, vf, fq, fk, "fox_attn_fwd")

    prep_p = (_row(w["q_norm_w"][i]), _row(w["kv_norm_w"][i]))
    (cqn, ckvn, krr), _ = _scan_fwd(_f_mla_prep, "mla_prep_fwd", n_row, [cq, ckv, kr], [cos_k, sin_k], prep_p,
                                    [rot], [], [(MLA_Q_RANK, BF16), (MLA_KV_RANK, BF16), (LANES, BF16)])
    q_flat = _mm(cqn, w["w_uq"][i], "nn", F32, "mla_q_up")
    (q_rope,), _ = _scan_fwd(_f_rope_q, "rope_q_fwd", n_row, [q_flat], [cos_q, sin_q], [], [], [],
                             [(q_flat.shape[1], BF16)])
    kv = _mm(ckvn, w["w_ukv"][i], "nn", BF16, "mla_kv_up")
    nope = MLA_HEADS * MLA_NOPE
    half = MLA_ROPE // 2
    q_m = jnp.concatenate([_to_heads(q_rope[:, :nope], MLA_HEADS), _to_heads(q_rope[:, nope:nope + LANES], MLA_HEADS),
                           _to_heads(q_rope[:, nope + LANES:], MLA_HEADS)], axis=2)
    kv_h = _to_heads(kv, MLA_HEADS)
    k_rope = jnp.broadcast_to(krr[None, :, :MLA_ROPE], (MLA_HEADS, rows, MLA_ROPE))
    k_m = jnp.concatenate([kv_h[:, :, :MLA_NOPE], k_rope], axis=2)
    v_m = kv_h[:, :, MLA_NOPE:]
    o_mla, lse_mla = _attn_fwd(q_m, k_m, v_m, None, None, "mla_attn_fwd")
    mix = jnp.concatenate([_from_heads(o_fox), _from_heads(o_mla)], axis=1)
    saved = dict(qf=qf, kf=kf, vf=vf, fq=fq, fk=fk, o_fox=o_fox, lse_fox=lse_fox, fl=fl, fox_p=fox_p, fox_c=fox_c,
                 cq=cq, ckv=ckv, kr=kr, prep_p=prep_p, cqn=cqn, ckvn=ckvn, q_flat=q_flat, q_m=q_m, k_m=k_m, v_m=v_m,
                 o_mla=o_mla, lse_mla=lse_mla, mix=mix)
    return mix, saved


def _odd_backward(dmix, x_bf, sv, w, i, d_r1, tables):
    rows = x_bf.shape[0]
    cos_q, sin_q, cos_k, sin_k, rot = tables
    n_row = rows // ROW_TILE
    nope = MLA_HEADS * MLA_NOPE
    half = MLA_ROPE // 2
    do_fox = _to_heads(dmix[:, :FOX_WIDTH], FOX_HEADS)
    do_mla = _to_heads(dmix[:, FOX_WIDTH:], MLA_HEADS)
    fox_args = (sv["qf"], sv["kf"], sv["vf"], sv["o_fox"], do_fox, sv["lse_fox"], sv["fq"], sv["fk"])
    dqf, delta_fox = _attn_bwd_dq(*fox_args, "fox_attn_dq")
    dkf, dvf, dfk = _attn_bwd_dkv(*fox_args[:3], delta_fox, *fox_args[4:], "fox_attn_dkv")
    dfcum = _pad_cols(dfk[:, 0, :].T, LANES)
    n_gate = rows // min(ATTN_TILE, rows)
    (dfl,), (g_fb,) = _scan_bwd(_f_fox_gate, "fox_gate_bwd", n_gate, [sv["fl"]], [], sv["fox_p"], [], sv["fox_c"],
                                [dfcum])
    mla_args = (sv["q_m"], sv["k_m"], sv["v_m"], sv["o_mla"], do_mla, sv["lse_mla"], None, None)
    dq_m, delta_mla = _attn_bwd_dq(*mla_args, "mla_attn_dq")
    dk_m, dv_m = _attn_bwd_dkv(*mla_args[:3], delta_mla, *mla_args[4:], "mla_attn_dkv")
    dq_rope = jnp.concatenate([_from_heads(dq_m[:, :, :MLA_NOPE]), _from_heads(dq_m[:, :, MLA_NOPE:MLA_NOPE + half]),
                               _from_heads(dq_m[:, :, MLA_NOPE + half:])], axis=1)
    (dq_flat,), _ = _scan_bwd(_f_rope_q, "rope_q_bwd", n_row, [sv["q_flat"]], [cos_q, sin_q], [], [], [], [dq_rope])
    g_w_uq = _mm(sv["cqn"], dq_flat, "tn", F32, "mla_dw_uq")
    dcqn = _mm(dq_flat, w["w_uq"][i], "nt", BF16, "mla_dcqn")
    dkv = _from_heads(jnp.concatenate([dk_m[:, :, :MLA_NOPE], dv_m], axis=2))
    g_w_ukv = _mm(sv["ckvn"], dkv, "tn", F32, "mla_dw_ukv")
    dckvn = _mm(dkv, w["w_ukv"][i], "nt", BF16, "mla_dckvn")
    dkrr = _head_sum(dk_m, "mla_dk_rope_sum")
    (dcq, dckv, dkr), (g_qw, g_kvw) = _scan_bwd(_f_mla_prep, "mla_prep_bwd", n_row, [sv["cq"], sv["ckv"], sv["kr"]],
                                                [cos_k, sin_k], sv["prep_p"], [rot], [], [dcqn, dckvn, dkrr])
    dproj = jnp.concatenate([_from_heads(dqf).astype(F32) * FOX_SCALE, _from_heads(dkf).astype(F32), _from_heads(dvf).astype(F32),
                             dcq, dckv, dkr, dfl], axis=1).astype(BF16)
    g_w_in = _mm(x_bf, dproj, "tn", F32, "odd_dw_in")
    dx = _mm(dproj, w["odd_w_in"][i], "nt", F32, "odd_dx", extra=d_r1, alpha=ALPHA)
    grads = dict(odd_w_in=_unprep_odd_w_in(g_w_in), fgate_b=g_fb[0, :FOX_HEADS], q_norm_w=g_qw[0], kv_norm_w=g_kvw[0],
                 w_uq=_unprep_w_uq(g_w_uq), w_ukv=g_w_ukv)
    return dx, grads


def _head_sum(dk_m, name):
    H, T, dk = dk_m.shape
    tt = _pick(T, (512, 256, 128))

    def body(d_ref, o_ref):
        acc = d_ref[0].astype(F32)
        for h in range(1, H):
            acc = acc + d_ref[h].astype(F32)
        o_ref[...] = jnp.concatenate([acc[:, MLA_NOPE:], jnp.zeros((tt, LANES - MLA_ROPE), F32)], axis=1).astype(BF16)

    return pl.pallas_call(
        body, name=name, grid=(T // tt,), in_specs=[pl.BlockSpec((H, tt, dk), lambda i: (0, i, 0))],
        out_specs=pl.BlockSpec((tt, LANES), lambda i: (i, 0)), out_shape=jax.ShapeDtypeStruct((T, LANES), BF16),
        compiler_params=_cparams(("parallel",)),
    )(dk_m)


def _local_step(x, target, w, small):
    rows = x.shape[0]
    n_row = rows // ROW_TILE
    tables = _rope_tables(rows)
    saved = []
    x_f32 = x
    x_bf = x.astype(BF16)
    for l in range(DEPTH):
        i = l // 2
        if l % 2 == 0:
            mix, sv = _even_forward(x_bf, w, i)
            w_out = w["even_w_out"][i]
        else:
            mix, sv = _odd_forward(x_bf, w, i, tables)
            w_out = w["odd_w_out"][i]
        r1 = _mm(mix, w_out, "nn", F32, "mix_out_even" if l % 2 == 0 else "mix_out_odd", extra=x_f32, alpha=ALPHA)
        ln1_p = (_row(small["ln_mix_g"][l]), _row(small["ln_mix_b"][l]))
        (x_mid,), _ = _scan_fwd(_f_ln, "ln_fwd", n_row, [r1], [], ln1_p, [], [], [(D_MODEL, F32)])
        x_mid_bf = x_mid.astype(BF16)
        gu = _mm(x_mid_bf, w["ffn_w_gu"][l], "nn", BF16, "ffn_gu")
        (act,), _ = _scan_fwd(_f_act, "ffn_act_fwd", n_row, [gu], [], [], [], [], [(D_FF, BF16)])
        r2 = _mm(act, w["ffn_w_down"][l], "nn", F32, "ffn_down", extra=x_mid, alpha=ALPHA)
        ln2_p = (_row(small["ln_ffn_g"][l]), _row(small["ln_ffn_b"][l]))
        (x_out,), _ = _scan_fwd(_f_ln, "ln_fwd", n_row, [r2], [], ln2_p, [], [], [(D_MODEL, F32)])
        saved.append(dict(sv=sv, x_bf=x_bf, r1=r1, ln1_p=ln1_p, x_mid_bf=x_mid_bf, gu=gu, act=act, r2=r2, ln2_p=ln2_p,
                          w_out=w_out))
        x_f32, x_bf = x_out, x_out.astype(BF16)

    dy, loss_part = _loss_head(x_f32, target, "loss_head")
    loss = 0.5 * jnp.sum(loss_part) / D_MODEL

    layer_grads = []
    for l in reversed(range(DEPTH)):
        i = l // 2
        s = saved[l]
        (d_r2,), (g_ln2_g, g_ln2_b) = _scan_bwd(_f_ln, "ln_bwd", n_row, [s["r2"]], [], s["ln2_p"], [], [], [dy])
        g_down = _mm(s["act"], d_r2, "tn", F32, "ffn_dw_down")
        dact = _mm(d_r2, w["ffn_w_down"][l], "nt", BF16, "ffn_dact")
        (dgu,), _ = _scan_bwd(_f_act, "ffn_act_bwd", n_row, [s["gu"]], [], [], [], [], [dact])
        g_gu = _mm(s["x_mid_bf"], dgu, "tn", F32, "ffn_dw_gu")
        dx_mid = _mm(dgu, w["ffn_w_gu"][l], "nt", F32, "ffn_dx", extra=d_r2, alpha=ALPHA)
        (d_r1,), (g_ln1_g, g_ln1_b) = _scan_bwd(_f_ln, "ln_bwd", n_row, [s["r1"]], [], s["ln1_p"], [], [], [dx_mid])
        g_w_out = _mm(s["sv"]["mix"], d_r1, "tn", F32, "even_dw_out" if l % 2 == 0 else "odd_dw_out")
        dmix = _mm(d_r1, s["w_out"], "nt", BF16, "even_dmix" if l % 2 == 0 else "odd_dmix")
        if l % 2 == 0:
            dy, g = _even_backward(dmix, s["x_bf"], s["sv"], w, i, d_r1)
            g["even_w_out"] = g_w_out
        else:
            dy, g = _odd_backward(dmix, s["x_bf"], s["sv"], w, i, d_r1, tables)
            g["odd_w_out"] = g_w_out
        g.update(ffn_w_gate=g_gu[:, :D_FF], ffn_w_up=g_gu[:, D_FF:], ffn_w_down=g_down, ln_mix_g=g_ln1_g[0],
                 ln_mix_b=g_ln1_b[0], ln_ffn_g=g_ln2_g[0], ln_ffn_b=g_ln2_b[0])
        layer_grads.append((l, g))
    return loss, dy, layer_grads


EVEN_NAMES = ("even_w_in", "pool_w", "pool_scale", "conv_w", "conv_b", "dt_bias", "a_log", "d_skip", "ssm_norm_w",
              "even_w_out")
ODD_NAMES = ("odd_w_in", "fgate_b", "q_norm_w", "w_uq", "kv_norm_w", "w_ukv", "odd_w_out")
PER_LAYER_NAMES = ("ffn_w_gate", "ffn_w_up", "ffn_w_down", "ln_mix_g", "ln_mix_b", "ln_ffn_g", "ln_ffn_b")
WEIGHT_NAMES = EVEN_NAMES + ODD_NAMES + PER_LAYER_NAMES


def _stack_grads(layer_grads):
    by_layer = dict(layer_grads)
    out = {}
    for n in EVEN_NAMES:
        out[n] = jnp.stack([by_layer[l][n] for l in range(0, DEPTH, 2)])
    for n in ODD_NAMES:
        out[n] = jnp.stack([by_layer[l][n] for l in range(1, DEPTH, 2)])
    for n in PER_LAYER_NAMES:
        out[n] = jnp.stack([by_layer[l][n] for l in range(DEPTH)])
    return out


def _prepare_weights(full):
    w = {}
    w["even_w_in"] = [_prep_even_w_in(full["even_w_in"][i]) for i in range(2)]
    w["even_w_out"] = [full["even_w_out"][i] for i in range(2)]
    w["odd_w_in"] = [_prep_odd_w_in(full["odd_w_in"][i]) for i in range(2)]
    w["w_uq"] = [_prep_w_uq(full["w_uq"][i]) for i in range(2)]
    w["w_ukv"] = [full["w_ukv"][i] for i in range(2)]
    w["odd_w_out"] = [full["odd_w_out"][i] for i in range(2)]
    w["ffn_w_gu"] = [jnp.concatenate([full["ffn_w_gate"][l], full["ffn_w_up"][l]], axis=1) for l in range(DEPTH)]
    w["ffn_w_down"] = [full["ffn_w_down"][l] for l in range(DEPTH)]
    for n in ("conv_w", "q_norm_w", "kv_norm_w"):
        w[n] = full[n]
    return w


def _flatten_small(vals):
    flat = jnp.concatenate([vals[n].reshape(-1) for n in REPLICATED])
    n = flat.shape[0]
    per = LANES * 8
    total = -(-n // per) * per
    return jnp.pad(flat, (0, total - n)).reshape(total // LANES, LANES)


def _unflatten_small(mat, like):
    flat = mat.reshape(-1)
    out, off = {}, 0
    for n in REPLICATED:
        size = math.prod(like[n].shape)
        out[n] = flat[off:off + size].reshape(like[n].shape)
        off += size
    return out


def kernel(x, even_w_in, pool_w, pool_scale, conv_w, conv_b, dt_bias, a_log, d_skip, ssm_norm_w, even_w_out, odd_w_in, fgate_b, q_norm_w, w_uq, kv_norm_w, w_ukv, odd_w_out, ffn_w_gate, ffn_w_up, ffn_w_down, ln_mix_g, ln_mix_b, ln_ffn_g, ln_ffn_b, loss_target, m_even_w_in, m_pool_w, m_pool_scale, m_conv_w, m_conv_b, m_dt_bias, m_a_log, m_d_skip, m_ssm_norm_w, m_even_w_out, m_odd_w_in, m_fgate_b, m_q_norm_w, m_w_uq, m_kv_norm_w, m_w_ukv, m_odd_w_out, m_ffn_w_gate, m_ffn_w_up, m_ffn_w_down, m_ln_mix_g, m_ln_mix_b, m_ln_ffn_g, m_ln_ffn_b, v_even_w_in, v_pool_w, v_pool_scale, v_conv_w, v_conv_b, v_dt_bias, v_a_log, v_d_skip, v_ssm_norm_w, v_even_w_out, v_odd_w_in, v_fgate_b, v_q_norm_w, v_w_uq, v_kv_norm_w, v_w_ukv, v_odd_w_out, v_ffn_w_gate, v_ffn_w_up, v_ffn_w_down, v_ln_mix_g, v_ln_mix_b, v_ln_ffn_g, v_ln_ffn_b):
    weights = dict(even_w_in=even_w_in, pool_w=pool_w, pool_scale=pool_scale, conv_w=conv_w, conv_b=conv_b,
                   dt_bias=dt_bias, a_log=a_log, d_skip=d_skip, ssm_norm_w=ssm_norm_w, even_w_out=even_w_out,
                   odd_w_in=odd_w_in, fgate_b=fgate_b, q_norm_w=q_norm_w, w_uq=w_uq, kv_norm_w=kv_norm_w, w_ukv=w_ukv,
                   odd_w_out=odd_w_out, ffn_w_gate=ffn_w_gate, ffn_w_up=ffn_w_up, ffn_w_down=ffn_w_down,
                   ln_mix_g=ln_mix_g, ln_mix_b=ln_mix_b, ln_ffn_g=ln_ffn_g, ln_ffn_b=ln_ffn_b)
    m_in = dict(even_w_in=m_even_w_in, pool_w=m_pool_w, pool_scale=m_pool_scale, conv_w=m_conv_w, conv_b=m_conv_b,
                dt_bias=m_dt_bias, a_log=m_a_log, d_skip=m_d_skip, ssm_norm_w=m_ssm_norm_w, even_w_out=m_even_w_out,
                odd_w_in=m_odd_w_in, fgate_b=m_fgate_b, q_norm_w=m_q_norm_w, w_uq=m_w_uq, kv_norm_w=m_kv_norm_w,
                w_ukv=m_w_ukv, odd_w_out=m_odd_w_out, ffn_w_gate=m_ffn_w_gate, ffn_w_up=m_ffn_w_up,
                ffn_w_down=m_ffn_w_down, ln_mix_g=m_ln_mix_g, ln_mix_b=m_ln_mix_b, ln_ffn_g=m_ln_ffn_g,
                ln_ffn_b=m_ln_ffn_b)
    v_in = dict(even_w_in=v_even_w_in, pool_w=v_pool_w, pool_scale=v_pool_scale, conv_w=v_conv_w, conv_b=v_conv_b,
                dt_bias=v_dt_bias, a_log=v_a_log, d_skip=v_d_skip, ssm_norm_w=v_ssm_norm_w, even_w_out=v_even_w_out,
                odd_w_in=v_odd_w_in, fgate_b=v_fgate_b, q_norm_w=v_q_norm_w, w_uq=v_w_uq, kv_norm_w=v_kv_norm_w,
                w_ukv=v_w_ukv, odd_w_out=v_odd_w_out, ffn_w_gate=v_ffn_w_gate, ffn_w_up=v_ffn_w_up,
                ffn_w_down=v_ffn_w_down, ln_mix_g=v_ln_mix_g, ln_mix_b=v_ln_mix_b, ln_ffn_g=v_ln_ffn_g,
                ln_ffn_b=v_ln_ffn_b)
    shards = {n: weights[n] for n, _ in SHARDED}

    gathered = _chip_allgather(_pack_weight_shards(shards), "weights_allgather")
    w = _prepare_weights(_unpack_gathered(gathered, shards))
    small = {n: weights[n] for n in REPLICATED}
    w.update(small)

    loss_local, dx, layer_grads = _local_step(x[0], loss_target[0], w, small)
    grads_full = _stack_grads(layer_grads)

    reduced = _reduce_scatter(_pack_grads(grads_full, shards))
    grads = _unpack_reduced(reduced, shards)
    small_sum = _all_reduce_small(_flatten_small(grads_full), "small_grads_allreduce")
    grads.update(_unflatten_small(small_sum, small))
    loss = lax.psum(loss_local, ("x", "y", "c"))

    deltas, new_m, new_v = {}, {}, {}
    for n in WEIGHT_NAMES:
        deltas[n], new_m[n], new_v[n] = _adamw(weights[n], grads[n], m_in[n], v_in[n], "adamw_" + n)
    return (loss, dx[None], *[grads[n] for n in WEIGHT_NAMES], *[deltas[n] for n in WEIGHT_NAMES],
            *[new_m[n] for n in WEIGHT_NAMES], *[new_v[n] for n in WEIGHT_NAMES])
```

```python
import functools
import math

import numpy as np
import jax
import jax.numpy as jnp
from jax import lax
from jax.experimental import pallas as pl
from jax.experimental.pallas import tpu as pltpu

F32 = jnp.float32
BF16 = jnp.bfloat16
HI = lax.Precision.HIGHEST
MESH_ID = pl.DeviceIdType.MESH

VMEM_LIMIT_BYTES = 56 * 1024 * 1024
LANES = 128

D_MODEL = 1024
DEPTH = 4
POOL_WINDOWS = (2, 4, 8, 16)
POOL_GROUP = 128
POOL_WIDTH = 512
SSM_D_INNER = 1024
SSM_HEAD_DIM = 64
SSM_HEADS = 16
SSM_GROUPS = 2
SSM_STATE = 128
SSM_CONV = 4
SSM_CHUNK = 128
SSM_CONV_DIM = 1536
EVEN_IN = 3088
EVEN_IN_PAD = 3200
FOX_HEADS = 8
FOX_WIDTH = 512
MLA_HEADS = 8
MLA_NOPE = 64
MLA_ROPE = 32
MLA_V = 64
MLA_Q_RANK = 512
MLA_KV_RANK = 256
ROPE_THETA = 10000.0
ODD_IN = 2344
ODD_IN_PAD = 2560
D_FF = 2816
ALPHA = (2 * DEPTH) ** 0.25
LN_EPS = 1e-5
RMS_EPS = 1e-6
ADAM_LR = 0.001
ADAM_B1 = 0.9
ADAM_B2 = 0.999
ADAM_EPS = 1e-08
ADAM_WD = 0.01
ADAM_STEP = 10
NEG_BIG = -1e30

ATTN_TILE = 512
ATTN_WIDE = 2048
FOX_SCALE = 0.125
MLA_SCALE = (MLA_NOPE + MLA_ROPE) ** -0.5
ROW_TILE = 256


def _cparams(sem=None):
    return pltpu.CompilerParams(dimension_semantics=sem, vmem_limit_bytes=VMEM_LIMIT_BYTES)


def _pick(d, prefs):
    for p in prefs:
        if d % p == 0:
            return p
    return d


_M_PREFS = (2048, 1024, 512, 640, 1408, 768, 384, 256, 128)
_N_PREFS = (1024, 512, 640, 1408, 768, 384, 256, 128)
_K_PREFS = (1024, 512, 640, 1408, 768, 256, 128)
MM_MAX_ACC_ELEMS = 1408 * 1024


def _mm(a, b, mode, out_dtype, name, extra=None, alpha=1.0):
    if mode == "nn":
        (m, k), (k2, n) = a.shape, b.shape
    elif mode == "nt":
        (m, k), (n, k2) = a.shape, b.shape
    else:
        (k, m), (k2, n) = a.shape, b.shape
    assert k == k2, (a.shape, b.shape, mode)
    tn, tk = _pick(n, _N_PREFS), _pick(k, _K_PREFS)
    tm = _pick(m, tuple(p for p in _M_PREFS if p * tn <= MM_MAX_ACC_ELEMS))
    nk = k // tk
    if mode == "nn":
        a_spec = pl.BlockSpec((tm, tk), lambda i, j, kk: (i, kk))
        b_spec = pl.BlockSpec((tk, tn), lambda i, j, kk: (kk, j))
        dims = (((1,), (0,)), ((), ()))
    elif mode == "nt":
        a_spec = pl.BlockSpec((tm, tk), lambda i, j, kk: (i, kk))
        b_spec = pl.BlockSpec((tn, tk), lambda i, j, kk: (j, kk))
        dims = (((1,), (1,)), ((), ()))
    else:
        a_spec = pl.BlockSpec((tk, tm), lambda i, j, kk: (kk, i))
        b_spec = pl.BlockSpec((tk, tn), lambda i, j, kk: (kk, j))
        dims = (((0,), (0,)), ((), ()))
    o_spec = pl.BlockSpec((tm, tn), lambda i, j, kk: (i, j))
    has_extra = extra is not None

    def body(*refs):
        if has_extra:
            a_ref, b_ref, e_ref, o_ref, acc = refs
        else:
            a_ref, b_ref, o_ref, acc = refs
        kk = pl.program_id(2)

        @pl.when(kk == 0)
        def _():
            acc[...] = jnp.zeros_like(acc)

        acc[...] += lax.dot_general(a_ref[...].astype(BF16), b_ref[...].astype(BF16), dims,
                                    preferred_element_type=F32)

        @pl.when(kk == nk - 1)
        def _():
            r = acc[...]
            if has_extra:
                r = r + alpha * e_ref[...].astype(F32)
            o_ref[...] = r.astype(o_ref.dtype)

    ins = [a, b] + ([extra] if has_extra else [])
    specs = [a_spec, b_spec] + ([o_spec] if has_extra else [])
    return pl.pallas_call(
        body, name=name, grid=(m // tm, n // tn, nk), in_specs=specs, out_specs=o_spec,
        out_shape=jax.ShapeDtypeStruct((m, n), out_dtype),
        scratch_shapes=[pltpu.VMEM((tm, tn), F32)],
        compiler_params=_cparams(("parallel", "parallel", "arbitrary")),
    )(*ins)


def _full_spec(shape):
    nd = len(shape)
    return pl.BlockSpec(tuple(shape), lambda i, _nd=nd: (0,) * _nd)


def _scan_fwd(f, name, n, tiles, ctiles, params, cparams, carry_shapes, out_defs):
    rows = tiles[0].shape[0]
    tt = rows // n
    nt, nct, npar, ncp, ncar, nout = len(tiles), len(ctiles), len(params), len(cparams), len(carry_shapes), len(out_defs)

    def body(*refs):
        pos = 0
        t_refs = refs[pos:pos + nt]; pos += nt
        ct_refs = refs[pos:pos + nct]; pos += nct
        p_refs = refs[pos:pos + npar]; pos += npar
        cp_refs = refs[pos:pos + ncp]; pos += ncp
        o_refs = refs[pos:pos + nout]; pos += nout
        cs_refs = refs[pos:pos + ncar]; pos += ncar
        c_scr = refs[pos:pos + ncar]
        i = pl.program_id(0)

        @pl.when(i == 0)
        def _():
            for c in c_scr:
                c[...] = jnp.zeros_like(c)

        carry = tuple(c[...] for c in c_scr)
        for s, c in zip(cs_refs, carry):
            s[0] = c
        new_carry, outs = f(carry, tuple(r[...] for r in t_refs), tuple(r[...] for r in ct_refs),
                            tuple(r[...] for r in p_refs), tuple(r[...] for r in cp_refs), i)
        for o_ref, o in zip(o_refs, outs):
            o_ref[...] = o
        for c, v in zip(c_scr, new_carry):
            c[...] = v

    tile_spec = lambda arr: pl.BlockSpec((tt, arr.shape[1]), lambda i: (i, 0))
    in_specs = ([tile_spec(t) for t in tiles] + [tile_spec(t) for t in ctiles]
                + [_full_spec(p.shape) for p in params] + [_full_spec(p.shape) for p in cparams])
    out_specs = ([pl.BlockSpec((tt, c), lambda i: (i, 0)) for c, _ in out_defs]
                 + [pl.BlockSpec((1,) + tuple(s), lambda i: (i, 0, 0)) for s in carry_shapes])
    out_shape = ([jax.ShapeDtypeStruct((rows, c), dt) for c, dt in out_defs]
                 + [jax.ShapeDtypeStruct((n,) + tuple(s), F32) for s in carry_shapes])
    res = pl.pallas_call(
        body, name=name, grid=(n,), in_specs=in_specs, out_specs=out_specs, out_shape=out_shape,
        scratch_shapes=[pltpu.VMEM(tuple(s), F32) for s in carry_shapes],
        compiler_params=_cparams(("arbitrary",)),
    )(*tiles, *ctiles, *params, *cparams)
    return list(res[:nout]), list(res[nout:])


def _scan_bwd(f, name, n, tiles, ctiles, params, cparams, carries, douts):
    rows = tiles[0].shape[0]
    tt = rows // n
    nt, nct, npar, ncp, ncar, nout = len(tiles), len(ctiles), len(params), len(cparams), len(carries), len(douts)

    def body(*refs):
        pos = 0
        t_refs = refs[pos:pos + nt]; pos += nt
        ct_refs = refs[pos:pos + nct]; pos += nct
        p_refs = refs[pos:pos + npar]; pos += npar
        cp_refs = refs[pos:pos + ncp]; pos += ncp
        cs_refs = refs[pos:pos + ncar]; pos += ncar
        do_refs = refs[pos:pos + nout]; pos += nout
        dt_refs = refs[pos:pos + nt]; pos += nt
        dp_refs = refs[pos:pos + npar]; pos += npar
        dc_scr = refs[pos:pos + ncar]
        i = pl.program_id(0)

        @pl.when(i == 0)
        def _():
            for c in dc_scr:
                c[...] = jnp.zeros_like(c)
            for d in dp_refs:
                d[...] = jnp.zeros_like(d)

        ctv = tuple(r[...] for r in ct_refs)
        cpv = tuple(r[...] for r in cp_refs)

        def g(c, t, p):
            return f(c, t, ctv, p, cpv, n - 1 - i)

        _, vjp = jax.vjp(g, tuple(s[0] for s in cs_refs), tuple(r[...] for r in t_refs),
                         tuple(r[...] for r in p_refs))
        dc, dt, dp = vjp((tuple(c[...] for c in dc_scr), tuple(r[...] for r in do_refs)))
        for r, v in zip(dt_refs, dt):
            r[...] = v
        for r, v in zip(dp_refs, dp):
            r[...] += v
        for c, v in zip(dc_scr, dc):
            c[...] = v

    rev_tile = lambda arr: pl.BlockSpec((tt, arr.shape[1]), lambda i: (n - 1 - i, 0))
    in_specs = ([rev_tile(t) for t in tiles] + [rev_tile(t) for t in ctiles]
                + [_full_spec(p.shape) for p in params] + [_full_spec(p.shape) for p in cparams]
                + [pl.BlockSpec((1,) + tuple(c.shape[1:]), lambda i: (n - 1 - i, 0, 0)) for c in carries]
                + [rev_tile(d) for d in douts])
    out_specs = [rev_tile(t) for t in tiles] + [_full_spec(p.shape) for p in params]
    out_shape = ([jax.ShapeDtypeStruct(t.shape, t.dtype) for t in tiles]
                 + [jax.ShapeDtypeStruct(p.shape, F32) for p in params])
    res = pl.pallas_call(
        body, name=name, grid=(n,), in_specs=in_specs, out_specs=out_specs, out_shape=out_shape,
        scratch_shapes=[pltpu.VMEM(tuple(c.shape[1:]), F32) for c in carries],
        compiler_params=_cparams(("arbitrary",)),
    )(*tiles, *ctiles, *params, *cparams, *carries, *douts)
    return list(res[:nt]), list(res[nt:])


def _silu(x):
    return x * jax.nn.sigmoid(x)


def _softplus(x):
    return jnp.maximum(x, 0.0) + jnp.log(1.0 + jnp.exp(-jnp.abs(x)))


def _f_ln(carry, tiles, ctiles, params, cparams, idx):
    (r,), (g, b) = tiles, params
    mu = jnp.mean(r, axis=-1, keepdims=True)
    xc = r - mu
    var = jnp.mean(xc * xc, axis=-1, keepdims=True)
    return (), (xc * lax.rsqrt(var + LN_EPS) * g + b,)


def _f_act(carry, tiles, ctiles, params, cparams, idx):
    (gu,) = tiles
    g = gu[:, :D_FF].astype(F32)
    u = gu[:, D_FF:].astype(F32)
    return (), ((_silu(g) * u).astype(BF16),)


def _f_pool(carry, tiles, ctiles, params, cparams, idx):
    (prev,), (u,), (pool_w, pool_scale) = carry, tiles, params
    tt = u.shape[0]
    halo = prev.shape[0]
    ext = jnp.concatenate([prev, u], axis=0)
    pos = idx * tt + lax.broadcasted_iota(jnp.int32, (tt, 1), 0)
    ys = []
    for g, w in enumerate(POOL_WINDOWS):
        lo, hi = g * POOL_GROUP, (g + 1) * POOL_GROUP
        eg = ext[:, lo:hi]
        s = eg[halo:halo + tt]
        for j in range(1, w):
            s = s + eg[halo - j:halo - j + tt]
        count = jnp.minimum(pos + 1, w).astype(F32)
        diff = s / count - u[:, lo:hi]
        ys.append(jnp.dot(diff.astype(BF16), pool_w[g].astype(BF16), preferred_element_type=F32))
    y = jnp.concatenate(ys, axis=1) * pool_scale
    return (u[tt - halo:, :],), (y.astype(BF16),)


def _f_conv(carry, tiles, ctiles, params, cparams, idx):
    (prev,), (xbc,), (conv_w, conv_b) = carry, tiles, params
    tt = xbc.shape[0]
    halo = prev.shape[0]
    ext = jnp.concatenate([prev, xbc], axis=0)
    acc = jnp.zeros_like(xbc) + conv_b
    for k in range(SSM_CONV):
        off = halo - (SSM_CONV - 1) + k
        acc = acc + ext[off:off + tt] * conv_w[k:k + 1, :]
    return (xbc[tt - halo:, :],), (_silu(acc),)


def _f_ssd(carry, tiles, ctiles, params, cparams, idx):
    (state,), (xa, dtr, z), (dt_bias, a_log, d_skip, norm_w) = carry, tiles, params
    L, P, N, E = SSM_CHUNK, SSM_HEAD_DIM, SSM_STATE, SSM_HEADS // SSM_GROUPS
    dt = _softplus(dtr + dt_bias)
    da = dt * (-jnp.exp(a_log))
    r = lax.broadcasted_iota(jnp.int32, (L, L), 0)
    c = lax.broadcasted_iota(jnp.int32, (L, L), 1)
    tri = c <= r
    acs = jnp.dot(tri.astype(F32), da, precision=HI, preferred_element_type=F32)
    acs_t = acs.T
    ys, new_states = [], []
    for g in range(SSM_GROUPS):
        bg = xa[:, SSM_D_INNER + g * N:SSM_D_INNER + (g + 1) * N].astype(BF16)
        cg = xa[:, SSM_D_INNER + (SSM_GROUPS + g) * N:SSM_D_INNER + (SSM_GROUPS + g + 1) * N].astype(BF16)
        cb = lax.dot_general(cg, bg, (((1,), (1,)), ((), ())), preferred_element_type=F32)
        for e in range(E):
            h = g * E + e
            xh = xa[:, h * P:(h + 1) * P]
            col = acs[:, h:h + 1]
            row = acs_t[h:h + 1, :]
            last = acs[L - 1:L, h:h + 1]
            lmat = jnp.exp(jnp.where(tri, col - row, NEG_BIG))
            xdt_f = xh * dt[:, h:h + 1]
            y_diag = jnp.dot((cb * lmat).astype(BF16), xdt_f.astype(BF16), preferred_element_type=F32)
            xdec = (xdt_f * jnp.exp(last - col)).astype(BF16)
            st = lax.dot_general(xdec, bg, (((0,), (0,)), ((), ())), preferred_element_type=F32)
            prev = state[h * P:(h + 1) * P, :]
            y_off = lax.dot_general(cg, prev.astype(BF16), (((1,), (1,)), ((), ())),
                                    preferred_element_type=F32) * jnp.exp(col)
            new_states.append(prev * jnp.exp(last) + st)
            ys.append(y_diag + y_off + xh * d_skip[:, h:h + 1])
    y = jnp.concatenate(ys, axis=1) * _silu(z)
    y = y * lax.rsqrt(jnp.mean(y * y, axis=-1, keepdims=True) + RMS_EPS) * norm_w
    return (jnp.concatenate(new_states, axis=0),), (y.astype(BF16),)


def _f_fox_gate(carry, tiles, ctiles, params, cparams, idx):
    (run,), (fl,), (fb,) = carry, tiles, params
    tt = fl.shape[0]
    logf = -_softplus(-(fl + fb))
    r = lax.broadcasted_iota(jnp.int32, (tt, tt), 0)
    c = lax.broadcasted_iota(jnp.int32, (tt, tt), 1)
    cum = jnp.dot((c <= r).astype(F32), logf, precision=HI, preferred_element_type=F32) + run[0:1, :]
    return (jnp.broadcast_to(cum[tt - 1:tt, :], run.shape),), (cum,)


def _rms(x, w):
    return x * lax.rsqrt(jnp.mean(x * x, axis=-1, keepdims=True) + RMS_EPS) * w


def _f_mla_prep(carry, tiles, ctiles, params, cparams, idx):
    (cq, ckv, kr), (ck, sk), (qw, kvw), (rk,) = tiles, ctiles, params, cparams
    rot = jnp.dot(kr, rk, precision=HI, preferred_element_type=F32)
    return (), (_rms(cq, qw).astype(BF16), _rms(ckv, kvw).astype(BF16), (kr * ck + rot * sk).astype(BF16))


def _f_rope_q(carry, tiles, ctiles, params, cparams, idx):
    (q,), (cos, sin) = tiles, ctiles
    nope = MLA_HEADS * MLA_NOPE
    x1 = q[:, nope:nope + LANES]
    x2 = q[:, nope + LANES:]
    roped = jnp.concatenate([q[:, :nope], x1 * cos - x2 * sin, x2 * cos + x1 * sin], axis=1)
    return (), ((roped * MLA_SCALE).astype(BF16),)


def _scores(q, k, fq, fk, masked, row0, col0):
    s = lax.dot_general(q, k, (((1,), (1,)), ((), ())), preferred_element_type=F32)
    if fq is not None:
        s = s + fq - fk
    if masked:
        rows = row0 + lax.broadcasted_iota(jnp.int32, s.shape, 0)
        cols = col0 + lax.broadcasted_iota(jnp.int32, s.shape, 1)
        s = jnp.where(cols <= rows, s, NEG_BIG)
    return s


def _q_major_tables(T, tq, tk):
    r = tk // tq
    qi = np.concatenate([np.full(i // r + 1, i, np.int32) for i in range(T // tq)])
    ki = np.concatenate([np.arange(i // r + 1, dtype=np.int32) for i in range(T // tq)])
    kind = np.where(ki == qi // r, qi % r + 1, 0).astype(np.int32)
    return [jnp.asarray(a) for a in (qi, ki, (ki == 0).astype(np.int32), kind)]


def _k_major_tables(T, tq, tk):
    r = tq // tk
    nq = T // tq
    ki = np.concatenate([np.full(nq - i // r, i, np.int32) for i in range(T // tk)])
    qi = np.concatenate([np.arange(i // r, nq, dtype=np.int32) for i in range(T // tk)])
    kind = np.where(qi == ki // r, r - ki % r, 0).astype(np.int32)
    return [jnp.asarray(a) for a in (ki, qi, (qi == nq - 1).astype(np.int32), kind)]


def _attn_tiles(T):
    return min(ATTN_TILE, T), min(ATTN_WIDE, T)


def _attn_fwd(q, k, v, fq, fk, name):
    H, T, dk = q.shape
    dv = v.shape[2]
    tq, tk = _attn_tiles(T)
    decay = fq is not None
    tables = _q_major_tables(T, tq, tk)

    def body(qi_ref, ki_ref, first_ref, kind_ref, *refs):
        if decay:
            q_ref, k_ref, v_ref, fq_ref, fk_ref, o_ref, lse_ref, m_s, l_s, acc = refs
        else:
            q_ref, k_ref, v_ref, o_ref, lse_ref, m_s, l_s, acc = refs
        t = pl.program_id(1)
        qi, ki = qi_ref[t], ki_ref[t]

        @pl.when(first_ref[t] == 1)
        def _():
            m_s[...] = jnp.full_like(m_s, NEG_BIG)
            l_s[...] = jnp.zeros_like(l_s)
            acc[...] = jnp.zeros_like(acc)

        def step(kind):
            w = tk if kind == 0 else kind * tq
            s = _scores(q_ref[0], k_ref[0, :w, :], fq_ref[0] if decay else None,
                        fk_ref[0, :, :w] if decay else None, kind > 0, qi * tq, ki * tk)
            m_new = jnp.maximum(m_s[...], jnp.max(s, axis=-1, keepdims=True))
            p = jnp.exp(s - m_new)
            corr = jnp.exp(m_s[...] - m_new)
            l_s[...] = corr * l_s[...] + jnp.sum(p, axis=-1, keepdims=True)
            acc[...] = corr * acc[...] + jnp.dot(p.astype(BF16), v_ref[0, :w, :], preferred_element_type=F32)
            m_s[...] = m_new
            if kind > 0:
                o_ref[0] = (acc[...] / l_s[...]).astype(o_ref.dtype)
                lse_ref[0] = m_s[...] + jnp.log(l_s[...])

        for kind in range(tk // tq + 1):
            pl.when(kind_ref[t] == kind)(functools.partial(step, kind))

    qspec = lambda d: pl.BlockSpec((1, tq, d), lambda h, t, qi, ki, fi, la: (h, qi[t], 0))
    kspec = lambda d: pl.BlockSpec((1, tk, d), lambda h, t, qi, ki, fi, la: (h, ki[t], 0))
    in_specs = [qspec(dk), kspec(dk), kspec(dv)]
    ins = [q, k, v]
    if decay:
        in_specs += [qspec(1), pl.BlockSpec((1, 1, tk), lambda h, t, qi, ki, fi, la: (h, 0, ki[t]))]
        ins += [fq, fk]
    grid_spec = pltpu.PrefetchScalarGridSpec(
        num_scalar_prefetch=4, grid=(H, int(tables[0].shape[0])), in_specs=in_specs,
        out_specs=[qspec(dv), qspec(1)],
        scratch_shapes=[pltpu.VMEM((tq, 1), F32), pltpu.VMEM((tq, 1), F32), pltpu.VMEM((tq, dv), F32)])
    return pl.pallas_call(
        body, name=name, grid_spec=grid_spec,
        out_shape=[jax.ShapeDtypeStruct((H, T, dv), BF16), jax.ShapeDtypeStruct((H, T, 1), F32)],
        compiler_params=_cparams(("parallel", "arbitrary")),
    )(*tables, *ins)


def _attn_bwd_dq(q, k, v, o, do, lse, fq, fk, name):
    H, T, dk = q.shape
    dv = v.shape[2]
    tq, tk = _attn_tiles(T)
    decay = fq is not None
    tables = _q_major_tables(T, tq, tk)

    def body(qi_ref, ki_ref, first_ref, kind_ref, *refs):
        if decay:
            q_ref, k_ref, v_ref, o_ref, do_ref, lse_ref, fq_ref, fk_ref, dq_ref, dl_ref, acc, dl, leak = refs
        else:
            q_ref, k_ref, v_ref, o_ref, do_ref, lse_ref, dq_ref, dl_ref, acc, dl, leak = refs
        t = pl.program_id(1)
        qi, ki = qi_ref[t], ki_ref[t]

        @pl.when(first_ref[t] == 1)
        def _():
            acc[...] = jnp.zeros_like(acc)
            leak[...] = jnp.zeros_like(leak)
            dl[...] = jnp.sum(do_ref[0].astype(F32) * o_ref[0].astype(F32), axis=-1, keepdims=True)

        def step(kind):
            w = tk if kind == 0 else kind * tq
            k_v = k_ref[0, :w, :]
            s = _scores(q_ref[0], k_v, fq_ref[0] if decay else None, fk_ref[0, :, :w] if decay else None,
                        kind > 0, qi * tq, ki * tk)
            p = jnp.exp(s - lse_ref[0])
            dp = lax.dot_general(do_ref[0], v_ref[0, :w, :], (((1,), (1,)), ((), ())), preferred_element_type=F32)
            ds = p * (dp - dl[...])
            leak[...] += jnp.sum(ds, axis=-1, keepdims=True)
            acc[...] += jnp.dot(ds.astype(BF16), k_v, preferred_element_type=F32)
            if kind > 0:
                dq_ref[0] = acc[...].astype(dq_ref.dtype)
                dl_ref[0] = dl[...] + leak[...]

        for kind in range(tk // tq + 1):
            pl.when(kind_ref[t] == kind)(functools.partial(step, kind))

    qspec = lambda d: pl.BlockSpec((1, tq, d), lambda h, t, qi, ki, fi, la: (h, qi[t], 0))
    kspec = lambda d: pl.BlockSpec((1, tk, d), lambda h, t, qi, ki, fi, la: (h, ki[t], 0))
    in_specs = [qspec(dk), kspec(dk), kspec(dv), qspec(dv), qspec(dv), qspec(1)]
    ins = [q, k, v, o, do, lse]
    if decay:
        in_specs += [qspec(1), pl.BlockSpec((1, 1, tk), lambda h, t, qi, ki, fi, la: (h, 0, ki[t]))]
        ins += [fq, fk]
    grid_spec = pltpu.PrefetchScalarGridSpec(
        num_scalar_prefetch=4, grid=(H, int(tables[0].shape[0])), in_specs=in_specs,
        out_specs=[qspec(dk), qspec(1)],
        scratch_shapes=[pltpu.VMEM((tq, dk), F32), pltpu.VMEM((tq, 1), F32), pltpu.VMEM((tq, 1), F32)])
    return pl.pallas_call(
        body, name=name, grid_spec=grid_spec,
        out_shape=[jax.ShapeDtypeStruct((H, T, dk), BF16), jax.ShapeDtypeStruct((H, T, 1), F32)],
        compiler_params=_cparams(("parallel", "arbitrary")),
    )(*tables, *ins)


def _attn_bwd_dkv(q, k, v, delta, do, lse, fq, fk, name):
    H, T, dk = q.shape
    dv = v.shape[2]
    tk, tq = _attn_tiles(T)
    decay = fq is not None
    tables = _k_major_tables(T, tq, tk)

    def body(ki_ref, qi_ref, last_ref, kind_ref, *refs):
        if decay:
            q_ref, k_ref, v_ref, dl_ref, do_ref, lse_ref, fq_ref, fk_ref, dk_ref, dv_ref, df_ref, dk_s, dv_s, df_s = refs
        else:
            q_ref, k_ref, v_ref, dl_ref, do_ref, lse_ref, dk_ref, dv_ref, dk_s, dv_s = refs
        t = pl.program_id(1)
        ki, qi = ki_ref[t], qi_ref[t]

        @pl.when(kind_ref[t] > 0)
        def _():
            dk_s[...] = jnp.zeros_like(dk_s)
            dv_s[...] = jnp.zeros_like(dv_s)
            if decay:
                df_s[...] = jnp.zeros_like(df_s)

        def step(kind):
            off = 0 if kind == 0 else tq - kind * tk
            q_v, do_v = q_ref[0, off:, :], do_ref[0, off:, :]
            s = _scores(q_v, k_ref[0], fq_ref[0, off:, :] if decay else None, fk_ref[0] if decay else None,
                        kind > 0, qi * tq + off, ki * tk)
            p = jnp.exp(s - lse_ref[0, off:, :])
            dv_s[...] += lax.dot_general(p.astype(BF16), do_v, (((0,), (0,)), ((), ())), preferred_element_type=F32)
            dp = lax.dot_general(do_v, v_ref[0], (((1,), (1,)), ((), ())), preferred_element_type=F32)
            ds = p * (dp - dl_ref[0, off:, :])
            dk_s[...] += lax.dot_general(ds.astype(BF16), q_v, (((0,), (0,)), ((), ())),
                                         preferred_element_type=F32)
            if decay:
                df_s[...] -= jnp.sum(ds, axis=0, keepdims=True)

        for kind in range(tq // tk + 1):
            pl.when(kind_ref[t] == kind)(functools.partial(step, kind))

        @pl.when(last_ref[t] == 1)
        def _():
            dk_ref[0] = dk_s[...].astype(dk_ref.dtype)
            dv_ref[0] = dv_s[...].astype(dv_ref.dtype)
            if decay:
                df_ref[0] = df_s[...]

    qspec = lambda d: pl.BlockSpec((1, tq, d), lambda h, t, ki, qi, fi, la: (h, qi[t], 0))
    kspec = lambda d: pl.BlockSpec((1, tk, d), lambda h, t, ki, qi, fi, la: (h, ki[t], 0))
    in_specs = [qspec(dk), kspec(dk), kspec(dv), qspec(1), qspec(dv), qspec(1)]
    ins = [q, k, v, delta, do, lse]
    out_specs = [kspec(dk), kspec(dv)]
    out_shape = [jax.ShapeDtypeStruct((H, T, dk), BF16), jax.ShapeDtypeStruct((H, T, dv), BF16)]
    scratch = [pltpu.VMEM((tk, dk), F32), pltpu.VMEM((tk, dv), F32)]
    if decay:
        fkspec = pl.BlockSpec((1, 1, tk), lambda h, t, ki, qi, fi, la: (h, 0, ki[t]))
        in_specs += [qspec(1), fkspec]
        ins += [fq, fk]
        out_specs.append(fkspec)
        out_shape.append(jax.ShapeDtypeStruct((H, 1, T), F32))
        scratch.append(pltpu.VMEM((1, tk), F32))
    grid_spec = pltpu.PrefetchScalarGridSpec(
        num_scalar_prefetch=4, grid=(H, int(tables[0].shape[0])), in_specs=in_specs, out_specs=out_specs,
        scratch_shapes=scratch)
    return pl.pallas_call(
        body, name=name, grid_spec=grid_spec, out_shape=out_shape, compiler_params=_cparams(("parallel", "arbitrary")),
    )(*tables, *ins)


def _loss_head(y, target, name):
    rows, d = y.shape
    tt = _pick(rows, (512, 256, 128))

    def body(y_ref, t_ref, dy_ref, part_ref):
        @pl.when(pl.program_id(0) == 0)
        def _():
            part_ref[...] = jnp.zeros_like(part_ref)

        err = y_ref[...] - t_ref[...]
        dy_ref[...] = err * (1.0 / d)
        sq = jnp.sum(err * err, axis=0, keepdims=True)
        folded = sq[:, :LANES]
        for j in range(1, d // LANES):
            folded = folded + sq[:, j * LANES:(j + 1) * LANES]
        part_ref[...] += folded

    spec = pl.BlockSpec((tt, d), lambda i: (i, 0))
    return pl.pallas_call(
        body, name=name, grid=(rows // tt,), in_specs=[spec, spec],
        out_specs=[spec, pl.BlockSpec((1, LANES), lambda i: (0, 0))],
        out_shape=[jax.ShapeDtypeStruct((rows, d), F32), jax.ShapeDtypeStruct((1, LANES), F32)],
        compiler_params=_cparams(("arbitrary",)),
    )(y, target)


def _adamw(w, g, m, v, name):
    shape = w.shape
    cols = shape[-1]
    rows = math.prod(shape[:-1])
    w2, g2, m2, v2 = (a.reshape(rows, cols) for a in (w, g, m, v))
    tr = rows
    for cand in (512, 256, 128, 64, 32, 16, 8):
        if rows % cand == 0:
            tr = cand
            break
    c1 = 1.0 / (1.0 - ADAM_B1 ** ADAM_STEP)
    c2 = 1.0 / (1.0 - ADAM_B2 ** ADAM_STEP)

    def body(w_ref, g_ref, m_ref, v_ref, d_ref, nm_ref, nv_ref):
        gv = g_ref[...]
        nm = ADAM_B1 * m_ref[...] + (1.0 - ADAM_B1) * gv
        nv = ADAM_B2 * v_ref[...] + (1.0 - ADAM_B2) * gv * gv
        d_ref[...] = -ADAM_LR * ((nm * c1) / (jnp.sqrt(nv * c2) + ADAM_EPS) + ADAM_WD * w_ref[...])
        nm_ref[...] = nm
        nv_ref[...] = nv

    spec = pl.BlockSpec((tr, cols), lambda i: (i, 0))
    sds = jax.ShapeDtypeStruct((rows, cols), F32)
    d, nm, nv = pl.pallas_call(
        body, name=name, grid=(rows // tr,), in_specs=[spec] * 4, out_specs=[spec] * 3, out_shape=[sds] * 3,
        compiler_params=_cparams(("parallel",)),
    )(w2, g2, m2, v2)
    return d.reshape(shape), nm.reshape(shape), nv.reshape(shape)


_ANY = pl.BlockSpec(memory_space=pl.ANY)


def _my_xyc():
    return lax.axis_index("x"), lax.axis_index("y"), lax.axis_index("c")


def _chip_allgather(shards, name):
    n = len(shards)

    def body(*refs):
        x_refs, out_refs = refs[:n], refs[n:2 * n]
        send_sems, recv_sems, loc_sems = refs[2 * n:]
        x, y, c = _my_xyc()
        myq = 2 * x + y
        chips = [(1 - x, y), (x, 1 - y), (1 - x, 1 - y)]
        started = []
        for i in range(n):
            local = pltpu.make_async_copy(x_refs[i], out_refs[i].at[myq], loc_sems.at[i])
            local.start()
            started.append(local)
        sends = []
        for j, (px, py) in enumerate(chips):
            for i in range(n):
                cp = pltpu.make_async_remote_copy(src_ref=x_refs[i], dst_ref=out_refs[i].at[myq],
                                                  send_sem=send_sems.at[3 * i + j], recv_sem=recv_sems.at[3 * i + j],
                                                  device_id=(px, py, c), device_id_type=MESH_ID)
                cp.start()
                sends.append(cp)
        for j, (px, py) in enumerate(chips):
            for i in range(n):
                pltpu.make_async_remote_copy(src_ref=x_refs[i], dst_ref=out_refs[i].at[2 * px + py],
                                             send_sem=send_sems.at[3 * i + j], recv_sem=recv_sems.at[3 * i + j],
                                             device_id=(px, py, c), device_id_type=MESH_ID).wait_recv()
        for cp in sends:
            cp.wait_send()
        for local in started:
            local.wait()

    return pl.pallas_call(
        body, name=name, in_specs=[_ANY] * n, out_specs=[_ANY] * n,
        out_shape=[jax.ShapeDtypeStruct((4,) + s.shape, s.dtype) for s in shards],
        scratch_shapes=[pltpu.SemaphoreType.DMA((3 * n,)), pltpu.SemaphoreType.DMA((3 * n,)),
                        pltpu.SemaphoreType.DMA((n,))],
    )(*shards)


def _sibling_swap_halves(g, name):
    q, rows, cols = g.shape
    rh = rows // 2

    def body(g_ref, out_ref, send_sem, recv_sem):
        x, y, c = _my_xyc()
        src = g_ref.at[:, pl.ds((1 - c) * rh, rh), :]
        cp = pltpu.make_async_remote_copy(src_ref=src, dst_ref=out_ref, send_sem=send_sem, recv_sem=recv_sem,
                                          device_id=(x, y, 1 - c), device_id_type=MESH_ID)
        cp.start()
        cp.wait()

    return pl.pallas_call(
        body, name=name, in_specs=[_ANY], out_specs=_ANY,
        out_shape=jax.ShapeDtypeStruct((q, rh, cols), g.dtype),
        scratch_shapes=[pltpu.SemaphoreType.DMA, pltpu.SemaphoreType.DMA],
    )(g)


def _add_own_half(g, other, name):
    q, rows, cols = g.shape
    rh = rows // 2
    tr = _pick(rh, (512, 256, 128, 64, 32, 16, 8))
    nb = rh // tr
    cidx = lax.axis_index("c").astype(jnp.int32).reshape(1)

    def body(c_ref, g_ref, o_ref, out_ref):
        out_ref[...] = (g_ref[...] + o_ref[...]).astype(out_ref.dtype)

    grid_spec = pltpu.PrefetchScalarGridSpec(
        num_scalar_prefetch=1, grid=(q, nb),
        in_specs=[pl.BlockSpec((1, tr, cols), lambda a, i, c_ref: (a, c_ref[0] * nb + i, 0)),
                  pl.BlockSpec((1, tr, cols), lambda a, i, c_ref: (a, i, 0))],
        out_specs=pl.BlockSpec((1, tr, cols), lambda a, i, c_ref: (a, i, 0)))
    return pl.pallas_call(
        body, name=name, grid_spec=grid_spec, out_shape=jax.ShapeDtypeStruct((q, rh, cols), BF16),
        compiler_params=_cparams(("parallel", "parallel")),
    )(cidx, g, other)


def _chip_exchange(s, name):
    def body(s_ref, out_ref, send_sems, recv_sems, loc_sem):
        x, y, c = _my_xyc()
        myq = 2 * x + y
        local = pltpu.make_async_copy(s_ref.at[myq], out_ref.at[myq], loc_sem)
        local.start()
        chips = [(1 - x, y), (x, 1 - y), (1 - x, 1 - y)]
        sends = []
        for j, (px, py) in enumerate(chips):
            cp = pltpu.make_async_remote_copy(src_ref=s_ref.at[2 * px + py], dst_ref=out_ref.at[myq],
                                              send_sem=send_sems.at[j], recv_sem=recv_sems.at[j],
                                              device_id=(px, py, c), device_id_type=MESH_ID)
            cp.start()
            sends.append(cp)
        for j, (px, py) in enumerate(chips):
            pltpu.make_async_remote_copy(src_ref=s_ref.at[myq], dst_ref=out_ref.at[2 * px + py],
                                         send_sem=send_sems.at[j], recv_sem=recv_sems.at[j],
                                         device_id=(px, py, c), device_id_type=MESH_ID).wait_recv()
        for cp in sends:
            cp.wait_send()
        local.wait()

    return pl.pallas_call(
        body, name=name, in_specs=[_ANY], out_specs=_ANY, out_shape=jax.ShapeDtypeStruct(s.shape, s.dtype),
        scratch_shapes=[pltpu.SemaphoreType.DMA((3,)), pltpu.SemaphoreType.DMA((3,)), pltpu.SemaphoreType.DMA],
    )(s)


def _sum_leading(a, name):
    q, rows, cols = a.shape
    tr = _pick(rows, (512, 256, 128, 64, 32, 16, 8))

    def body(a_ref, out_ref):
        acc = a_ref[0].astype(F32)
        for j in range(1, q):
            acc = acc + a_ref[j].astype(F32)
        out_ref[...] = acc

    return pl.pallas_call(
        body, name=name, grid=(rows // tr,), in_specs=[pl.BlockSpec((q, tr, cols), lambda i: (0, i, 0))],
        out_specs=pl.BlockSpec((tr, cols), lambda i: (i, 0)), out_shape=jax.ShapeDtypeStruct((rows, cols), F32),
        compiler_params=_cparams(("parallel",)),
    )(a)


def _sibling_join_halves(r, name):
    rh, cols = r.shape

    def body(r_ref, out_ref, send_sem, recv_sem, loc_sem):
        x, y, c = _my_xyc()
        local = pltpu.make_async_copy(r_ref, out_ref.at[c], loc_sem)
        local.start()
        cp = pltpu.make_async_remote_copy(src_ref=r_ref, dst_ref=out_ref.at[c], send_sem=send_sem,
                                          recv_sem=recv_sem, device_id=(x, y, 1 - c), device_id_type=MESH_ID)
        cp.start()
        pltpu.make_async_remote_copy(src_ref=r_ref, dst_ref=out_ref.at[1 - c], send_sem=send_sem,
                                     recv_sem=recv_sem, device_id=(x, y, 1 - c), device_id_type=MESH_ID).wait_recv()
        cp.wait_send()
        local.wait()

    return pl.pallas_call(
        body, name=name, in_specs=[_ANY], out_specs=_ANY, out_shape=jax.ShapeDtypeStruct((2, rh, cols), r.dtype),
        scratch_shapes=[pltpu.SemaphoreType.DMA, pltpu.SemaphoreType.DMA, pltpu.SemaphoreType.DMA],
    )(r).reshape(2 * rh, cols)


def _all_reduce_small(vec, name):
    rows, cols = vec.shape

    def body(v_ref, out_ref, buf, send_sems, recv_sems):
        x, y, c = _my_xyc()
        me = 4 * x + 2 * y + c
        buf[me] = v_ref[...]
        sends = []
        for kk in range(1, 8):
            peer = (1 - x if kk & 4 else x, 1 - y if kk & 2 else y, 1 - c if kk & 1 else c)
            cp = pltpu.make_async_remote_copy(src_ref=v_ref, dst_ref=buf.at[me], send_sem=send_sems.at[kk - 1],
                                              recv_sem=recv_sems.at[kk - 1], device_id=peer, device_id_type=MESH_ID)
            cp.start()
            sends.append(cp)
        for kk in range(1, 8):
            px, py, pc = (1 - x if kk & 4 else x, 1 - y if kk & 2 else y, 1 - c if kk & 1 else c)
            pltpu.make_async_remote_copy(src_ref=v_ref, dst_ref=buf.at[4 * px + 2 * py + pc],
                                         send_sem=send_sems.at[kk - 1], recv_sem=recv_sems.at[kk - 1],
                                         device_id=(px, py, pc), device_id_type=MESH_ID).wait_recv()
        for cp in sends:
            cp.wait_send()
        acc = buf[0]
        for j in range(1, 8):
            acc = acc + buf[j]
        out_ref[...] = acc

    vm = pl.BlockSpec(memory_space=pltpu.VMEM)
    return pl.pallas_call(
        body, name=name, in_specs=[vm], out_specs=vm, out_shape=jax.ShapeDtypeStruct((rows, cols), F32),
        scratch_shapes=[pltpu.VMEM((8, rows, cols), F32), pltpu.SemaphoreType.DMA((7,)), pltpu.SemaphoreType.DMA((7,))],
        compiler_params=pltpu.CompilerParams(vmem_limit_bytes=VMEM_LIMIT_BYTES),
    )(vec)


def _reduce_scatter(g):
    other = _sibling_swap_halves(g, "rs_swap_halves")
    s = _add_own_half(g, other, "rs_add_halves")
    recv = _chip_exchange(s, "rs_chip_exchange")
    r = _sum_leading(recv, "rs_sum_chips")
    return _sibling_join_halves(r, "rs_join_halves")


PACK_COLS = 1024
PACK_ROW_MULTIPLE = 1024
SHARDED = (("even_w_in", 2), ("conv_w", 2), ("even_w_out", 1), ("odd_w_in", 2), ("q_norm_w", 1), ("w_uq", 2),
           ("kv_norm_w", 1), ("w_ukv", 2), ("odd_w_out", 1), ("ffn_w_gate", 2), ("ffn_w_up", 2), ("ffn_w_down", 1))
KEEP_F32 = ("conv_w", "q_norm_w", "kv_norm_w")
REPLICATED = ("pool_w", "pool_scale", "conv_b", "dt_bias", "a_log", "d_skip", "ssm_norm_w", "fgate_b",
              "ln_mix_g", "ln_mix_b", "ln_ffn_g", "ln_ffn_b")


def _gather_weights(shards):
    sent = [shards[name] if name in KEEP_F32 else shards[name].astype(BF16) for name, _ in SHARDED]
    gathered = _chip_allgather(sent, "weights_allgather")
    return {name: jnp.concatenate([g[q] for q in range(4)], axis=axis)
            for (name, axis), g in zip(SHARDED, gathered)}


def _pack_grads(grads, shards):
    pieces = []
    for name, axis in SHARDED:
        width = shards[name].shape[axis]
        for g in grads[name]:
            ax = axis - 1
            split = g.reshape(g.shape[:ax] + (4, width) + g.shape[ax + 1:])
            pieces.append(jnp.moveaxis(split, ax, 0).reshape(4, -1))
    n = sum(p.shape[1] for p in pieces)
    per = PACK_COLS * PACK_ROW_MULTIPLE
    total = -(-n // per) * per
    pieces.append(jnp.zeros((4, total - n), F32))
    return jnp.concatenate(pieces, axis=1).reshape(4, total // PACK_COLS, PACK_COLS)


def _unpack_reduced(reduced, shards):
    flat = reduced.reshape(-1)
    out, off = {}, 0
    for name, _ in SHARDED:
        shp = shards[name].shape
        size = math.prod(shp)
        out[name] = flat[off:off + size].reshape(shp)
        off += size
    return out


def _to_heads(t, nh):
    rows, width = t.shape
    return t.reshape(rows, nh, width // nh).transpose(1, 0, 2)


def _from_heads(t):
    nh, rows, d = t.shape
    return t.transpose(1, 0, 2).reshape(rows, nh * d)


def _rope_tables(rows):
    half = MLA_ROPE // 2
    freqs = jnp.power(ROPE_THETA, -jnp.arange(half, dtype=F32) / half)
    ang = jnp.arange(rows, dtype=F32)[:, None] * freqs[None, :]
    cos, sin = jnp.cos(ang), jnp.sin(ang)
    cos_q, sin_q = jnp.tile(cos, (1, MLA_HEADS)), jnp.tile(sin, (1, MLA_HEADS))
    zeros = jnp.zeros((rows, LANES - MLA_ROPE), F32)
    cos_k = jnp.concatenate([cos, cos, zeros], axis=1)
    sin_k = jnp.concatenate([sin, sin, zeros], axis=1)
    r = lax.broadcasted_iota(jnp.int32, (LANES, LANES), 0)
    c = lax.broadcasted_iota(jnp.int32, (LANES, LANES), 1)
    rot = (jnp.where((r == c + half) & (c < half), -1.0, 0.0)
           + jnp.where((c == r + half) & (r < half), 1.0, 0.0)).astype(F32)
    return cos_q, sin_q, cos_k, sin_k, rot


def _pad_cols(a, width):
    return jnp.pad(a, ((0, 0), (0, width - a.shape[1])))


def _prep_even_w_in(w):
    return _pad_cols(w, EVEN_IN_PAD)


_ODD_CUT = (1536, 1544, 2312, 2344)


def _prep_odd_w_in(w):
    c0, c1, c2, c3 = _ODD_CUT
    return jnp.concatenate([w[:, :c0], w[:, c1:c2], _pad_cols(w[:, c2:c3], LANES), _pad_cols(w[:, c0:c1], LANES)],
                           axis=1)


def _unprep_odd_w_in(g):
    c0, c1, c2, c3 = _ODD_CUT
    n1 = c0 + (c2 - c1)
    return jnp.concatenate([g[:, :c0], g[:, n1 + LANES:n1 + LANES + (c1 - c0)], g[:, c0:n1],
                            g[:, n1:n1 + (c3 - c2)]], axis=1)


def _prep_w_uq(w):
    r = w.reshape(w.shape[0], MLA_HEADS, MLA_NOPE + MLA_ROPE)
    half = MLA_ROPE // 2
    return jnp.concatenate([r[:, :, :MLA_NOPE].reshape(w.shape[0], -1),
                            r[:, :, MLA_NOPE:MLA_NOPE + half].reshape(w.shape[0], -1),
                            r[:, :, MLA_NOPE + half:].reshape(w.shape[0], -1)], axis=1)


def _unprep_w_uq(g):
    rows = g.shape[0]
    half = MLA_ROPE // 2
    nope = MLA_HEADS * MLA_NOPE
    return jnp.concatenate([g[:, :nope].reshape(rows, MLA_HEADS, MLA_NOPE),
                            g[:, nope:nope + LANES].reshape(rows, MLA_HEADS, half),
                            g[:, nope + LANES:].reshape(rows, MLA_HEADS, half)], axis=2).reshape(rows, -1)


def _row(v, width=None):
    v = v.reshape(1, -1)
    return v if width is None else _pad_cols(v, width)


def _even_forward(x_bf, w, i):
    rows = x_bf.shape[0]
    proj = _mm(x_bf, w["even_w_in"][i], "nn", F32, "even_proj")
    u, z, xbc, dtr = (proj[:, :512], proj[:, 512:1536], proj[:, 1536:3072], proj[:, 3072:3200])
    n_row = rows // ROW_TILE
    pool_p = (w["pool_w"][i], _row(w["pool_scale"][i]))
    (y_pool,), pool_c = _scan_fwd(_f_pool, "pool_fwd", n_row, [u], [], pool_p, [], [(16, POOL_WIDTH)],
                                  [(POOL_WIDTH, BF16)])
    conv_p = (w["conv_w"][i], _row(w["conv_b"][i]))
    (xa,), conv_c = _scan_fwd(_f_conv, "conv_fwd", n_row, [xbc], [], conv_p, [], [(8, SSM_CONV_DIM)],
                              [(SSM_CONV_DIM, F32)])
    ssd_p = (_row(w["dt_bias"][i], LANES), _row(w["a_log"][i], LANES), _row(w["d_skip"][i], LANES),
             _row(w["ssm_norm_w"][i]))
    (y_ssm,), ssd_c = _scan_fwd(_f_ssd, "ssd_fwd", rows // SSM_CHUNK, [xa, dtr, z], [], ssd_p, [],
                                [(SSM_D_INNER, SSM_STATE)], [(SSM_D_INNER, BF16)])
    mix = jnp.concatenate([y_pool, y_ssm], axis=1)
    saved = dict(u=u, z=z, xbc=xbc, dtr=dtr, xa=xa, mix=mix, pool_p=pool_p, pool_c=pool_c, conv_p=conv_p,
                 conv_c=conv_c, ssd_p=ssd_p, ssd_c=ssd_c)
    return mix, saved


def _even_backward(dmix, x_bf, sv, w, i, d_r1):
    rows = x_bf.shape[0]
    n_row = rows // ROW_TILE
    dy_pool, dy_ssm = dmix[:, :POOL_WIDTH], dmix[:, POOL_WIDTH:]
    (du,), (g_pool_w, g_pool_scale) = _scan_bwd(_f_pool, "pool_bwd", n_row, [sv["u"]], [], sv["pool_p"], [],
                                                sv["pool_c"], [dy_pool])
    (dxa, ddtr, dz), (g_dt_bias, g_a_log, g_d_skip, g_norm_w) = _scan_bwd(
        _f_ssd, "ssd_bwd", rows // SSM_CHUNK, [sv["xa"], sv["dtr"], sv["z"]], [], sv["ssd_p"], [], sv["ssd_c"],
        [dy_ssm])
    (dxbc,), (g_conv_w, g_conv_b) = _scan_bwd(_f_conv, "conv_bwd", n_row, [sv["xbc"]], [], sv["conv_p"], [],
                                              sv["conv_c"], [dxa])
    dproj = jnp.concatenate([du, dz, dxbc, ddtr], axis=1).astype(BF16)
    g_w_in = _mm(x_bf, dproj, "tn", F32, "even_dw_in")[:, :EVEN_IN]
    dx = _mm(dproj, w["even_w_in"][i], "nt", F32, "even_dx", extra=d_r1, alpha=ALPHA)
    grads = dict(even_w_in=g_w_in, pool_w=g_pool_w, pool_scale=g_pool_scale[0], conv_w=g_conv_w, conv_b=g_conv_b[0],
                 dt_bias=g_dt_bias[0, :SSM_HEADS], a_log=g_a_log[0, :SSM_HEADS], d_skip=g_d_skip[0, :SSM_HEADS],
                 ssm_norm_w=g_norm_w[0])
    return dx, grads


def _odd_forward(x_bf, w, i, tables):
    rows = x_bf.shape[0]
    cos_q, sin_q, cos_k, sin_k, rot = tables
    proj = _mm(x_bf, w["odd_w_in"][i], "nn", F32, "odd_proj")
    qf, kf, vf = (_to_heads((proj[:, j * 512:(j + 1) * 512] * sc).astype(BF16), FOX_HEADS)
                  for j, sc in enumerate((FOX_SCALE, 1.0, 1.0)))
    cq, ckv = proj[:, 1536:2048], proj[:, 2048:2304]
    kr, fl = proj[:, 2304:2432], proj[:, 2432:2560]
    n_row = rows // ROW_TILE
    fox_p = (_row(w["fgate_b"][i], LANES),)
    n_gate = rows // min(ATTN_TILE, rows)
    (fcum,), fox_c = _scan_fwd(_f_fox_gate, "fox_gate_fwd", n_gate, [fl], [], fox_p, [], [(8, LANES)], [(LANES, F32)])
    fc_heads = fcum[:, :FOX_HEADS].T
    fq, fk = fc_heads[:, :, None], fc_heads[:, None, :]
    o_fox, lse_fox = _attn_fwd(qf, kf, vf, fq, fk, "fox_attn_fwd")

    prep_p = (_row(w["q_norm_w"][i]), _row(w["kv_norm_w"][i]))
    (cqn, ckvn, krr), _ = _scan_fwd(_f_mla_prep, "mla_prep_fwd", n_row, [cq, ckv, kr], [cos_k, sin_k], prep_p,
                                    [rot], [], [(MLA_Q_RANK, BF16), (MLA_KV_RANK, BF16), (LANES, BF16)])
    q_flat = _mm(cqn, w["w_uq"][i], "nn", F32, "mla_q_up")
    (q_rope,), _ = _scan_fwd(_f_rope_q, "rope_q_fwd", n_row, [q_flat], [cos_q, sin_q], [], [], [],
                             [(q_flat.shape[1], BF16)])
    kv = _mm(ckvn, w["w_ukv"][i], "nn", BF16, "mla_kv_up")
    nope = MLA_HEADS * MLA_NOPE
    half = MLA_ROPE // 2
    q_m = jnp.concatenate([_to_heads(q_rope[:, :nope], MLA_HEADS), _to_heads(q_rope[:, nope:nope + LANES], MLA_HEADS),
                           _to_heads(q_rope[:, nope + LANES:], MLA_HEADS)], axis=2)
    kv_h = _to_heads(kv, MLA_HEADS)
    k_rope = jnp.broadcast_to(krr[None, :, :MLA_ROPE], (MLA_HEADS, rows, MLA_ROPE))
    k_m = jnp.concatenate([kv_h[:, :, :MLA_NOPE], k_rope], axis=2)
    v_m = kv_h[:, :, MLA_NOPE:]
    o_mla, lse_mla = _attn_fwd(q_m, k_m, v_m, None, None, "mla_attn_fwd")
    mix = jnp.concatenate([_from_heads(o_fox), _from_heads(o_mla)], axis=1)
    saved = dict(qf=qf, kf=kf, vf=vf, fq=fq, fk=fk, o_fox=o_fox, lse_fox=lse_fox, fl=fl, fox_p=fox_p, fox_c=fox_c,
                 cq=cq, ckv=ckv, kr=kr, prep_p=prep_p, cqn=cqn, ckvn=ckvn, q_flat=q_flat, q_m=q_m, k_m=k_m, v_m=v_m,
                 o_mla=o_mla, lse_mla=lse_mla, mix=mix)
    return mix, saved


def _odd_backward(dmix, x_bf, sv, w, i, d_r1, tables):
    rows = x_bf.shape[0]
    cos_q, sin_q, cos_k, sin_k, rot = tables
    n_row = rows // ROW_TILE
    nope = MLA_HEADS * MLA_NOPE
    half = MLA_ROPE // 2
    do_fox = _to_heads(dmix[:, :FOX_WIDTH], FOX_HEADS)
    do_mla = _to_heads(dmix[:, FOX_WIDTH:], MLA_HEADS)
    fox_args = (sv["qf"], sv["kf"], sv["vf"], sv["o_fox"], do_fox, sv["lse_fox"], sv["fq"], sv["fk"])
    dqf, delta_fox = _attn_bwd_dq(*fox_args, "fox_attn_dq")
    dkf, dvf, dfk = _attn_bwd_dkv(*fox_args[:3], delta_fox, *fox_args[4:], "fox_attn_dkv")
    dfcum = _pad_cols(dfk[:, 0, :].T, LANES)
    n_gate = rows // min(ATTN_TILE, rows)
    (dfl,), (g_fb,) = _scan_bwd(_f_fox_gate, "fox_gate_bwd", n_gate, [sv["fl"]], [], sv["fox_p"], [], sv["fox_c"],
                                [dfcum])
    mla_args = (sv["q_m"], sv["k_m"], sv["v_m"], sv["o_mla"], do_mla, sv["lse_mla"], None, None)
    dq_m, delta_mla = _attn_bwd_dq(*mla_args, "mla_attn_dq")
    dk_m, dv_m = _attn_bwd_dkv(*mla_args[:3], delta_mla, *mla_args[4:], "mla_attn_dkv")
    dq_rope = jnp.concatenate([_from_heads(dq_m[:, :, :MLA_NOPE]), _from_heads(dq_m[:, :, MLA_NOPE:MLA_NOPE + half]),
                               _from_heads(dq_m[:, :, MLA_NOPE + half:])], axis=1)
    (dq_flat,), _ = _scan_bwd(_f_rope_q, "rope_q_bwd", n_row, [sv["q_flat"]], [cos_q, sin_q], [], [], [], [dq_rope])
    g_w_uq = _mm(sv["cqn"], dq_flat, "tn", F32, "mla_dw_uq")
    dcqn = _mm(dq_flat, w["w_uq"][i], "nt", BF16, "mla_dcqn")
    dkv = _from_heads(jnp.concatenate([dk_m[:, :, :MLA_NOPE], dv_m], axis=2))
    g_w_ukv = _mm(sv["ckvn"], dkv, "tn", F32, "mla_dw_ukv")
    dckvn = _mm(dkv, w["w_ukv"][i], "nt", BF16, "mla_dckvn")
    dkrr = _head_sum(dk_m, "mla_dk_rope_sum")
    (dcq, dckv, dkr), (g_qw, g_kvw) = _scan_bwd(_f_mla_prep, "mla_prep_bwd", n_row, [sv["cq"], sv["ckv"], sv["kr"]],
                                                [cos_k, sin_k], sv["prep_p"], [rot], [], [dcqn, dckvn, dkrr])
    dproj = jnp.concatenate([_from_heads(dqf).astype(F32) * FOX_SCALE, _from_heads(dkf).astype(F32), _from_heads(dvf).astype(F32),
                             dcq, dckv, dkr, dfl], axis=1).astype(BF16)
    g_w_in = _mm(x_bf, dproj, "tn", F32, "odd_dw_in")
    dx = _mm(dproj, w["odd_w_in"][i], "nt", F32, "odd_dx", extra=d_r1, alpha=ALPHA)
    grads = dict(odd_w_in=_unprep_odd_w_in(g_w_in), fgate_b=g_fb[0, :FOX_HEADS], q_norm_w=g_qw[0], kv_norm_w=g_kvw[0],
                 w_uq=_unprep_w_uq(g_w_uq), w_ukv=g_w_ukv)
    return dx, grads


def _head_sum(dk_m, name):
    H, T, dk = dk_m.shape
    tt = _pick(T, (512, 256, 128))

    def body(d_ref, o_ref):
        acc = d_ref[0].astype(F32)
        for h in range(1, H):
            acc = acc + d_ref[h].astype(F32)
        o_ref[...] = jnp.concatenate([acc[:, MLA_NOPE:], jnp.zeros((tt, LANES - MLA_ROPE), F32)], axis=1).astype(BF16)

    return pl.pallas_call(
        body, name=name, grid=(T // tt,), in_specs=[pl.BlockSpec((H, tt, dk), lambda i: (0, i, 0))],
        out_specs=pl.BlockSpec((tt, LANES), lambda i: (i, 0)), out_shape=jax.ShapeDtypeStruct((T, LANES), BF16),
        compiler_params=_cparams(("parallel",)),
    )(dk_m)


def _local_step(x, target, w, small):
    rows = x.shape[0]
    n_row = rows // ROW_TILE
    tables = _rope_tables(rows)
    saved = []
    x_f32 = x
    x_bf = x.astype(BF16)
    for l in range(DEPTH):
        i = l // 2
        if l % 2 == 0:
            mix, sv = _even_forward(x_bf, w, i)
            w_out = w["even_w_out"][i]
        else:
            mix, sv = _odd_forward(x_bf, w, i, tables)
            w_out = w["odd_w_out"][i]
        r1 = _mm(mix, w_out, "nn", F32, "mix_out_even" if l % 2 == 0 else "mix_out_odd", extra=x_f32, alpha=ALPHA)
        ln1_p = (_row(small["ln_mix_g"][l]), _row(small["ln_mix_b"][l]))
        (x_mid,), _ = _scan_fwd(_f_ln, "ln_fwd", n_row, [r1], [], ln1_p, [], [], [(D_MODEL, F32)])
        x_mid_bf = x_mid.astype(BF16)
        gu = _mm(x_mid_bf, w["ffn_w_gu"][l], "nn", BF16, "ffn_gu")
        (act,), _ = _scan_fwd(_f_act, "ffn_act_fwd", n_row, [gu], [], [], [], [], [(D_FF, BF16)])
        r2 = _mm(act, w["ffn_w_down"][l], "nn", F32, "ffn_down", extra=x_mid, alpha=ALPHA)
        ln2_p = (_row(small["ln_ffn_g"][l]), _row(small["ln_ffn_b"][l]))
        (x_out,), _ = _scan_fwd(_f_ln, "ln_fwd", n_row, [r2], [], ln2_p, [], [], [(D_MODEL, F32)])
        saved.append(dict(sv=sv, x_bf=x_bf, r1=r1, ln1_p=ln1_p, x_mid_bf=x_mid_bf, gu=gu, act=act, r2=r2, ln2_p=ln2_p,
                          w_out=w_out))
        x_f32, x_bf = x_out, x_out.astype(BF16)

    dy, loss_part = _loss_head(x_f32, target, "loss_head")
    loss = 0.5 * jnp.sum(loss_part) / D_MODEL

    layer_grads = []
    for l in reversed(range(DEPTH)):
        i = l // 2
        s = saved[l]
        (d_r2,), (g_ln2_g, g_ln2_b) = _scan_bwd(_f_ln, "ln_bwd", n_row, [s["r2"]], [], s["ln2_p"], [], [], [dy])
        g_down = _mm(s["act"], d_r2, "tn", F32, "ffn_dw_down")
        dact = _mm(d_r2, w["ffn_w_down"][l], "nt", BF16, "ffn_dact")
        (dgu,), _ = _scan_bwd(_f_act, "ffn_act_bwd", n_row, [s["gu"]], [], [], [], [], [dact])
        g_gu = _mm(s["x_mid_bf"], dgu, "tn", F32, "ffn_dw_gu")
        dx_mid = _mm(dgu, w["ffn_w_gu"][l], "nt", F32, "ffn_dx", extra=d_r2, alpha=ALPHA)
        (d_r1,), (g_ln1_g, g_ln1_b) = _scan_bwd(_f_ln, "ln_bwd", n_row, [s["r1"]], [], s["ln1_p"], [], [], [dx_mid])
        g_w_out = _mm(s["sv"]["mix"], d_r1, "tn", F32, "even_dw_out" if l % 2 == 0 else "odd_dw_out")
        dmix = _mm(d_r1, s["w_out"], "nt", BF16, "even_dmix" if l % 2 == 0 else "odd_dmix")
        if l % 2 == 0:
            dy, g = _even_backward(dmix, s["x_bf"], s["sv"], w, i, d_r1)
            g["even_w_out"] = g_w_out
        else:
            dy, g = _odd_backward(dmix, s["x_bf"], s["sv"], w, i, d_r1, tables)
            g["odd_w_out"] = g_w_out
        g.update(ffn_w_gate=g_gu[:, :D_FF], ffn_w_up=g_gu[:, D_FF:], ffn_w_down=g_down, ln_mix_g=g_ln1_g[0],
                 ln_mix_b=g_ln1_b[0], ln_ffn_g=g_ln2_g[0], ln_ffn_b=g_ln2_b[0])
        layer_grads.append((l, g))
    return loss, dy, layer_grads


EVEN_NAMES = ("even_w_in", "pool_w", "pool_scale", "conv_w", "conv_b", "dt_bias", "a_log", "d_skip", "ssm_norm_w",
              "even_w_out")
ODD_NAMES = ("odd_w_in", "fgate_b", "q_norm_w", "w_uq", "kv_norm_w", "w_ukv", "odd_w_out")
PER_LAYER_NAMES = ("ffn_w_gate", "ffn_w_up", "ffn_w_down", "ln_mix_g", "ln_mix_b", "ln_ffn_g", "ln_ffn_b")
WEIGHT_NAMES = EVEN_NAMES + ODD_NAMES + PER_LAYER_NAMES


def _grads_by_name(layer_grads):
    by_layer = dict(layer_grads)
    out = {}
    for n in EVEN_NAMES:
        out[n] = [by_layer[l][n] for l in range(0, DEPTH, 2)]
    for n in ODD_NAMES:
        out[n] = [by_layer[l][n] for l in range(1, DEPTH, 2)]
    for n in PER_LAYER_NAMES:
        out[n] = [by_layer[l][n] for l in range(DEPTH)]
    return out


def _prepare_weights(full):
    w = {}
    w["even_w_in"] = [_prep_even_w_in(full["even_w_in"][i]) for i in range(2)]
    w["even_w_out"] = [full["even_w_out"][i] for i in range(2)]
    w["odd_w_in"] = [_prep_odd_w_in(full["odd_w_in"][i]) for i in range(2)]
    w["w_uq"] = [_prep_w_uq(full["w_uq"][i]) for i in range(2)]
    w["w_ukv"] = [full["w_ukv"][i] for i in range(2)]
    w["odd_w_out"] = [full["odd_w_out"][i] for i in range(2)]
    w["ffn_w_gu"] = [jnp.concatenate([full["ffn_w_gate"][l], full["ffn_w_up"][l]], axis=1) for l in range(DEPTH)]
    w["ffn_w_down"] = [full["ffn_w_down"][l] for l in range(DEPTH)]
    for n in ("conv_w", "q_norm_w", "kv_norm_w"):
        w[n] = full[n]
    return w


def _flatten_small(grads):
    flat = jnp.concatenate([g.reshape(-1) for n in REPLICATED for g in grads[n]])
    n = flat.shape[0]
    per = LANES * 8
    total = -(-n // per) * per
    return jnp.pad(flat, (0, total - n)).reshape(total // LANES, LANES)


def _unflatten_small(mat, like):
    flat = mat.reshape(-1)
    out, off = {}, 0
    for n in REPLICATED:
        size = math.prod(like[n].shape)
        out[n] = flat[off:off + size].reshape(like[n].shape)
        off += size
    return out


def kernel(x, even_w_in, pool_w, pool_scale, conv_w, conv_b, dt_bias, a_log, d_skip, ssm_norm_w, even_w_out, odd_w_in, fgate_b, q_norm_w, w_uq, kv_norm_w, w_ukv, odd_w_out, ffn_w_gate, ffn_w_up, ffn_w_down, ln_mix_g, ln_mix_b, ln_ffn_g, ln_ffn_b, loss_target, m_even_w_in, m_pool_w, m_pool_scale, m_conv_w, m_conv_b, m_dt_bias, m_a_log, m_d_skip, m_ssm_norm_w, m_even_w_out, m_odd_w_in, m_fgate_b, m_q_norm_w, m_w_uq, m_kv_norm_w, m_w_ukv, m_odd_w_out, m_ffn_w_gate, m_ffn_w_up, m_ffn_w_down, m_ln_mix_g, m_ln_mix_b, m_ln_ffn_g, m_ln_ffn_b, v_even_w_in, v_pool_w, v_pool_scale, v_conv_w, v_conv_b, v_dt_bias, v_a_log, v_d_skip, v_ssm_norm_w, v_even_w_out, v_odd_w_in, v_fgate_b, v_q_norm_w, v_w_uq, v_kv_norm_w, v_w_ukv, v_odd_w_out, v_ffn_w_gate, v_ffn_w_up, v_ffn_w_down, v_ln_mix_g, v_ln_mix_b, v_ln_ffn_g, v_ln_ffn_b):
    weights = dict(even_w_in=even_w_in, pool_w=pool_w, pool_scale=pool_scale, conv_w=conv_w, conv_b=conv_b,
                   dt_bias=dt_bias, a_log=a_log, d_skip=d_skip, ssm_norm_w=ssm_norm_w, even_w_out=even_w_out,
                   odd_w_in=odd_w_in, fgate_b=fgate_b, q_norm_w=q_norm_w, w_uq=w_uq, kv_norm_w=kv_norm_w, w_ukv=w_ukv,
                   odd_w_out=odd_w_out, ffn_w_gate=ffn_w_gate, ffn_w_up=ffn_w_up, ffn_w_down=ffn_w_down,
                   ln_mix_g=ln_mix_g, ln_mix_b=ln_mix_b, ln_ffn_g=ln_ffn_g, ln_ffn_b=ln_ffn_b)
    m_in = dict(even_w_in=m_even_w_in, pool_w=m_pool_w, pool_scale=m_pool_scale, conv_w=m_conv_w, conv_b=m_conv_b,
                dt_bias=m_dt_bias, a_log=m_a_log, d_skip=m_d_skip, ssm_norm_w=m_ssm_norm_w, even_w_out=m_even_w_out,
                odd_w_in=m_odd_w_in, fgate_b=m_fgate_b, q_norm_w=m_q_norm_w, w_uq=m_w_uq, kv_norm_w=m_kv_norm_w,
                w_ukv=m_w_ukv, odd_w_out=m_odd_w_out, ffn_w_gate=m_ffn_w_gate, ffn_w_up=m_ffn_w_up,
                ffn_w_down=m_ffn_w_down, ln_mix_g=m_ln_mix_g, ln_mix_b=m_ln_mix_b, ln_ffn_g=m_ln_ffn_g,
                ln_ffn_b=m_ln_ffn_b)
    v_in = dict(even_w_in=v_even_w_in, pool_w=v_pool_w, pool_scale=v_pool_scale, conv_w=v_conv_w, conv_b=v_conv_b,
                dt_bias=v_dt_bias, a_log=v_a_log, d_skip=v_d_skip, ssm_norm_w=v_ssm_norm_w, even_w_out=v_even_w_out,
                odd_w_in=v_odd_w_in, fgate_b=v_fgate_b, q_norm_w=v_q_norm_w, w_uq=v_w_uq, kv_norm_w=v_kv_norm_w,
                w_ukv=v_w_ukv, odd_w_out=v_odd_w_out, ffn_w_gate=v_ffn_w_gate, ffn_w_up=v_ffn_w_up,
                ffn_w_down=v_ffn_w_down, ln_mix_g=v_ln_mix_g, ln_mix_b=v_ln_mix_b, ln_ffn_g=v_ln_ffn_g,
                ln_ffn_b=v_ln_ffn_b)
    shards = {n: weights[n] for n, _ in SHARDED}

    w = _prepare_weights(_gather_weights(shards))
    small = {n: weights[n] for n in REPLICATED}
    w.update(small)

    loss_local, dx, layer_grads = _local_step(x[0], loss_target[0], w, small)
    grads_full = _grads_by_name(layer_grads)

    reduced = _reduce_scatter(_pack_grads(grads_full, shards))
    grads = _unpack_reduced(reduced, shards)
    small_sum = _all_reduce_small(_flatten_small(grads_full), "small_grads_allreduce")
    grads.update(_unflatten_small(small_sum, small))
    loss = lax.psum(loss_local, ("x", "y", "c"))

    deltas, new_m, new_v = {}, {}, {}
    for n in WEIGHT_NAMES:
        deltas[n], new_m[n], new_v[n] = _adamw(weights[n], grads[n], m_in[n], v_in[n], "adamw_" + n)
    return (loss, dx[None], *[grads[n] for n in WEIGHT_NAMES], *[deltas[n] for n in WEIGHT_NAMES],
            *[new_m[n] for n in WEIGHT_NAMES], *[new_v[n] for n in WEIGHT_NAMES])
```

```python
import functools
import math

import numpy as np
import jax
import jax.numpy as jnp
from jax import lax
from jax.experimental import pallas as pl
from jax.experimental.pallas import tpu as pltpu

F32 = jnp.float32
BF16 = jnp.bfloat16
HI = lax.Precision.HIGHEST
MESH_ID = pl.DeviceIdType.MESH

VMEM_LIMIT_BYTES = 56 * 1024 * 1024
LANES = 128

D_MODEL = 1024
DEPTH = 4
POOL_WINDOWS = (2, 4, 8, 16)
POOL_GROUP = 128
POOL_WIDTH = 512
SSM_D_INNER = 1024
SSM_HEAD_DIM = 64
SSM_HEADS = 16
SSM_GROUPS = 2
SSM_STATE = 128
SSM_CONV = 4
SSM_CHUNK = 128
SSM_CONV_DIM = 1536
EVEN_IN = 3088
EVEN_IN_PAD = 3200
FOX_HEADS = 8
FOX_WIDTH = 512
MLA_HEADS = 8
MLA_NOPE = 64
MLA_ROPE = 32
MLA_V = 64
MLA_Q_RANK = 512
MLA_KV_RANK = 256
ROPE_THETA = 10000.0
ODD_IN = 2344
ODD_IN_PAD = 2560
D_FF = 2816
ALPHA = (2 * DEPTH) ** 0.25
LN_EPS = 1e-5
RMS_EPS = 1e-6
ADAM_LR = 0.001
ADAM_B1 = 0.9
ADAM_B2 = 0.999
ADAM_EPS = 1e-08
ADAM_WD = 0.01
ADAM_STEP = 10
NEG_BIG = -1e30

ATTN_TILE = 512
ATTN_WIDE = 2048
FOX_SCALE = 0.125
MLA_SCALE = (MLA_NOPE + MLA_ROPE) ** -0.5
ROW_TILE = 256


def _cparams(sem=None):
    return pltpu.CompilerParams(dimension_semantics=sem, vmem_limit_bytes=VMEM_LIMIT_BYTES)


def _pick(d, prefs):
    for p in prefs:
        if d % p == 0:
            return p
    return d


_M_PREFS = (2048, 1024, 512, 640, 1408, 768, 384, 256, 128)
_N_PREFS = (1024, 512, 640, 1408, 768, 384, 256, 128)
_K_PREFS = (1024, 512, 640, 1408, 768, 256, 128)
MM_MAX_ACC_ELEMS = 1408 * 1024


def _mm(a, b, mode, out_dtype, name, extra=None, alpha=1.0):
    if mode == "nn":
        (m, k), (k2, n) = a.shape, b.shape
    elif mode == "nt":
        (m, k), (n, k2) = a.shape, b.shape
    else:
        (k, m), (k2, n) = a.shape, b.shape
    assert k == k2, (a.shape, b.shape, mode)
    tn, tk = _pick(n, _N_PREFS), _pick(k, _K_PREFS)
    tm = _pick(m, tuple(p for p in _M_PREFS if p * tn <= MM_MAX_ACC_ELEMS))
    nk = k // tk
    if mode == "nn":
        a_spec = pl.BlockSpec((tm, tk), lambda i, j, kk: (i, kk))
        b_spec = pl.BlockSpec((tk, tn), lambda i, j, kk: (kk, j))
        dims = (((1,), (0,)), ((), ()))
    elif mode == "nt":
        a_spec = pl.BlockSpec((tm, tk), lambda i, j, kk: (i, kk))
        b_spec = pl.BlockSpec((tn, tk), lambda i, j, kk: (j, kk))
        dims = (((1,), (1,)), ((), ()))
    else:
        a_spec = pl.BlockSpec((tk, tm), lambda i, j, kk: (kk, i))
        b_spec = pl.BlockSpec((tk, tn), lambda i, j, kk: (kk, j))
        dims = (((0,), (0,)), ((), ()))
    o_spec = pl.BlockSpec((tm, tn), lambda i, j, kk: (i, j))
    has_extra = extra is not None

    def body(*refs):
        if has_extra:
            a_ref, b_ref, e_ref, o_ref, acc = refs
        else:
            a_ref, b_ref, o_ref, acc = refs
        kk = pl.program_id(2)

        @pl.when(kk == 0)
        def _():
            acc[...] = jnp.zeros_like(acc)

        acc[...] += lax.dot_general(a_ref[...].astype(BF16), b_ref[...].astype(BF16), dims,
                                    preferred_element_type=F32)

        @pl.when(kk == nk - 1)
        def _():
            r = acc[...]
            if has_extra:
                r = r + alpha * e_ref[...].astype(F32)
            o_ref[...] = r.astype(o_ref.dtype)

    ins = [a, b] + ([extra] if has_extra else [])
    specs = [a_spec, b_spec] + ([o_spec] if has_extra else [])
    return pl.pallas_call(
        body, name=name, grid=(m // tm, n // tn, nk), in_specs=specs, out_specs=o_spec,
        out_shape=jax.ShapeDtypeStruct((m, n), out_dtype),
        scratch_shapes=[pltpu.VMEM((tm, tn), F32)],
        compiler_params=_cparams(("parallel", "parallel", "arbitrary")),
    )(*ins)


def _full_spec(shape):
    nd = len(shape)
    return pl.BlockSpec(tuple(shape), lambda i, _nd=nd: (0,) * _nd)


def _scan_fwd(f, name, n, tiles, ctiles, params, cparams, carry_shapes, out_defs):
    rows = tiles[0].shape[0]
    tt = rows // n
    nt, nct, npar, ncp, ncar, nout = len(tiles), len(ctiles), len(params), len(cparams), len(carry_shapes), len(out_defs)

    def body(*refs):
        pos = 0
        t_refs = refs[pos:pos + nt]; pos += nt
        ct_refs = refs[pos:pos + nct]; pos += nct
        p_refs = refs[pos:pos + npar]; pos += npar
        cp_refs = refs[pos:pos + ncp]; pos += ncp
        o_refs = refs[pos:pos + nout]; pos += nout
        cs_refs = refs[pos:pos + ncar]; pos += ncar
        c_scr = refs[pos:pos + ncar]
        i = pl.program_id(0)

        @pl.when(i == 0)
        def _():
            for c in c_scr:
                c[...] = jnp.zeros_like(c)

        carry = tuple(c[...] for c in c_scr)
        for s, c in zip(cs_refs, carry):
            s[0] = c
        new_carry, outs = f(carry, tuple(r[...] for r in t_refs), tuple(r[...] for r in ct_refs),
                            tuple(r[...] for r in p_refs), tuple(r[...] for r in cp_refs), i)
        for o_ref, o in zip(o_refs, outs):
            o_ref[...] = o
        for c, v in zip(c_scr, new_carry):
            c[...] = v

    tile_spec = lambda arr: pl.BlockSpec((tt, arr.shape[1]), lambda i: (i, 0))
    in_specs = ([tile_spec(t) for t in tiles] + [tile_spec(t) for t in ctiles]
                + [_full_spec(p.shape) for p in params] + [_full_spec(p.shape) for p in cparams])
    out_specs = ([pl.BlockSpec((tt, c), lambda i: (i, 0)) for c, _ in out_defs]
                 + [pl.BlockSpec((1,) + tuple(s), lambda i: (i, 0, 0)) for s in carry_shapes])
    out_shape = ([jax.ShapeDtypeStruct((rows, c), dt) for c, dt in out_defs]
                 + [jax.ShapeDtypeStruct((n,) + tuple(s), F32) for s in carry_shapes])
    res = pl.pallas_call(
        body, name=name, grid=(n,), in_specs=in_specs, out_specs=out_specs, out_shape=out_shape,
        scratch_shapes=[pltpu.VMEM(tuple(s), F32) for s in carry_shapes],
        compiler_params=_cparams(("arbitrary",)),
    )(*tiles, *ctiles, *params, *cparams)
    return list(res[:nout]), list(res[nout:])


def _scan_bwd(f, name, n, tiles, ctiles, params, cparams, carries, douts):
    rows = tiles[0].shape[0]
    tt = rows // n
    nt, nct, npar, ncp, ncar, nout = len(tiles), len(ctiles), len(params), len(cparams), len(carries), len(douts)

    def body(*refs):
        pos = 0
        t_refs = refs[pos:pos + nt]; pos += nt
        ct_refs = refs[pos:pos + nct]; pos += nct
        p_refs = refs[pos:pos + npar]; pos += npar
        cp_refs = refs[pos:pos + ncp]; pos += ncp
        cs_refs = refs[pos:pos + ncar]; pos += ncar
        do_refs = refs[pos:pos + nout]; pos += nout
        dt_refs = refs[pos:pos + nt]; pos += nt
        dp_refs = refs[pos:pos + npar]; pos += npar
        dc_scr = refs[pos:pos + ncar]
        i = pl.program_id(0)

        @pl.when(i == 0)
        def _():
            for c in dc_scr:
                c[...] = jnp.zeros_like(c)
            for d in dp_refs:
                d[...] = jnp.zeros_like(d)

        ctv = tuple(r[...] for r in ct_refs)
        cpv = tuple(r[...] for r in cp_refs)

        def g(c, t, p):
            return f(c, t, ctv, p, cpv, n - 1 - i)

        _, vjp = jax.vjp(g, tuple(s[0] for s in cs_refs), tuple(r[...] for r in t_refs),
                         tuple(r[...] for r in p_refs))
        dc, dt, dp = vjp((tuple(c[...] for c in dc_scr), tuple(r[...] for r in do_refs)))
        for r, v in zip(dt_refs, dt):
            r[...] = v
        for r, v in zip(dp_refs, dp):
            r[...] += v
        for c, v in zip(dc_scr, dc):
            c[...] = v

    rev_tile = lambda arr: pl.BlockSpec((tt, arr.shape[1]), lambda i: (n - 1 - i, 0))
    in_specs = ([rev_tile(t) for t in tiles] + [rev_tile(t) for t in ctiles]
                + [_full_spec(p.shape) for p in params] + [_full_spec(p.shape) for p in cparams]
                + [pl.BlockSpec((1,) + tuple(c.shape[1:]), lambda i: (n - 1 - i, 0, 0)) for c in carries]
                + [rev_tile(d) for d in douts])
    out_specs = [rev_tile(t) for t in tiles] + [_full_spec(p.shape) for p in params]
    out_shape = ([jax.ShapeDtypeStruct(t.shape, t.dtype) for t in tiles]
                 + [jax.ShapeDtypeStruct(p.shape, F32) for p in params])
    res = pl.pallas_call(
        body, name=name, grid=(n,), in_specs=in_specs, out_specs=out_specs, out_shape=out_shape,
        scratch_shapes=[pltpu.VMEM(tuple(c.shape[1:]), F32) for c in carries],
        compiler_params=_cparams(("arbitrary",)),
    )(*tiles, *ctiles, *params, *cparams, *carries, *douts)
    return list(res[:nt]), list(res[nt:])


def _silu(x):
    return x * jax.nn.sigmoid(x)


def _softplus(x):
    return jnp.maximum(x, 0.0) + jnp.log(1.0 + jnp.exp(-jnp.abs(x)))


def _f_ln(carry, tiles, ctiles, params, cparams, idx):
    (r,), (g, b) = tiles, params
    mu = jnp.mean(r, axis=-1, keepdims=True)
    xc = r - mu
    var = jnp.mean(xc * xc, axis=-1, keepdims=True)
    return (), (xc * lax.rsqrt(var + LN_EPS) * g + b,)


def _f_act(carry, tiles, ctiles, params, cparams, idx):
    (gu,) = tiles
    g = gu[:, :D_FF].astype(F32)
    u = gu[:, D_FF:].astype(F32)
    return (), ((_silu(g) * u).astype(BF16),)


def _f_pool(carry, tiles, ctiles, params, cparams, idx):
    (prev,), (u,), (pool_w, pool_scale) = carry, tiles, params
    tt = u.shape[0]
    halo = prev.shape[0]
    ext = jnp.concatenate([prev, u], axis=0)
    pos = idx * tt + lax.broadcasted_iota(jnp.int32, (tt, 1), 0)
    ys = []
    for g, w in enumerate(POOL_WINDOWS):
        lo, hi = g * POOL_GROUP, (g + 1) * POOL_GROUP
        eg = ext[:, lo:hi]
        s = eg[halo:halo + tt]
        for j in range(1, w):
            s = s + eg[halo - j:halo - j + tt]
        count = jnp.minimum(pos + 1, w).astype(F32)
        diff = s / count - u[:, lo:hi]
        ys.append(jnp.dot(diff.astype(BF16), pool_w[g].astype(BF16), preferred_element_type=F32))
    y = jnp.concatenate(ys, axis=1) * pool_scale
    return (u[tt - halo:, :],), (y.astype(BF16),)


def _f_conv(carry, tiles, ctiles, params, cparams, idx):
    (prev,), (xbc,), (conv_w, conv_b) = carry, tiles, params
    tt = xbc.shape[0]
    halo = prev.shape[0]
    ext = jnp.concatenate([prev, xbc], axis=0)
    acc = jnp.zeros_like(xbc) + conv_b
    for k in range(SSM_CONV):
        off = halo - (SSM_CONV - 1) + k
        acc = acc + ext[off:off + tt] * conv_w[k:k + 1, :]
    return (xbc[tt - halo:, :],), (_silu(acc),)


def _f_ssd(carry, tiles, ctiles, params, cparams, idx):
    (state,), (xa, dtr, z), (dt_bias, a_log, d_skip, norm_w) = carry, tiles, params
    L, P, N, E = SSM_CHUNK, SSM_HEAD_DIM, SSM_STATE, SSM_HEADS // SSM_GROUPS
    dt = _softplus(dtr + dt_bias)
    da = dt * (-jnp.exp(a_log))
    r = lax.broadcasted_iota(jnp.int32, (L, L), 0)
    c = lax.broadcasted_iota(jnp.int32, (L, L), 1)
    tri = c <= r
    acs = jnp.dot(tri.astype(F32), da, precision=HI, preferred_element_type=F32)
    acs_t = acs.T
    ys, new_states = [], []
    for g in range(SSM_GROUPS):
        bg = xa[:, SSM_D_INNER + g * N:SSM_D_INNER + (g + 1) * N].astype(BF16)
        cg = xa[:, SSM_D_INNER + (SSM_GROUPS + g) * N:SSM_D_INNER + (SSM_GROUPS + g + 1) * N].astype(BF16)
        cb = lax.dot_general(cg, bg, (((1,), (1,)), ((), ())), preferred_element_type=F32)
        for e in range(E):
            h = g * E + e
            xh = xa[:, h * P:(h + 1) * P]
            col = acs[:, h:h + 1]
            row = acs_t[h:h + 1, :]
            last = acs[L - 1:L, h:h + 1]
            lmat = jnp.exp(jnp.where(tri, col - row, NEG_BIG))
            xdt_f = xh * dt[:, h:h + 1]
            y_diag = jnp.dot((cb * lmat).astype(BF16), xdt_f.astype(BF16), preferred_element_type=F32)
            xdec = (xdt_f * jnp.exp(last - col)).astype(BF16)
            st = lax.dot_general(xdec, bg, (((0,), (0,)), ((), ())), preferred_element_type=F32)
            prev = state[h * P:(h + 1) * P, :]
            y_off = lax.dot_general(cg, prev.astype(BF16), (((1,), (1,)), ((), ())),
                                    preferred_element_type=F32) * jnp.exp(col)
            new_states.append(prev * jnp.exp(last) + st)
            ys.append(y_diag + y_off + xh * d_skip[:, h:h + 1])
    y = jnp.concatenate(ys, axis=1) * _silu(z)
    y = y * lax.rsqrt(jnp.mean(y * y, axis=-1, keepdims=True) + RMS_EPS) * norm_w
    return (jnp.concatenate(new_states, axis=0),), (y.astype(BF16),)


def _f_fox_gate(carry, tiles, ctiles, params, cparams, idx):
    (run,), (fl,), (fb,) = carry, tiles, params
    tt = fl.shape[0]
    logf = -_softplus(-(fl + fb))
    r = lax.broadcasted_iota(jnp.int32, (tt, tt), 0)
    c = lax.broadcasted_iota(jnp.int32, (tt, tt), 1)
    cum = jnp.dot((c <= r).astype(F32), logf, precision=HI, preferred_element_type=F32) + run[0:1, :]
    return (jnp.broadcast_to(cum[tt - 1:tt, :], run.shape),), (cum,)


def _rms(x, w):
    return x * lax.rsqrt(jnp.mean(x * x, axis=-1, keepdims=True) + RMS_EPS) * w


def _f_mla_prep(carry, tiles, ctiles, params, cparams, idx):
    (cq, ckv, kr), (ck, sk), (qw, kvw), (rk,) = tiles, ctiles, params, cparams
    rot = jnp.dot(kr, rk, precision=HI, preferred_element_type=F32)
    return (), (_rms(cq, qw).astype(BF16), _rms(ckv, kvw).astype(BF16), (kr * ck + rot * sk).astype(BF16))


def _f_rope_q(carry, tiles, ctiles, params, cparams, idx):
    (q,), (cos, sin) = tiles, ctiles
    nope = MLA_HEADS * MLA_NOPE
    x1 = q[:, nope:nope + LANES]
    x2 = q[:, nope + LANES:]
    roped = jnp.concatenate([q[:, :nope], x1 * cos - x2 * sin, x2 * cos + x1 * sin], axis=1)
    return (), ((roped * MLA_SCALE).astype(BF16),)


def _scores(q, k, fq, fk, masked, row0, col0):
    s = lax.dot_general(q, k, (((1,), (1,)), ((), ())), preferred_element_type=F32)
    if fq is not None:
        s = s + fq - fk
    if masked:
        rows = row0 + lax.broadcasted_iota(jnp.int32, s.shape, 0)
        cols = col0 + lax.broadcasted_iota(jnp.int32, s.shape, 1)
        s = jnp.where(cols <= rows, s, NEG_BIG)
    return s


def _q_major_tables(T, tq, tk):
    r = tk // tq
    qi = np.concatenate([np.full(i // r + 1, i, np.int32) for i in range(T // tq)])
    ki = np.concatenate([np.arange(i // r + 1, dtype=np.int32) for i in range(T // tq)])
    kind = np.where(ki == qi // r, qi % r + 1, 0).astype(np.int32)
    return [jnp.asarray(a) for a in (qi, ki, (ki == 0).astype(np.int32), kind)]


def _k_major_tables(T, tq, tk):
    r = tq // tk
    nq = T // tq
    ki = np.concatenate([np.full(nq - i // r, i, np.int32) for i in range(T // tk)])
    qi = np.concatenate([np.arange(i // r, nq, dtype=np.int32) for i in range(T // tk)])
    kind = np.where(qi == ki // r, r - ki % r, 0).astype(np.int32)
    return [jnp.asarray(a) for a in (ki, qi, (qi == nq - 1).astype(np.int32), kind)]


def _attn_tiles(T):
    return min(ATTN_TILE, T), min(ATTN_WIDE, T)


def _attn_fwd(q, k, v, fq, fk, name):
    H, T, dk = q.shape
    dv = v.shape[2]
    tq, tk = _attn_tiles(T)
    decay = fq is not None
    tables = _q_major_tables(T, tq, tk)

    def body(qi_ref, ki_ref, first_ref, kind_ref, *refs):
        if decay:
            q_ref, k_ref, v_ref, fq_ref, fk_ref, o_ref, lse_ref, m_s, l_s, acc = refs
        else:
            q_ref, k_ref, v_ref, o_ref, lse_ref, m_s, l_s, acc = refs
        t = pl.program_id(1)
        qi, ki = qi_ref[t], ki_ref[t]

        @pl.when(first_ref[t] == 1)
        def _():
            m_s[...] = jnp.full_like(m_s, NEG_BIG)
            l_s[...] = jnp.zeros_like(l_s)
            acc[...] = jnp.zeros_like(acc)

        def step(kind):
            w = tk if kind == 0 else kind * tq
            s = _scores(q_ref[0], k_ref[0, :w, :], fq_ref[0] if decay else None,
                        fk_ref[0, :, :w] if decay else None, kind > 0, qi * tq, ki * tk)
            m_new = jnp.maximum(m_s[...], jnp.max(s, axis=-1, keepdims=True))
            p = jnp.exp(s - m_new)
            corr = jnp.exp(m_s[...] - m_new)
            l_s[...] = corr * l_s[...] + jnp.sum(p, axis=-1, keepdims=True)
            acc[...] = corr * acc[...] + jnp.dot(p.astype(BF16), v_ref[0, :w, :], preferred_element_type=F32)
            m_s[...] = m_new
            if kind > 0:
                o_ref[0] = (acc[...] / l_s[...]).astype(o_ref.dtype)
                lse_ref[0] = m_s[...] + jnp.log(l_s[...])

        for kind in range(tk // tq + 1):
            pl.when(kind_ref[t] == kind)(functools.partial(step, kind))

    qspec = lambda d: pl.BlockSpec((1, tq, d), lambda h, t, qi, ki, fi, la: (h, qi[t], 0))
    kspec = lambda d: pl.BlockSpec((1, tk, d), lambda h, t, qi, ki, fi, la: (h, ki[t], 0))
    in_specs = [qspec(dk), kspec(dk), kspec(dv)]
    ins = [q, k, v]
    if decay:
        in_specs += [qspec(1), pl.BlockSpec((1, 1, tk), lambda h, t, qi, ki, fi, la: (h, 0, ki[t]))]
        ins += [fq, fk]
    grid_spec = pltpu.PrefetchScalarGridSpec(
        num_scalar_prefetch=4, grid=(H, int(tables[0].shape[0])), in_specs=in_specs,
        out_specs=[qspec(dv), qspec(1)],
        scratch_shapes=[pltpu.VMEM((tq, 1), F32), pltpu.VMEM((tq, 1), F32), pltpu.VMEM((tq, dv), F32)])
    return pl.pallas_call(
        body, name=name, grid_spec=grid_spec,
        out_shape=[jax.ShapeDtypeStruct((H, T, dv), BF16), jax.ShapeDtypeStruct((H, T, 1), F32)],
        compiler_params=_cparams(("parallel", "arbitrary")),
    )(*tables, *ins)


def _attn_bwd_dq(q, k, v, o, do, lse, fq, fk, name):
    H, T, dk = q.shape
    dv = v.shape[2]
    tq, tk = _attn_tiles(T)
    decay = fq is not None
    tables = _q_major_tables(T, tq, tk)

    def body(qi_ref, ki_ref, first_ref, kind_ref, *refs):
        if decay:
            q_ref, k_ref, v_ref, o_ref, do_ref, lse_ref, fq_ref, fk_ref, dq_ref, dl_ref, acc, dl, leak = refs
        else:
            q_ref, k_ref, v_ref, o_ref, do_ref, lse_ref, dq_ref, dl_ref, acc, dl, leak = refs
        t = pl.program_id(1)
        qi, ki = qi_ref[t], ki_ref[t]

        @pl.when(first_ref[t] == 1)
        def _():
            acc[...] = jnp.zeros_like(acc)
            leak[...] = jnp.zeros_like(leak)
            dl[...] = jnp.sum(do_ref[0].astype(F32) * o_ref[0].astype(F32), axis=-1, keepdims=True)

        def step(kind):
            w = tk if kind == 0 else kind * tq
            k_v = k_ref[0, :w, :]
            s = _scores(q_ref[0], k_v, fq_ref[0] if decay else None, fk_ref[0, :, :w] if decay else None,
                        kind > 0, qi * tq, ki * tk)
            p = jnp.exp(s - lse_ref[0])
            dp = lax.dot_general(do_ref[0], v_ref[0, :w, :], (((1,), (1,)), ((), ())), preferred_element_type=F32)
            ds = p * (dp - dl[...])
            leak[...] += jnp.sum(ds, axis=-1, keepdims=True)
            acc[...] += jnp.dot(ds.astype(BF16), k_v, preferred_element_type=F32)
            if kind > 0:
                dq_ref[0] = acc[...].astype(dq_ref.dtype)
                dl_ref[0] = dl[...] + leak[...]

        for kind in range(tk // tq + 1):
            pl.when(kind_ref[t] == kind)(functools.partial(step, kind))

    qspec = lambda d: pl.BlockSpec((1, tq, d), lambda h, t, qi, ki, fi, la: (h, qi[t], 0))
    kspec = lambda d: pl.BlockSpec((1, tk, d), lambda h, t, qi, ki, fi, la: (h, ki[t], 0))
    in_specs = [qspec(dk), kspec(dk), kspec(dv), qspec(dv), qspec(dv), qspec(1)]
    ins = [q, k, v, o, do, lse]
    if decay:
        in_specs += [qspec(1), pl.BlockSpec((1, 1, tk), lambda h, t, qi, ki, fi, la: (h, 0, ki[t]))]
        ins += [fq, fk]
    grid_spec = pltpu.PrefetchScalarGridSpec(
        num_scalar_prefetch=4, grid=(H, int(tables[0].shape[0])), in_specs=in_specs,
        out_specs=[qspec(dk), qspec(1)],
        scratch_shapes=[pltpu.VMEM((tq, dk), F32), pltpu.VMEM((tq, 1), F32), pltpu.VMEM((tq, 1), F32)])
    return pl.pallas_call(
        body, name=name, grid_spec=grid_spec,
        out_shape=[jax.ShapeDtypeStruct((H, T, dk), BF16), jax.ShapeDtypeStruct((H, T, 1), F32)],
        compiler_params=_cparams(("parallel", "arbitrary")),
    )(*tables, *ins)


def _attn_bwd_dkv(q, k, v, delta, do, lse, fq, fk, name):
    H, T, dk = q.shape
    dv = v.shape[2]
    tk, tq = _attn_tiles(T)
    decay = fq is not None
    tables = _k_major_tables(T, tq, tk)

    def body(ki_ref, qi_ref, last_ref, kind_ref, *refs):
        if decay:
            q_ref, k_ref, v_ref, dl_ref, do_ref, lse_ref, fq_ref, fk_ref, dk_ref, dv_ref, df_ref, dk_s, dv_s, df_s = refs
        else:
            q_ref, k_ref, v_ref, dl_ref, do_ref, lse_ref, dk_ref, dv_ref, dk_s, dv_s = refs
        t = pl.program_id(1)
        ki, qi = ki_ref[t], qi_ref[t]

        @pl.when(kind_ref[t] > 0)
        def _():
            dk_s[...] = jnp.zeros_like(dk_s)
            dv_s[...] = jnp.zeros_like(dv_s)
            if decay:
                df_s[...] = jnp.zeros_like(df_s)

        def step(kind):
            off = 0 if kind == 0 else tq - kind * tk
            q_v, do_v = q_ref[0, off:, :], do_ref[0, off:, :]
            s = _scores(q_v, k_ref[0], fq_ref[0, off:, :] if decay else None, fk_ref[0] if decay else None,
                        kind > 0, qi * tq + off, ki * tk)
            p = jnp.exp(s - lse_ref[0, off:, :])
            dv_s[...] += lax.dot_general(p.astype(BF16), do_v, (((0,), (0,)), ((), ())), preferred_element_type=F32)
            dp = lax.dot_general(do_v, v_ref[0], (((1,), (1,)), ((), ())), preferred_element_type=F32)
            ds = p * (dp - dl_ref[0, off:, :])
            dk_s[...] += lax.dot_general(ds.astype(BF16), q_v, (((0,), (0,)), ((), ())),
                                         preferred_element_type=F32)
            if decay:
                df_s[...] -= jnp.sum(ds, axis=0, keepdims=True)

        for kind in range(tq // tk + 1):
            pl.when(kind_ref[t] == kind)(functools.partial(step, kind))

        @pl.when(last_ref[t] == 1)
        def _():
            dk_ref[0] = dk_s[...].astype(dk_ref.dtype)
            dv_ref[0] = dv_s[...].astype(dv_ref.dtype)
            if decay:
                df_ref[0] = df_s[...]

    qspec = lambda d: pl.BlockSpec((1, tq, d), lambda h, t, ki, qi, fi, la: (h, qi[t], 0))
    kspec = lambda d: pl.BlockSpec((1, tk, d), lambda h, t, ki, qi, fi, la: (h, ki[t], 0))
    in_specs = [qspec(dk), kspec(dk), kspec(dv), qspec(1), qspec(dv), qspec(1)]
    ins = [q, k, v, delta, do, lse]
    out_specs = [kspec(dk), kspec(dv)]
    out_shape = [jax.ShapeDtypeStruct((H, T, dk), BF16), jax.ShapeDtypeStruct((H, T, dv), BF16)]
    scratch = [pltpu.VMEM((tk, dk), F32), pltpu.VMEM((tk, dv), F32)]
    if decay:
        fkspec = pl.BlockSpec((1, 1, tk), lambda h, t, ki, qi, fi, la: (h, 0, ki[t]))
        in_specs += [qspec(1), fkspec]
        ins += [fq, fk]
        out_specs.append(fkspec)
        out_shape.append(jax.ShapeDtypeStruct((H, 1, T), F32))
        scratch.append(pltpu.VMEM((1, tk), F32))
    grid_spec = pltpu.PrefetchScalarGridSpec(
        num_scalar_prefetch=4, grid=(H, int(tables[0].shape[0])), in_specs=in_specs, out_specs=out_specs,
        scratch_shapes=scratch)
    return pl.pallas_call(
        body, name=name, grid_spec=grid_spec, out_shape=out_shape, compiler_params=_cparams(("parallel", "arbitrary")),
    )(*tables, *ins)


def _attn_bwd_fused(q, k, v, o, do, lse, name):
    H, T, dk = q.shape
    dv = v.shape[2]
    tk, tq = _attn_tiles(T)
    tables = _k_major_tables(T, tq, tk)
    npairs = int(tables[0].shape[0])

    def body(ki_ref, qi_ref, last_ref, kind_ref, q_ref, k_ref, v_ref, o_ref, do_ref, lse_ref,
             dq_ref, dk_ref, dv_ref, dq_s, dk_s, dv_s):
        t = pl.program_id(1)
        ki, qi = ki_ref[t], qi_ref[t]

        @pl.when(t == 0)
        def _():
            dq_s[...] = jnp.zeros_like(dq_s)

        @pl.when(kind_ref[t] > 0)
        def _():
            dk_s[...] = jnp.zeros_like(dk_s)
            dv_s[...] = jnp.zeros_like(dv_s)

        def step(kind):
            off = 0 if kind == 0 else tq - kind * tk
            q_v, do_v, k_v = q_ref[0, off:, :], do_ref[0, off:, :], k_ref[0]
            s = _scores(q_v, k_v, None, None, kind > 0, qi * tq + off, ki * tk)
            p = jnp.exp(s - lse_ref[0, off:, :])
            delta = jnp.sum(do_v.astype(F32) * o_ref[0, off:, :].astype(F32), axis=-1, keepdims=True)
            dv_s[...] += lax.dot_general(p.astype(BF16), do_v, (((0,), (0,)), ((), ())), preferred_element_type=F32)
            dp = lax.dot_general(do_v, v_ref[0], (((1,), (1,)), ((), ())), preferred_element_type=F32)
            ds = (p * (dp - delta)).astype(BF16)
            dk_s[...] += lax.dot_general(ds, q_v, (((0,), (0,)), ((), ())), preferred_element_type=F32)
            rows = pl.ds(pl.multiple_of(qi * tq + off, tk), tq - off)
            dq_s[rows, :] += jnp.dot(ds, k_v, preferred_element_type=F32)

        for kind in range(tq // tk + 1):
            pl.when(kind_ref[t] == kind)(functools.partial(step, kind))

        @pl.when(last_ref[t] == 1)
        def _():
            dk_ref[0] = dk_s[...].astype(dk_ref.dtype)
            dv_ref[0] = dv_s[...].astype(dv_ref.dtype)

        @pl.when(t == npairs - 1)
        def _():
            dq_ref[0] = dq_s[...].astype(dq_ref.dtype)

    qspec = lambda d: pl.BlockSpec((1, tq, d), lambda h, t, ki, qi, la, kd: (h, qi[t], 0))
    kspec = lambda d: pl.BlockSpec((1, tk, d), lambda h, t, ki, qi, la, kd: (h, ki[t], 0))
    grid_spec = pltpu.PrefetchScalarGridSpec(
        num_scalar_prefetch=4, grid=(H, npairs),
        in_specs=[qspec(dk), kspec(dk), kspec(dv), qspec(dv), qspec(dv), qspec(1)],
        out_specs=[pl.BlockSpec((1, T, dk), lambda h, t, ki, qi, la, kd: (h, 0, 0)), kspec(dk), kspec(dv)],
        scratch_shapes=[pltpu.VMEM((T, dk), F32), pltpu.VMEM((tk, dk), F32), pltpu.VMEM((tk, dv), F32)])
    return pl.pallas_call(
        body, name=name, grid_spec=grid_spec,
        out_shape=[jax.ShapeDtypeStruct((H, T, dk), BF16), jax.ShapeDtypeStruct((H, T, dk), BF16),
                   jax.ShapeDtypeStruct((H, T, dv), BF16)],
        compiler_params=_cparams(("parallel", "arbitrary")),
    )(*tables, q, k, v, o, do, lse)


def _loss_head(y, target, name):
    rows, d = y.shape
    tt = _pick(rows, (512, 256, 128))

    def body(y_ref, t_ref, dy_ref, part_ref):
        @pl.when(pl.program_id(0) == 0)
        def _():
            part_ref[...] = jnp.zeros_like(part_ref)

        err = y_ref[...] - t_ref[...]
        dy_ref[...] = err * (1.0 / d)
        sq = jnp.sum(err * err, axis=0, keepdims=True)
        folded = sq[:, :LANES]
        for j in range(1, d // LANES):
            folded = folded + sq[:, j * LANES:(j + 1) * LANES]
        part_ref[...] += folded

    spec = pl.BlockSpec((tt, d), lambda i: (i, 0))
    return pl.pallas_call(
        body, name=name, grid=(rows // tt,), in_specs=[spec, spec],
        out_specs=[spec, pl.BlockSpec((1, LANES), lambda i: (0, 0))],
        out_shape=[jax.ShapeDtypeStruct((rows, d), F32), jax.ShapeDtypeStruct((1, LANES), F32)],
        compiler_params=_cparams(("arbitrary",)),
    )(y, target)


def _adamw(w, g, m, v, name):
    shape = w.shape
    cols = shape[-1]
    rows = math.prod(shape[:-1])
    w2, g2, m2, v2 = (a.reshape(rows, cols) for a in (w, g, m, v))
    tr = rows
    for cand in (512, 256, 128, 64, 32, 16, 8):
        if rows % cand == 0:
            tr = cand
            break
    c1 = 1.0 / (1.0 - ADAM_B1 ** ADAM_STEP)
    c2 = 1.0 / (1.0 - ADAM_B2 ** ADAM_STEP)

    def body(w_ref, g_ref, m_ref, v_ref, d_ref, nm_ref, nv_ref):
        gv = g_ref[...]
        nm = ADAM_B1 * m_ref[...] + (1.0 - ADAM_B1) * gv
        nv = ADAM_B2 * v_ref[...] + (1.0 - ADAM_B2) * gv * gv
        d_ref[...] = -ADAM_LR * ((nm * c1) / (jnp.sqrt(nv * c2) + ADAM_EPS) + ADAM_WD * w_ref[...])
        nm_ref[...] = nm
        nv_ref[...] = nv

    spec = pl.BlockSpec((tr, cols), lambda i: (i, 0))
    sds = jax.ShapeDtypeStruct((rows, cols), F32)
    d, nm, nv = pl.pallas_call(
        body, name=name, grid=(rows // tr,), in_specs=[spec] * 4, out_specs=[spec] * 3, out_shape=[sds] * 3,
        compiler_params=_cparams(("parallel",)),
    )(w2, g2, m2, v2)
    return d.reshape(shape), nm.reshape(shape), nv.reshape(shape)


_ANY = pl.BlockSpec(memory_space=pl.ANY)


def _my_xyc():
    return lax.axis_index("x"), lax.axis_index("y"), lax.axis_index("c")


def _chip_allgather_core_half(shards, name):
    n = len(shards)
    half = [s.shape[0] // 2 for s in shards]

    def body(*refs):
        full_refs, out_refs = refs[:n], refs[n:2 * n]
        send_sems, recv_sems, loc_sems = refs[2 * n:]
        x, y, c = _my_xyc()
        myq = 2 * x + y
        x_refs = [full_refs[i].at[pl.ds(c * half[i], half[i])] for i in range(n)]
        chips = [(1 - x, y), (x, 1 - y), (1 - x, 1 - y)]
        started = []
        for i in range(n):
            local = pltpu.make_async_copy(x_refs[i], out_refs[i].at[myq], loc_sems.at[i])
            local.start()
            started.append(local)
        sends = []
        for j, (px, py) in enumerate(chips):
            for i in range(n):
                cp = pltpu.make_async_remote_copy(src_ref=x_refs[i], dst_ref=out_refs[i].at[myq],
                                                  send_sem=send_sems.at[3 * i + j], recv_sem=recv_sems.at[3 * i + j],
                                                  device_id=(px, py, c), device_id_type=MESH_ID)
                cp.start()
                sends.append(cp)
        for j, (px, py) in enumerate(chips):
            for i in range(n):
                pltpu.make_async_remote_copy(src_ref=x_refs[i], dst_ref=out_refs[i].at[2 * px + py],
                                             send_sem=send_sems.at[3 * i + j], recv_sem=recv_sems.at[3 * i + j],
                                             device_id=(px, py, c), device_id_type=MESH_ID).wait_recv()
        for cp in sends:
            cp.wait_send()
        for local in started:
            local.wait()

    return pl.pallas_call(
        body, name=name, in_specs=[_ANY] * n, out_specs=[_ANY] * n,
        out_shape=[jax.ShapeDtypeStruct((4, h) + s.shape[1:], s.dtype) for s, h in zip(shards, half)],
        scratch_shapes=[pltpu.SemaphoreType.DMA((3 * n,)), pltpu.SemaphoreType.DMA((3 * n,)),
                        pltpu.SemaphoreType.DMA((n,))],
    )(*shards)


def _sibling_swap_halves(g, name):
    q, rows, cols = g.shape
    rh = rows // 2

    def body(g_ref, out_ref, send_sem, recv_sem):
        x, y, c = _my_xyc()
        src = g_ref.at[:, pl.ds((1 - c) * rh, rh), :]
        cp = pltpu.make_async_remote_copy(src_ref=src, dst_ref=out_ref, send_sem=send_sem, recv_sem=recv_sem,
                                          device_id=(x, y, 1 - c), device_id_type=MESH_ID)
        cp.start()
        cp.wait()

    return pl.pallas_call(
        body, name=name, in_specs=[_ANY], out_specs=_ANY,
        out_shape=jax.ShapeDtypeStruct((q, rh, cols), g.dtype),
        scratch_shapes=[pltpu.SemaphoreType.DMA, pltpu.SemaphoreType.DMA],
    )(g)


def _add_own_half(g, other, name):
    q, rows, cols = g.shape
    rh = rows // 2
    tr = _pick(rh, (512, 256, 128, 64, 32, 16, 8))
    nb = rh // tr
    cidx = lax.axis_index("c").astype(jnp.int32).reshape(1)

    def body(c_ref, g_ref, o_ref, out_ref):
        out_ref[...] = (g_ref[...] + o_ref[...]).astype(out_ref.dtype)

    grid_spec = pltpu.PrefetchScalarGridSpec(
        num_scalar_prefetch=1, grid=(q, nb),
        in_specs=[pl.BlockSpec((1, tr, cols), lambda a, i, c_ref: (a, c_ref[0] * nb + i, 0)),
                  pl.BlockSpec((1, tr, cols), lambda a, i, c_ref: (a, i, 0))],
        out_specs=pl.BlockSpec((1, tr, cols), lambda a, i, c_ref: (a, i, 0)))
    return pl.pallas_call(
        body, name=name, grid_spec=grid_spec, out_shape=jax.ShapeDtypeStruct((q, rh, cols), BF16),
        compiler_params=_cparams(("parallel", "parallel")),
    )(cidx, g, other)


def _chip_exchange(s, name):
    def body(s_ref, out_ref, send_sems, recv_sems, loc_sem):
        x, y, c = _my_xyc()
        myq = 2 * x + y
        local = pltpu.make_async_copy(s_ref.at[myq], out_ref.at[myq], loc_sem)
        local.start()
        chips = [(1 - x, y), (x, 1 - y), (1 - x, 1 - y)]
        sends = []
        for j, (px, py) in enumerate(chips):
            cp = pltpu.make_async_remote_copy(src_ref=s_ref.at[2 * px + py], dst_ref=out_ref.at[myq],
                                              send_sem=send_sems.at[j], recv_sem=recv_sems.at[j],
                                              device_id=(px, py, c), device_id_type=MESH_ID)
            cp.start()
            sends.append(cp)
        for j, (px, py) in enumerate(chips):
            pltpu.make_async_remote_copy(src_ref=s_ref.at[myq], dst_ref=out_ref.at[2 * px + py],
                                         send_sem=send_sems.at[j], recv_sem=recv_sems.at[j],
                                         device_id=(px, py, c), device_id_type=MESH_ID).wait_recv()
        for cp in sends:
            cp.wait_send()
        local.wait()

    return pl.pallas_call(
        body, name=name, in_specs=[_ANY], out_specs=_ANY, out_shape=jax.ShapeDtypeStruct(s.shape, s.dtype),
        scratch_shapes=[pltpu.SemaphoreType.DMA((3,)), pltpu.SemaphoreType.DMA((3,)), pltpu.SemaphoreType.DMA],
    )(s)


def _sum_leading(a, name):
    q, rows, cols = a.shape
    tr = _pick(rows, (512, 256, 128, 64, 32, 16, 8))

    def body(a_ref, out_ref):
        acc = a_ref[0].astype(F32)
        for j in range(1, q):
            acc = acc + a_ref[j].astype(F32)
        out_ref[...] = acc

    return pl.pallas_call(
        body, name=name, grid=(rows // tr,), in_specs=[pl.BlockSpec((q, tr, cols), lambda i: (0, i, 0))],
        out_specs=pl.BlockSpec((tr, cols), lambda i: (i, 0)), out_shape=jax.ShapeDtypeStruct((rows, cols), F32),
        compiler_params=_cparams(("parallel",)),
    )(a)


def _sibling_swap(arrays, name):
    n = len(arrays)

    def body(*refs):
        a_refs, out_refs = refs[:n], refs[n:2 * n]
        send_sems, recv_sems = refs[2 * n:]
        x, y, c = _my_xyc()
        copies = []
        for i in range(n):
            cp = pltpu.make_async_remote_copy(src_ref=a_refs[i], dst_ref=out_refs[i], send_sem=send_sems.at[i],
                                              recv_sem=recv_sems.at[i], device_id=(x, y, 1 - c),
                                              device_id_type=MESH_ID)
            cp.start()
            copies.append(cp)
        for cp in copies:
            cp.wait()

    return pl.pallas_call(
        body, name=name, in_specs=[_ANY] * n, out_specs=[_ANY] * n,
        out_shape=[jax.ShapeDtypeStruct(a.shape, a.dtype) for a in arrays],
        scratch_shapes=[pltpu.SemaphoreType.DMA((n,)), pltpu.SemaphoreType.DMA((n,))],
    )(*arrays)


def _by_core(mine, other):
    c = lax.axis_index("c")
    return jnp.where(c == 0, jnp.stack([mine, other]), jnp.stack([other, mine]))


def _all_reduce_small(vec, name):
    rows, cols = vec.shape

    def body(v_ref, out_ref, buf, send_sems, recv_sems):
        x, y, c = _my_xyc()
        me = 4 * x + 2 * y + c
        buf[me] = v_ref[...]
        sends = []
        for kk in range(1, 8):
            peer = (1 - x if kk & 4 else x, 1 - y if kk & 2 else y, 1 - c if kk & 1 else c)
            cp = pltpu.make_async_remote_copy(src_ref=v_ref, dst_ref=buf.at[me], send_sem=send_sems.at[kk - 1],
                                              recv_sem=recv_sems.at[kk - 1], device_id=peer, device_id_type=MESH_ID)
            cp.start()
            sends.append(cp)
        for kk in range(1, 8):
            px, py, pc = (1 - x if kk & 4 else x, 1 - y if kk & 2 else y, 1 - c if kk & 1 else c)
            pltpu.make_async_remote_copy(src_ref=v_ref, dst_ref=buf.at[4 * px + 2 * py + pc],
                                         send_sem=send_sems.at[kk - 1], recv_sem=recv_sems.at[kk - 1],
                                         device_id=(px, py, pc), device_id_type=MESH_ID).wait_recv()
        for cp in sends:
            cp.wait_send()
        acc = buf[0]
        for j in range(1, 8):
            acc = acc + buf[j]
        out_ref[...] = acc

    vm = pl.BlockSpec(memory_space=pltpu.VMEM)
    return pl.pallas_call(
        body, name=name, in_specs=[vm], out_specs=vm, out_shape=jax.ShapeDtypeStruct((rows, cols), F32),
        scratch_shapes=[pltpu.VMEM((8, rows, cols), F32), pltpu.SemaphoreType.DMA((7,)), pltpu.SemaphoreType.DMA((7,))],
        compiler_params=pltpu.CompilerParams(vmem_limit_bytes=VMEM_LIMIT_BYTES),
    )(vec)


def _reduce_scatter(g):
    other = _sibling_swap_halves(g, "rs_swap_halves")
    s = _add_own_half(g, other, "rs_add_halves")
    recv = _chip_exchange(s, "rs_chip_exchange")
    r = _sum_leading(recv, "rs_sum_chips")
    (other_r,) = _sibling_swap([r], "rs_join_halves")
    return _by_core(r, other_r).reshape(g.shape[1], g.shape[2])


PACK_COLS = 1024
PACK_ROW_MULTIPLE = 1024
SHARDED = (("even_w_in", 2), ("conv_w", 2), ("even_w_out", 1), ("odd_w_in", 2), ("q_norm_w", 1), ("w_uq", 2),
           ("kv_norm_w", 1), ("w_ukv", 2), ("odd_w_out", 1), ("ffn_w_gate", 2), ("ffn_w_up", 2), ("ffn_w_down", 1))
KEEP_F32 = ("conv_w", "q_norm_w", "kv_norm_w")
REPLICATED = ("pool_w", "pool_scale", "conv_b", "dt_bias", "a_log", "d_skip", "ssm_norm_w", "fgate_b",
              "ln_mix_g", "ln_mix_b", "ln_ffn_g", "ln_ffn_b")


def _gather_weights(shards):
    sent = [shards[name] if name in KEEP_F32 else shards[name].astype(BF16) for name, _ in SHARDED]
    mine = _chip_allgather_core_half(sent, "weights_allgather")
    others = _sibling_swap(mine, "weights_sibling_swap")
    out = {}
    for (name, axis), m, o in zip(SHARDED, mine, others):
        halves = _by_core(m, o)
        out[name] = jnp.concatenate([jnp.concatenate([halves[h, q] for q in range(4)], axis=axis) for h in range(2)],
                                    axis=0)
    return out


def _pack_grads(grads, shards):
    pieces = []
    for name, axis in SHARDED:
        width = shards[name].shape[axis]
        for g in grads[name]:
            ax = axis - 1
            split = g.reshape(g.shape[:ax] + (4, width) + g.shape[ax + 1:])
            pieces.append(jnp.moveaxis(split, ax, 0).reshape(4, -1))
    n = sum(p.shape[1] for p in pieces)
    per = PACK_COLS * PACK_ROW_MULTIPLE
    total = -(-n // per) * per
    pieces.append(jnp.zeros((4, total - n), F32))
    return jnp.concatenate(pieces, axis=1).reshape(4, total // PACK_COLS, PACK_COLS)


def _unpack_reduced(reduced, shards):
    flat = reduced.reshape(-1)
    out, off = {}, 0
    for name, _ in SHARDED:
        shp = shards[name].shape
        size = math.prod(shp)
        out[name] = flat[off:off + size].reshape(shp)
        off += size
    return out


def _to_heads(t, nh):
    rows, width = t.shape
    return t.reshape(rows, nh, width // nh).transpose(1, 0, 2)


def _from_heads(t):
    nh, rows, d = t.shape
    return t.transpose(1, 0, 2).reshape(rows, nh * d)


def _rope_tables(rows):
    half = MLA_ROPE // 2
    freqs = jnp.power(ROPE_THETA, -jnp.arange(half, dtype=F32) / half)
    ang = jnp.arange(rows, dtype=F32)[:, None] * freqs[None, :]
    cos, sin = jnp.cos(ang), jnp.sin(ang)
    cos_q, sin_q = jnp.tile(cos, (1, MLA_HEADS)), jnp.tile(sin, (1, MLA_HEADS))
    zeros = jnp.zeros((rows, LANES - MLA_ROPE), F32)
    cos_k = jnp.concatenate([cos, cos, zeros], axis=1)
    sin_k = jnp.concatenate([sin, sin, zeros], axis=1)
    r = lax.broadcasted_iota(jnp.int32, (LANES, LANES), 0)
    c = lax.broadcasted_iota(jnp.int32, (LANES, LANES), 1)
    rot = (jnp.where((r == c + half) & (c < half), -1.0, 0.0)
           + jnp.where((c == r + half) & (r < half), 1.0, 0.0)).astype(F32)
    return cos_q, sin_q, cos_k, sin_k, rot


def _pad_cols(a, width):
    return jnp.pad(a, ((0, 0), (0, width - a.shape[1])))


def _prep_even_w_in(w):
    return _pad_cols(w, EVEN_IN_PAD)


_ODD_CUT = (1536, 1544, 2312, 2344)


def _prep_odd_w_in(w):
    c0, c1, c2, c3 = _ODD_CUT
    return jnp.concatenate([w[:, :c0], w[:, c1:c2], _pad_cols(w[:, c2:c3], LANES), _pad_cols(w[:, c0:c1], LANES)],
                           axis=1)


def _unprep_odd_w_in(g):
    c0, c1, c2, c3 = _ODD_CUT
    n1 = c0 + (c2 - c1)
    return jnp.concatenate([g[:, :c0], g[:, n1 + LANES:n1 + LANES + (c1 - c0)], g[:, c0:n1],
                            g[:, n1:n1 + (c3 - c2)]], axis=1)


def _prep_w_uq(w):
    r = w.reshape(w.shape[0], MLA_HEADS, MLA_NOPE + MLA_ROPE)
    half = MLA_ROPE // 2
    return jnp.concatenate([r[:, :, :MLA_NOPE].reshape(w.shape[0], -1),
                            r[:, :, MLA_NOPE:MLA_NOPE + half].reshape(w.shape[0], -1),
                            r[:, :, MLA_NOPE + half:].reshape(w.shape[0], -1)], axis=1)


def _unprep_w_uq(g):
    rows = g.shape[0]
    half = MLA_ROPE // 2
    nope = MLA_HEADS * MLA_NOPE
    return jnp.concatenate([g[:, :nope].reshape(rows, MLA_HEADS, MLA_NOPE),
                            g[:, nope:nope + LANES].reshape(rows, MLA_HEADS, half),
                            g[:, nope + LANES:].reshape(rows, MLA_HEADS, half)], axis=2).reshape(rows, -1)


def _row(v, width=None):
    v = v.reshape(1, -1)
    return v if width is None else _pad_cols(v, width)


def _even_forward(x_bf, w, i):
    rows = x_bf.shape[0]
    proj = _mm(x_bf, w["even_w_in"][i], "nn", F32, "even_proj")
    u, z, xbc, dtr = (proj[:, :512], proj[:, 512:1536], proj[:, 1536:3072], proj[:, 3072:3200])
    n_row = rows // ROW_TILE
    pool_p = (w["pool_w"][i], _row(w["pool_scale"][i]))
    (y_pool,), pool_c = _scan_fwd(_f_pool, "pool_fwd", n_row, [u], [], pool_p, [], [(16, POOL_WIDTH)],
                                  [(POOL_WIDTH, BF16)])
    conv_p = (w["conv_w"][i], _row(w["conv_b"][i]))
    (xa,), conv_c = _scan_fwd(_f_conv, "conv_fwd", n_row, [xbc], [], conv_p, [], [(8, SSM_CONV_DIM)],
                              [(SSM_CONV_DIM, F32)])
    ssd_p = (_row(w["dt_bias"][i], LANES), _row(w["a_log"][i], LANES), _row(w["d_skip"][i], LANES),
             _row(w["ssm_norm_w"][i]))
    (y_ssm,), ssd_c = _scan_fwd(_f_ssd, "ssd_fwd", rows // SSM_CHUNK, [xa, dtr, z], [], ssd_p, [],
                                [(SSM_D_INNER, SSM_STATE)], [(SSM_D_INNER, BF16)])
    mix = jnp.concatenate([y_pool, y_ssm], axis=1)
    saved = dict(u=u, z=z, xbc=xbc, dtr=dtr, xa=xa, mix=mix, pool_p=pool_p, pool_c=pool_c, conv_p=conv_p,
                 conv_c=conv_c, ssd_p=ssd_p, ssd_c=ssd_c)
    return mix, saved


def _even_backward(dmix, x_bf, sv, w, i, d_r1):
    rows = x_bf.shape[0]
    n_row = rows // ROW_TILE
    dy_pool, dy_ssm = dmix[:, :POOL_WIDTH], dmix[:, POOL_WIDTH:]
    (du,), (g_pool_w, g_pool_scale) = _scan_bwd(_f_pool, "pool_bwd", n_row, [sv["u"]], [], sv["pool_p"], [],
                                                sv["pool_c"], [dy_pool])
    (dxa, ddtr, dz), (g_dt_bias, g_a_log, g_d_skip, g_norm_w) = _scan_bwd(
        _f_ssd, "ssd_bwd", rows // SSM_CHUNK, [sv["xa"], sv["dtr"], sv["z"]], [], sv["ssd_p"], [], sv["ssd_c"],
        [dy_ssm])
    (dxbc,), (g_conv_w, g_conv_b) = _scan_bwd(_f_conv, "conv_bwd", n_row, [sv["xbc"]], [], sv["conv_p"], [],
                                              sv["conv_c"], [dxa])
    dproj = jnp.concatenate([du, dz, dxbc, ddtr], axis=1).astype(BF16)
    g_w_in = _mm(x_bf, dproj, "tn", F32, "even_dw_in")[:, :EVEN_IN]
    dx = _mm(dproj, w["even_w_in"][i], "nt", F32, "even_dx", extra=d_r1, alpha=ALPHA)
    grads = dict(even_w_in=g_w_in, pool_w=g_pool_w, pool_scale=g_pool_scale[0], conv_w=g_conv_w, conv_b=g_conv_b[0],
                 dt_bias=g_dt_bias[0, :SSM_HEADS], a_log=g_a_log[0, :SSM_HEADS], d_skip=g_d_skip[0, :SSM_HEADS],
                 ssm_norm_w=g_norm_w[0])
    return dx, grads


def _odd_forward(x_bf, w, i, tables):
    rows = x_bf.shape[0]
    cos_q, sin_q, cos_k, sin_k, rot = tables
    proj = _mm(x_bf, w["odd_w_in"][i], "nn", F32, "odd_proj")
    qf, kf, vf = (_to_heads((proj[:, j * 512:(j + 1) * 512] * sc).astype(BF16), FOX_HEADS)
                  for j, sc in enumerate((FOX_SCALE, 1.0, 1.0)))
    cq, ckv = proj[:, 1536:2048], proj[:, 2048:2304]
    kr, fl = proj[:, 2304:2432], proj[:, 2432:2560]
    n_row = rows // ROW_TILE
    fox_p = (_row(w["fgate_b"][i], LANES),)
    n_gate = rows // min(ATTN_TILE, rows)
    (fcum,), fox_c = _scan_fwd(_f_fox_gate, "fox_gate_fwd", n_gate, [fl], [], fox_p, [], [(8, LANES)], [(LANES, F32)])
    fc_heads = fcum[:, :FOX_HEADS].T
    fq, fk = fc_heads[:, :, None], fc_heads[:, None, :]
    o_fox, lse_fox = _attn_fwd(qf, kf, vf, fq, fk, "fox_attn_fwd")

    prep_p = (_row(w["q_norm_w"][i]), _row(w["kv_norm_w"][i]))
    (cqn, ckvn, krr), _ = _scan_fwd(_f_mla_prep, "mla_prep_fwd", n_row, [cq, ckv, kr], [cos_k, sin_k], prep_p,
                                    [rot], [], [(MLA_Q_RANK, BF16), (MLA_KV_RANK, BF16), (LANES, BF16)])
    q_flat = _mm(cqn, w["w_uq"][i], "nn", F32, "mla_q_up")
    (q_rope,), _ = _scan_fwd(_f_rope_q, "rope_q_fwd", n_row, [q_flat], [cos_q, sin_q], [], [], [],
                             [(q_flat.shape[1], BF16)])
    kv = _mm(ckvn, w["w_ukv"][i], "nn", BF16, "mla_kv_up")
    nope = MLA_HEADS * MLA_NOPE
    half = MLA_ROPE // 2
    q_m = jnp.concatenate([_to_heads(q_rope[:, :nope], MLA_HEADS), _to_heads(q_rope[:, nope:nope + LANES], MLA_HEADS),
                           _to_heads(q_rope[:, nope + LANES:], MLA_HEADS)], axis=2)
    kv_h = _to_heads(kv, MLA_HEADS)
    k_rope = jnp.broadcast_to(krr[None, :, :MLA_ROPE], (MLA_HEADS, rows, MLA_ROPE))
    k_m = jnp.concatenate([kv_h[:, :, :MLA_NOPE], k_rope], axis=2)
    v_m = kv_h[:, :, MLA_NOPE:]
    o_mla, lse_mla = _attn_fwd(q_m, k_m, v_m, None, None, "mla_attn_fwd")
    mix = jnp.concatenate([_from_heads(o_fox), _from_heads(o_mla)], axis=1)
    saved = dict(qf=qf, kf=kf, vf=vf, fq=fq, fk=fk, o_fox=o_fox, lse_fox=lse_fox, fl=fl, fox_p=fox_p, fox_c=fox_c,
                 cq=cq, ckv=ckv, kr=kr, prep_p=prep_p, cqn=cqn, ckvn=ckvn, q_flat=q_flat, q_m=q_m, k_m=k_m, v_m=v_m,
                 o_mla=o_mla, lse_mla=lse_mla, mix=mix)
    return mix, saved


def _odd_backward(dmix, x_bf, sv, w, i, d_r1, tables):
    rows = x_bf.shape[0]
    cos_q, sin_q, cos_k, sin_k, rot = tables
    n_row = rows // ROW_TILE
    nope = MLA_HEADS * MLA_NOPE
    half = MLA_ROPE // 2
    do_fox = _to_heads(dmix[:, :FOX_WIDTH], FOX_HEADS)
    do_mla = _to_heads(dmix[:, FOX_WIDTH:], MLA_HEADS)
    fox_args = (sv["qf"], sv["kf"], sv["vf"], sv["o_fox"], do_fox, sv["lse_fox"], sv["fq"], sv["fk"])
    dqf, delta_fox = _attn_bwd_dq(*fox_args, "fox_attn_dq")
    dkf, dvf, dfk = _attn_bwd_dkv(*fox_args[:3], delta_fox, *fox_args[4:], "fox_attn_dkv")
    dfcum = _pad_cols(dfk[:, 0, :].T, LANES)
    n_gate = rows // min(ATTN_TILE, rows)
    (dfl,), (g_fb,) = _scan_bwd(_f_fox_gate, "fox_gate_bwd", n_gate, [sv["fl"]], [], sv["fox_p"], [], sv["fox_c"],
                                [dfcum])
    mla_args = (sv["q_m"], sv["k_m"], sv["v_m"], sv["o_mla"], do_mla, sv["lse_mla"], None, None)
    dq_m, dk_m, dv_m = _attn_bwd_fused(*mla_args[:6], "mla_attn_bwd")
    dq_rope = jnp.concatenate([_from_heads(dq_m[:, :, :MLA_NOPE]), _from_heads(dq_m[:, :, MLA_NOPE:MLA_NOPE + half]),
                               _from_heads(dq_m[:, :, MLA_NOPE + half:])], axis=1)
    (dq_flat,), _ = _scan_bwd(_f_rope_q, "rope_q_bwd", n_row, [sv["q_flat"]], [cos_q, sin_q], [], [], [], [dq_rope])
    g_w_uq = _mm(sv["cqn"], dq_flat, "tn", F32, "mla_dw_uq")
    dcqn = _mm(dq_flat, w["w_uq"][i], "nt", BF16, "mla_dcqn")
    dkv = _from_heads(jnp.concatenate([dk_m[:, :, :MLA_NOPE], dv_m], axis=2))
    g_w_ukv = _mm(sv["ckvn"], dkv, "tn", F32, "mla_dw_ukv")
    dckvn = _mm(dkv, w["w_ukv"][i], "nt", BF16, "mla_dckvn")
    dkrr = _head_sum(dk_m, "mla_dk_rope_sum")
    (dcq, dckv, dkr), (g_qw, g_kvw) = _scan_bwd(_f_mla_prep, "mla_prep_bwd", n_row, [sv["cq"], sv["ckv"], sv["kr"]],
                                                [cos_k, sin_k], sv["prep_p"], [rot], [], [dcqn, dckvn, dkrr])
    dproj = jnp.concatenate([_from_heads(dqf).astype(F32) * FOX_SCALE, _from_heads(dkf).astype(F32), _from_heads(dvf).astype(F32),
                             dcq, dckv, dkr, dfl], axis=1).astype(BF16)
    g_w_in = _mm(x_bf, dproj, "tn", F32, "odd_dw_in")
    dx = _mm(dproj, w["odd_w_in"][i], "nt", F32, "odd_dx", extra=d_r1, alpha=ALPHA)
    grads = dict(odd_w_in=_unprep_odd_w_in(g_w_in), fgate_b=g_fb[0, :FOX_HEADS], q_norm_w=g_qw[0], kv_norm_w=g_kvw[0],
                 w_uq=_unprep_w_uq(g_w_uq), w_ukv=g_w_ukv)
    return dx, grads


def _head_sum(dk_m, name):
    H, T, dk = dk_m.shape
    tt = _pick(T, (512, 256, 128))

    def body(d_ref, o_ref):
        acc = d_ref[0].astype(F32)
        for h in range(1, H):
            acc = acc + d_ref[h].astype(F32)
        o_ref[...] = jnp.concatenate([acc[:, MLA_NOPE:], jnp.zeros((tt, LANES - MLA_ROPE), F32)], axis=1).astype(BF16)

    return pl.pallas_call(
        body, name=name, grid=(T // tt,), in_specs=[pl.BlockSpec((H, tt, dk), lambda i: (0, i, 0))],
        out_specs=pl.BlockSpec((tt, LANES), lambda i: (i, 0)), out_shape=jax.ShapeDtypeStruct((T, LANES), BF16),
        compiler_params=_cparams(("parallel",)),
    )(dk_m)


def _local_step(x, target, w, small):
    rows = x.shape[0]
    n_row = rows // ROW_TILE
    tables = _rope_tables(rows)
    saved = []
    x_f32 = x
    x_bf = x.astype(BF16)
    for l in range(DEPTH):
        i = l // 2
        if l % 2 == 0:
            mix, sv = _even_forward(x_bf, w, i)
            w_out = w["even_w_out"][i]
        else:
            mix, sv = _odd_forward(x_bf, w, i, tables)
            w_out = w["odd_w_out"][i]
        r1 = _mm(mix, w_out, "nn", F32, "mix_out_even" if l % 2 == 0 else "mix_out_odd", extra=x_f32, alpha=ALPHA)
        ln1_p = (_row(small["ln_mix_g"][l]), _row(small["ln_mix_b"][l]))
        (x_mid,), _ = _scan_fwd(_f_ln, "ln_fwd", n_row, [r1], [], ln1_p, [], [], [(D_MODEL, F32)])
        x_mid_bf = x_mid.astype(BF16)
        gu = _mm(x_mid_bf, w["ffn_w_gu"][l], "nn", BF16, "ffn_gu")
        (act,), _ = _scan_fwd(_f_act, "ffn_act_fwd", n_row, [gu], [], [], [], [], [(D_FF, BF16)])
        r2 = _mm(act, w["ffn_w_down"][l], "nn", F32, "ffn_down", extra=x_mid, alpha=ALPHA)
        ln2_p = (_row(small["ln_ffn_g"][l]), _row(small["ln_ffn_b"][l]))
        (x_out,), _ = _scan_fwd(_f_ln, "ln_fwd", n_row, [r2], [], ln2_p, [], [], [(D_MODEL, F32)])
        saved.append(dict(sv=sv, x_bf=x_bf, r1=r1, ln1_p=ln1_p, x_mid_bf=x_mid_bf, gu=gu, act=act, r2=r2, ln2_p=ln2_p,
                          w_out=w_out))
        x_f32, x_bf = x_out, x_out.astype(BF16)

    dy, loss_part = _loss_head(x_f32, target, "loss_head")
    loss = 0.5 * jnp.sum(loss_part) / D_MODEL

    layer_grads = []
    for l in reversed(range(DEPTH)):
        i = l // 2
        s = saved[l]
        (d_r2,), (g_ln2_g, g_ln2_b) = _scan_bwd(_f_ln, "ln_bwd", n_row, [s["r2"]], [], s["ln2_p"], [], [], [dy])
        g_down = _mm(s["act"], d_r2, "tn", F32, "ffn_dw_down")
        dact = _mm(d_r2, w["ffn_w_down"][l], "nt", BF16, "ffn_dact")
        (dgu,), _ = _scan_bwd(_f_act, "ffn_act_bwd", n_row, [s["gu"]], [], [], [], [], [dact])
        g_gu = _mm(s["x_mid_bf"], dgu, "tn", F32, "ffn_dw_gu")
        dx_mid = _mm(dgu, w["ffn_w_gu"][l], "nt", F32, "ffn_dx", extra=d_r2, alpha=ALPHA)
        (d_r1,), (g_ln1_g, g_ln1_b) = _scan_bwd(_f_ln, "ln_bwd", n_row, [s["r1"]], [], s["ln1_p"], [], [], [dx_mid])
        g_w_out = _mm(s["sv"]["mix"], d_r1, "tn", F32, "even_dw_out" if l % 2 == 0 else "odd_dw_out")
        dmix = _mm(d_r1, s["w_out"], "nt", BF16, "even_dmix" if l % 2 == 0 else "odd_dmix")
        if l % 2 == 0:
            dy, g = _even_backward(dmix, s["x_bf"], s["sv"], w, i, d_r1)
            g["even_w_out"] = g_w_out
        else:
            dy, g = _odd_backward(dmix, s["x_bf"], s["sv"], w, i, d_r1, tables)
            g["odd_w_out"] = g_w_out
        g.update(ffn_w_gate=g_gu[:, :D_FF], ffn_w_up=g_gu[:, D_FF:], ffn_w_down=g_down, ln_mix_g=g_ln1_g[0],
                 ln_mix_b=g_ln1_b[0], ln_ffn_g=g_ln2_g[0], ln_ffn_b=g_ln2_b[0])
        layer_grads.append((l, g))
    return loss, dy, layer_grads


EVEN_NAMES = ("even_w_in", "pool_w", "pool_scale", "conv_w", "conv_b", "dt_bias", "a_log", "d_skip", "ssm_norm_w",
              "even_w_out")
ODD_NAMES = ("odd_w_in", "fgate_b", "q_norm_w", "w_uq", "kv_norm_w", "w_ukv", "odd_w_out")
PER_LAYER_NAMES = ("ffn_w_gate", "ffn_w_up", "ffn_w_down", "ln_mix_g", "ln_mix_b", "ln_ffn_g", "ln_ffn_b")
WEIGHT_NAMES = EVEN_NAMES + ODD_NAMES + PER_LAYER_NAMES


def _grads_by_name(layer_grads):
    by_layer = dict(layer_grads)
    out = {}
    for n in EVEN_NAMES:
        out[n] = [by_layer[l][n] for l in range(0, DEPTH, 2)]
    for n in ODD_NAMES:
        out[n] = [by_layer[l][n] for l in range(1, DEPTH, 2)]
    for n in PER_LAYER_NAMES:
        out[n] = [by_layer[l][n] for l in range(DEPTH)]
    return out


def _prepare_weights(full):
    w = {}
    w["even_w_in"] = [_prep_even_w_in(full["even_w_in"][i]) for i in range(2)]
    w["even_w_out"] = [full["even_w_out"][i] for i in range(2)]
    w["odd_w_in"] = [_prep_odd_w_in(full["odd_w_in"][i]) for i in range(2)]
    w["w_uq"] = [_prep_w_uq(full["w_uq"][i]) for i in range(2)]
    w["w_ukv"] = [full["w_ukv"][i] for i in range(2)]
    w["odd_w_out"] = [full["odd_w_out"][i] for i in range(2)]
    w["ffn_w_gu"] = [jnp.concatenate([full["ffn_w_gate"][l], full["ffn_w_up"][l]], axis=1) for l in range(DEPTH)]
    w["ffn_w_down"] = [full["ffn_w_down"][l] for l in range(DEPTH)]
    for n in ("conv_w", "q_norm_w", "kv_norm_w"):
        w[n] = full[n]
    return w


def _flatten_small(grads):
    flat = jnp.concatenate([g.reshape(-1) for n in REPLICATED for g in grads[n]])
    n = flat.shape[0]
    per = LANES * 8
    total = -(-n // per) * per
    return jnp.pad(flat, (0, total - n)).reshape(total // LANES, LANES)


def _unflatten_small(mat, like):
    flat = mat.reshape(-1)
    out, off = {}, 0
    for n in REPLICATED:
        size = math.prod(like[n].shape)
        out[n] = flat[off:off + size].reshape(like[n].shape)
        off += size
    return out


def kernel(x, even_w_in, pool_w, pool_scale, conv_w, conv_b, dt_bias, a_log, d_skip, ssm_norm_w, even_w_out, odd_w_in, fgate_b, q_norm_w, w_uq, kv_norm_w, w_ukv, odd_w_out, ffn_w_gate, ffn_w_up, ffn_w_down, ln_mix_g, ln_mix_b, ln_ffn_g, ln_ffn_b, loss_target, m_even_w_in, m_pool_w, m_pool_scale, m_conv_w, m_conv_b, m_dt_bias, m_a_log, m_d_skip, m_ssm_norm_w, m_even_w_out, m_odd_w_in, m_fgate_b, m_q_norm_w, m_w_uq, m_kv_norm_w, m_w_ukv, m_odd_w_out, m_ffn_w_gate, m_ffn_w_up, m_ffn_w_down, m_ln_mix_g, m_ln_mix_b, m_ln_ffn_g, m_ln_ffn_b, v_even_w_in, v_pool_w, v_pool_scale, v_conv_w, v_conv_b, v_dt_bias, v_a_log, v_d_skip, v_ssm_norm_w, v_even_w_out, v_odd_w_in, v_fgate_b, v_q_norm_w, v_w_uq, v_kv_norm_w, v_w_ukv, v_odd_w_out, v_ffn_w_gate, v_ffn_w_up, v_ffn_w_down, v_ln_mix_g, v_ln_mix_b, v_ln_ffn_g, v_ln_ffn_b):
    weights = dict(even_w_in=even_w_in, pool_w=pool_w, pool_scale=pool_scale, conv_w=conv_w, conv_b=conv_b,
                   dt_bias=dt_bias, a_log=a_log, d_skip=d_skip, ssm_norm_w=ssm_norm_w, even_w_out=even_w_out,
                   odd_w_in=odd_w_in, fgate_b=fgate_b, q_norm_w=q_norm_w, w_uq=w_uq, kv_norm_w=kv_norm_w, w_ukv=w_ukv,
                   odd_w_out=odd_w_out, ffn_w_gate=ffn_w_gate, ffn_w_up=ffn_w_up, ffn_w_down=ffn_w_down,
                   ln_mix_g=ln_mix_g, ln_mix_b=ln_mix_b, ln_ffn_g=ln_ffn_g, ln_ffn_b=ln_ffn_b)
    m_in = dict(even_w_in=m_even_w_in, pool_w=m_pool_w, pool_scale=m_pool_scale, conv_w=m_conv_w, conv_b=m_conv_b,
                dt_bias=m_dt_bias, a_log=m_a_log, d_skip=m_d_skip, ssm_norm_w=m_ssm_norm_w, even_w_out=m_even_w_out,
                odd_w_in=m_odd_w_in, fgate_b=m_fgate_b, q_norm_w=m_q_norm_w, w_uq=m_w_uq, kv_norm_w=m_kv_norm_w,
                w_ukv=m_w_ukv, odd_w_out=m_odd_w_out, ffn_w_gate=m_ffn_w_gate, ffn_w_up=m_ffn_w_up,
                ffn_w_down=m_ffn_w_down, ln_mix_g=m_ln_mix_g, ln_mix_b=m_ln_mix_b, ln_ffn_g=m_ln_ffn_g,
                ln_ffn_b=m_ln_ffn_b)
    v_in = dict(even_w_in=v_even_w_in, pool_w=v_pool_w, pool_scale=v_pool_scale, conv_w=v_conv_w, conv_b=v_conv_b,
                dt_bias=v_dt_bias, a_log=v_a_log, d_skip=v_d_skip, ssm_norm_w=v_ssm_norm_w, even_w_out=v_even_w_out,
                odd_w_in=v_odd_w_in, fgate_b=v_fgate_b, q_norm_w=v_q_norm_w, w_uq=v_w_uq, kv_norm_w=v_kv_norm_w,
                w_ukv=v_w_ukv, odd_w_out=v_odd_w_out, ffn_w_gate=v_ffn_w_gate, ffn_w_up=v_ffn_w_up,
                ffn_w_down=v_ffn_w_down, ln_mix_g=v_ln_mix_g, ln_mix_b=v_ln_mix_b, ln_ffn_g=v_ln_ffn_g,
                ln_ffn_b=v_ln_ffn_b)
    shards = {n: weights[n] for n, _ in SHARDED}

    w = _prepare_weights(_gather_weights(shards))
    small = {n: weights[n] for n in REPLICATED}
    w.update(small)

    loss_local, dx, layer_grads = _local_step(x[0], loss_target[0], w, small)
    grads_full = _grads_by_name(layer_grads)

    reduced = _reduce_scatter(_pack_grads(grads_full, shards))
    grads = _unpack_reduced(reduced, shards)
    small_sum = _all_reduce_small(_flatten_small(grads_full), "small_grads_allreduce")
    grads.update(_unflatten_small(small_sum, small))
    loss = lax.psum(loss_local, ("x", "y", "c"))

    deltas, new_m, new_v = {}, {}, {}
    for n in WEIGHT_NAMES:
        deltas[n], new_m[n], new_v[n] = _adamw(weights[n], grads[n], m_in[n], v_in[n], "adamw_" + n)
    return (loss, dx[None], *[grads[n] for n in WEIGHT_NAMES], *[deltas[n] for n in WEIGHT_NAMES],
            *[new_m[n] for n in WEIGHT_NAMES], *[new_v[n] for n in WEIGHT_NAMES])
```

```python
import functools
import math

import numpy as np
import jax
import jax.numpy as jnp
from jax import lax
from jax.experimental import pallas as pl
from jax.experimental.pallas import tpu as pltpu

F32 = jnp.float32
BF16 = jnp.bfloat16
HI = lax.Precision.HIGHEST
MESH_ID = pl.DeviceIdType.MESH

VMEM_LIMIT_BYTES = 56 * 1024 * 1024
LANES = 128

D_MODEL = 1024
DEPTH = 4
POOL_WINDOWS = (2, 4, 8, 16)
POOL_GROUP = 128
POOL_WIDTH = 512
SSM_D_INNER = 1024
SSM_HEAD_DIM = 64
SSM_HEADS = 16
SSM_GROUPS = 2
SSM_STATE = 128
SSM_CONV = 4
SSM_CHUNK = 128
SSM_CONV_DIM = 1536
EVEN_IN = 3088
EVEN_IN_PAD = 3200
FOX_HEADS = 8
FOX_WIDTH = 512
MLA_HEADS = 8
MLA_NOPE = 64
MLA_ROPE = 32
MLA_V = 64
MLA_Q_RANK = 512
MLA_KV_RANK = 256
ROPE_THETA = 10000.0
ODD_IN = 2344
ODD_IN_PAD = 2560
D_FF = 2816
ALPHA = (2 * DEPTH) ** 0.25
LN_EPS = 1e-5
RMS_EPS = 1e-6
ADAM_LR = 0.001
ADAM_B1 = 0.9
ADAM_B2 = 0.999
ADAM_EPS = 1e-08
ADAM_WD = 0.01
ADAM_STEP = 10
NEG_BIG = -1e30

ATTN_TILE = 512
ATTN_WIDE = 2048
FOX_SCALE = 0.125
MLA_SCALE = (MLA_NOPE + MLA_ROPE) ** -0.5
ROW_TILE = 256


def _cparams(sem=None):
    return pltpu.CompilerParams(dimension_semantics=sem, vmem_limit_bytes=VMEM_LIMIT_BYTES)


def _pick(d, prefs):
    for p in prefs:
        if d % p == 0:
            return p
    return d


_M_PREFS = (2048, 1024, 512, 640, 1408, 768, 384, 256, 128)
_N_PREFS = (1024, 1408, 512, 640, 768, 384, 256, 128)
_K_PREFS = (1024, 1408, 512, 640, 768, 256, 128)
MM_MAX_ACC_ELEMS = 1408 * 1024


def _mm(a, b, mode, out_dtype, name, extra=None, alpha=1.0):
    if mode == "nn":
        (m, k), (k2, n) = a.shape, b.shape
    elif mode == "nt":
        (m, k), (n, k2) = a.shape, b.shape
    else:
        (k, m), (k2, n) = a.shape, b.shape
    assert k == k2, (a.shape, b.shape, mode)
    tn, tk = _pick(n, _N_PREFS), _pick(k, _K_PREFS)
    tm = _pick(m, tuple(p for p in _M_PREFS if p * tn <= MM_MAX_ACC_ELEMS))
    nk = k // tk
    if mode == "nn":
        a_spec = pl.BlockSpec((tm, tk), lambda i, j, kk: (i, kk))
        b_spec = pl.BlockSpec((tk, tn), lambda i, j, kk: (kk, j))
        dims = (((1,), (0,)), ((), ()))
    elif mode == "nt":
        a_spec = pl.BlockSpec((tm, tk), lambda i, j, kk: (i, kk))
        b_spec = pl.BlockSpec((tn, tk), lambda i, j, kk: (j, kk))
        dims = (((1,), (1,)), ((), ()))
    else:
        a_spec = pl.BlockSpec((tk, tm), lambda i, j, kk: (kk, i))
        b_spec = pl.BlockSpec((tk, tn), lambda i, j, kk: (kk, j))
        dims = (((0,), (0,)), ((), ()))
    o_spec = pl.BlockSpec((tm, tn), lambda i, j, kk: (i, j))
    has_extra = extra is not None

    def body(*refs):
        if has_extra:
            a_ref, b_ref, e_ref, o_ref, acc = refs
        else:
            a_ref, b_ref, o_ref, acc = refs
        kk = pl.program_id(2)

        @pl.when(kk == 0)
        def _():
            acc[...] = jnp.zeros_like(acc)

        acc[...] += lax.dot_general(a_ref[...].astype(BF16), b_ref[...].astype(BF16), dims,
                                    preferred_element_type=F32)

        @pl.when(kk == nk - 1)
        def _():
            r = acc[...]
            if has_extra:
                r = r + alpha * e_ref[...].astype(F32)
            o_ref[...] = r.astype(o_ref.dtype)

    ins = [a, b] + ([extra] if has_extra else [])
    specs = [a_spec, b_spec] + ([o_spec] if has_extra else [])
    return pl.pallas_call(
        body, name=name, grid=(m // tm, n // tn, nk), in_specs=specs, out_specs=o_spec,
        out_shape=jax.ShapeDtypeStruct((m, n), out_dtype),
        scratch_shapes=[pltpu.VMEM((tm, tn), F32)],
        compiler_params=_cparams(("parallel", "parallel", "arbitrary")),
    )(*ins)


def _full_spec(shape):
    nd = len(shape)
    return pl.BlockSpec(tuple(shape), lambda i, _nd=nd: (0,) * _nd)


def _scan_fwd(f, name, n, tiles, ctiles, params, cparams, carry_shapes, out_defs):
    rows = tiles[0].shape[0]
    tt = rows // n
    nt, nct, npar, ncp, ncar, nout = len(tiles), len(ctiles), len(params), len(cparams), len(carry_shapes), len(out_defs)

    def body(*refs):
        pos = 0
        t_refs = refs[pos:pos + nt]; pos += nt
        ct_refs = refs[pos:pos + nct]; pos += nct
        p_refs = refs[pos:pos + npar]; pos += npar
        cp_refs = refs[pos:pos + ncp]; pos += ncp
        o_refs = refs[pos:pos + nout]; pos += nout
        cs_refs = refs[pos:pos + ncar]; pos += ncar
        c_scr = refs[pos:pos + ncar]
        i = pl.program_id(0)

        @pl.when(i == 0)
        def _():
            for c in c_scr:
                c[...] = jnp.zeros_like(c)

        carry = tuple(c[...] for c in c_scr)
        for s, c in zip(cs_refs, carry):
            s[0] = c
        new_carry, outs = f(carry, tuple(r[...] for r in t_refs), tuple(r[...] for r in ct_refs),
                            tuple(r[...] for r in p_refs), tuple(r[...] for r in cp_refs), i)
        for o_ref, o in zip(o_refs, outs):
            o_ref[...] = o
        for c, v in zip(c_scr, new_carry):
            c[...] = v

    tile_spec = lambda arr: pl.BlockSpec((tt, arr.shape[1]), lambda i: (i, 0))
    in_specs = ([tile_spec(t) for t in tiles] + [tile_spec(t) for t in ctiles]
                + [_full_spec(p.shape) for p in params] + [_full_spec(p.shape) for p in cparams])
    out_specs = ([pl.BlockSpec((tt, c), lambda i: (i, 0)) for c, _ in out_defs]
                 + [pl.BlockSpec((1,) + tuple(s), lambda i: (i, 0, 0)) for s in carry_shapes])
    out_shape = ([jax.ShapeDtypeStruct((rows, c), dt) for c, dt in out_defs]
                 + [jax.ShapeDtypeStruct((n,) + tuple(s), F32) for s in carry_shapes])
    res = pl.pallas_call(
        body, name=name, grid=(n,), in_specs=in_specs, out_specs=out_specs, out_shape=out_shape,
        scratch_shapes=[pltpu.VMEM(tuple(s), F32) for s in carry_shapes],
        compiler_params=_cparams(("arbitrary",)),
    )(*tiles, *ctiles, *params, *cparams)
    return list(res[:nout]), list(res[nout:])


def _scan_bwd(f, name, n, tiles, ctiles, params, cparams, carries, douts):
    rows = tiles[0].shape[0]
    tt = rows // n
    nt, nct, npar, ncp, ncar, nout = len(tiles), len(ctiles), len(params), len(cparams), len(carries), len(douts)

    def body(*refs):
        pos = 0
        t_refs = refs[pos:pos + nt]; pos += nt
        ct_refs = refs[pos:pos + nct]; pos += nct
        p_refs = refs[pos:pos + npar]; pos += npar
        cp_refs = refs[pos:pos + ncp]; pos += ncp
        cs_refs = refs[pos:pos + ncar]; pos += ncar
        do_refs = refs[pos:pos + nout]; pos += nout
        dt_refs = refs[pos:pos + nt]; pos += nt
        dp_refs = refs[pos:pos + npar]; pos += npar
        dc_scr = refs[pos:pos + ncar]
        i = pl.program_id(0)

        @pl.when(i == 0)
        def _():
            for c in dc_scr:
                c[...] = jnp.zeros_like(c)
            for d in dp_refs:
                d[...] = jnp.zeros_like(d)

        ctv = tuple(r[...] for r in ct_refs)
        cpv = tuple(r[...] for r in cp_refs)

        def g(c, t, p):
            return f(c, t, ctv, p, cpv, n - 1 - i)

        _, vjp = jax.vjp(g, tuple(s[0] for s in cs_refs), tuple(r[...] for r in t_refs),
                         tuple(r[...] for r in p_refs))
        dc, dt, dp = vjp((tuple(c[...] for c in dc_scr), tuple(r[...] for r in do_refs)))
        for r, v in zip(dt_refs, dt):
            r[...] = v
        for r, v in zip(dp_refs, dp):
            r[...] += v
        for c, v in zip(dc_scr, dc):
            c[...] = v

    rev_tile = lambda arr: pl.BlockSpec((tt, arr.shape[1]), lambda i: (n - 1 - i, 0))
    in_specs = ([rev_tile(t) for t in tiles] + [rev_tile(t) for t in ctiles]
                + [_full_spec(p.shape) for p in params] + [_full_spec(p.shape) for p in cparams]
                + [pl.BlockSpec((1,) + tuple(c.shape[1:]), lambda i: (n - 1 - i, 0, 0)) for c in carries]
                + [rev_tile(d) for d in douts])
    out_specs = [rev_tile(t) for t in tiles] + [_full_spec(p.shape) for p in params]
    out_shape = ([jax.ShapeDtypeStruct(t.shape, t.dtype) for t in tiles]
                 + [jax.ShapeDtypeStruct(p.shape, F32) for p in params])
    res = pl.pallas_call(
        body, name=name, grid=(n,), in_specs=in_specs, out_specs=out_specs, out_shape=out_shape,
        scratch_shapes=[pltpu.VMEM(tuple(c.shape[1:]), F32) for c in carries],
        compiler_params=_cparams(("arbitrary",)),
    )(*tiles, *ctiles, *params, *cparams, *carries, *douts)
    return list(res[:nt]), list(res[nt:])


def _silu(x):
    return x * jax.nn.sigmoid(x)


def _softplus(x):
    return jnp.maximum(x, 0.0) + jnp.log(1.0 + jnp.exp(-jnp.abs(x)))


def _f_ln(carry, tiles, ctiles, params, cparams, idx):
    (r,), (g, b) = tiles, params
    mu = jnp.mean(r, axis=-1, keepdims=True)
    xc = r - mu
    var = jnp.mean(xc * xc, axis=-1, keepdims=True)
    return (), (xc * lax.rsqrt(var + LN_EPS) * g + b,)


def _f_ln_with_bf16(carry, tiles, ctiles, params, cparams, idx):
    _, (y,) = _f_ln(carry, tiles, ctiles, params, cparams, idx)
    return (), (y, y.astype(BF16))


def _f_act(carry, tiles, ctiles, params, cparams, idx):
    (gu,) = tiles
    g = gu[:, :D_FF].astype(F32)
    u = gu[:, D_FF:].astype(F32)
    return (), ((_silu(g) * u).astype(BF16),)


def _f_pool(carry, tiles, ctiles, params, cparams, idx):
    (prev,), (u,), (pool_w, pool_scale) = carry, tiles, params
    tt = u.shape[0]
    halo = prev.shape[0]
    ext = jnp.concatenate([prev, u], axis=0)
    pos = idx * tt + lax.broadcasted_iota(jnp.int32, (tt, 1), 0)
    ys = []
    for g, w in enumerate(POOL_WINDOWS):
        lo, hi = g * POOL_GROUP, (g + 1) * POOL_GROUP
        eg = ext[:, lo:hi]
        s = eg[halo:halo + tt]
        for j in range(1, w):
            s = s + eg[halo - j:halo - j + tt]
        count = jnp.minimum(pos + 1, w).astype(F32)
        diff = s / count - u[:, lo:hi]
        ys.append(jnp.dot(diff.astype(BF16), pool_w[g].astype(BF16), preferred_element_type=F32))
    y = jnp.concatenate(ys, axis=1) * pool_scale
    return (u[tt - halo:, :],), (y.astype(BF16),)


def _f_conv(carry, tiles, ctiles, params, cparams, idx):
    (prev,), (xbc,), (conv_w, conv_b) = carry, tiles, params
    tt = xbc.shape[0]
    halo = prev.shape[0]
    ext = jnp.concatenate([prev, xbc], axis=0)
    acc = jnp.zeros_like(xbc) + conv_b
    for k in range(SSM_CONV):
        off = halo - (SSM_CONV - 1) + k
        acc = acc + ext[off:off + tt] * conv_w[k:k + 1, :]
    return (xbc[tt - halo:, :],), (_silu(acc),)


def _head_expand_matrix():
    r = lax.broadcasted_iota(jnp.int32, (LANES, SSM_D_INNER), 0)
    c = lax.broadcasted_iota(jnp.int32, (LANES, SSM_D_INNER), 1)
    return ((c >= r * SSM_HEAD_DIM) & (c < (r + 1) * SSM_HEAD_DIM)).astype(BF16)


def _head_expand(v, e):
    hi = v.astype(BF16)
    lo = (v - hi.astype(F32)).astype(BF16)
    return jnp.dot(hi, e, preferred_element_type=F32) + jnp.dot(lo, e, preferred_element_type=F32)


def _f_ssd(carry, tiles, ctiles, params, cparams, idx):
    (state,), (xa, dtr, z), (dt_bias, a_log, d_skip, norm_w) = carry, tiles, params
    L, P, N, E = SSM_CHUNK, SSM_HEAD_DIM, SSM_STATE, SSM_HEADS // SSM_GROUPS
    gw = E * P
    em = _head_expand_matrix()
    dt = _softplus(dtr + dt_bias)
    da = dt * (-jnp.exp(a_log))
    r = lax.broadcasted_iota(jnp.int32, (L, L), 0)
    c = lax.broadcasted_iota(jnp.int32, (L, L), 1)
    tri = c <= r
    acs = jnp.dot(tri.astype(F32), da, precision=HI, preferred_element_type=F32)
    acs_t = acs.T
    last = acs[L - 1:L, :]
    xs = xa[:, :SSM_D_INNER]
    xdt = xs * _head_expand(dt, em)
    xdt_b = xdt.astype(BF16)
    xdec = (xdt * _head_expand(jnp.exp(last - acs), em)).astype(BF16)
    eacs = _head_expand(jnp.exp(acs), em)
    y_parts, st_parts = [], []
    for g in range(SSM_GROUPS):
        bg = xa[:, SSM_D_INNER + g * N:SSM_D_INNER + (g + 1) * N].astype(BF16)
        cg = xa[:, SSM_D_INNER + (SSM_GROUPS + g) * N:SSM_D_INNER + (SSM_GROUPS + g + 1) * N].astype(BF16)
        cb = lax.dot_general(cg, bg, (((1,), (1,)), ((), ())), preferred_element_type=F32)
        prev_g = state[:, g * gw:(g + 1) * gw]
        y_off = jnp.dot(cg, prev_g.astype(BF16), preferred_element_type=F32) * eacs[:, g * gw:(g + 1) * gw]
        st_parts.append(lax.dot_general(bg, xdec[:, g * gw:(g + 1) * gw], (((0,), (0,)), ((), ())),
                                        preferred_element_type=F32))
        diag = []
        for e in range(E):
            h = g * E + e
            lmat = jnp.exp(jnp.where(tri, acs[:, h:h + 1] - acs_t[h:h + 1, :], NEG_BIG))
            diag.append(jnp.dot((cb * lmat).astype(BF16), xdt_b[:, h * P:(h + 1) * P], preferred_element_type=F32))
        y_parts.append(jnp.concatenate(diag, axis=1) + y_off)
    y = jnp.concatenate(y_parts, axis=1) + xs * _head_expand(d_skip, em)
    new_state = state * _head_expand(jnp.exp(last), em) + jnp.concatenate(st_parts, axis=1)
    y = y * _silu(z)
    y = y * lax.rsqrt(jnp.mean(y * y, axis=-1, keepdims=True) + RMS_EPS) * norm_w
    return (new_state,), (y.astype(BF16),)


def _f_fox_gate(carry, tiles, ctiles, params, cparams, idx):
    (run,), (fl,), (fb,) = carry, tiles, params
    tt = fl.shape[0]
    logf = -_softplus(-(fl + fb))
    r = lax.broadcasted_iota(jnp.int32, (tt, tt), 0)
    c = lax.broadcasted_iota(jnp.int32, (tt, tt), 1)
    cum = jnp.dot((c <= r).astype(F32), logf, precision=HI, preferred_element_type=F32) + run[0:1, :]
    return (jnp.broadcast_to(cum[tt - 1:tt, :], run.shape),), (cum,)


def _rms(x, w):
    return x * lax.rsqrt(jnp.mean(x * x, axis=-1, keepdims=True) + RMS_EPS) * w


def _f_mla_prep(carry, tiles, ctiles, params, cparams, idx):
    (cq, ckv, kr), (ck, sk), (qw, kvw), (rk,) = tiles, ctiles, params, cparams
    rot = jnp.dot(kr, rk, precision=HI, preferred_element_type=F32)
    return (), (_rms(cq, qw).astype(BF16), _rms(ckv, kvw).astype(BF16), (kr * ck + rot * sk).astype(BF16))


def _f_rope_q(carry, tiles, ctiles, params, cparams, idx):
    (q,), (cos, sin) = tiles, ctiles
    nope = MLA_HEADS * MLA_NOPE
    x1 = q[:, nope:nope + LANES]
    x2 = q[:, nope + LANES:]
    roped = jnp.concatenate([q[:, :nope], x1 * cos - x2 * sin, x2 * cos + x1 * sin], axis=1)
    return (), ((roped * MLA_SCALE).astype(BF16),)


def _scores(q, k, fq, fk, masked, row0, col0):
    s = lax.dot_general(q, k, (((1,), (1,)), ((), ())), preferred_element_type=F32)
    if fq is not None:
        s = s + fq - fk
    if masked:
        rows = row0 + lax.broadcasted_iota(jnp.int32, s.shape, 0)
        cols = col0 + lax.broadcasted_iota(jnp.int32, s.shape, 1)
        s = jnp.where(cols <= rows, s, NEG_BIG)
    return s


def _q_major_tables(T, tq, tk):
    r = tk // tq
    qi = np.concatenate([np.full(i // r + 1, i, np.int32) for i in range(T // tq)])
    ki = np.concatenate([np.arange(i // r + 1, dtype=np.int32) for i in range(T // tq)])
    kind = np.where(ki == qi // r, qi % r + 1, 0).astype(np.int32)
    return [jnp.asarray(a) for a in (qi, ki, (ki == 0).astype(np.int32), kind)]


def _k_major_tables(T, tq, tk):
    r = tq // tk
    nq = T // tq
    ki = np.concatenate([np.full(nq - i // r, i, np.int32) for i in range(T // tk)])
    qi = np.concatenate([np.arange(i // r, nq, dtype=np.int32) for i in range(T // tk)])
    kind = np.where(qi == ki // r, r - ki % r, 0).astype(np.int32)
    return [jnp.asarray(a) for a in (ki, qi, (qi == nq - 1).astype(np.int32), kind)]


def _attn_tiles(T):
    return min(ATTN_TILE, T), min(ATTN_WIDE, T)


def _attn_fwd(q, k, v, fq, fk, name):
    H, T, dk = q.shape
    dv = v.shape[2]
    tq, tk = _attn_tiles(T)
    decay = fq is not None
    tables = _q_major_tables(T, tq, tk)

    def body(qi_ref, ki_ref, first_ref, kind_ref, *refs):
        if decay:
            q_ref, k_ref, v_ref, fq_ref, fk_ref, o_ref, lse_ref, m_s, l_s, acc = refs
        else:
            q_ref, k_ref, v_ref, o_ref, lse_ref, m_s, l_s, acc = refs
        t = pl.program_id(1)
        qi, ki = qi_ref[t], ki_ref[t]

        @pl.when(first_ref[t] == 1)
        def _():
            m_s[...] = jnp.full_like(m_s, NEG_BIG)
            l_s[...] = jnp.zeros_like(l_s)
            acc[...] = jnp.zeros_like(acc)

        def step(kind):
            w = tk if kind == 0 else kind * tq
            s = _scores(q_ref[0], k_ref[0, :w, :], fq_ref[0] if decay else None,
                        fk_ref[0, :, :w] if decay else None, kind > 0, qi * tq, ki * tk)
            m_new = jnp.maximum(m_s[...], jnp.max(s, axis=-1, keepdims=True))
            p = jnp.exp(s - m_new)
            corr = jnp.exp(m_s[...] - m_new)
            l_s[...] = corr * l_s[...] + jnp.sum(p, axis=-1, keepdims=True)
            acc[...] = corr * acc[...] + jnp.dot(p.astype(BF16), v_ref[0, :w, :], preferred_element_type=F32)
            m_s[...] = m_new
            if kind > 0:
                o_ref[0] = (acc[...] / l_s[...]).astype(o_ref.dtype)
                lse_ref[0] = m_s[...] + jnp.log(l_s[...])

        for kind in range(tk // tq + 1):
            pl.when(kind_ref[t] == kind)(functools.partial(step, kind))

    qspec = lambda d: pl.BlockSpec((1, tq, d), lambda h, t, qi, ki, fi, la: (h, qi[t], 0))
    kspec = lambda d: pl.BlockSpec((1, tk, d), lambda h, t, qi, ki, fi, la: (h, ki[t], 0))
    in_specs = [qspec(dk), kspec(dk), kspec(dv)]
    ins = [q, k, v]
    if decay:
        in_specs += [qspec(1), pl.BlockSpec((1, 1, tk), lambda h, t, qi, ki, fi, la: (h, 0, ki[t]))]
        ins += [fq, fk]
    grid_spec = pltpu.PrefetchScalarGridSpec(
        num_scalar_prefetch=4, grid=(H, int(tables[0].shape[0])), in_specs=in_specs,
        out_specs=[qspec(dv), qspec(1)],
        scratch_shapes=[pltpu.VMEM((tq, 1), F32), pltpu.VMEM((tq, 1), F32), pltpu.VMEM((tq, dv), F32)])
    return pl.pallas_call(
        body, name=name, grid_spec=grid_spec,
        out_shape=[jax.ShapeDtypeStruct((H, T, dv), BF16), jax.ShapeDtypeStruct((H, T, 1), F32)],
        compiler_params=_cparams(("parallel", "arbitrary")),
    )(*tables, *ins)


def _attn_bwd_dq(q, k, v, o, do, lse, fq, fk, name):
    H, T, dk = q.shape
    dv = v.shape[2]
    tq, tk = _attn_tiles(T)
    decay = fq is not None
    tables = _q_major_tables(T, tq, tk)

    def body(qi_ref, ki_ref, first_ref, kind_ref, *refs):
        if decay:
            q_ref, k_ref, v_ref, o_ref, do_ref, lse_ref, fq_ref, fk_ref, dq_ref, dl_ref, acc, dl, leak = refs
        else:
            q_ref, k_ref, v_ref, o_ref, do_ref, lse_ref, dq_ref, dl_ref, acc, dl, leak = refs
        t = pl.program_id(1)
        qi, ki = qi_ref[t], ki_ref[t]

        @pl.when(first_ref[t] == 1)
        def _():
            acc[...] = jnp.zeros_like(acc)
            leak[...] = jnp.zeros_like(leak)
            dl[...] = jnp.sum(do_ref[0].astype(F32) * o_ref[0].astype(F32), axis=-1, keepdims=True)

        def step(kind):
            w = tk if kind == 0 else kind * tq
            k_v = k_ref[0, :w, :]
            s = _scores(q_ref[0], k_v, fq_ref[0] if decay else None, fk_ref[0, :, :w] if decay else None,
                        kind > 0, qi * tq, ki * tk)
            p = jnp.exp(s - lse_ref[0])
            dp = lax.dot_general(do_ref[0], v_ref[0, :w, :], (((1,), (1,)), ((), ())), preferred_element_type=F32)
            ds = p * (dp - dl[...])
            leak[...] += jnp.sum(ds, axis=-1, keepdims=True)
            acc[...] += jnp.dot(ds.astype(BF16), k_v, preferred_element_type=F32)
            if kind > 0:
                dq_ref[0] = acc[...].astype(dq_ref.dtype)
                dl_ref[0] = dl[...] + leak[...]

        for kind in range(tk // tq + 1):
            pl.when(kind_ref[t] == kind)(functools.partial(step, kind))

    qspec = lambda d: pl.BlockSpec((1, tq, d), lambda h, t, qi, ki, fi, la: (h, qi[t], 0))
    kspec = lambda d: pl.BlockSpec((1, tk, d), lambda h, t, qi, ki, fi, la: (h, ki[t], 0))
    in_specs = [qspec(dk), kspec(dk), kspec(dv), qspec(dv), qspec(dv), qspec(1)]
    ins = [q, k, v, o, do, lse]
    if decay:
        in_specs += [qspec(1), pl.BlockSpec((1, 1, tk), lambda h, t, qi, ki, fi, la: (h, 0, ki[t]))]
        ins += [fq, fk]
    grid_spec = pltpu.PrefetchScalarGridSpec(
        num_scalar_prefetch=4, grid=(H, int(tables[0].shape[0])), in_specs=in_specs,
        out_specs=[qspec(dk), qspec(1)],
        scratch_shapes=[pltpu.VMEM((tq, dk), F32), pltpu.VMEM((tq, 1), F32), pltpu.VMEM((tq, 1), F32)])
    return pl.pallas_call(
        body, name=name, grid_spec=grid_spec,
        out_shape=[jax.ShapeDtypeStruct((H, T, dk), BF16), jax.ShapeDtypeStruct((H, T, 1), F32)],
        compiler_params=_cparams(("parallel", "arbitrary")),
    )(*tables, *ins)


def _attn_bwd_dkv(q, k, v, delta, do, lse, fq, fk, name):
    H, T, dk = q.shape
    dv = v.shape[2]
    tk, tq = _attn_tiles(T)
    decay = fq is not None
    tables = _k_major_tables(T, tq, tk)

    def body(ki_ref, qi_ref, last_ref, kind_ref, *refs):
        if decay:
            q_ref, k_ref, v_ref, dl_ref, do_ref, lse_ref, fq_ref, fk_ref, dk_ref, dv_ref, df_ref, dk_s, dv_s, df_s = refs
        else:
            q_ref, k_ref, v_ref, dl_ref, do_ref, lse_ref, dk_ref, dv_ref, dk_s, dv_s = refs
        t = pl.program_id(1)
        ki, qi = ki_ref[t], qi_ref[t]

        @pl.when(kind_ref[t] > 0)
        def _():
            dk_s[...] = jnp.zeros_like(dk_s)
            dv_s[...] = jnp.zeros_like(dv_s)
            if decay:
                df_s[...] = jnp.zeros_like(df_s)

        def step(kind):
            off = 0 if kind == 0 else tq - kind * tk
            q_v, do_v = q_ref[0, off:, :], do_ref[0, off:, :]
            s = _scores(q_v, k_ref[0], fq_ref[0, off:, :] if decay else None, fk_ref[0] if decay else None,
                        kind > 0, qi * tq + off, ki * tk)
            p = jnp.exp(s - lse_ref[0, off:, :])
            dv_s[...] += lax.dot_general(p.astype(BF16), do_v, (((0,), (0,)), ((), ())), preferred_element_type=F32)
            dp = lax.dot_general(do_v, v_ref[0], (((1,), (1,)), ((), ())), preferred_element_type=F32)
            ds = p * (dp - dl_ref[0, off:, :])
            dk_s[...] += lax.dot_general(ds.astype(BF16), q_v, (((0,), (0,)), ((), ())),
                                         preferred_element_type=F32)
            if decay:
                df_s[...] -= jnp.sum(ds, axis=0, keepdims=True)

        for kind in range(tq // tk + 1):
            pl.when(kind_ref[t] == kind)(functools.partial(step, kind))

        @pl.when(last_ref[t] == 1)
        def _():
            dk_ref[0] = dk_s[...].astype(dk_ref.dtype)
            dv_ref[0] = dv_s[...].astype(dv_ref.dtype)
            if decay:
                df_ref[0] = df_s[...]

    qspec = lambda d: pl.BlockSpec((1, tq, d), lambda h, t, ki, qi, fi, la: (h, qi[t], 0))
    kspec = lambda d: pl.BlockSpec((1, tk, d), lambda h, t, ki, qi, fi, la: (h, ki[t], 0))
    in_specs = [qspec(dk), kspec(dk), kspec(dv), qspec(1), qspec(dv), qspec(1)]
    ins = [q, k, v, delta, do, lse]
    out_specs = [kspec(dk), kspec(dv)]
    out_shape = [jax.ShapeDtypeStruct((H, T, dk), BF16), jax.ShapeDtypeStruct((H, T, dv), BF16)]
    scratch = [pltpu.VMEM((tk, dk), F32), pltpu.VMEM((tk, dv), F32)]
    if decay:
        fkspec = pl.BlockSpec((1, 1, tk), lambda h, t, ki, qi, fi, la: (h, 0, ki[t]))
        in_specs += [qspec(1), fkspec]
        ins += [fq, fk]
        out_specs.append(fkspec)
        out_shape.append(jax.ShapeDtypeStruct((H, 1, T), F32))
        scratch.append(pltpu.VMEM((1, tk), F32))
    grid_spec = pltpu.PrefetchScalarGridSpec(
        num_scalar_prefetch=4, grid=(H, int(tables[0].shape[0])), in_specs=in_specs, out_specs=out_specs,
        scratch_shapes=scratch)
    return pl.pallas_call(
        body, name=name, grid_spec=grid_spec, out_shape=out_shape, compiler_params=_cparams(("parallel", "arbitrary")),
    )(*tables, *ins)


def _attn_bwd_fused(q, k, v, o, do, lse, name):
    H, T, dk = q.shape
    dv = v.shape[2]
    tk, tq = _attn_tiles(T)
    tables = _k_major_tables(T, tq, tk)
    npairs = int(tables[0].shape[0])

    def body(ki_ref, qi_ref, last_ref, kind_ref, q_ref, k_ref, v_ref, o_ref, do_ref, lse_ref,
             dq_ref, dk_ref, dv_ref, dq_s, dk_s, dv_s):
        t = pl.program_id(1)
        ki, qi = ki_ref[t], qi_ref[t]

        @pl.when(t == 0)
        def _():
            dq_s[...] = jnp.zeros_like(dq_s)

        @pl.when(kind_ref[t] > 0)
        def _():
            dk_s[...] = jnp.zeros_like(dk_s)
            dv_s[...] = jnp.zeros_like(dv_s)

        def step(kind):
            off = 0 if kind == 0 else tq - kind * tk
            q_v, do_v, k_v = q_ref[0, off:, :], do_ref[0, off:, :], k_ref[0]
            s = _scores(q_v, k_v, None, None, kind > 0, qi * tq + off, ki * tk)
            p = jnp.exp(s - lse_ref[0, off:, :])
            delta = jnp.sum(do_v.astype(F32) * o_ref[0, off:, :].astype(F32), axis=-1, keepdims=True)
            dv_s[...] += lax.dot_general(p.astype(BF16), do_v, (((0,), (0,)), ((), ())), preferred_element_type=F32)
            dp = lax.dot_general(do_v, v_ref[0], (((1,), (1,)), ((), ())), preferred_element_type=F32)
            ds = (p * (dp - delta)).astype(BF16)
            dk_s[...] += lax.dot_general(ds, q_v, (((0,), (0,)), ((), ())), preferred_element_type=F32)
            rows = pl.ds(pl.multiple_of(qi * tq + off, tk), tq - off)
            dq_s[rows, :] += jnp.dot(ds, k_v, preferred_element_type=F32)

        for kind in range(tq // tk + 1):
            pl.when(kind_ref[t] == kind)(functools.partial(step, kind))

        @pl.when(last_ref[t] == 1)
        def _():
            dk_ref[0] = dk_s[...].astype(dk_ref.dtype)
            dv_ref[0] = dv_s[...].astype(dv_ref.dtype)

        @pl.when(t == npairs - 1)
        def _():
            dq_ref[0] = dq_s[...].astype(dq_ref.dtype)

    qspec = lambda d: pl.BlockSpec((1, tq, d), lambda h, t, ki, qi, la, kd: (h, qi[t], 0))
    kspec = lambda d: pl.BlockSpec((1, tk, d), lambda h, t, ki, qi, la, kd: (h, ki[t], 0))
    grid_spec = pltpu.PrefetchScalarGridSpec(
        num_scalar_prefetch=4, grid=(H, npairs),
        in_specs=[qspec(dk), kspec(dk), kspec(dv), qspec(dv), qspec(dv), qspec(1)],
        out_specs=[pl.BlockSpec((1, T, dk), lambda h, t, ki, qi, la, kd: (h, 0, 0)), kspec(dk), kspec(dv)],
        scratch_shapes=[pltpu.VMEM((T, dk), F32), pltpu.VMEM((tk, dk), F32), pltpu.VMEM((tk, dv), F32)])
    return pl.pallas_call(
        body, name=name, grid_spec=grid_spec,
        out_shape=[jax.ShapeDtypeStruct((H, T, dk), BF16), jax.ShapeDtypeStruct((H, T, dk), BF16),
                   jax.ShapeDtypeStruct((H, T, dv), BF16)],
        compiler_params=_cparams(("parallel", "arbitrary")),
    )(*tables, q, k, v, o, do, lse)


def _loss_head(y, target, name):
    rows, d = y.shape
    tt = _pick(rows, (512, 256, 128))

    def body(y_ref, t_ref, dy_ref, part_ref):
        @pl.when(pl.program_id(0) == 0)
        def _():
            part_ref[...] = jnp.zeros_like(part_ref)

        err = y_ref[...] - t_ref[...]
        dy_ref[...] = err * (1.0 / d)
        sq = jnp.sum(err * err, axis=0, keepdims=True)
        folded = sq[:, :LANES]
        for j in range(1, d // LANES):
            folded = folded + sq[:, j * LANES:(j + 1) * LANES]
        part_ref[...] += folded

    spec = pl.BlockSpec((tt, d), lambda i: (i, 0))
    return pl.pallas_call(
        body, name=name, grid=(rows // tt,), in_specs=[spec, spec],
        out_specs=[spec, pl.BlockSpec((1, LANES), lambda i: (0, 0))],
        out_shape=[jax.ShapeDtypeStruct((rows, d), F32), jax.ShapeDtypeStruct((1, LANES), F32)],
        compiler_params=_cparams(("arbitrary",)),
    )(y, target)


def _adamw(w, g, m, v, name):
    shape = w.shape
    cols = shape[-1]
    rows = math.prod(shape[:-1])
    w2, g2, m2, v2 = (a.reshape(rows, cols) for a in (w, g, m, v))
    tr = rows
    for cand in (512, 256, 128, 64, 32, 16, 8):
        if rows % cand == 0:
            tr = cand
            break
    c1 = 1.0 / (1.0 - ADAM_B1 ** ADAM_STEP)
    c2 = 1.0 / (1.0 - ADAM_B2 ** ADAM_STEP)

    def body(w_ref, g_ref, m_ref, v_ref, d_ref, nm_ref, nv_ref):
        gv = g_ref[...]
        nm = ADAM_B1 * m_ref[...] + (1.0 - ADAM_B1) * gv
        nv = ADAM_B2 * v_ref[...] + (1.0 - ADAM_B2) * gv * gv
        d_ref[...] = -ADAM_LR * ((nm * c1) / (jnp.sqrt(nv * c2) + ADAM_EPS) + ADAM_WD * w_ref[...])
        nm_ref[...] = nm
        nv_ref[...] = nv

    spec = pl.BlockSpec((tr, cols), lambda i: (i, 0))
    sds = jax.ShapeDtypeStruct((rows, cols), F32)
    d, nm, nv = pl.pallas_call(
        body, name=name, grid=(rows // tr,), in_specs=[spec] * 4, out_specs=[spec] * 3, out_shape=[sds] * 3,
        compiler_params=_cparams(("parallel",)),
    )(w2, g2, m2, v2)
    return d.reshape(shape), nm.reshape(shape), nv.reshape(shape)


_ANY = pl.BlockSpec(memory_space=pl.ANY)


def _my_xyc():
    return lax.axis_index("x"), lax.axis_index("y"), lax.axis_index("c")


def _chip_allgather_core_half(shards, name):
    n = len(shards)
    half = [s.shape[0] // 2 for s in shards]

    def body(*refs):
        full_refs, out_refs = refs[:n], refs[n:2 * n]
        send_sems, recv_sems, loc_sems = refs[2 * n:]
        x, y, c = _my_xyc()
        myq = 2 * x + y
        x_refs = [full_refs[i].at[pl.ds(c * half[i], half[i])] for i in range(n)]
        chips = [(1 - x, y), (x, 1 - y), (1 - x, 1 - y)]
        started = []
        for i in range(n):
            local = pltpu.make_async_copy(x_refs[i], out_refs[i].at[myq], loc_sems.at[i])
            local.start()
            started.append(local)
        sends = []
        for j, (px, py) in enumerate(chips):
            for i in range(n):
                cp = pltpu.make_async_remote_copy(src_ref=x_refs[i], dst_ref=out_refs[i].at[myq],
                                                  send_sem=send_sems.at[3 * i + j], recv_sem=recv_sems.at[3 * i + j],
                                                  device_id=(px, py, c), device_id_type=MESH_ID)
                cp.start()
                sends.append(cp)
        for j, (px, py) in enumerate(chips):
            for i in range(n):
                pltpu.make_async_remote_copy(src_ref=x_refs[i], dst_ref=out_refs[i].at[2 * px + py],
                                             send_sem=send_sems.at[3 * i + j], recv_sem=recv_sems.at[3 * i + j],
                                             device_id=(px, py, c), device_id_type=MESH_ID).wait_recv()
        for cp in sends:
            cp.wait_send()
        for local in started:
            local.wait()

    return pl.pallas_call(
        body, name=name, in_specs=[_ANY] * n, out_specs=[_ANY] * n,
        out_shape=[jax.ShapeDtypeStruct((4, h) + s.shape[1:], s.dtype) for s, h in zip(shards, half)],
        scratch_shapes=[pltpu.SemaphoreType.DMA((3 * n,)), pltpu.SemaphoreType.DMA((3 * n,)),
                        pltpu.SemaphoreType.DMA((n,))],
    )(*shards)


def _sibling_swap_halves(g, name):
    q, rows, cols = g.shape
    rh = rows // 2

    def body(g_ref, out_ref, send_sem, recv_sem):
        x, y, c = _my_xyc()
        src = g_ref.at[:, pl.ds((1 - c) * rh, rh), :]
        cp = pltpu.make_async_remote_copy(src_ref=src, dst_ref=out_ref, send_sem=send_sem, recv_sem=recv_sem,
                                          device_id=(x, y, 1 - c), device_id_type=MESH_ID)
        cp.start()
        cp.wait()

    return pl.pallas_call(
        body, name=name, in_specs=[_ANY], out_specs=_ANY,
        out_shape=jax.ShapeDtypeStruct((q, rh, cols), g.dtype),
        scratch_shapes=[pltpu.SemaphoreType.DMA, pltpu.SemaphoreType.DMA],
    )(g)


def _add_own_half(g, other, name):
    q, rows, cols = g.shape
    rh = rows // 2
    tr = _pick(rh, (512, 256, 128, 64, 32, 16, 8))
    nb = rh // tr
    cidx = lax.axis_index("c").astype(jnp.int32).reshape(1)

    def body(c_ref, g_ref, o_ref, out_ref):
        out_ref[...] = (g_ref[...] + o_ref[...]).astype(out_ref.dtype)

    grid_spec = pltpu.PrefetchScalarGridSpec(
        num_scalar_prefetch=1, grid=(q, nb),
        in_specs=[pl.BlockSpec((1, tr, cols), lambda a, i, c_ref: (a, c_ref[0] * nb + i, 0)),
                  pl.BlockSpec((1, tr, cols), lambda a, i, c_ref: (a, i, 0))],
        out_specs=pl.BlockSpec((1, tr, cols), lambda a, i, c_ref: (a, i, 0)))
    return pl.pallas_call(
        body, name=name, grid_spec=grid_spec, out_shape=jax.ShapeDtypeStruct((q, rh, cols), BF16),
        compiler_params=_cparams(("parallel", "parallel")),
    )(cidx, g, other)


def _chip_exchange(s, name):
    def body(s_ref, out_ref, send_sems, recv_sems, loc_sem):
        x, y, c = _my_xyc()
        myq = 2 * x + y
        local = pltpu.make_async_copy(s_ref.at[myq], out_ref.at[myq], loc_sem)
        local.start()
        chips = [(1 - x, y), (x, 1 - y), (1 - x, 1 - y)]
        sends = []
        for j, (px, py) in enumerate(chips):
            cp = pltpu.make_async_remote_copy(src_ref=s_ref.at[2 * px + py], dst_ref=out_ref.at[myq],
                                              send_sem=send_sems.at[j], recv_sem=recv_sems.at[j],
                                              device_id=(px, py, c), device_id_type=MESH_ID)
            cp.start()
            sends.append(cp)
        for j, (px, py) in enumerate(chips):
            pltpu.make_async_remote_copy(src_ref=s_ref.at[myq], dst_ref=out_ref.at[2 * px + py],
                                         send_sem=send_sems.at[j], recv_sem=recv_sems.at[j],
                                         device_id=(px, py, c), device_id_type=MESH_ID).wait_recv()
        for cp in sends:
            cp.wait_send()
        local.wait()

    return pl.pallas_call(
        body, name=name, in_specs=[_ANY], out_specs=_ANY, out_shape=jax.ShapeDtypeStruct(s.shape, s.dtype),
        scratch_shapes=[pltpu.SemaphoreType.DMA((3,)), pltpu.SemaphoreType.DMA((3,)), pltpu.SemaphoreType.DMA],
    )(s)


def _sum_leading(a, name):
    q, rows, cols = a.shape
    tr = _pick(rows, (512, 256, 128, 64, 32, 16, 8))

    def body(a_ref, out_ref):
        acc = a_ref[0].astype(F32)
        for j in range(1, q):
            acc = acc + a_ref[j].astype(F32)
        out_ref[...] = acc

    return pl.pallas_call(
        body, name=name, grid=(rows // tr,), in_specs=[pl.BlockSpec((q, tr, cols), lambda i: (0, i, 0))],
        out_specs=pl.BlockSpec((tr, cols), lambda i: (i, 0)), out_shape=jax.ShapeDtypeStruct((rows, cols), F32),
        compiler_params=_cparams(("parallel",)),
    )(a)


def _sibling_swap(arrays, name):
    n = len(arrays)

    def body(*refs):
        a_refs, out_refs = refs[:n], refs[n:2 * n]
        send_sems, recv_sems = refs[2 * n:]
        x, y, c = _my_xyc()
        copies = []
        for i in range(n):
            cp = pltpu.make_async_remote_copy(src_ref=a_refs[i], dst_ref=out_refs[i], send_sem=send_sems.at[i],
                                              recv_sem=recv_sems.at[i], device_id=(x, y, 1 - c),
                                              device_id_type=MESH_ID)
            cp.start()
            copies.append(cp)
        for cp in copies:
            cp.wait()

    return pl.pallas_call(
        body, name=name, in_specs=[_ANY] * n, out_specs=[_ANY] * n,
        out_shape=[jax.ShapeDtypeStruct(a.shape, a.dtype) for a in arrays],
        scratch_shapes=[pltpu.SemaphoreType.DMA((n,)), pltpu.SemaphoreType.DMA((n,))],
    )(*arrays)


def _by_core(mine, other):
    c = lax.axis_index("c")
    return jnp.where(c == 0, jnp.stack([mine, other]), jnp.stack([other, mine]))


def _all_reduce_small(vec, name):
    rows, cols = vec.shape

    def body(v_ref, out_ref, buf, send_sems, recv_sems):
        x, y, c = _my_xyc()
        me = 4 * x + 2 * y + c
        buf[me] = v_ref[...]
        sends = []
        for kk in range(1, 8):
            peer = (1 - x if kk & 4 else x, 1 - y if kk & 2 else y, 1 - c if kk & 1 else c)
            cp = pltpu.make_async_remote_copy(src_ref=v_ref, dst_ref=buf.at[me], send_sem=send_sems.at[kk - 1],
                                              recv_sem=recv_sems.at[kk - 1], device_id=peer, device_id_type=MESH_ID)
            cp.start()
            sends.append(cp)
        for kk in range(1, 8):
            px, py, pc = (1 - x if kk & 4 else x, 1 - y if kk & 2 else y, 1 - c if kk & 1 else c)
            pltpu.make_async_remote_copy(src_ref=v_ref, dst_ref=buf.at[4 * px + 2 * py + pc],
                                         send_sem=send_sems.at[kk - 1], recv_sem=recv_sems.at[kk - 1],
                                         device_id=(px, py, pc), device_id_type=MESH_ID).wait_recv()
        for cp in sends:
            cp.wait_send()
        acc = buf[0]
        for j in range(1, 8):
            acc = acc + buf[j]
        out_ref[...] = acc

    vm = pl.BlockSpec(memory_space=pltpu.VMEM)
    return pl.pallas_call(
        body, name=name, in_specs=[vm], out_specs=vm, out_shape=jax.ShapeDtypeStruct((rows, cols), F32),
        scratch_shapes=[pltpu.VMEM((8, rows, cols), F32), pltpu.SemaphoreType.DMA((7,)), pltpu.SemaphoreType.DMA((7,))],
        compiler_params=pltpu.CompilerParams(vmem_limit_bytes=VMEM_LIMIT_BYTES),
    )(vec)


def _reduce_scatter(g):
    other = _sibling_swap_halves(g, "rs_swap_halves")
    s = _add_own_half(g, other, "rs_add_halves")
    recv = _chip_exchange(s, "rs_chip_exchange")
    r = _sum_leading(recv, "rs_sum_chips")
    (other_r,) = _sibling_swap([r], "rs_join_halves")
    return _by_core(r, other_r).reshape(g.shape[1], g.shape[2])


PACK_COLS = 1024
PACK_ROW_MULTIPLE = 1024
SHARDED = (("even_w_in", 2), ("conv_w", 2), ("even_w_out", 1), ("odd_w_in", 2), ("q_norm_w", 1), ("w_uq", 2),
           ("kv_norm_w", 1), ("w_ukv", 2), ("odd_w_out", 1), ("ffn_w_gate", 2), ("ffn_w_up", 2), ("ffn_w_down", 1))
KEEP_F32 = ("conv_w", "q_norm_w", "kv_norm_w")
REPLICATED = ("pool_w", "pool_scale", "conv_b", "dt_bias", "a_log", "d_skip", "ssm_norm_w", "fgate_b",
              "ln_mix_g", "ln_mix_b", "ln_ffn_g", "ln_ffn_b")


def _gather_weights(shards):
    sent = [shards[name] if name in KEEP_F32 else shards[name].astype(BF16) for name, _ in SHARDED]
    mine = _chip_allgather_core_half(sent, "weights_allgather")
    others = _sibling_swap(mine, "weights_sibling_swap")
    out = {}
    for (name, axis), m, o in zip(SHARDED, mine, others):
        halves = _by_core(m, o)
        out[name] = jnp.concatenate([jnp.concatenate([halves[h, q] for q in range(4)], axis=axis) for h in range(2)],
                                    axis=0)
    return out


def _pack_grads(grads, shards):
    pieces = []
    for name, axis in SHARDED:
        width = shards[name].shape[axis]
        for g in grads[name]:
            ax = axis - 1
            split = g.reshape(g.shape[:ax] + (4, width) + g.shape[ax + 1:])
            pieces.append(jnp.moveaxis(split, ax, 0).reshape(4, -1))
    n = sum(p.shape[1] for p in pieces)
    per = PACK_COLS * PACK_ROW_MULTIPLE
    total = -(-n // per) * per
    pieces.append(jnp.zeros((4, total - n), F32))
    return jnp.concatenate(pieces, axis=1).reshape(4, total // PACK_COLS, PACK_COLS)


def _unpack_reduced(reduced, shards):
    flat = reduced.reshape(-1)
    out, off = {}, 0
    for name, _ in SHARDED:
        shp = shards[name].shape
        size = math.prod(shp)
        out[name] = flat[off:off + size].reshape(shp)
        off += size
    return out


def _to_heads(t, nh):
    rows, width = t.shape
    return t.reshape(rows, nh, width // nh).transpose(1, 0, 2)


def _from_heads(t):
    nh, rows, d = t.shape
    return t.transpose(1, 0, 2).reshape(rows, nh * d)


def _rope_tables(rows):
    half = MLA_ROPE // 2
    freqs = jnp.power(ROPE_THETA, -jnp.arange(half, dtype=F32) / half)
    ang = jnp.arange(rows, dtype=F32)[:, None] * freqs[None, :]
    cos, sin = jnp.cos(ang), jnp.sin(ang)
    cos_q, sin_q = jnp.tile(cos, (1, MLA_HEADS)), jnp.tile(sin, (1, MLA_HEADS))
    zeros = jnp.zeros((rows, LANES - MLA_ROPE), F32)
    cos_k = jnp.concatenate([cos, cos, zeros], axis=1)
    sin_k = jnp.concatenate([sin, sin, zeros], axis=1)
    r = lax.broadcasted_iota(jnp.int32, (LANES, LANES), 0)
    c = lax.broadcasted_iota(jnp.int32, (LANES, LANES), 1)
    rot = (jnp.where((r == c + half) & (c < half), -1.0, 0.0)
           + jnp.where((c == r + half) & (r < half), 1.0, 0.0)).astype(F32)
    return cos_q, sin_q, cos_k, sin_k, rot


def _pad_cols(a, width):
    return jnp.pad(a, ((0, 0), (0, width - a.shape[1])))


def _prep_even_w_in(w):
    return _pad_cols(w, EVEN_IN_PAD)


_ODD_CUT = (1536, 1544, 2312, 2344)


def _prep_odd_w_in(w):
    c0, c1, c2, c3 = _ODD_CUT
    return jnp.concatenate([w[:, :c0], w[:, c1:c2], _pad_cols(w[:, c2:c3], LANES), _pad_cols(w[:, c0:c1], LANES)],
                           axis=1)


def _unprep_odd_w_in(g):
    c0, c1, c2, c3 = _ODD_CUT
    n1 = c0 + (c2 - c1)
    return jnp.concatenate([g[:, :c0], g[:, n1 + LANES:n1 + LANES + (c1 - c0)], g[:, c0:n1],
                            g[:, n1:n1 + (c3 - c2)]], axis=1)


def _prep_w_uq(w):
    r = w.reshape(w.shape[0], MLA_HEADS, MLA_NOPE + MLA_ROPE)
    half = MLA_ROPE // 2
    return jnp.concatenate([r[:, :, :MLA_NOPE].reshape(w.shape[0], -1),
                            r[:, :, MLA_NOPE:MLA_NOPE + half].reshape(w.shape[0], -1),
                            r[:, :, MLA_NOPE + half:].reshape(w.shape[0], -1)], axis=1)


def _unprep_w_uq(g):
    rows = g.shape[0]
    half = MLA_ROPE // 2
    nope = MLA_HEADS * MLA_NOPE
    return jnp.concatenate([g[:, :nope].reshape(rows, MLA_HEADS, MLA_NOPE),
                            g[:, nope:nope + LANES].reshape(rows, MLA_HEADS, half),
                            g[:, nope + LANES:].reshape(rows, MLA_HEADS, half)], axis=2).reshape(rows, -1)


def _row(v, width=None):
    v = v.reshape(1, -1)
    return v if width is None else _pad_cols(v, width)


def _even_forward(x_bf, w, i):
    rows = x_bf.shape[0]
    proj = _mm(x_bf, w["even_w_in"][i], "nn", F32, "even_proj")
    u, z, xbc, dtr = (proj[:, :512], proj[:, 512:1536], proj[:, 1536:3072], proj[:, 3072:3200])
    n_row = rows // ROW_TILE
    pool_p = (w["pool_w"][i], _row(w["pool_scale"][i]))
    (y_pool,), pool_c = _scan_fwd(_f_pool, "pool_fwd", n_row, [u], [], pool_p, [], [(16, POOL_WIDTH)],
                                  [(POOL_WIDTH, BF16)])
    conv_p = (w["conv_w"][i], _row(w["conv_b"][i]))
    (xa,), conv_c = _scan_fwd(_f_conv, "conv_fwd", n_row, [xbc], [], conv_p, [], [(8, SSM_CONV_DIM)],
                              [(SSM_CONV_DIM, F32)])
    ssd_p = (_row(w["dt_bias"][i], LANES), _row(w["a_log"][i], LANES), _row(w["d_skip"][i], LANES),
             _row(w["ssm_norm_w"][i]))
    (y_ssm,), ssd_c = _scan_fwd(_f_ssd, "ssd_fwd", rows // SSM_CHUNK, [xa, dtr, z], [], ssd_p, [],
                                [(SSM_STATE, SSM_D_INNER)], [(SSM_D_INNER, BF16)])
    mix = jnp.concatenate([y_pool, y_ssm], axis=1)
    saved = dict(u=u, z=z, xbc=xbc, dtr=dtr, xa=xa, mix=mix, pool_p=pool_p, pool_c=pool_c, conv_p=conv_p,
                 conv_c=conv_c, ssd_p=ssd_p, ssd_c=ssd_c)
    return mix, saved


def _even_backward(dmix, x_bf, sv, w, i, d_r1):
    rows = x_bf.shape[0]
    n_row = rows // ROW_TILE
    dy_pool, dy_ssm = dmix[:, :POOL_WIDTH], dmix[:, POOL_WIDTH:]
    (du,), (g_pool_w, g_pool_scale) = _scan_bwd(_f_pool, "pool_bwd", n_row, [sv["u"]], [], sv["pool_p"], [],
                                                sv["pool_c"], [dy_pool])
    (dxa, ddtr, dz), (g_dt_bias, g_a_log, g_d_skip, g_norm_w) = _scan_bwd(
        _f_ssd, "ssd_bwd", rows // SSM_CHUNK, [sv["xa"], sv["dtr"], sv["z"]], [], sv["ssd_p"], [], sv["ssd_c"],
        [dy_ssm])
    (dxbc,), (g_conv_w, g_conv_b) = _scan_bwd(_f_conv, "conv_bwd", n_row, [sv["xbc"]], [], sv["conv_p"], [],
                                              sv["conv_c"], [dxa])
    dproj = jnp.concatenate([du, dz, dxbc, ddtr], axis=1).astype(BF16)
    g_w_in = _mm(x_bf, dproj, "tn", F32, "even_dw_in")[:, :EVEN_IN]
    dx = _mm(dproj, w["even_w_in"][i], "nt", F32, "even_dx", extra=d_r1, alpha=ALPHA)
    grads = dict(even_w_in=g_w_in, pool_w=g_pool_w, pool_scale=g_pool_scale[0], conv_w=g_conv_w, conv_b=g_conv_b[0],
                 dt_bias=g_dt_bias[0, :SSM_HEADS], a_log=g_a_log[0, :SSM_HEADS], d_skip=g_d_skip[0, :SSM_HEADS],
                 ssm_norm_w=g_norm_w[0])
    return dx, grads


def _odd_forward(x_bf, w, i, tables):
    rows = x_bf.shape[0]
    cos_q, sin_q, cos_k, sin_k, rot = tables
    proj = _mm(x_bf, w["odd_w_in"][i], "nn", F32, "odd_proj")
    qf, kf, vf = (_to_heads((proj[:, j * 512:(j + 1) * 512] * sc).astype(BF16), FOX_HEADS)
                  for j, sc in enumerate((FOX_SCALE, 1.0, 1.0)))
    cq, ckv = proj[:, 1536:2048], proj[:, 2048:2304]
    kr, fl = proj[:, 2304:2432], proj[:, 2432:2560]
    n_row = rows // ROW_TILE
    fox_p = (_row(w["fgate_b"][i], LANES),)
    n_gate = rows // min(ATTN_TILE, rows)
    (fcum,), fox_c = _scan_fwd(_f_fox_gate, "fox_gate_fwd", n_gate, [fl], [], fox_p, [], [(8, LANES)], [(LANES, F32)])
    fc_heads = fcum[:, :FOX_HEADS].T
    fq, fk = fc_heads[:, :, None], fc_heads[:, None, :]
    o_fox, lse_fox = _attn_fwd(qf, kf, vf, fq, fk, "fox_attn_fwd")

    prep_p = (_row(w["q_norm_w"][i]), _row(w["kv_norm_w"][i]))
    (cqn, ckvn, krr), _ = _scan_fwd(_f_mla_prep, "mla_prep_fwd", n_row, [cq, ckv, kr], [cos_k, sin_k], prep_p,
                                    [rot], [], [(MLA_Q_RANK, BF16), (MLA_KV_RANK, BF16), (LANES, BF16)])
    q_flat = _mm(cqn, w["w_uq"][i], "nn", F32, "mla_q_up")
    (q_rope,), _ = _scan_fwd(_f_rope_q, "rope_q_fwd", n_row, [q_flat], [cos_q, sin_q], [], [], [],
                             [(q_flat.shape[1], BF16)])
    kv = _mm(ckvn, w["w_ukv"][i], "nn", BF16, "mla_kv_up")
    nope = MLA_HEADS * MLA_NOPE
    half = MLA_ROPE // 2
    q_m = jnp.concatenate([_to_heads(q_rope[:, :nope], MLA_HEADS), _to_heads(q_rope[:, nope:nope + LANES], MLA_HEADS),
                           _to_heads(q_rope[:, nope + LANES:], MLA_HEADS)], axis=2)
    kv_h = _to_heads(kv, MLA_HEADS)
    k_rope = jnp.broadcast_to(krr[None, :, :MLA_ROPE], (MLA_HEADS, rows, MLA_ROPE))
    k_m = jnp.concatenate([kv_h[:, :, :MLA_NOPE], k_rope], axis=2)
    v_m = kv_h[:, :, MLA_NOPE:]
    o_mla, lse_mla = _attn_fwd(q_m, k_m, v_m, None, None, "mla_attn_fwd")
    mix = jnp.concatenate([_from_heads(o_fox), _from_heads(o_mla)], axis=1)
    saved = dict(qf=qf, kf=kf, vf=vf, fq=fq, fk=fk, o_fox=o_fox, lse_fox=lse_fox, fl=fl, fox_p=fox_p, fox_c=fox_c,
                 cq=cq, ckv=ckv, kr=kr, prep_p=prep_p, cqn=cqn, ckvn=ckvn, q_flat=q_flat, q_m=q_m, k_m=k_m, v_m=v_m,
                 o_mla=o_mla, lse_mla=lse_mla, mix=mix)
    return mix, saved


def _odd_backward(dmix, x_bf, sv, w, i, d_r1, tables):
    rows = x_bf.shape[0]
    cos_q, sin_q, cos_k, sin_k, rot = tables
    n_row = rows // ROW_TILE
    nope = MLA_HEADS * MLA_NOPE
    half = MLA_ROPE // 2
    do_fox = _to_heads(dmix[:, :FOX_WIDTH], FOX_HEADS)
    do_mla = _to_heads(dmix[:, FOX_WIDTH:], MLA_HEADS)
    fox_args = (sv["qf"], sv["kf"], sv["vf"], sv["o_fox"], do_fox, sv["lse_fox"], sv["fq"], sv["fk"])
    dqf, delta_fox = _attn_bwd_dq(*fox_args, "fox_attn_dq")
    dkf, dvf, dfk = _attn_bwd_dkv(*fox_args[:3], delta_fox, *fox_args[4:], "fox_attn_dkv")
    dfcum = _pad_cols(dfk[:, 0, :].T, LANES)
    n_gate = rows // min(ATTN_TILE, rows)
    (dfl,), (g_fb,) = _scan_bwd(_f_fox_gate, "fox_gate_bwd", n_gate, [sv["fl"]], [], sv["fox_p"], [], sv["fox_c"],
                                [dfcum])
    mla_args = (sv["q_m"], sv["k_m"], sv["v_m"], sv["o_mla"], do_mla, sv["lse_mla"], None, None)
    dq_m, dk_m, dv_m = _attn_bwd_fused(*mla_args[:6], "mla_attn_bwd")
    dq_rope = jnp.concatenate([_from_heads(dq_m[:, :, :MLA_NOPE]), _from_heads(dq_m[:, :, MLA_NOPE:MLA_NOPE + half]),
                               _from_heads(dq_m[:, :, MLA_NOPE + half:])], axis=1)
    (dq_flat,), _ = _scan_bwd(_f_rope_q, "rope_q_bwd", n_row, [sv["q_flat"]], [cos_q, sin_q], [], [], [], [dq_rope])
    g_w_uq = _mm(sv["cqn"], dq_flat, "tn", F32, "mla_dw_uq")
    dcqn = _mm(dq_flat, w["w_uq"][i], "nt", BF16, "mla_dcqn")
    dkv = _from_heads(jnp.concatenate([dk_m[:, :, :MLA_NOPE], dv_m], axis=2))
    g_w_ukv = _mm(sv["ckvn"], dkv, "tn", F32, "mla_dw_ukv")
    dckvn = _mm(dkv, w["w_ukv"][i], "nt", BF16, "mla_dckvn")
    dkrr = _head_sum(dk_m, "mla_dk_rope_sum")
    (dcq, dckv, dkr), (g_qw, g_kvw) = _scan_bwd(_f_mla_prep, "mla_prep_bwd", n_row, [sv["cq"], sv["ckv"], sv["kr"]],
                                                [cos_k, sin_k], sv["prep_p"], [rot], [], [dcqn, dckvn, dkrr])
    dproj = jnp.concatenate([_from_heads(dqf).astype(F32) * FOX_SCALE, _from_heads(dkf).astype(F32), _from_heads(dvf).astype(F32),
                             dcq, dckv, dkr, dfl], axis=1).astype(BF16)
    g_w_in = _mm(x_bf, dproj, "tn", F32, "odd_dw_in")
    dx = _mm(dproj, w["odd_w_in"][i], "nt", F32, "odd_dx", extra=d_r1, alpha=ALPHA)
    grads = dict(odd_w_in=_unprep_odd_w_in(g_w_in), fgate_b=g_fb[0, :FOX_HEADS], q_norm_w=g_qw[0], kv_norm_w=g_kvw[0],
                 w_uq=_unprep_w_uq(g_w_uq), w_ukv=g_w_ukv)
    return dx, grads


def _head_sum(dk_m, name):
    H, T, dk = dk_m.shape
    tt = _pick(T, (512, 256, 128))

    def body(d_ref, o_ref):
        acc = d_ref[0].astype(F32)
        for h in range(1, H):
            acc = acc + d_ref[h].astype(F32)
        o_ref[...] = jnp.concatenate([acc[:, MLA_NOPE:], jnp.zeros((tt, LANES - MLA_ROPE), F32)], axis=1).astype(BF16)

    return pl.pallas_call(
        body, name=name, grid=(T // tt,), in_specs=[pl.BlockSpec((H, tt, dk), lambda i: (0, i, 0))],
        out_specs=pl.BlockSpec((tt, LANES), lambda i: (i, 0)), out_shape=jax.ShapeDtypeStruct((T, LANES), BF16),
        compiler_params=_cparams(("parallel",)),
    )(dk_m)


def _local_step(x, target, w, small):
    rows = x.shape[0]
    n_row = rows // ROW_TILE
    tables = _rope_tables(rows)
    saved = []
    x_f32 = x
    x_bf = x.astype(BF16)
    for l in range(DEPTH):
        i = l // 2
        if l % 2 == 0:
            mix, sv = _even_forward(x_bf, w, i)
            w_out = w["even_w_out"][i]
        else:
            mix, sv = _odd_forward(x_bf, w, i, tables)
            w_out = w["odd_w_out"][i]
        r1 = _mm(mix, w_out, "nn", F32, "mix_out_even" if l % 2 == 0 else "mix_out_odd", extra=x_f32, alpha=ALPHA)
        ln1_p = (_row(small["ln_mix_g"][l]), _row(small["ln_mix_b"][l]))
        ln_outs = [(D_MODEL, F32), (D_MODEL, BF16)]
        (x_mid, x_mid_bf), _ = _scan_fwd(_f_ln_with_bf16, "ln_fwd", n_row, [r1], [], ln1_p, [], [], ln_outs)
        gu = _mm(x_mid_bf, w["ffn_w_gu"][l], "nn", BF16, "ffn_gu")
        (act,), _ = _scan_fwd(_f_act, "ffn_act_fwd", n_row, [gu], [], [], [], [], [(D_FF, BF16)])
        r2 = _mm(act, w["ffn_w_down"][l], "nn", F32, "ffn_down", extra=x_mid, alpha=ALPHA)
        ln2_p = (_row(small["ln_ffn_g"][l]), _row(small["ln_ffn_b"][l]))
        (x_out, x_out_bf), _ = _scan_fwd(_f_ln_with_bf16, "ln_fwd", n_row, [r2], [], ln2_p, [], [], ln_outs)
        saved.append(dict(sv=sv, x_bf=x_bf, r1=r1, ln1_p=ln1_p, x_mid_bf=x_mid_bf, gu=gu, act=act, r2=r2, ln2_p=ln2_p,
                          w_out=w_out))
        x_f32, x_bf = x_out, x_out_bf

    dy, loss_part = _loss_head(x_f32, target, "loss_head")
    loss = 0.5 * jnp.sum(loss_part) / D_MODEL

    layer_grads = []
    for l in reversed(range(DEPTH)):
        i = l // 2
        s = saved[l]
        (d_r2,), (g_ln2_g, g_ln2_b) = _scan_bwd(_f_ln, "ln_bwd", n_row, [s["r2"]], [], s["ln2_p"], [], [], [dy])
        g_down = _mm(s["act"], d_r2, "tn", F32, "ffn_dw_down")
        dact = _mm(d_r2, w["ffn_w_down"][l], "nt", BF16, "ffn_dact")
        (dgu,), _ = _scan_bwd(_f_act, "ffn_act_bwd", n_row, [s["gu"]], [], [], [], [], [dact])
        g_gu = _mm(s["x_mid_bf"], dgu, "tn", F32, "ffn_dw_gu")
        dx_mid = _mm(dgu, w["ffn_w_gu"][l], "nt", F32, "ffn_dx", extra=d_r2, alpha=ALPHA)
        (d_r1,), (g_ln1_g, g_ln1_b) = _scan_bwd(_f_ln, "ln_bwd", n_row, [s["r1"]], [], s["ln1_p"], [], [], [dx_mid])
        g_w_out = _mm(s["sv"]["mix"], d_r1, "tn", F32, "even_dw_out" if l % 2 == 0 else "odd_dw_out")
        dmix = _mm(d_r1, s["w_out"], "nt", BF16, "even_dmix" if l % 2 == 0 else "odd_dmix")
        if l % 2 == 0:
            dy, g = _even_backward(dmix, s["x_bf"], s["sv"], w, i, d_r1)
            g["even_w_out"] = g_w_out
        else:
            dy, g = _odd_backward(dmix, s["x_bf"], s["sv"], w, i, d_r1, tables)
            g["odd_w_out"] = g_w_out
        g.update(ffn_w_gate=g_gu[:, :D_FF], ffn_w_up=g_gu[:, D_FF:], ffn_w_down=g_down, ln_mix_g=g_ln1_g[0],
                 ln_mix_b=g_ln1_b[0], ln_ffn_g=g_ln2_g[0], ln_ffn_b=g_ln2_b[0])
        layer_grads.append((l, g))
    return loss, dy, layer_grads


EVEN_NAMES = ("even_w_in", "pool_w", "pool_scale", "conv_w", "conv_b", "dt_bias", "a_log", "d_skip", "ssm_norm_w",
              "even_w_out")
ODD_NAMES = ("odd_w_in", "fgate_b", "q_norm_w", "w_uq", "kv_norm_w", "w_ukv", "odd_w_out")
PER_LAYER_NAMES = ("ffn_w_gate", "ffn_w_up", "ffn_w_down", "ln_mix_g", "ln_mix_b", "ln_ffn_g", "ln_ffn_b")
WEIGHT_NAMES = EVEN_NAMES + ODD_NAMES + PER_LAYER_NAMES


def _grads_by_name(layer_grads):
    by_layer = dict(layer_grads)
    out = {}
    for n in EVEN_NAMES:
        out[n] = [by_layer[l][n] for l in range(0, DEPTH, 2)]
    for n in ODD_NAMES:
        out[n] = [by_layer[l][n] for l in range(1, DEPTH, 2)]
    for n in PER_LAYER_NAMES:
        out[n] = [by_layer[l][n] for l in range(DEPTH)]
    return out


def _prepare_weights(full):
    w = {}
    w["even_w_in"] = [_prep_even_w_in(full["even_w_in"][i]) for i in range(2)]
    w["even_w_out"] = [full["even_w_out"][i] for i in range(2)]
    w["odd_w_in"] = [_prep_odd_w_in(full["odd_w_in"][i]) for i in range(2)]
    w["w_uq"] = [_prep_w_uq(full["w_uq"][i]) for i in range(2)]
    w["w_ukv"] = [full["w_ukv"][i] for i in range(2)]
    w["odd_w_out"] = [full["odd_w_out"][i] for i in range(2)]
    w["ffn_w_gu"] = [jnp.concatenate([full["ffn_w_gate"][l], full["ffn_w_up"][l]], axis=1) for l in range(DEPTH)]
    w["ffn_w_down"] = [full["ffn_w_down"][l] for l in range(DEPTH)]
    for n in ("conv_w", "q_norm_w", "kv_norm_w"):
        w[n] = full[n]
    return w


def _flatten_small(grads):
    flat = jnp.concatenate([g.reshape(-1) for n in REPLICATED for g in grads[n]])
    n = flat.shape[0]
    per = LANES * 8
    total = -(-n // per) * per
    return jnp.pad(flat, (0, total - n)).reshape(total // LANES, LANES)


def _unflatten_small(mat, like):
    flat = mat.reshape(-1)
    out, off = {}, 0
    for n in REPLICATED:
        size = math.prod(like[n].shape)
        out[n] = flat[off:off + size].reshape(like[n].shape)
        off += size
    return out


def kernel(x, even_w_in, pool_w, pool_scale, conv_w, conv_b, dt_bias, a_log, d_skip, ssm_norm_w, even_w_out, odd_w_in, fgate_b, q_norm_w, w_uq, kv_norm_w, w_ukv, odd_w_out, ffn_w_gate, ffn_w_up, ffn_w_down, ln_mix_g, ln_mix_b, ln_ffn_g, ln_ffn_b, loss_target, m_even_w_in, m_pool_w, m_pool_scale, m_conv_w, m_conv_b, m_dt_bias, m_a_log, m_d_skip, m_ssm_norm_w, m_even_w_out, m_odd_w_in, m_fgate_b, m_q_norm_w, m_w_uq, m_kv_norm_w, m_w_ukv, m_odd_w_out, m_ffn_w_gate, m_ffn_w_up, m_ffn_w_down, m_ln_mix_g, m_ln_mix_b, m_ln_ffn_g, m_ln_ffn_b, v_even_w_in, v_pool_w, v_pool_scale, v_conv_w, v_conv_b, v_dt_bias, v_a_log, v_d_skip, v_ssm_norm_w, v_even_w_out, v_odd_w_in, v_fgate_b, v_q_norm_w, v_w_uq, v_kv_norm_w, v_w_ukv, v_odd_w_out, v_ffn_w_gate, v_ffn_w_up, v_ffn_w_down, v_ln_mix_g, v_ln_mix_b, v_ln_ffn_g, v_ln_ffn_b):
    weights = dict(even_w_in=even_w_in, pool_w=pool_w, pool_scale=pool_scale, conv_w=conv_w, conv_b=conv_b,
                   dt_bias=dt_bias, a_log=a_log, d_skip=d_skip, ssm_norm_w=ssm_norm_w, even_w_out=even_w_out,
                   odd_w_in=odd_w_in, fgate_b=fgate_b, q_norm_w=q_norm_w, w_uq=w_uq, kv_norm_w=kv_norm_w, w_ukv=w_ukv,
                   odd_w_out=odd_w_out, ffn_w_gate=ffn_w_gate, ffn_w_up=ffn_w_up, ffn_w_down=ffn_w_down,
                   ln_mix_g=ln_mix_g, ln_mix_b=ln_mix_b, ln_ffn_g=ln_ffn_g, ln_ffn_b=ln_ffn_b)
    m_in = dict(even_w_in=m_even_w_in, pool_w=m_pool_w, pool_scale=m_pool_scale, conv_w=m_conv_w, conv_b=m_conv_b,
                dt_bias=m_dt_bias, a_log=m_a_log, d_skip=m_d_skip, ssm_norm_w=m_ssm_norm_w, even_w_out=m_even_w_out,
                odd_w_in=m_odd_w_in, fgate_b=m_fgate_b, q_norm_w=m_q_norm_w, w_uq=m_w_uq, kv_norm_w=m_kv_norm_w,
                w_ukv=m_w_ukv, odd_w_out=m_odd_w_out, ffn_w_gate=m_ffn_w_gate, ffn_w_up=m_ffn_w_up,
                ffn_w_down=m_ffn_w_down, ln_mix_g=m_ln_mix_g, ln_mix_b=m_ln_mix_b, ln_ffn_g=m_ln_ffn_g,
                ln_ffn_b=m_ln_ffn_b)
    v_in = dict(even_w_in=v_even_w_in, pool_w=v_pool_w, pool_scale=v_pool_scale, conv_w=v_conv_w, conv_b=v_conv_b,
                dt_bias=v_dt_bias, a_log=v_a_log, d_skip=v_d_skip, ssm_norm_w=v_ssm_norm_w, even_w_out=v_even_w_out,
                odd_w_in=v_odd_w_in, fgate_b=v_fgate_b, q_norm_w=v_q_norm_w, w_uq=v_w_uq, kv_norm_w=v_kv_norm_w,
                w_ukv=v_w_ukv, odd_w_out=v_odd_w_out, ffn_w_gate=v_ffn_w_gate, ffn_w_up=v_ffn_w_up,
                ffn_w_down=v_ffn_w_down, ln_mix_g=v_ln_mix_g, ln_mix_b=v_ln_mix_b, ln_ffn_g=v_ln_ffn_g,
                ln_ffn_b=v_ln_ffn_b)
    shards = {n: weights[n] for n, _ in SHARDED}

    w = _prepare_weights(_gather_weights(shards))
    small = {n: weights[n] for n in REPLICATED}
    w.update(small)

    loss_local, dx, layer_grads = _local_step(x[0], loss_target[0], w, small)
    grads_full = _grads_by_name(layer_grads)

    reduced = _reduce_scatter(_pack_grads(grads_full, shards))
    grads = _unpack_reduced(reduced, shards)
    small_sum = _all_reduce_small(_flatten_small(grads_full), "small_grads_allreduce")
    grads.update(_unflatten_small(small_sum, small))
    loss = lax.psum(loss_local, ("x", "y", "c"))

    deltas, new_m, new_v = {}, {}, {}
    for n in WEIGHT_NAMES:
        deltas[n], new_m[n], new_v[n] = _adamw(weights[n], grads[n], m_in[n], v_in[n], "adamw_" + n)
    return (loss, dx[None], *[grads[n] for n in WEIGHT_NAMES], *[deltas[n] for n in WEIGHT_NAMES],
            *[new_m[n] for n in WEIGHT_NAMES], *[new_v[n] for n in WEIGHT_NAMES])
```

```python
import functools
import math

import numpy as np
import jax
import jax.numpy as jnp
from jax import lax
from jax.experimental import pallas as pl
from jax.experimental.pallas import tpu as pltpu

F32 = jnp.float32
BF16 = jnp.bfloat16
HI = lax.Precision.HIGHEST
MESH_ID = pl.DeviceIdType.MESH

VMEM_LIMIT_BYTES = 56 * 1024 * 1024
LANES = 128

D_MODEL = 1024
DEPTH = 4
POOL_WINDOWS = (2, 4, 8, 16)
POOL_GROUP = 128
POOL_WIDTH = 512
SSM_D_INNER = 1024
SSM_HEAD_DIM = 64
SSM_HEADS = 16
SSM_GROUPS = 2
SSM_STATE = 128
SSM_CONV = 4
SSM_CHUNK = 128
SSM_CONV_DIM = 1536
EVEN_IN = 3088
EVEN_IN_PAD = 3200
FOX_HEADS = 8
FOX_WIDTH = 512
MLA_HEADS = 8
MLA_NOPE = 64
MLA_ROPE = 32
MLA_V = 64
MLA_Q_RANK = 512
MLA_KV_RANK = 256
ROPE_THETA = 10000.0
ODD_IN = 2344
ODD_IN_PAD = 2560
D_FF = 2816
ALPHA = (2 * DEPTH) ** 0.25
LN_EPS = 1e-5
RMS_EPS = 1e-6
ADAM_LR = 0.001
ADAM_B1 = 0.9
ADAM_B2 = 0.999
ADAM_EPS = 1e-08
ADAM_WD = 0.01
ADAM_STEP = 10
NEG_BIG = -1e30

ATTN_TILE = 512
ATTN_WIDE = 2048
FOX_SCALE = 0.125
MLA_SCALE = (MLA_NOPE + MLA_ROPE) ** -0.5
ROW_TILE = 256


def _cparams(sem=None):
    return pltpu.CompilerParams(dimension_semantics=sem, vmem_limit_bytes=VMEM_LIMIT_BYTES)


def _pick(d, prefs):
    for p in prefs:
        if d % p == 0:
            return p
    return d


_M_PREFS = (2048, 1024, 512, 640, 1408, 768, 384, 256, 128)
_N_PREFS = (1024, 1408, 512, 640, 768, 384, 256, 128)
_K_PREFS = (1024, 1408, 512, 640, 768, 256, 128)
MM_MAX_ACC_ELEMS = 1408 * 1024


def _mm(a, b, mode, out_dtype, name, extra=None, alpha=1.0):
    if mode == "nn":
        (m, k), (k2, n) = a.shape, b.shape
    elif mode == "nt":
        (m, k), (n, k2) = a.shape, b.shape
    else:
        (k, m), (k2, n) = a.shape, b.shape
    assert k == k2, (a.shape, b.shape, mode)
    tn, tk = _pick(n, _N_PREFS), _pick(k, _K_PREFS)
    tm = _pick(m, tuple(p for p in _M_PREFS if p * tn <= MM_MAX_ACC_ELEMS))
    nk = k // tk
    if mode == "nn":
        a_spec = pl.BlockSpec((tm, tk), lambda i, j, kk: (i, kk))
        b_spec = pl.BlockSpec((tk, tn), lambda i, j, kk: (kk, j))
        dims = (((1,), (0,)), ((), ()))
    elif mode == "nt":
        a_spec = pl.BlockSpec((tm, tk), lambda i, j, kk: (i, kk))
        b_spec = pl.BlockSpec((tn, tk), lambda i, j, kk: (j, kk))
        dims = (((1,), (1,)), ((), ()))
    else:
        a_spec = pl.BlockSpec((tk, tm), lambda i, j, kk: (kk, i))
        b_spec = pl.BlockSpec((tk, tn), lambda i, j, kk: (kk, j))
        dims = (((0,), (0,)), ((), ()))
    o_spec = pl.BlockSpec((tm, tn), lambda i, j, kk: (i, j))
    has_extra = extra is not None

    def body(*refs):
        if has_extra:
            a_ref, b_ref, e_ref, o_ref, acc = refs
        else:
            a_ref, b_ref, o_ref, acc = refs
        kk = pl.program_id(2)

        @pl.when(kk == 0)
        def _():
            acc[...] = jnp.zeros_like(acc)

        acc[...] += lax.dot_general(a_ref[...].astype(BF16), b_ref[...].astype(BF16), dims,
                                    preferred_element_type=F32)

        @pl.when(kk == nk - 1)
        def _():
            r = acc[...]
            if has_extra:
                r = r + alpha * e_ref[...].astype(F32)
            o_ref[...] = r.astype(o_ref.dtype)

    ins = [a, b] + ([extra] if has_extra else [])
    specs = [a_spec, b_spec] + ([o_spec] if has_extra else [])
    return pl.pallas_call(
        body, name=name, grid=(m // tm, n // tn, nk), in_specs=specs, out_specs=o_spec,
        out_shape=jax.ShapeDtypeStruct((m, n), out_dtype),
        scratch_shapes=[pltpu.VMEM((tm, tn), F32)],
        compiler_params=_cparams(("parallel", "parallel", "arbitrary")),
    )(*ins)


def _full_spec(shape):
    nd = len(shape)
    return pl.BlockSpec(tuple(shape), lambda i, _nd=nd: (0,) * _nd)


class _Cols:
    def __init__(self, arr, block, width):
        self.arr, self.block, self.width = arr, block, width
        self.shape, self.dtype = (arr.shape[0], width), arr.dtype


def _tile_arr(t):
    return t.arr if isinstance(t, _Cols) else t


def _tile_spec(t, tt, row_index):
    block = t.block if isinstance(t, _Cols) else 0
    return pl.BlockSpec((tt, t.shape[1]), lambda i: (row_index(i), block))


def _scan_fwd(f, name, n, tiles, ctiles, params, cparams, carry_shapes, out_defs):
    rows = tiles[0].shape[0]
    tt = rows // n
    nt, nct, npar, ncp, ncar, nout = len(tiles), len(ctiles), len(params), len(cparams), len(carry_shapes), len(out_defs)

    def body(*refs):
        pos = 0
        t_refs = refs[pos:pos + nt]; pos += nt
        ct_refs = refs[pos:pos + nct]; pos += nct
        p_refs = refs[pos:pos + npar]; pos += npar
        cp_refs = refs[pos:pos + ncp]; pos += ncp
        o_refs = refs[pos:pos + nout]; pos += nout
        cs_refs = refs[pos:pos + ncar]; pos += ncar
        c_scr = refs[pos:pos + ncar]
        i = pl.program_id(0)

        @pl.when(i == 0)
        def _():
            for c in c_scr:
                c[...] = jnp.zeros_like(c)

        carry = tuple(c[...] for c in c_scr)
        for s, c in zip(cs_refs, carry):
            s[0] = c
        new_carry, outs = f(carry, tuple(r[...] for r in t_refs), tuple(r[...] for r in ct_refs),
                            tuple(r[...] for r in p_refs), tuple(r[...] for r in cp_refs), i)
        for o_ref, o in zip(o_refs, outs):
            o_ref[...] = o
        for c, v in zip(c_scr, new_carry):
            c[...] = v

    tile_spec = lambda t: _tile_spec(t, tt, lambda i: i)
    in_specs = ([tile_spec(t) for t in tiles] + [tile_spec(t) for t in ctiles]
                + [_full_spec(p.shape) for p in params] + [_full_spec(p.shape) for p in cparams])
    out_specs = ([pl.BlockSpec((tt, c), lambda i: (i, 0)) for c, _ in out_defs]
                 + [pl.BlockSpec((1,) + tuple(s), lambda i: (i, 0, 0)) for s in carry_shapes])
    out_shape = ([jax.ShapeDtypeStruct((rows, c), dt) for c, dt in out_defs]
                 + [jax.ShapeDtypeStruct((n,) + tuple(s), F32) for s in carry_shapes])
    res = pl.pallas_call(
        body, name=name, grid=(n,), in_specs=in_specs, out_specs=out_specs, out_shape=out_shape,
        scratch_shapes=[pltpu.VMEM(tuple(s), F32) for s in carry_shapes],
        compiler_params=_cparams(("arbitrary",)),
    )(*map(_tile_arr, tiles), *map(_tile_arr, ctiles), *params, *cparams)
    return list(res[:nout]), list(res[nout:])


def _scan_bwd(f, name, n, tiles, ctiles, params, cparams, carries, douts):
    rows = tiles[0].shape[0]
    tt = rows // n
    nt, nct, npar, ncp, ncar, nout = len(tiles), len(ctiles), len(params), len(cparams), len(carries), len(douts)

    def body(*refs):
        pos = 0
        t_refs = refs[pos:pos + nt]; pos += nt
        ct_refs = refs[pos:pos + nct]; pos += nct
        p_refs = refs[pos:pos + npar]; pos += npar
        cp_refs = refs[pos:pos + ncp]; pos += ncp
        cs_refs = refs[pos:pos + ncar]; pos += ncar
        do_refs = refs[pos:pos + nout]; pos += nout
        dt_refs = refs[pos:pos + nt]; pos += nt
        dp_refs = refs[pos:pos + npar]; pos += npar
        dc_scr = refs[pos:pos + ncar]
        i = pl.program_id(0)

        @pl.when(i == 0)
        def _():
            for c in dc_scr:
                c[...] = jnp.zeros_like(c)
            for d in dp_refs:
                d[...] = jnp.zeros_like(d)

        ctv = tuple(r[...] for r in ct_refs)
        cpv = tuple(r[...] for r in cp_refs)

        def g(c, t, p):
            return f(c, t, ctv, p, cpv, n - 1 - i)

        _, vjp = jax.vjp(g, tuple(s[0] for s in cs_refs), tuple(r[...] for r in t_refs),
                         tuple(r[...] for r in p_refs))
        dc, dt, dp = vjp((tuple(c[...] for c in dc_scr), tuple(r[...] for r in do_refs)))
        for r, v in zip(dt_refs, dt):
            r[...] = v
        for r, v in zip(dp_refs, dp):
            r[...] += v
        for c, v in zip(dc_scr, dc):
            c[...] = v

    rev_tile = lambda t: _tile_spec(t, tt, lambda i: n - 1 - i)
    rev_out = lambda t: pl.BlockSpec((tt, t.shape[1]), lambda i: (n - 1 - i, 0))
    in_specs = ([rev_tile(t) for t in tiles] + [rev_tile(t) for t in ctiles]
                + [_full_spec(p.shape) for p in params] + [_full_spec(p.shape) for p in cparams]
                + [pl.BlockSpec((1,) + tuple(c.shape[1:]), lambda i: (n - 1 - i, 0, 0)) for c in carries]
                + [rev_tile(d) for d in douts])
    out_specs = [rev_out(t) for t in tiles] + [_full_spec(p.shape) for p in params]
    out_shape = ([jax.ShapeDtypeStruct(t.shape, t.dtype) for t in tiles]
                 + [jax.ShapeDtypeStruct(p.shape, F32) for p in params])
    res = pl.pallas_call(
        body, name=name, grid=(n,), in_specs=in_specs, out_specs=out_specs, out_shape=out_shape,
        scratch_shapes=[pltpu.VMEM(tuple(c.shape[1:]), F32) for c in carries],
        compiler_params=_cparams(("arbitrary",)),
    )(*map(_tile_arr, tiles), *map(_tile_arr, ctiles), *params, *cparams, *carries, *map(_tile_arr, douts))
    return list(res[:nt]), list(res[nt:])


def _silu(x):
    return x * jax.nn.sigmoid(x)


def _softplus(x):
    return jnp.maximum(x, 0.0) + jnp.log(1.0 + jnp.exp(-jnp.abs(x)))


def _f_ln(carry, tiles, ctiles, params, cparams, idx):
    (r,), (g, b) = tiles, params
    mu = jnp.mean(r, axis=-1, keepdims=True)
    xc = r - mu
    var = jnp.mean(xc * xc, axis=-1, keepdims=True)
    return (), (xc * lax.rsqrt(var + LN_EPS) * g + b,)


def _f_ln_with_bf16(carry, tiles, ctiles, params, cparams, idx):
    _, (y,) = _f_ln(carry, tiles, ctiles, params, cparams, idx)
    return (), (y, y.astype(BF16))


def _f_act(carry, tiles, ctiles, params, cparams, idx):
    (gu,) = tiles
    g = gu[:, :D_FF].astype(F32)
    u = gu[:, D_FF:].astype(F32)
    return (), ((_silu(g) * u).astype(BF16),)


def _f_pool(carry, tiles, ctiles, params, cparams, idx):
    (prev,), (u,), (pool_w, pool_scale) = carry, tiles, params
    tt = u.shape[0]
    halo = prev.shape[0]
    ext = jnp.concatenate([prev, u], axis=0)
    pos = idx * tt + lax.broadcasted_iota(jnp.int32, (tt, 1), 0)
    ys = []
    for g, w in enumerate(POOL_WINDOWS):
        lo, hi = g * POOL_GROUP, (g + 1) * POOL_GROUP
        eg = ext[:, lo:hi]
        s = eg[halo:halo + tt]
        for j in range(1, w):
            s = s + eg[halo - j:halo - j + tt]
        count = jnp.minimum(pos + 1, w).astype(F32)
        diff = s / count - u[:, lo:hi]
        ys.append(jnp.dot(diff.astype(BF16), pool_w[g].astype(BF16), preferred_element_type=F32))
    y = jnp.concatenate(ys, axis=1) * pool_scale
    return (u[tt - halo:, :],), (y.astype(BF16),)


def _f_conv(carry, tiles, ctiles, params, cparams, idx):
    (prev,), (xbc,), (conv_w, conv_b) = carry, tiles, params
    tt = xbc.shape[0]
    halo = prev.shape[0]
    ext = jnp.concatenate([prev, xbc], axis=0)
    acc = jnp.zeros_like(xbc) + conv_b
    for k in range(SSM_CONV):
        off = halo - (SSM_CONV - 1) + k
        acc = acc + ext[off:off + tt] * conv_w[k:k + 1, :]
    return (xbc[tt - halo:, :],), (_silu(acc),)


def _head_expand_matrix():
    r = lax.broadcasted_iota(jnp.int32, (LANES, SSM_D_INNER), 0)
    c = lax.broadcasted_iota(jnp.int32, (LANES, SSM_D_INNER), 1)
    return ((c >= r * SSM_HEAD_DIM) & (c < (r + 1) * SSM_HEAD_DIM)).astype(BF16)


def _head_expand(v, e):
    hi = v.astype(BF16)
    lo = (v - hi.astype(F32)).astype(BF16)
    return jnp.dot(hi, e, preferred_element_type=F32) + jnp.dot(lo, e, preferred_element_type=F32)


def _f_ssd(carry, tiles, ctiles, params, cparams, idx):
    (state,), (xa, dtr, z), (dt_bias, a_log, d_skip, norm_w) = carry, tiles, params
    L, P, N, E = SSM_CHUNK, SSM_HEAD_DIM, SSM_STATE, SSM_HEADS // SSM_GROUPS
    gw = E * P
    em = _head_expand_matrix()
    dt = _softplus(dtr + dt_bias)
    da = dt * (-jnp.exp(a_log))
    r = lax.broadcasted_iota(jnp.int32, (L, L), 0)
    c = lax.broadcasted_iota(jnp.int32, (L, L), 1)
    tri = c <= r
    acs = jnp.dot(tri.astype(F32), da, precision=HI, preferred_element_type=F32)
    acs_t = acs.T
    last = acs[L - 1:L, :]
    xs = xa[:, :SSM_D_INNER]
    xdt = xs * _head_expand(dt, em)
    xdt_b = xdt.astype(BF16)
    xdec = (xdt * _head_expand(jnp.exp(last - acs), em)).astype(BF16)
    eacs = _head_expand(jnp.exp(acs), em)
    y_parts, st_parts = [], []
    for g in range(SSM_GROUPS):
        bg = xa[:, SSM_D_INNER + g * N:SSM_D_INNER + (g + 1) * N].astype(BF16)
        cg = xa[:, SSM_D_INNER + (SSM_GROUPS + g) * N:SSM_D_INNER + (SSM_GROUPS + g + 1) * N].astype(BF16)
        cb = lax.dot_general(cg, bg, (((1,), (1,)), ((), ())), preferred_element_type=F32)
        prev_g = state[:, g * gw:(g + 1) * gw]
        y_off = jnp.dot(cg, prev_g.astype(BF16), preferred_element_type=F32) * eacs[:, g * gw:(g + 1) * gw]
        st_parts.append(lax.dot_general(bg, xdec[:, g * gw:(g + 1) * gw], (((0,), (0,)), ((), ())),
                                        preferred_element_type=F32))
        diag = []
        for e in range(E):
            h = g * E + e
            lmat = jnp.exp(jnp.where(tri, acs[:, h:h + 1] - acs_t[h:h + 1, :], NEG_BIG))
            diag.append(jnp.dot((cb * lmat).astype(BF16), xdt_b[:, h * P:(h + 1) * P], preferred_element_type=F32))
        y_parts.append(jnp.concatenate(diag, axis=1) + y_off)
    y = jnp.concatenate(y_parts, axis=1) + xs * _head_expand(d_skip, em)
    new_state = state * _head_expand(jnp.exp(last), em) + jnp.concatenate(st_parts, axis=1)
    y = y * _silu(z)
    y = y * lax.rsqrt(jnp.mean(y * y, axis=-1, keepdims=True) + RMS_EPS) * norm_w
    return (new_state,), (y.astype(BF16),)


def _f_fox_gate(carry, tiles, ctiles, params, cparams, idx):
    (run,), (fl,), (fb,) = carry, tiles, params
    tt = fl.shape[0]
    logf = -_softplus(-(fl + fb))
    r = lax.broadcasted_iota(jnp.int32, (tt, tt), 0)
    c = lax.broadcasted_iota(jnp.int32, (tt, tt), 1)
    cum = jnp.dot((c <= r).astype(F32), logf, precision=HI, preferred_element_type=F32) + run[0:1, :]
    return (jnp.broadcast_to(cum[tt - 1:tt, :], run.shape),), (cum,)


def _rms(x, w):
    return x * lax.rsqrt(jnp.mean(x * x, axis=-1, keepdims=True) + RMS_EPS) * w


def _f_mla_prep(carry, tiles, ctiles, params, cparams, idx):
    (cq, ckv, kr), (ck, sk), (qw, kvw), (rk,) = tiles, ctiles, params, cparams
    rot = jnp.dot(kr, rk, precision=HI, preferred_element_type=F32)
    return (), (_rms(cq, qw).astype(BF16), _rms(ckv, kvw).astype(BF16), (kr * ck + rot * sk).astype(BF16))


def _f_rope_q(carry, tiles, ctiles, params, cparams, idx):
    (q,), (cos, sin) = tiles, ctiles
    nope = MLA_HEADS * MLA_NOPE
    x1 = q[:, nope:nope + LANES]
    x2 = q[:, nope + LANES:]
    roped = jnp.concatenate([q[:, :nope], x1 * cos - x2 * sin, x2 * cos + x1 * sin], axis=1)
    return (), ((roped * MLA_SCALE).astype(BF16),)


def _scores(q, k, fq, fk, masked, row0, col0):
    s = lax.dot_general(q, k, (((1,), (1,)), ((), ())), preferred_element_type=F32)
    if fq is not None:
        s = s + fq - fk
    if masked:
        rows = row0 + lax.broadcasted_iota(jnp.int32, s.shape, 0)
        cols = col0 + lax.broadcasted_iota(jnp.int32, s.shape, 1)
        s = jnp.where(cols <= rows, s, NEG_BIG)
    return s


def _q_major_tables(T, tq, tk):
    r = tk // tq
    qi = np.concatenate([np.full(i // r + 1, i, np.int32) for i in range(T // tq)])
    ki = np.concatenate([np.arange(i // r + 1, dtype=np.int32) for i in range(T // tq)])
    kind = np.where(ki == qi // r, qi % r + 1, 0).astype(np.int32)
    return [jnp.asarray(a) for a in (qi, ki, (ki == 0).astype(np.int32), kind)]


def _k_major_tables(T, tq, tk):
    r = tq // tk
    nq = T // tq
    ki = np.concatenate([np.full(nq - i // r, i, np.int32) for i in range(T // tk)])
    qi = np.concatenate([np.arange(i // r, nq, dtype=np.int32) for i in range(T // tk)])
    kind = np.where(qi == ki // r, r - ki % r, 0).astype(np.int32)
    return [jnp.asarray(a) for a in (ki, qi, (qi == nq - 1).astype(np.int32), kind)]


def _attn_tiles(T):
    return min(ATTN_TILE, T), min(ATTN_WIDE, T)


def _attn_fwd(q, k, v, fq, fk, name):
    H, T, dk = q.shape
    dv = v.shape[2]
    tq, tk = _attn_tiles(T)
    decay = fq is not None
    tables = _q_major_tables(T, tq, tk)

    def body(qi_ref, ki_ref, first_ref, kind_ref, *refs):
        if decay:
            q_ref, k_ref, v_ref, fq_ref, fk_ref, o_ref, lse_ref, m_s, l_s, acc = refs
        else:
            q_ref, k_ref, v_ref, o_ref, lse_ref, m_s, l_s, acc = refs
        t = pl.program_id(1)
        qi, ki = qi_ref[t], ki_ref[t]

        @pl.when(first_ref[t] == 1)
        def _():
            m_s[...] = jnp.full_like(m_s, NEG_BIG)
            l_s[...] = jnp.zeros_like(l_s)
            acc[...] = jnp.zeros_like(acc)

        def step(kind):
            w = tk if kind == 0 else kind * tq
            s = _scores(q_ref[0], k_ref[0, :w, :], fq_ref[0] if decay else None,
                        fk_ref[0, :, :w] if decay else None, kind > 0, qi * tq, ki * tk)
            m_new = jnp.maximum(m_s[...], jnp.max(s, axis=-1, keepdims=True))
            p = jnp.exp(s - m_new)
            corr = jnp.exp(m_s[...] - m_new)
            l_s[...] = corr * l_s[...] + jnp.sum(p, axis=-1, keepdims=True)
            acc[...] = corr * acc[...] + jnp.dot(p.astype(BF16), v_ref[0, :w, :], preferred_element_type=F32)
            m_s[...] = m_new
            if kind > 0:
                o_ref[0] = (acc[...] / l_s[...]).astype(o_ref.dtype)
                lse_ref[0] = m_s[...] + jnp.log(l_s[...])

        for kind in range(tk // tq + 1):
            pl.when(kind_ref[t] == kind)(functools.partial(step, kind))

    qspec = lambda d: pl.BlockSpec((1, tq, d), lambda h, t, qi, ki, fi, la: (h, qi[t], 0))
    kspec = lambda d: pl.BlockSpec((1, tk, d), lambda h, t, qi, ki, fi, la: (h, ki[t], 0))
    in_specs = [qspec(dk), kspec(dk), kspec(dv)]
    ins = [q, k, v]
    if decay:
        in_specs += [qspec(1), pl.BlockSpec((1, 1, tk), lambda h, t, qi, ki, fi, la: (h, 0, ki[t]))]
        ins += [fq, fk]
    grid_spec = pltpu.PrefetchScalarGridSpec(
        num_scalar_prefetch=4, grid=(H, int(tables[0].shape[0])), in_specs=in_specs,
        out_specs=[qspec(dv), qspec(1)],
        scratch_shapes=[pltpu.VMEM((tq, 1), F32), pltpu.VMEM((tq, 1), F32), pltpu.VMEM((tq, dv), F32)])
    return pl.pallas_call(
        body, name=name, grid_spec=grid_spec,
        out_shape=[jax.ShapeDtypeStruct((H, T, dv), BF16), jax.ShapeDtypeStruct((H, T, 1), F32)],
        compiler_params=_cparams(("parallel", "arbitrary")),
    )(*tables, *ins)


def _attn_bwd_dq(q, k, v, o, do, lse, fq, fk, name):
    H, T, dk = q.shape
    dv = v.shape[2]
    tq, tk = _attn_tiles(T)
    decay = fq is not None
    tables = _q_major_tables(T, tq, tk)

    def body(qi_ref, ki_ref, first_ref, kind_ref, *refs):
        if decay:
            q_ref, k_ref, v_ref, o_ref, do_ref, lse_ref, fq_ref, fk_ref, dq_ref, dl_ref, acc, dl, leak = refs
        else:
            q_ref, k_ref, v_ref, o_ref, do_ref, lse_ref, dq_ref, dl_ref, acc, dl, leak = refs
        t = pl.program_id(1)
        qi, ki = qi_ref[t], ki_ref[t]

        @pl.when(first_ref[t] == 1)
        def _():
            acc[...] = jnp.zeros_like(acc)
            leak[...] = jnp.zeros_like(leak)
            dl[...] = jnp.sum(do_ref[0].astype(F32) * o_ref[0].astype(F32), axis=-1, keepdims=True)

        def step(kind):
            w = tk if kind == 0 else kind * tq
            k_v = k_ref[0, :w, :]
            s = _scores(q_ref[0], k_v, fq_ref[0] if decay else None, fk_ref[0, :, :w] if decay else None,
                        kind > 0, qi * tq, ki * tk)
            p = jnp.exp(s - lse_ref[0])
            dp = lax.dot_general(do_ref[0], v_ref[0, :w, :], (((1,), (1,)), ((), ())), preferred_element_type=F32)
            ds = p * (dp - dl[...])
            leak[...] += jnp.sum(ds, axis=-1, keepdims=True)
            acc[...] += jnp.dot(ds.astype(BF16), k_v, preferred_element_type=F32)
            if kind > 0:
                dq_ref[0] = acc[...].astype(dq_ref.dtype)
                dl_ref[0] = dl[...] + leak[...]

        for kind in range(tk // tq + 1):
            pl.when(kind_ref[t] == kind)(functools.partial(step, kind))

    qspec = lambda d: pl.BlockSpec((1, tq, d), lambda h, t, qi, ki, fi, la: (h, qi[t], 0))
    kspec = lambda d: pl.BlockSpec((1, tk, d), lambda h, t, qi, ki, fi, la: (h, ki[t], 0))
    in_specs = [qspec(dk), kspec(dk), kspec(dv), qspec(dv), qspec(dv), qspec(1)]
    ins = [q, k, v, o, do, lse]
    if decay:
        in_specs += [qspec(1), pl.BlockSpec((1, 1, tk), lambda h, t, qi, ki, fi, la: (h, 0, ki[t]))]
        ins += [fq, fk]
    grid_spec = pltpu.PrefetchScalarGridSpec(
        num_scalar_prefetch=4, grid=(H, int(tables[0].shape[0])), in_specs=in_specs,
        out_specs=[qspec(dk), qspec(1)],
        scratch_shapes=[pltpu.VMEM((tq, dk), F32), pltpu.VMEM((tq, 1), F32), pltpu.VMEM((tq, 1), F32)])
    return pl.pallas_call(
        body, name=name, grid_spec=grid_spec,
        out_shape=[jax.ShapeDtypeStruct((H, T, dk), BF16), jax.ShapeDtypeStruct((H, T, 1), F32)],
        compiler_params=_cparams(("parallel", "arbitrary")),
    )(*tables, *ins)


def _attn_bwd_dkv(q, k, v, delta, do, lse, fq, fk, name):
    H, T, dk = q.shape
    dv = v.shape[2]
    tk, tq = _attn_tiles(T)
    decay = fq is not None
    tables = _k_major_tables(T, tq, tk)

    def body(ki_ref, qi_ref, last_ref, kind_ref, *refs):
        if decay:
            q_ref, k_ref, v_ref, dl_ref, do_ref, lse_ref, fq_ref, fk_ref, dk_ref, dv_ref, df_ref, dk_s, dv_s, df_s = refs
        else:
            q_ref, k_ref, v_ref, dl_ref, do_ref, lse_ref, dk_ref, dv_ref, dk_s, dv_s = refs
        t = pl.program_id(1)
        ki, qi = ki_ref[t], qi_ref[t]

        @pl.when(kind_ref[t] > 0)
        def _():
            dk_s[...] = jnp.zeros_like(dk_s)
            dv_s[...] = jnp.zeros_like(dv_s)
            if decay:
                df_s[...] = jnp.zeros_like(df_s)

        def step(kind):
            off = 0 if kind == 0 else tq - kind * tk
            q_v, do_v = q_ref[0, off:, :], do_ref[0, off:, :]
            s = _scores(q_v, k_ref[0], fq_ref[0, off:, :] if decay else None, fk_ref[0] if decay else None,
                        kind > 0, qi * tq + off, ki * tk)
            p = jnp.exp(s - lse_ref[0, off:, :])
            dv_s[...] += lax.dot_general(p.astype(BF16), do_v, (((0,), (0,)), ((), ())), preferred_element_type=F32)
            dp = lax.dot_general(do_v, v_ref[0], (((1,), (1,)), ((), ())), preferred_element_type=F32)
            ds = p * (dp - dl_ref[0, off:, :])
            dk_s[...] += lax.dot_general(ds.astype(BF16), q_v, (((0,), (0,)), ((), ())),
                                         preferred_element_type=F32)
            if decay:
                df_s[...] -= jnp.sum(ds, axis=0, keepdims=True)

        for kind in range(tq // tk + 1):
            pl.when(kind_ref[t] == kind)(functools.partial(step, kind))

        @pl.when(last_ref[t] == 1)
        def _():
            dk_ref[0] = dk_s[...].astype(dk_ref.dtype)
            dv_ref[0] = dv_s[...].astype(dv_ref.dtype)
            if decay:
                df_ref[0] = df_s[...]

    qspec = lambda d: pl.BlockSpec((1, tq, d), lambda h, t, ki, qi, fi, la: (h, qi[t], 0))
    kspec = lambda d: pl.BlockSpec((1, tk, d), lambda h, t, ki, qi, fi, la: (h, ki[t], 0))
    in_specs = [qspec(dk), kspec(dk), kspec(dv), qspec(1), qspec(dv), qspec(1)]
    ins = [q, k, v, delta, do, lse]
    out_specs = [kspec(dk), kspec(dv)]
    out_shape = [jax.ShapeDtypeStruct((H, T, dk), BF16), jax.ShapeDtypeStruct((H, T, dv), BF16)]
    scratch = [pltpu.VMEM((tk, dk), F32), pltpu.VMEM((tk, dv), F32)]
    if decay:
        fkspec = pl.BlockSpec((1, 1, tk), lambda h, t, ki, qi, fi, la: (h, 0, ki[t]))
        in_specs += [qspec(1), fkspec]
        ins += [fq, fk]
        out_specs.append(fkspec)
        out_shape.append(jax.ShapeDtypeStruct((H, 1, T), F32))
        scratch.append(pltpu.VMEM((1, tk), F32))
    grid_spec = pltpu.PrefetchScalarGridSpec(
        num_scalar_prefetch=4, grid=(H, int(tables[0].shape[0])), in_specs=in_specs, out_specs=out_specs,
        scratch_shapes=scratch)
    return pl.pallas_call(
        body, name=name, grid_spec=grid_spec, out_shape=out_shape, compiler_params=_cparams(("parallel", "arbitrary")),
    )(*tables, *ins)


def _attn_bwd_fused(q, k, v, o, do, lse, name):
    H, T, dk = q.shape
    dv = v.shape[2]
    tk, tq = _attn_tiles(T)
    tables = _k_major_tables(T, tq, tk)
    npairs = int(tables[0].shape[0])

    def body(ki_ref, qi_ref, last_ref, kind_ref, q_ref, k_ref, v_ref, o_ref, do_ref, lse_ref,
             dq_ref, dk_ref, dv_ref, dq_s, dk_s, dv_s):
        t = pl.program_id(1)
        ki, qi = ki_ref[t], qi_ref[t]

        @pl.when(t == 0)
        def _():
            dq_s[...] = jnp.zeros_like(dq_s)

        @pl.when(kind_ref[t] > 0)
        def _():
            dk_s[...] = jnp.zeros_like(dk_s)
            dv_s[...] = jnp.zeros_like(dv_s)

        def step(kind):
            off = 0 if kind == 0 else tq - kind * tk
            q_v, do_v, k_v = q_ref[0, off:, :], do_ref[0, off:, :], k_ref[0]
            s = _scores(q_v, k_v, None, None, kind > 0, qi * tq + off, ki * tk)
            p = jnp.exp(s - lse_ref[0, off:, :])
            delta = jnp.sum(do_v.astype(F32) * o_ref[0, off:, :].astype(F32), axis=-1, keepdims=True)
            dv_s[...] += lax.dot_general(p.astype(BF16), do_v, (((0,), (0,)), ((), ())), preferred_element_type=F32)
            dp = lax.dot_general(do_v, v_ref[0], (((1,), (1,)), ((), ())), preferred_element_type=F32)
            ds = (p * (dp - delta)).astype(BF16)
            dk_s[...] += lax.dot_general(ds, q_v, (((0,), (0,)), ((), ())), preferred_element_type=F32)
            rows = pl.ds(pl.multiple_of(qi * tq + off, tk), tq - off)
            dq_s[rows, :] += jnp.dot(ds, k_v, preferred_element_type=F32)

        for kind in range(tq // tk + 1):
            pl.when(kind_ref[t] == kind)(functools.partial(step, kind))

        @pl.when(last_ref[t] == 1)
        def _():
            dk_ref[0] = dk_s[...].astype(dk_ref.dtype)
            dv_ref[0] = dv_s[...].astype(dv_ref.dtype)

        @pl.when(t == npairs - 1)
        def _():
            dq_ref[0] = dq_s[...].astype(dq_ref.dtype)

    qspec = lambda d: pl.BlockSpec((1, tq, d), lambda h, t, ki, qi, la, kd: (h, qi[t], 0))
    kspec = lambda d: pl.BlockSpec((1, tk, d), lambda h, t, ki, qi, la, kd: (h, ki[t], 0))
    grid_spec = pltpu.PrefetchScalarGridSpec(
        num_scalar_prefetch=4, grid=(H, npairs),
        in_specs=[qspec(dk), kspec(dk), kspec(dv), qspec(dv), qspec(dv), qspec(1)],
        out_specs=[pl.BlockSpec((1, T, dk), lambda h, t, ki, qi, la, kd: (h, 0, 0)), kspec(dk), kspec(dv)],
        scratch_shapes=[pltpu.VMEM((T, dk), F32), pltpu.VMEM((tk, dk), F32), pltpu.VMEM((tk, dv), F32)])
    return pl.pallas_call(
        body, name=name, grid_spec=grid_spec,
        out_shape=[jax.ShapeDtypeStruct((H, T, dk), BF16), jax.ShapeDtypeStruct((H, T, dk), BF16),
                   jax.ShapeDtypeStruct((H, T, dv), BF16)],
        compiler_params=_cparams(("parallel", "arbitrary")),
    )(*tables, q, k, v, o, do, lse)


def _loss_head(y, target, name):
    rows, d = y.shape
    tt = _pick(rows, (512, 256, 128))

    def body(y_ref, t_ref, dy_ref, part_ref):
        @pl.when(pl.program_id(0) == 0)
        def _():
            part_ref[...] = jnp.zeros_like(part_ref)

        err = y_ref[...] - t_ref[...]
        dy_ref[...] = err * (1.0 / d)
        sq = jnp.sum(err * err, axis=0, keepdims=True)
        folded = sq[:, :LANES]
        for j in range(1, d // LANES):
            folded = folded + sq[:, j * LANES:(j + 1) * LANES]
        part_ref[...] += folded

    spec = pl.BlockSpec((tt, d), lambda i: (i, 0))
    return pl.pallas_call(
        body, name=name, grid=(rows // tt,), in_specs=[spec, spec],
        out_specs=[spec, pl.BlockSpec((1, LANES), lambda i: (0, 0))],
        out_shape=[jax.ShapeDtypeStruct((rows, d), F32), jax.ShapeDtypeStruct((1, LANES), F32)],
        compiler_params=_cparams(("arbitrary",)),
    )(y, target)


def _adamw(w, g, m, v, name):
    shape = w.shape
    if len(shape) != 3:
        view = (1, shape[0], shape[1]) if len(shape) == 2 else (shape[0], math.prod(shape[1:-1]), shape[-1])
        d, nm, nv = _adamw(*(a.reshape(view) for a in (w, g, m, v)), name)
        return d.reshape(shape), nm.reshape(shape), nv.reshape(shape)
    layers, rows, cols = shape
    tr = _pick(rows, (512, 256, 128, 64, 32, 16, 8))
    c1 = 1.0 / (1.0 - ADAM_B1 ** ADAM_STEP)
    c2 = 1.0 / (1.0 - ADAM_B2 ** ADAM_STEP)

    def body(w_ref, g_ref, m_ref, v_ref, d_ref, nm_ref, nv_ref):
        gv = g_ref[...]
        nm = ADAM_B1 * m_ref[...] + (1.0 - ADAM_B1) * gv
        nv = ADAM_B2 * v_ref[...] + (1.0 - ADAM_B2) * gv * gv
        d_ref[...] = -ADAM_LR * ((nm * c1) / (jnp.sqrt(nv * c2) + ADAM_EPS) + ADAM_WD * w_ref[...])
        nm_ref[...] = nm
        nv_ref[...] = nv

    spec = pl.BlockSpec((1, tr, cols), lambda l, i: (l, i, 0))
    sds = jax.ShapeDtypeStruct(shape, F32)
    return pl.pallas_call(
        body, name=name, grid=(layers, rows // tr), in_specs=[spec] * 4, out_specs=[spec] * 3, out_shape=[sds] * 3,
        compiler_params=_cparams(("parallel", "parallel")),
    )(w, g, m, v)


_ANY = pl.BlockSpec(memory_space=pl.ANY)


def _my_xyc():
    return lax.axis_index("x"), lax.axis_index("y"), lax.axis_index("c")


def _chip_allgather_core_half(shards, name):
    n = len(shards)
    half = [s.shape[0] // 2 for s in shards]

    def body(*refs):
        full_refs, out_refs = refs[:n], refs[n:2 * n]
        send_sems, recv_sems, loc_sems = refs[2 * n:]
        x, y, c = _my_xyc()
        myq = 2 * x + y
        x_refs = [full_refs[i].at[pl.ds(c * half[i], half[i])] for i in range(n)]
        chips = [(1 - x, y), (x, 1 - y), (1 - x, 1 - y)]
        started = []
        for i in range(n):
            local = pltpu.make_async_copy(x_refs[i], out_refs[i].at[myq], loc_sems.at[i])
            local.start()
            started.append(local)
        sends = []
        for j, (px, py) in enumerate(chips):
            for i in range(n):
                cp = pltpu.make_async_remote_copy(src_ref=x_refs[i], dst_ref=out_refs[i].at[myq],
                                                  send_sem=send_sems.at[3 * i + j], recv_sem=recv_sems.at[3 * i + j],
                                                  device_id=(px, py, c), device_id_type=MESH_ID)
                cp.start()
                sends.append(cp)
        for j, (px, py) in enumerate(chips):
            for i in range(n):
                pltpu.make_async_remote_copy(src_ref=x_refs[i], dst_ref=out_refs[i].at[2 * px + py],
                                             send_sem=send_sems.at[3 * i + j], recv_sem=recv_sems.at[3 * i + j],
                                             device_id=(px, py, c), device_id_type=MESH_ID).wait_recv()
        for cp in sends:
            cp.wait_send()
        for local in started:
            local.wait()

    return pl.pallas_call(
        body, name=name, in_specs=[_ANY] * n, out_specs=[_ANY] * n,
        out_shape=[jax.ShapeDtypeStruct((4, h) + s.shape[1:], s.dtype) for s, h in zip(shards, half)],
        scratch_shapes=[pltpu.SemaphoreType.DMA((3 * n,)), pltpu.SemaphoreType.DMA((3 * n,)),
                        pltpu.SemaphoreType.DMA((n,))],
    )(*shards)


def _sibling_swap_halves(g, name):
    q, rows, cols = g.shape
    rh = rows // 2

    def body(g_ref, out_ref, send_sem, recv_sem):
        x, y, c = _my_xyc()
        src = g_ref.at[:, pl.ds((1 - c) * rh, rh), :]
        cp = pltpu.make_async_remote_copy(src_ref=src, dst_ref=out_ref, send_sem=send_sem, recv_sem=recv_sem,
                                          device_id=(x, y, 1 - c), device_id_type=MESH_ID)
        cp.start()
        cp.wait()

    return pl.pallas_call(
        body, name=name, in_specs=[_ANY], out_specs=_ANY,
        out_shape=jax.ShapeDtypeStruct((q, rh, cols), g.dtype),
        scratch_shapes=[pltpu.SemaphoreType.DMA, pltpu.SemaphoreType.DMA],
    )(g)


def _add_own_half(g, other, name):
    q, rows, cols = g.shape
    rh = rows // 2
    tr = _pick(rh, (512, 256, 128, 64, 32, 16, 8))
    nb = rh // tr
    cidx = lax.axis_index("c").astype(jnp.int32).reshape(1)

    def body(c_ref, g_ref, o_ref, out_ref):
        out_ref[...] = (g_ref[...] + o_ref[...]).astype(out_ref.dtype)

    grid_spec = pltpu.PrefetchScalarGridSpec(
        num_scalar_prefetch=1, grid=(q, nb),
        in_specs=[pl.BlockSpec((1, tr, cols), lambda a, i, c_ref: (a, c_ref[0] * nb + i, 0)),
                  pl.BlockSpec((1, tr, cols), lambda a, i, c_ref: (a, i, 0))],
        out_specs=pl.BlockSpec((1, tr, cols), lambda a, i, c_ref: (a, i, 0)))
    return pl.pallas_call(
        body, name=name, grid_spec=grid_spec, out_shape=jax.ShapeDtypeStruct((q, rh, cols), BF16),
        compiler_params=_cparams(("parallel", "parallel")),
    )(cidx, g, other)


def _chip_exchange(s, name):
    def body(s_ref, out_ref, send_sems, recv_sems, loc_sem):
        x, y, c = _my_xyc()
        myq = 2 * x + y
        local = pltpu.make_async_copy(s_ref.at[myq], out_ref.at[myq], loc_sem)
        local.start()
        chips = [(1 - x, y), (x, 1 - y), (1 - x, 1 - y)]
        sends = []
        for j, (px, py) in enumerate(chips):
            cp = pltpu.make_async_remote_copy(src_ref=s_ref.at[2 * px + py], dst_ref=out_ref.at[myq],
                                              send_sem=send_sems.at[j], recv_sem=recv_sems.at[j],
                                              device_id=(px, py, c), device_id_type=MESH_ID)
            cp.start()
            sends.append(cp)
        for j, (px, py) in enumerate(chips):
            pltpu.make_async_remote_copy(src_ref=s_ref.at[myq], dst_ref=out_ref.at[2 * px + py],
                                         send_sem=send_sems.at[j], recv_sem=recv_sems.at[j],
                                         device_id=(px, py, c), device_id_type=MESH_ID).wait_recv()
        for cp in sends:
            cp.wait_send()
        local.wait()

    return pl.pallas_call(
        body, name=name, in_specs=[_ANY], out_specs=_ANY, out_shape=jax.ShapeDtypeStruct(s.shape, s.dtype),
        scratch_shapes=[pltpu.SemaphoreType.DMA((3,)), pltpu.SemaphoreType.DMA((3,)), pltpu.SemaphoreType.DMA],
    )(s)


def _sum_leading(a, name):
    q, rows, cols = a.shape
    tr = _pick(rows, (512, 256, 128, 64, 32, 16, 8))

    def body(a_ref, out_ref):
        acc = a_ref[0].astype(F32)
        for j in range(1, q):
            acc = acc + a_ref[j].astype(F32)
        out_ref[...] = acc

    return pl.pallas_call(
        body, name=name, grid=(rows // tr,), in_specs=[pl.BlockSpec((q, tr, cols), lambda i: (0, i, 0))],
        out_specs=pl.BlockSpec((tr, cols), lambda i: (i, 0)), out_shape=jax.ShapeDtypeStruct((rows, cols), F32),
        compiler_params=_cparams(("parallel",)),
    )(a)


def _sibling_swap(arrays, name):
    n = len(arrays)

    def body(*refs):
        a_refs, out_refs = refs[:n], refs[n:2 * n]
        send_sems, recv_sems = refs[2 * n:]
        x, y, c = _my_xyc()
        copies = []
        for i in range(n):
            cp = pltpu.make_async_remote_copy(src_ref=a_refs[i], dst_ref=out_refs[i], send_sem=send_sems.at[i],
                                              recv_sem=recv_sems.at[i], device_id=(x, y, 1 - c),
                                              device_id_type=MESH_ID)
            cp.start()
            copies.append(cp)
        for cp in copies:
            cp.wait()

    return pl.pallas_call(
        body, name=name, in_specs=[_ANY] * n, out_specs=[_ANY] * n,
        out_shape=[jax.ShapeDtypeStruct(a.shape, a.dtype) for a in arrays],
        scratch_shapes=[pltpu.SemaphoreType.DMA((n,)), pltpu.SemaphoreType.DMA((n,))],
    )(*arrays)


def _by_core(mine, other):
    c = lax.axis_index("c")
    return jnp.where(c == 0, jnp.stack([mine, other]), jnp.stack([other, mine]))


def _all_reduce_small(vec, name):
    rows, cols = vec.shape

    def body(v_ref, out_ref, buf, send_sems, recv_sems):
        x, y, c = _my_xyc()
        me = 4 * x + 2 * y + c
        buf[me] = v_ref[...]
        sends = []
        for kk in range(1, 8):
            peer = (1 - x if kk & 4 else x, 1 - y if kk & 2 else y, 1 - c if kk & 1 else c)
            cp = pltpu.make_async_remote_copy(src_ref=v_ref, dst_ref=buf.at[me], send_sem=send_sems.at[kk - 1],
                                              recv_sem=recv_sems.at[kk - 1], device_id=peer, device_id_type=MESH_ID)
            cp.start()
            sends.append(cp)
        for kk in range(1, 8):
            px, py, pc = (1 - x if kk & 4 else x, 1 - y if kk & 2 else y, 1 - c if kk & 1 else c)
            pltpu.make_async_remote_copy(src_ref=v_ref, dst_ref=buf.at[4 * px + 2 * py + pc],
                                         send_sem=send_sems.at[kk - 1], recv_sem=recv_sems.at[kk - 1],
                                         device_id=(px, py, pc), device_id_type=MESH_ID).wait_recv()
        for cp in sends:
            cp.wait_send()
        acc = buf[0]
        for j in range(1, 8):
            acc = acc + buf[j]
        out_ref[...] = acc

    vm = pl.BlockSpec(memory_space=pltpu.VMEM)
    return pl.pallas_call(
        body, name=name, in_specs=[vm], out_specs=vm, out_shape=jax.ShapeDtypeStruct((rows, cols), F32),
        scratch_shapes=[pltpu.VMEM((8, rows, cols), F32), pltpu.SemaphoreType.DMA((7,)), pltpu.SemaphoreType.DMA((7,))],
        compiler_params=pltpu.CompilerParams(vmem_limit_bytes=VMEM_LIMIT_BYTES),
    )(vec)


def _reduce_scatter(g):
    other = _sibling_swap_halves(g, "rs_swap_halves")
    s = _add_own_half(g, other, "rs_add_halves")
    recv = _chip_exchange(s, "rs_chip_exchange")
    r = _sum_leading(recv, "rs_sum_chips")
    (other_r,) = _sibling_swap([r], "rs_join_halves")
    return _by_core(r, other_r).reshape(g.shape[1], g.shape[2])


PACK_COLS = 1024
PACK_ROW_MULTIPLE = 1024
SHARDED = (("even_w_in", 2), ("conv_w", 2), ("even_w_out", 1), ("odd_w_in", 2), ("q_norm_w", 1), ("w_uq", 2),
           ("kv_norm_w", 1), ("w_ukv", 2), ("odd_w_out", 1), ("ffn_w_gate", 2), ("ffn_w_up", 2), ("ffn_w_down", 1))
KEEP_F32 = ("conv_w", "q_norm_w", "kv_norm_w")
REPLICATED = ("pool_w", "pool_scale", "conv_b", "dt_bias", "a_log", "d_skip", "ssm_norm_w", "fgate_b",
              "ln_mix_g", "ln_mix_b", "ln_ffn_g", "ln_ffn_b")


def _gather_weights(shards):
    sent = [shards[name] if name in KEEP_F32 else shards[name].astype(BF16) for name, _ in SHARDED]
    mine = _chip_allgather_core_half(sent, "weights_allgather")
    others = _sibling_swap(mine, "weights_sibling_swap")
    out = {}
    for (name, axis), m, o in zip(SHARDED, mine, others):
        halves = _by_core(m, o)
        out[name] = jnp.concatenate([jnp.concatenate([halves[h, q] for q in range(4)], axis=axis) for h in range(2)],
                                    axis=0)
    return out


def _pack_grads(grads, shards):
    pieces = []
    for name, axis in SHARDED:
        width = shards[name].shape[axis]
        for g in grads[name]:
            ax = axis - 1
            split = g.reshape(g.shape[:ax] + (4, width) + g.shape[ax + 1:])
            pieces.append(jnp.moveaxis(split, ax, 0).reshape(4, -1))
    n = sum(p.shape[1] for p in pieces)
    per = PACK_COLS * PACK_ROW_MULTIPLE
    total = -(-n // per) * per
    pieces.append(jnp.zeros((4, total - n), F32))
    return jnp.concatenate(pieces, axis=1).reshape(4, total // PACK_COLS, PACK_COLS)


def _unpack_reduced(reduced, shards):
    flat = reduced.reshape(-1)
    out, off = {}, 0
    for name, _ in SHARDED:
        shp = shards[name].shape
        size = math.prod(shp)
        out[name] = flat[off:off + size].reshape(shp)
        off += size
    return out


def _to_heads(t, nh):
    rows, width = t.shape
    return t.reshape(rows, nh, width // nh).transpose(1, 0, 2)


def _from_heads(t):
    nh, rows, d = t.shape
    return t.transpose(1, 0, 2).reshape(rows, nh * d)


def _rope_tables(rows):
    half = MLA_ROPE // 2
    freqs = jnp.power(ROPE_THETA, -jnp.arange(half, dtype=F32) / half)
    ang = jnp.arange(rows, dtype=F32)[:, None] * freqs[None, :]
    cos, sin = jnp.cos(ang), jnp.sin(ang)
    cos_q, sin_q = jnp.tile(cos, (1, MLA_HEADS)), jnp.tile(sin, (1, MLA_HEADS))
    zeros = jnp.zeros((rows, LANES - MLA_ROPE), F32)
    cos_k = jnp.concatenate([cos, cos, zeros], axis=1)
    sin_k = jnp.concatenate([sin, sin, zeros], axis=1)
    r = lax.broadcasted_iota(jnp.int32, (LANES, LANES), 0)
    c = lax.broadcasted_iota(jnp.int32, (LANES, LANES), 1)
    rot = (jnp.where((r == c + half) & (c < half), -1.0, 0.0)
           + jnp.where((c == r + half) & (r < half), 1.0, 0.0)).astype(F32)
    return cos_q, sin_q, cos_k, sin_k, rot


def _pad_cols(a, width):
    return jnp.pad(a, ((0, 0), (0, width - a.shape[1])))


_EVEN_CUT = (512, 1536, 3072, 3088)


def _prep_even_w_in(w):
    c0, c1, c2, c3 = _EVEN_CUT
    return jnp.concatenate([w[:, c1:c2], w[:, :c0], w[:, c0:c1], _pad_cols(w[:, c2:c3], LANES)], axis=1)


def _unprep_even_w_in(g):
    c0, c1, c2, c3 = _EVEN_CUT
    n_xbc, n_u = c2 - c1, c0
    return jnp.concatenate([g[:, n_xbc:n_xbc + n_u], g[:, n_xbc + n_u:c2], g[:, :n_xbc], g[:, c2:c3]], axis=1)


_ODD_CUT = (1536, 1544, 2312, 2344)


def _prep_odd_w_in(w):
    c0, c1, c2, c3 = _ODD_CUT
    return jnp.concatenate([w[:, :c0], w[:, c1:c2], _pad_cols(w[:, c2:c3], LANES), _pad_cols(w[:, c0:c1], LANES)],
                           axis=1)


def _unprep_odd_w_in(g):
    c0, c1, c2, c3 = _ODD_CUT
    n1 = c0 + (c2 - c1)
    return jnp.concatenate([g[:, :c0], g[:, n1 + LANES:n1 + LANES + (c1 - c0)], g[:, c0:n1],
                            g[:, n1:n1 + (c3 - c2)]], axis=1)


def _prep_w_uq(w):
    r = w.reshape(w.shape[0], MLA_HEADS, MLA_NOPE + MLA_ROPE)
    half = MLA_ROPE // 2
    return jnp.concatenate([r[:, :, :MLA_NOPE].reshape(w.shape[0], -1),
                            r[:, :, MLA_NOPE:MLA_NOPE + half].reshape(w.shape[0], -1),
                            r[:, :, MLA_NOPE + half:].reshape(w.shape[0], -1)], axis=1)


def _unprep_w_uq(g):
    rows = g.shape[0]
    half = MLA_ROPE // 2
    nope = MLA_HEADS * MLA_NOPE
    return jnp.concatenate([g[:, :nope].reshape(rows, MLA_HEADS, MLA_NOPE),
                            g[:, nope:nope + LANES].reshape(rows, MLA_HEADS, half),
                            g[:, nope + LANES:].reshape(rows, MLA_HEADS, half)], axis=2).reshape(rows, -1)


def _row(v, width=None):
    v = v.reshape(1, -1)
    return v if width is None else _pad_cols(v, width)


def _even_forward(x_bf, w, i):
    rows = x_bf.shape[0]
    proj = _mm(x_bf, w["even_w_in"][i], "nn", F32, "even_proj")
    xbc, u, z, dtr = _Cols(proj, 0, 1536), _Cols(proj, 3, 512), _Cols(proj, 2, 1024), _Cols(proj, 24, LANES)
    n_row = rows // ROW_TILE
    pool_p = (w["pool_w"][i], _row(w["pool_scale"][i]))
    (y_pool,), pool_c = _scan_fwd(_f_pool, "pool_fwd", n_row, [u], [], pool_p, [], [(16, POOL_WIDTH)],
                                  [(POOL_WIDTH, BF16)])
    conv_p = (w["conv_w"][i], _row(w["conv_b"][i]))
    (xa,), conv_c = _scan_fwd(_f_conv, "conv_fwd", n_row, [xbc], [], conv_p, [], [(8, SSM_CONV_DIM)],
                              [(SSM_CONV_DIM, F32)])
    ssd_p = (_row(w["dt_bias"][i], LANES), _row(w["a_log"][i], LANES), _row(w["d_skip"][i], LANES),
             _row(w["ssm_norm_w"][i]))
    (y_ssm,), ssd_c = _scan_fwd(_f_ssd, "ssd_fwd", rows // SSM_CHUNK, [xa, dtr, z], [], ssd_p, [],
                                [(SSM_STATE, SSM_D_INNER)], [(SSM_D_INNER, BF16)])
    mix = jnp.concatenate([y_pool, y_ssm], axis=1)
    saved = dict(u=u, z=z, xbc=xbc, dtr=dtr, xa=xa, mix=mix, pool_p=pool_p, pool_c=pool_c, conv_p=conv_p,
                 conv_c=conv_c, ssd_p=ssd_p, ssd_c=ssd_c)
    return mix, saved


def _even_backward(dmix, x_bf, sv, w, i, d_r1):
    rows = x_bf.shape[0]
    n_row = rows // ROW_TILE
    dy_pool, dy_ssm = dmix[:, :POOL_WIDTH], dmix[:, POOL_WIDTH:]
    (du,), (g_pool_w, g_pool_scale) = _scan_bwd(_f_pool, "pool_bwd", n_row, [sv["u"]], [], sv["pool_p"], [],
                                                sv["pool_c"], [dy_pool])
    (dxa, ddtr, dz), (g_dt_bias, g_a_log, g_d_skip, g_norm_w) = _scan_bwd(
        _f_ssd, "ssd_bwd", rows // SSM_CHUNK, [sv["xa"], sv["dtr"], sv["z"]], [], sv["ssd_p"], [], sv["ssd_c"],
        [dy_ssm])
    (dxbc,), (g_conv_w, g_conv_b) = _scan_bwd(_f_conv, "conv_bwd", n_row, [sv["xbc"]], [], sv["conv_p"], [],
                                              sv["conv_c"], [dxa])
    dproj = jnp.concatenate([dxbc, du, dz, ddtr], axis=1).astype(BF16)
    g_w_in = _unprep_even_w_in(_mm(x_bf, dproj, "tn", F32, "even_dw_in"))
    dx = _mm(dproj, w["even_w_in"][i], "nt", F32, "even_dx", extra=d_r1, alpha=ALPHA)
    grads = dict(even_w_in=g_w_in, pool_w=g_pool_w, pool_scale=g_pool_scale[0], conv_w=g_conv_w, conv_b=g_conv_b[0],
                 dt_bias=g_dt_bias[0, :SSM_HEADS], a_log=g_a_log[0, :SSM_HEADS], d_skip=g_d_skip[0, :SSM_HEADS],
                 ssm_norm_w=g_norm_w[0])
    return dx, grads


def _odd_forward(x_bf, w, i, tables):
    rows = x_bf.shape[0]
    cos_q, sin_q, cos_k, sin_k, rot = tables
    proj = _mm(x_bf, w["odd_w_in"][i], "nn", F32, "odd_proj")
    qf, kf, vf = (_to_heads((proj[:, j * 512:(j + 1) * 512] * sc).astype(BF16), FOX_HEADS)
                  for j, sc in enumerate((FOX_SCALE, 1.0, 1.0)))
    cq, ckv = _Cols(proj, 3, MLA_Q_RANK), _Cols(proj, 8, MLA_KV_RANK)
    kr, fl = _Cols(proj, 18, LANES), _Cols(proj, 19, LANES)
    n_row = rows // ROW_TILE
    fox_p = (_row(w["fgate_b"][i], LANES),)
    n_gate = rows // min(ATTN_TILE, rows)
    (fcum,), fox_c = _scan_fwd(_f_fox_gate, "fox_gate_fwd", n_gate, [fl], [], fox_p, [], [(8, LANES)], [(LANES, F32)])
    fc_heads = fcum[:, :FOX_HEADS].T
    fq, fk = fc_heads[:, :, None], fc_heads[:, None, :]
    o_fox, lse_fox = _attn_fwd(qf, kf, vf, fq, fk, "fox_attn_fwd")

    prep_p = (_row(w["q_norm_w"][i]), _row(w["kv_norm_w"][i]))
    (cqn, ckvn, krr), _ = _scan_fwd(_f_mla_prep, "mla_prep_fwd", n_row, [cq, ckv, kr], [cos_k, sin_k], prep_p,
                                    [rot], [], [(MLA_Q_RANK, BF16), (MLA_KV_RANK, BF16), (LANES, BF16)])
    q_flat = _mm(cqn, w["w_uq"][i], "nn", F32, "mla_q_up")
    (q_rope,), _ = _scan_fwd(_f_rope_q, "rope_q_fwd", n_row, [q_flat], [cos_q, sin_q], [], [], [],
                             [(q_flat.shape[1], BF16)])
    kv = _mm(ckvn, w["w_ukv"][i], "nn", BF16, "mla_kv_up")
    nope = MLA_HEADS * MLA_NOPE
    half = MLA_ROPE // 2
    q_m = jnp.concatenate([_to_heads(q_rope[:, :nope], MLA_HEADS), _to_heads(q_rope[:, nope:nope + LANES], MLA_HEADS),
                           _to_heads(q_rope[:, nope + LANES:], MLA_HEADS)], axis=2)
    kv_h = _to_heads(kv, MLA_HEADS)
    k_rope = jnp.broadcast_to(krr[None, :, :MLA_ROPE], (MLA_HEADS, rows, MLA_ROPE))
    k_m = jnp.concatenate([kv_h[:, :, :MLA_NOPE], k_rope], axis=2)
    v_m = kv_h[:, :, MLA_NOPE:]
    o_mla, lse_mla = _attn_fwd(q_m, k_m, v_m, None, None, "mla_attn_fwd")
    mix = jnp.concatenate([_from_heads(o_fox), _from_heads(o_mla)], axis=1)
    saved = dict(qf=qf, kf=kf, vf=vf, fq=fq, fk=fk, o_fox=o_fox, lse_fox=lse_fox, fl=fl, fox_p=fox_p, fox_c=fox_c,
                 cq=cq, ckv=ckv, kr=kr, prep_p=prep_p, cqn=cqn, ckvn=ckvn, q_flat=q_flat, q_m=q_m, k_m=k_m, v_m=v_m,
                 o_mla=o_mla, lse_mla=lse_mla, mix=mix)
    return mix, saved


def _odd_backward(dmix, x_bf, sv, w, i, d_r1, tables):
    rows = x_bf.shape[0]
    cos_q, sin_q, cos_k, sin_k, rot = tables
    n_row = rows // ROW_TILE
    nope = MLA_HEADS * MLA_NOPE
    half = MLA_ROPE // 2
    do_fox = _to_heads(dmix[:, :FOX_WIDTH], FOX_HEADS)
    do_mla = _to_heads(dmix[:, FOX_WIDTH:], MLA_HEADS)
    fox_args = (sv["qf"], sv["kf"], sv["vf"], sv["o_fox"], do_fox, sv["lse_fox"], sv["fq"], sv["fk"])
    dqf, delta_fox = _attn_bwd_dq(*fox_args, "fox_attn_dq")
    dkf, dvf, dfk = _attn_bwd_dkv(*fox_args[:3], delta_fox, *fox_args[4:], "fox_attn_dkv")
    dfcum = _pad_cols(dfk[:, 0, :].T, LANES)
    n_gate = rows // min(ATTN_TILE, rows)
    (dfl,), (g_fb,) = _scan_bwd(_f_fox_gate, "fox_gate_bwd", n_gate, [sv["fl"]], [], sv["fox_p"], [], sv["fox_c"],
                                [dfcum])
    mla_args = (sv["q_m"], sv["k_m"], sv["v_m"], sv["o_mla"], do_mla, sv["lse_mla"], None, None)
    dq_m, dk_m, dv_m = _attn_bwd_fused(*mla_args[:6], "mla_attn_bwd")
    dq_rope = jnp.concatenate([_from_heads(dq_m[:, :, :MLA_NOPE]), _from_heads(dq_m[:, :, MLA_NOPE:MLA_NOPE + half]),
                               _from_heads(dq_m[:, :, MLA_NOPE + half:])], axis=1)
    (dq_flat,), _ = _scan_bwd(_f_rope_q, "rope_q_bwd", n_row, [sv["q_flat"]], [cos_q, sin_q], [], [], [], [dq_rope])
    g_w_uq = _mm(sv["cqn"], dq_flat, "tn", F32, "mla_dw_uq")
    dcqn = _mm(dq_flat, w["w_uq"][i], "nt", BF16, "mla_dcqn")
    dkv = _from_heads(jnp.concatenate([dk_m[:, :, :MLA_NOPE], dv_m], axis=2))
    g_w_ukv = _mm(sv["ckvn"], dkv, "tn", F32, "mla_dw_ukv")
    dckvn = _mm(dkv, w["w_ukv"][i], "nt", BF16, "mla_dckvn")
    dkrr = _head_sum(dk_m, "mla_dk_rope_sum")
    (dcq, dckv, dkr), (g_qw, g_kvw) = _scan_bwd(_f_mla_prep, "mla_prep_bwd", n_row, [sv["cq"], sv["ckv"], sv["kr"]],
                                                [cos_k, sin_k], sv["prep_p"], [rot], [], [dcqn, dckvn, dkrr])
    dproj = jnp.concatenate([_from_heads(dqf).astype(F32) * FOX_SCALE, _from_heads(dkf).astype(F32), _from_heads(dvf).astype(F32),
                             dcq, dckv, dkr, dfl], axis=1).astype(BF16)
    g_w_in = _mm(x_bf, dproj, "tn", F32, "odd_dw_in")
    dx = _mm(dproj, w["odd_w_in"][i], "nt", F32, "odd_dx", extra=d_r1, alpha=ALPHA)
    grads = dict(odd_w_in=_unprep_odd_w_in(g_w_in), fgate_b=g_fb[0, :FOX_HEADS], q_norm_w=g_qw[0], kv_norm_w=g_kvw[0],
                 w_uq=_unprep_w_uq(g_w_uq), w_ukv=g_w_ukv)
    return dx, grads


def _head_sum(dk_m, name):
    H, T, dk = dk_m.shape
    tt = _pick(T, (512, 256, 128))

    def body(d_ref, o_ref):
        acc = d_ref[0].astype(F32)
        for h in range(1, H):
            acc = acc + d_ref[h].astype(F32)
        o_ref[...] = jnp.concatenate([acc[:, MLA_NOPE:], jnp.zeros((tt, LANES - MLA_ROPE), F32)], axis=1).astype(BF16)

    return pl.pallas_call(
        body, name=name, grid=(T // tt,), in_specs=[pl.BlockSpec((H, tt, dk), lambda i: (0, i, 0))],
        out_specs=pl.BlockSpec((tt, LANES), lambda i: (i, 0)), out_shape=jax.ShapeDtypeStruct((T, LANES), BF16),
        compiler_params=_cparams(("parallel",)),
    )(dk_m)


def _local_step(x, target, w, small):
    rows = x.shape[0]
    n_row = rows // ROW_TILE
    tables = _rope_tables(rows)
    saved = []
    x_f32 = x
    x_bf = x.astype(BF16)
    for l in range(DEPTH):
        i = l // 2
        if l % 2 == 0:
            mix, sv = _even_forward(x_bf, w, i)
            w_out = w["even_w_out"][i]
        else:
            mix, sv = _odd_forward(x_bf, w, i, tables)
            w_out = w["odd_w_out"][i]
        r1 = _mm(mix, w_out, "nn", F32, "mix_out_even" if l % 2 == 0 else "mix_out_odd", extra=x_f32, alpha=ALPHA)
        ln1_p = (_row(small["ln_mix_g"][l]), _row(small["ln_mix_b"][l]))
        ln_outs = [(D_MODEL, F32), (D_MODEL, BF16)]
        (x_mid, x_mid_bf), _ = _scan_fwd(_f_ln_with_bf16, "ln_fwd", n_row, [r1], [], ln1_p, [], [], ln_outs)
        gu = _mm(x_mid_bf, w["ffn_w_gu"][l], "nn", BF16, "ffn_gu")
        (act,), _ = _scan_fwd(_f_act, "ffn_act_fwd", n_row, [gu], [], [], [], [], [(D_FF, BF16)])
        r2 = _mm(act, w["ffn_w_down"][l], "nn", F32, "ffn_down", extra=x_mid, alpha=ALPHA)
        ln2_p = (_row(small["ln_ffn_g"][l]), _row(small["ln_ffn_b"][l]))
        (x_out, x_out_bf), _ = _scan_fwd(_f_ln_with_bf16, "ln_fwd", n_row, [r2], [], ln2_p, [], [], ln_outs)
        saved.append(dict(sv=sv, x_bf=x_bf, r1=r1, ln1_p=ln1_p, x_mid_bf=x_mid_bf, gu=gu, act=act, r2=r2, ln2_p=ln2_p,
                          w_out=w_out))
        x_f32, x_bf = x_out, x_out_bf

    dy, loss_part = _loss_head(x_f32, target, "loss_head")
    loss = 0.5 * jnp.sum(loss_part) / D_MODEL

    layer_grads = []
    for l in reversed(range(DEPTH)):
        i = l // 2
        s = saved[l]
        (d_r2,), (g_ln2_g, g_ln2_b) = _scan_bwd(_f_ln, "ln_bwd", n_row, [s["r2"]], [], s["ln2_p"], [], [], [dy])
        g_down = _mm(s["act"], d_r2, "tn", F32, "ffn_dw_down")
        dact = _mm(d_r2, w["ffn_w_down"][l], "nt", BF16, "ffn_dact")
        (dgu,), _ = _scan_bwd(_f_act, "ffn_act_bwd", n_row, [s["gu"]], [], [], [], [], [dact])
        g_gu = _mm(s["x_mid_bf"], dgu, "tn", F32, "ffn_dw_gu")
        dx_mid = _mm(dgu, w["ffn_w_gu"][l], "nt", F32, "ffn_dx", extra=d_r2, alpha=ALPHA)
        (d_r1,), (g_ln1_g, g_ln1_b) = _scan_bwd(_f_ln, "ln_bwd", n_row, [s["r1"]], [], s["ln1_p"], [], [], [dx_mid])
        g_w_out = _mm(s["sv"]["mix"], d_r1, "tn", F32, "even_dw_out" if l % 2 == 0 else "odd_dw_out")
        dmix = _mm(d_r1, s["w_out"], "nt", BF16, "even_dmix" if l % 2 == 0 else "odd_dmix")
        if l % 2 == 0:
            dy, g = _even_backward(dmix, s["x_bf"], s["sv"], w, i, d_r1)
            g["even_w_out"] = g_w_out
        else:
            dy, g = _odd_backward(dmix, s["x_bf"], s["sv"], w, i, d_r1, tables)
            g["odd_w_out"] = g_w_out
        g.update(ffn_w_gate=g_gu[:, :D_FF], ffn_w_up=g_gu[:, D_FF:], ffn_w_down=g_down, ln_mix_g=g_ln1_g[0],
                 ln_mix_b=g_ln1_b[0], ln_ffn_g=g_ln2_g[0], ln_ffn_b=g_ln2_b[0])
        layer_grads.append((l, g))
    return loss, dy, layer_grads


EVEN_NAMES = ("even_w_in", "pool_w", "pool_scale", "conv_w", "conv_b", "dt_bias", "a_log", "d_skip", "ssm_norm_w",
              "even_w_out")
ODD_NAMES = ("odd_w_in", "fgate_b", "q_norm_w", "w_uq", "kv_norm_w", "w_ukv", "odd_w_out")
PER_LAYER_NAMES = ("ffn_w_gate", "ffn_w_up", "ffn_w_down", "ln_mix_g", "ln_mix_b", "ln_ffn_g", "ln_ffn_b")
WEIGHT_NAMES = EVEN_NAMES + ODD_NAMES + PER_LAYER_NAMES


def _grads_by_name(layer_grads):
    by_layer = dict(layer_grads)
    out = {}
    for n in EVEN_NAMES:
        out[n] = [by_layer[l][n] for l in range(0, DEPTH, 2)]
    for n in ODD_NAMES:
        out[n] = [by_layer[l][n] for l in range(1, DEPTH, 2)]
    for n in PER_LAYER_NAMES:
        out[n] = [by_layer[l][n] for l in range(DEPTH)]
    return out


def _prepare_weights(full):
    w = {}
    w["even_w_in"] = [_prep_even_w_in(full["even_w_in"][i]) for i in range(2)]
    w["even_w_out"] = [full["even_w_out"][i] for i in range(2)]
    w["odd_w_in"] = [_prep_odd_w_in(full["odd_w_in"][i]) for i in range(2)]
    w["w_uq"] = [_prep_w_uq(full["w_uq"][i]) for i in range(2)]
    w["w_ukv"] = [full["w_ukv"][i] for i in range(2)]
    w["odd_w_out"] = [full["odd_w_out"][i] for i in range(2)]
    w["ffn_w_gu"] = [jnp.concatenate([full["ffn_w_gate"][l], full["ffn_w_up"][l]], axis=1) for l in range(DEPTH)]
    w["ffn_w_down"] = [full["ffn_w_down"][l] for l in range(DEPTH)]
    for n in ("conv_w", "q_norm_w", "kv_norm_w"):
        w[n] = full[n]
    return w


def _flatten_small(grads):
    flat = jnp.concatenate([g.reshape(-1) for n in REPLICATED for g in grads[n]])
    n = flat.shape[0]
    per = LANES * 8
    total = -(-n // per) * per
    return jnp.pad(flat, (0, total - n)).reshape(total // LANES, LANES)


def _unflatten_small(mat, like):
    flat = mat.reshape(-1)
    out, off = {}, 0
    for n in REPLICATED:
        size = math.prod(like[n].shape)
        out[n] = flat[off:off + size].reshape(like[n].shape)
        off += size
    return out


def kernel(x, even_w_in, pool_w, pool_scale, conv_w, conv_b, dt_bias, a_log, d_skip, ssm_norm_w, even_w_out, odd_w_in, fgate_b, q_norm_w, w_uq, kv_norm_w, w_ukv, odd_w_out, ffn_w_gate, ffn_w_up, ffn_w_down, ln_mix_g, ln_mix_b, ln_ffn_g, ln_ffn_b, loss_target, m_even_w_in, m_pool_w, m_pool_scale, m_conv_w, m_conv_b, m_dt_bias, m_a_log, m_d_skip, m_ssm_norm_w, m_even_w_out, m_odd_w_in, m_fgate_b, m_q_norm_w, m_w_uq, m_kv_norm_w, m_w_ukv, m_odd_w_out, m_ffn_w_gate, m_ffn_w_up, m_ffn_w_down, m_ln_mix_g, m_ln_mix_b, m_ln_ffn_g, m_ln_ffn_b, v_even_w_in, v_pool_w, v_pool_scale, v_conv_w, v_conv_b, v_dt_bias, v_a_log, v_d_skip, v_ssm_norm_w, v_even_w_out, v_odd_w_in, v_fgate_b, v_q_norm_w, v_w_uq, v_kv_norm_w, v_w_ukv, v_odd_w_out, v_ffn_w_gate, v_ffn_w_up, v_ffn_w_down, v_ln_mix_g, v_ln_mix_b, v_ln_ffn_g, v_ln_ffn_b):
    weights = dict(even_w_in=even_w_in, pool_w=pool_w, pool_scale=pool_scale, conv_w=conv_w, conv_b=conv_b,
                   dt_bias=dt_bias, a_log=a_log, d_skip=d_skip, ssm_norm_w=ssm_norm_w, even_w_out=even_w_out,
                   odd_w_in=odd_w_in, fgate_b=fgate_b, q_norm_w=q_norm_w, w_uq=w_uq, kv_norm_w=kv_norm_w, w_ukv=w_ukv,
                   odd_w_out=odd_w_out, ffn_w_gate=ffn_w_gate, ffn_w_up=ffn_w_up, ffn_w_down=ffn_w_down,
                   ln_mix_g=ln_mix_g, ln_mix_b=ln_mix_b, ln_ffn_g=ln_ffn_g, ln_ffn_b=ln_ffn_b)
    m_in = dict(even_w_in=m_even_w_in, pool_w=m_pool_w, pool_scale=m_pool_scale, conv_w=m_conv_w, conv_b=m_conv_b,
                dt_bias=m_dt_bias, a_log=m_a_log, d_skip=m_d_skip, ssm_norm_w=m_ssm_norm_w, even_w_out=m_even_w_out,
                odd_w_in=m_odd_w_in, fgate_b=m_fgate_b, q_norm_w=m_q_norm_w, w_uq=m_w_uq, kv_norm_w=m_kv_norm_w,
                w_ukv=m_w_ukv, odd_w_out=m_odd_w_out, ffn_w_gate=m_ffn_w_gate, ffn_w_up=m_ffn_w_up,
                ffn_w_down=m_ffn_w_down, ln_mix_g=m_ln_mix_g, ln_mix_b=m_ln_mix_b, ln_ffn_g=m_ln_ffn_g,
                ln_ffn_b=m_ln_ffn_b)
    v_in = dict(even_w_in=v_even_w_in, pool_w=v_pool_w, pool_scale=v_pool_scale, conv_w=v_conv_w, conv_b=v_conv_b,
                dt_bias=v_dt_bias, a_log=v_a_log, d_skip=v_d_skip, ssm_norm_w=v_ssm_norm_w, even_w_out=v_even_w_out,
                odd_w_in=v_odd_w_in, fgate_b=v_fgate_b, q_norm_w=v_q_norm_w, w_uq=v_w_uq, kv_norm_w=v_kv_norm_w,
                w_ukv=v_w_ukv, odd_w_out=v_odd_w_out, ffn_w_gate=v_ffn_w_gate, ffn_w_up=v_ffn_w_up,
                ffn_w_down=v_ffn_w_down, ln_mix_g=v_ln_mix_g, ln_mix_b=v_ln_mix_b, ln_ffn_g=v_ln_ffn_g,
                ln_ffn_b=v_ln_ffn_b)
    shards = {n: weights[n] for n, _ in SHARDED}

    w = _prepare_weights(_gather_weights(shards))
    small = {n: weights[n] for n in REPLICATED}
    w.update(small)

    loss_local, dx, layer_grads = _local_step(x[0], loss_target[0], w, small)
    grads_full = _grads_by_name(layer_grads)

    reduced = _reduce_scatter(_pack_grads(grads_full, shards))
    grads = _unpack_reduced(reduced, shards)
    small_sum = _all_reduce_small(_flatten_small(grads_full), "small_grads_allreduce")
    grads.update(_unflatten_small(small_sum, small))
    loss = lax.psum(loss_local, ("x", "y", "c"))

    deltas, new_m, new_v = {}, {}, {}
    for n in WEIGHT_NAMES:
        deltas[n], new_m[n], new_v[n] = _adamw(weights[n], grads[n], m_in[n], v_in[n], "adamw_" + n)
    return (loss, dx[None], *[grads[n] for n in WEIGHT_NAMES], *[deltas[n] for n in WEIGHT_NAMES],
            *[new_m[n] for n in WEIGHT_NAMES], *[new_v[n] for n in WEIGHT_NAMES])
```

```python
import functools
import math

import numpy as np
import jax
import jax.numpy as jnp
from jax import lax
from jax.experimental import pallas as pl
from jax.experimental.pallas import tpu as pltpu

F32 = jnp.float32
BF16 = jnp.bfloat16
HI = lax.Precision.HIGHEST
MESH_ID = pl.DeviceIdType.MESH

VMEM_LIMIT_BYTES = 56 * 1024 * 1024
LANES = 128

D_MODEL = 1024
DEPTH = 4
POOL_WINDOWS = (2, 4, 8, 16)
POOL_GROUP = 128
POOL_WIDTH = 512
SSM_D_INNER = 1024
SSM_HEAD_DIM = 64
SSM_HEADS = 16
SSM_GROUPS = 2
SSM_STATE = 128
SSM_CONV = 4
SSM_CHUNK = 128
SSM_CONV_DIM = 1536
EVEN_IN = 3088
EVEN_IN_PAD = 3200
FOX_HEADS = 8
FOX_WIDTH = 512
MLA_HEADS = 8
MLA_NOPE = 64
MLA_ROPE = 32
MLA_V = 64
MLA_Q_RANK = 512
MLA_KV_RANK = 256
ROPE_THETA = 10000.0
ODD_IN = 2344
ODD_IN_PAD = 2560
D_FF = 2816
ALPHA = (2 * DEPTH) ** 0.25
LN_EPS = 1e-5
RMS_EPS = 1e-6
ADAM_LR = 0.001
ADAM_B1 = 0.9
ADAM_B2 = 0.999
ADAM_EPS = 1e-08
ADAM_WD = 0.01
ADAM_STEP = 10
NEG_BIG = -1e30

ATTN_TILE = 512
ATTN_WIDE = 2048
FOX_SCALE = 0.125
MLA_SCALE = (MLA_NOPE + MLA_ROPE) ** -0.5
ROW_TILE = 256


def _cparams(sem=None):
    return pltpu.CompilerParams(dimension_semantics=sem, vmem_limit_bytes=VMEM_LIMIT_BYTES)


def _pick(d, prefs):
    for p in prefs:
        if d % p == 0:
            return p
    return d


_M_PREFS = (2048, 1024, 512, 640, 1408, 768, 384, 256, 128)
_N_PREFS = (1024, 1408, 512, 640, 768, 384, 256, 128)
_K_PREFS = (1024, 1408, 512, 640, 768, 256, 128)
MM_MAX_ACC_ELEMS = 1408 * 1024


def _mm(a, b, mode, out_dtype, name, extra=None, alpha=1.0):
    if mode == "nn":
        (m, k), (k2, n) = a.shape, b.shape
    elif mode == "nt":
        (m, k), (n, k2) = a.shape, b.shape
    else:
        (k, m), (k2, n) = a.shape, b.shape
    assert k == k2, (a.shape, b.shape, mode)
    tn, tk = _pick(n, _N_PREFS), _pick(k, _K_PREFS)
    tm = _pick(m, tuple(p for p in _M_PREFS if p * tn <= MM_MAX_ACC_ELEMS))
    nk = k // tk
    if mode == "nn":
        a_spec = pl.BlockSpec((tm, tk), lambda i, j, kk: (i, kk))
        b_spec = pl.BlockSpec((tk, tn), lambda i, j, kk: (kk, j))
        dims = (((1,), (0,)), ((), ()))
    elif mode == "nt":
        a_spec = pl.BlockSpec((tm, tk), lambda i, j, kk: (i, kk))
        b_spec = pl.BlockSpec((tn, tk), lambda i, j, kk: (j, kk))
        dims = (((1,), (1,)), ((), ()))
    else:
        a_spec = pl.BlockSpec((tk, tm), lambda i, j, kk: (kk, i))
        b_spec = pl.BlockSpec((tk, tn), lambda i, j, kk: (kk, j))
        dims = (((0,), (0,)), ((), ()))
    o_spec = pl.BlockSpec((tm, tn), lambda i, j, kk: (i, j))
    has_extra = extra is not None

    def body(*refs):
        if has_extra:
            a_ref, b_ref, e_ref, o_ref, acc = refs
        else:
            a_ref, b_ref, o_ref, acc = refs
        kk = pl.program_id(2)

        @pl.when(kk == 0)
        def _():
            acc[...] = jnp.zeros_like(acc)

        acc[...] += lax.dot_general(a_ref[...].astype(BF16), b_ref[...].astype(BF16), dims,
                                    preferred_element_type=F32)

        @pl.when(kk == nk - 1)
        def _():
            r = acc[...]
            if has_extra:
                r = r + alpha * e_ref[...].astype(F32)
            o_ref[...] = r.astype(o_ref.dtype)

    ins = [a, b] + ([extra] if has_extra else [])
    specs = [a_spec, b_spec] + ([o_spec] if has_extra else [])
    return pl.pallas_call(
        body, name=name, grid=(m // tm, n // tn, nk), in_specs=specs, out_specs=o_spec,
        out_shape=jax.ShapeDtypeStruct((m, n), out_dtype),
        scratch_shapes=[pltpu.VMEM((tm, tn), F32)],
        compiler_params=_cparams(("parallel", "parallel", "arbitrary")),
    )(*ins)


def _full_spec(shape):
    nd = len(shape)
    return pl.BlockSpec(tuple(shape), lambda i, _nd=nd: (0,) * _nd)


class _Cols:
    def __init__(self, arr, block, width):
        self.arr, self.block, self.width = arr, block, width
        self.shape, self.dtype = (arr.shape[0], width), arr.dtype


def _tile_arr(t):
    return t.arr if isinstance(t, _Cols) else t


def _tile_spec(t, tt, row_index):
    block = t.block if isinstance(t, _Cols) else 0
    return pl.BlockSpec((tt, t.shape[1]), lambda i: (row_index(i), block))


def _scan_fwd(f, name, n, tiles, ctiles, params, cparams, carry_shapes, out_defs):
    rows = tiles[0].shape[0]
    tt = rows // n
    nt, nct, npar, ncp, ncar, nout = len(tiles), len(ctiles), len(params), len(cparams), len(carry_shapes), len(out_defs)

    def body(*refs):
        pos = 0
        t_refs = refs[pos:pos + nt]; pos += nt
        ct_refs = refs[pos:pos + nct]; pos += nct
        p_refs = refs[pos:pos + npar]; pos += npar
        cp_refs = refs[pos:pos + ncp]; pos += ncp
        o_refs = refs[pos:pos + nout]; pos += nout
        cs_refs = refs[pos:pos + ncar]; pos += ncar
        c_scr = refs[pos:pos + ncar]
        i = pl.program_id(0)

        @pl.when(i == 0)
        def _():
            for c in c_scr:
                c[...] = jnp.zeros_like(c)

        carry = tuple(c[...] for c in c_scr)
        for s, c in zip(cs_refs, carry):
            s[0] = c
        new_carry, outs = f(carry, tuple(r[...] for r in t_refs), tuple(r[...] for r in ct_refs),
                            tuple(r[...] for r in p_refs), tuple(r[...] for r in cp_refs), i)
        for o_ref, o in zip(o_refs, outs):
            o_ref[...] = o
        for c, v in zip(c_scr, new_carry):
            c[...] = v

    tile_spec = lambda t: _tile_spec(t, tt, lambda i: i)
    in_specs = ([tile_spec(t) for t in tiles] + [tile_spec(t) for t in ctiles]
                + [_full_spec(p.shape) for p in params] + [_full_spec(p.shape) for p in cparams])
    out_specs = ([pl.BlockSpec((tt, c), lambda i: (i, 0)) for c, _ in out_defs]
                 + [pl.BlockSpec((1,) + tuple(s), lambda i: (i, 0, 0)) for s in carry_shapes])
    out_shape = ([jax.ShapeDtypeStruct((rows, c), dt) for c, dt in out_defs]
                 + [jax.ShapeDtypeStruct((n,) + tuple(s), F32) for s in carry_shapes])
    res = pl.pallas_call(
        body, name=name, grid=(n,), in_specs=in_specs, out_specs=out_specs, out_shape=out_shape,
        scratch_shapes=[pltpu.VMEM(tuple(s), F32) for s in carry_shapes],
        compiler_params=_cparams(("arbitrary",)),
    )(*map(_tile_arr, tiles), *map(_tile_arr, ctiles), *params, *cparams)
    return list(res[:nout]), list(res[nout:])


def _scan_bwd(f, name, n, tiles, ctiles, params, cparams, carries, douts):
    rows = tiles[0].shape[0]
    tt = rows // n
    nt, nct, npar, ncp, ncar, nout = len(tiles), len(ctiles), len(params), len(cparams), len(carries), len(douts)

    def body(*refs):
        pos = 0
        t_refs = refs[pos:pos + nt]; pos += nt
        ct_refs = refs[pos:pos + nct]; pos += nct
        p_refs = refs[pos:pos + npar]; pos += npar
        cp_refs = refs[pos:pos + ncp]; pos += ncp
        cs_refs = refs[pos:pos + ncar]; pos += ncar
        do_refs = refs[pos:pos + nout]; pos += nout
        dt_refs = refs[pos:pos + nt]; pos += nt
        dp_refs = refs[pos:pos + npar]; pos += npar
        dc_scr = refs[pos:pos + ncar]
        i = pl.program_id(0)

        @pl.when(i == 0)
        def _():
            for c in dc_scr:
                c[...] = jnp.zeros_like(c)
            for d in dp_refs:
                d[...] = jnp.zeros_like(d)

        ctv = tuple(r[...] for r in ct_refs)
        cpv = tuple(r[...] for r in cp_refs)

        def g(c, t, p):
            return f(c, t, ctv, p, cpv, n - 1 - i)

        _, vjp = jax.vjp(g, tuple(s[0] for s in cs_refs), tuple(r[...] for r in t_refs),
                         tuple(r[...] for r in p_refs))
        dc, dt, dp = vjp((tuple(c[...] for c in dc_scr), tuple(r[...] for r in do_refs)))
        for r, v in zip(dt_refs, dt):
            r[...] = v
        for r, v in zip(dp_refs, dp):
            r[...] += v
        for c, v in zip(dc_scr, dc):
            c[...] = v

    rev_tile = lambda t: _tile_spec(t, tt, lambda i: n - 1 - i)
    rev_out = lambda t: pl.BlockSpec((tt, t.shape[1]), lambda i: (n - 1 - i, 0))
    in_specs = ([rev_tile(t) for t in tiles] + [rev_tile(t) for t in ctiles]
                + [_full_spec(p.shape) for p in params] + [_full_spec(p.shape) for p in cparams]
                + [pl.BlockSpec((1,) + tuple(c.shape[1:]), lambda i: (n - 1 - i, 0, 0)) for c in carries]
                + [rev_tile(d) for d in douts])
    out_specs = [rev_out(t) for t in tiles] + [_full_spec(p.shape) for p in params]
    out_shape = ([jax.ShapeDtypeStruct(t.shape, t.dtype) for t in tiles]
                 + [jax.ShapeDtypeStruct(p.shape, F32) for p in params])
    res = pl.pallas_call(
        body, name=name, grid=(n,), in_specs=in_specs, out_specs=out_specs, out_shape=out_shape,
        scratch_shapes=[pltpu.VMEM(tuple(c.shape[1:]), F32) for c in carries],
        compiler_params=_cparams(("arbitrary",)),
    )(*map(_tile_arr, tiles), *map(_tile_arr, ctiles), *params, *cparams, *carries, *map(_tile_arr, douts))
    return list(res[:nt]), list(res[nt:])


def _silu(x):
    return x * jax.nn.sigmoid(x)


def _softplus(x):
    return jnp.maximum(x, 0.0) + jnp.log(1.0 + jnp.exp(-jnp.abs(x)))


def _f_ln(carry, tiles, ctiles, params, cparams, idx):
    (r,), (g, b) = tiles, params
    mu = jnp.mean(r, axis=-1, keepdims=True)
    xc = r - mu
    var = jnp.mean(xc * xc, axis=-1, keepdims=True)
    return (), (xc * lax.rsqrt(var + LN_EPS) * g + b,)


def _f_ln_with_bf16(carry, tiles, ctiles, params, cparams, idx):
    _, (y,) = _f_ln(carry, tiles, ctiles, params, cparams, idx)
    return (), (y, y.astype(BF16))


def _f_act(carry, tiles, ctiles, params, cparams, idx):
    (gu,) = tiles
    g = gu[:, :D_FF].astype(F32)
    u = gu[:, D_FF:].astype(F32)
    return (), ((_silu(g) * u).astype(BF16),)


def _f_pool(carry, tiles, ctiles, params, cparams, idx):
    (prev,), (u,), (pool_w, pool_scale) = carry, tiles, params
    tt = u.shape[0]
    halo = prev.shape[0]
    ext = jnp.concatenate([prev, u], axis=0)
    pos = idx * tt + lax.broadcasted_iota(jnp.int32, (tt, 1), 0)
    ys = []
    for g, w in enumerate(POOL_WINDOWS):
        lo, hi = g * POOL_GROUP, (g + 1) * POOL_GROUP
        eg = ext[:, lo:hi]
        s = eg[halo:halo + tt]
        for j in range(1, w):
            s = s + eg[halo - j:halo - j + tt]
        count = jnp.minimum(pos + 1, w).astype(F32)
        diff = s / count - u[:, lo:hi]
        ys.append(jnp.dot(diff.astype(BF16), pool_w[g].astype(BF16), preferred_element_type=F32))
    y = jnp.concatenate(ys, axis=1) * pool_scale
    return (u[tt - halo:, :],), (y.astype(BF16),)


def _f_conv(carry, tiles, ctiles, params, cparams, idx):
    (prev,), (xbc,), (conv_w, conv_b) = carry, tiles, params
    tt = xbc.shape[0]
    halo = prev.shape[0]
    ext = jnp.concatenate([prev, xbc], axis=0)
    acc = jnp.zeros_like(xbc) + conv_b
    for k in range(SSM_CONV):
        off = halo - (SSM_CONV - 1) + k
        acc = acc + ext[off:off + tt] * conv_w[k:k + 1, :]
    return (xbc[tt - halo:, :],), (_silu(acc),)


def _head_expand_matrix():
    r = lax.broadcasted_iota(jnp.int32, (LANES, SSM_D_INNER), 0)
    c = lax.broadcasted_iota(jnp.int32, (LANES, SSM_D_INNER), 1)
    return ((c >= r * SSM_HEAD_DIM) & (c < (r + 1) * SSM_HEAD_DIM)).astype(BF16)


def _head_expand(v, e):
    hi = v.astype(BF16)
    lo = (v - hi.astype(F32)).astype(BF16)
    return jnp.dot(hi, e, preferred_element_type=F32) + jnp.dot(lo, e, preferred_element_type=F32)


def _f_ssd(carry, tiles, ctiles, params, cparams, idx):
    (state,), (xa, dtr, z), (dt_bias, a_log, d_skip, norm_w) = carry, tiles, params
    L, P, N, E = SSM_CHUNK, SSM_HEAD_DIM, SSM_STATE, SSM_HEADS // SSM_GROUPS
    gw = E * P
    em = _head_expand_matrix()
    dt = _softplus(dtr + dt_bias)
    da = dt * (-jnp.exp(a_log))
    r = lax.broadcasted_iota(jnp.int32, (L, L), 0)
    c = lax.broadcasted_iota(jnp.int32, (L, L), 1)
    tri = c <= r
    acs = jnp.dot(tri.astype(F32), da, precision=HI, preferred_element_type=F32)
    acs_t = acs.T
    last = acs[L - 1:L, :]
    xs = xa[:, :SSM_D_INNER]
    xdt = xs * _head_expand(dt, em)
    xdt_b = xdt.astype(BF16)
    xdec = (xdt * _head_expand(jnp.exp(last - acs), em)).astype(BF16)
    eacs = _head_expand(jnp.exp(acs), em)
    y_parts, st_parts = [], []
    for g in range(SSM_GROUPS):
        bg = xa[:, SSM_D_INNER + g * N:SSM_D_INNER + (g + 1) * N].astype(BF16)
        cg = xa[:, SSM_D_INNER + (SSM_GROUPS + g) * N:SSM_D_INNER + (SSM_GROUPS + g + 1) * N].astype(BF16)
        cb = lax.dot_general(cg, bg, (((1,), (1,)), ((), ())), preferred_element_type=F32)
        prev_g = state[:, g * gw:(g + 1) * gw]
        y_off = jnp.dot(cg, prev_g.astype(BF16), preferred_element_type=F32) * eacs[:, g * gw:(g + 1) * gw]
        st_parts.append(lax.dot_general(bg, xdec[:, g * gw:(g + 1) * gw], (((0,), (0,)), ((), ())),
                                        preferred_element_type=F32))
        diag = []
        for e in range(E):
            h = g * E + e
            lmat = jnp.exp(jnp.where(tri, acs[:, h:h + 1] - acs_t[h:h + 1, :], NEG_BIG))
            diag.append(jnp.dot((cb * lmat).astype(BF16), xdt_b[:, h * P:(h + 1) * P], preferred_element_type=F32))
        y_parts.append(jnp.concatenate(diag, axis=1) + y_off)
    y = jnp.concatenate(y_parts, axis=1) + xs * _head_expand(d_skip, em)
    new_state = state * _head_expand(jnp.exp(last), em) + jnp.concatenate(st_parts, axis=1)
    y = y * _silu(z)
    y = y * lax.rsqrt(jnp.mean(y * y, axis=-1, keepdims=True) + RMS_EPS) * norm_w
    return (new_state,), (y.astype(BF16),)


def _f_fox_gate(carry, tiles, ctiles, params, cparams, idx):
    (run,), (fl,), (fb,) = carry, tiles, params
    tt = fl.shape[0]
    logf = -_softplus(-(fl + fb))
    r = lax.broadcasted_iota(jnp.int32, (tt, tt), 0)
    c = lax.broadcasted_iota(jnp.int32, (tt, tt), 1)
    cum = jnp.dot((c <= r).astype(F32), logf, precision=HI, preferred_element_type=F32) + run[0:1, :]
    return (jnp.broadcast_to(cum[tt - 1:tt, :], run.shape),), (cum,)


def _rms(x, w):
    return x * lax.rsqrt(jnp.mean(x * x, axis=-1, keepdims=True) + RMS_EPS) * w


def _f_mla_prep(carry, tiles, ctiles, params, cparams, idx):
    (cq, ckv, kr), (ck, sk), (qw, kvw), (rk,) = tiles, ctiles, params, cparams
    rot = jnp.dot(kr, rk, precision=HI, preferred_element_type=F32)
    return (), (_rms(cq, qw).astype(BF16), _rms(ckv, kvw).astype(BF16), (kr * ck + rot * sk).astype(BF16))


def _f_rope_q(carry, tiles, ctiles, params, cparams, idx):
    (q,), (cos, sin) = tiles, ctiles
    nope = MLA_HEADS * MLA_NOPE
    x1 = q[:, nope:nope + LANES]
    x2 = q[:, nope + LANES:]
    roped = jnp.concatenate([q[:, :nope], x1 * cos - x2 * sin, x2 * cos + x1 * sin], axis=1)
    return (), ((roped * MLA_SCALE).astype(BF16),)


def _scores(q, k, fq, fk, masked, row0, col0):
    s = lax.dot_general(q, k, (((1,), (1,)), ((), ())), preferred_element_type=F32)
    if fq is not None:
        s = s + fq - fk
    if masked:
        rows = row0 + lax.broadcasted_iota(jnp.int32, s.shape, 0)
        cols = col0 + lax.broadcasted_iota(jnp.int32, s.shape, 1)
        s = jnp.where(cols <= rows, s, NEG_BIG)
    return s


def _q_major_tables(T, tq, tk):
    r = tk // tq
    qi = np.concatenate([np.full(i // r + 1, i, np.int32) for i in range(T // tq)])
    ki = np.concatenate([np.arange(i // r + 1, dtype=np.int32) for i in range(T // tq)])
    kind = np.where(ki == qi // r, qi % r + 1, 0).astype(np.int32)
    return [jnp.asarray(a) for a in (qi, ki, (ki == 0).astype(np.int32), kind)]


def _k_major_tables(T, tq, tk):
    r = tq // tk
    nq = T // tq
    ki = np.concatenate([np.full(nq - i // r, i, np.int32) for i in range(T // tk)])
    qi = np.concatenate([np.arange(i // r, nq, dtype=np.int32) for i in range(T // tk)])
    kind = np.where(qi == ki // r, r - ki % r, 0).astype(np.int32)
    return [jnp.asarray(a) for a in (ki, qi, (qi == nq - 1).astype(np.int32), kind)]


def _attn_tiles(T):
    return min(ATTN_TILE, T), min(ATTN_WIDE, T)


def _attn_fwd(q, k, v, fq, fk, name):
    H, T, dk = q.shape
    dv = v.shape[2]
    tq, tk = _attn_tiles(T)
    decay = fq is not None
    tables = _q_major_tables(T, tq, tk)

    def body(qi_ref, ki_ref, first_ref, kind_ref, *refs):
        if decay:
            q_ref, k_ref, v_ref, fq_ref, fk_ref, o_ref, lse_ref, m_s, l_s, acc = refs
        else:
            q_ref, k_ref, v_ref, o_ref, lse_ref, m_s, l_s, acc = refs
        t = pl.program_id(1)
        qi, ki = qi_ref[t], ki_ref[t]

        @pl.when(first_ref[t] == 1)
        def _():
            m_s[...] = jnp.full_like(m_s, NEG_BIG)
            l_s[...] = jnp.zeros_like(l_s)
            acc[...] = jnp.zeros_like(acc)

        def step(kind):
            w = tk if kind == 0 else kind * tq
            s = _scores(q_ref[0], k_ref[0, :w, :], fq_ref[0] if decay else None,
                        fk_ref[0, :, :w] if decay else None, kind > 0, qi * tq, ki * tk)
            m_new = jnp.maximum(m_s[...], jnp.max(s, axis=-1, keepdims=True))
            p = jnp.exp(s - m_new)
            corr = jnp.exp(m_s[...] - m_new)
            l_s[...] = corr * l_s[...] + jnp.sum(p, axis=-1, keepdims=True)
            acc[...] = corr * acc[...] + jnp.dot(p.astype(BF16), v_ref[0, :w, :], preferred_element_type=F32)
            m_s[...] = m_new
            if kind > 0:
                o_ref[0] = (acc[...] / l_s[...]).astype(o_ref.dtype)
                lse_ref[0] = m_s[...] + jnp.log(l_s[...])

        for kind in range(tk // tq + 1):
            pl.when(kind_ref[t] == kind)(functools.partial(step, kind))

    qspec = lambda d: pl.BlockSpec((1, tq, d), lambda h, t, qi, ki, fi, la: (h, qi[t], 0))
    kspec = lambda d: pl.BlockSpec((1, tk, d), lambda h, t, qi, ki, fi, la: (h, ki[t], 0))
    in_specs = [qspec(dk), kspec(dk), kspec(dv)]
    ins = [q, k, v]
    if decay:
        in_specs += [qspec(1), pl.BlockSpec((1, 1, tk), lambda h, t, qi, ki, fi, la: (h, 0, ki[t]))]
        ins += [fq, fk]
    grid_spec = pltpu.PrefetchScalarGridSpec(
        num_scalar_prefetch=4, grid=(H, int(tables[0].shape[0])), in_specs=in_specs,
        out_specs=[qspec(dv), qspec(1)],
        scratch_shapes=[pltpu.VMEM((tq, 1), F32), pltpu.VMEM((tq, 1), F32), pltpu.VMEM((tq, dv), F32)])
    return pl.pallas_call(
        body, name=name, grid_spec=grid_spec,
        out_shape=[jax.ShapeDtypeStruct((H, T, dv), BF16), jax.ShapeDtypeStruct((H, T, 1), F32)],
        compiler_params=_cparams(("parallel", "arbitrary")),
    )(*tables, *ins)


def _attn_bwd_dq(q, k, v, o, do, lse, fq, fk, name):
    H, T, dk = q.shape
    dv = v.shape[2]
    tq, tk = _attn_tiles(T)
    decay = fq is not None
    tables = _q_major_tables(T, tq, tk)

    def body(qi_ref, ki_ref, first_ref, kind_ref, *refs):
        if decay:
            q_ref, k_ref, v_ref, o_ref, do_ref, lse_ref, fq_ref, fk_ref, dq_ref, dl_ref, acc, dl, leak = refs
        else:
            q_ref, k_ref, v_ref, o_ref, do_ref, lse_ref, dq_ref, dl_ref, acc, dl, leak = refs
        t = pl.program_id(1)
        qi, ki = qi_ref[t], ki_ref[t]

        @pl.when(first_ref[t] == 1)
        def _():
            acc[...] = jnp.zeros_like(acc)
            leak[...] = jnp.zeros_like(leak)
            dl[...] = jnp.sum(do_ref[0].astype(F32) * o_ref[0].astype(F32), axis=-1, keepdims=True)

        def step(kind):
            w = tk if kind == 0 else kind * tq
            k_v = k_ref[0, :w, :]
            s = _scores(q_ref[0], k_v, fq_ref[0] if decay else None, fk_ref[0, :, :w] if decay else None,
                        kind > 0, qi * tq, ki * tk)
            p = jnp.exp(s - lse_ref[0])
            dp = lax.dot_general(do_ref[0], v_ref[0, :w, :], (((1,), (1,)), ((), ())), preferred_element_type=F32)
            ds = p * (dp - dl[...])
            leak[...] += jnp.sum(ds, axis=-1, keepdims=True)
            acc[...] += jnp.dot(ds.astype(BF16), k_v, preferred_element_type=F32)
            if kind > 0:
                dq_ref[0] = acc[...].astype(dq_ref.dtype)
                dl_ref[0] = dl[...] + leak[...]

        for kind in range(tk // tq + 1):
            pl.when(kind_ref[t] == kind)(functools.partial(step, kind))

    qspec = lambda d: pl.BlockSpec((1, tq, d), lambda h, t, qi, ki, fi, la: (h, qi[t], 0))
    kspec = lambda d: pl.BlockSpec((1, tk, d), lambda h, t, qi, ki, fi, la: (h, ki[t], 0))
    in_specs = [qspec(dk), kspec(dk), kspec(dv), qspec(dv), qspec(dv), qspec(1)]
    ins = [q, k, v, o, do, lse]
    if decay:
        in_specs += [qspec(1), pl.BlockSpec((1, 1, tk), lambda h, t, qi, ki, fi, la: (h, 0, ki[t]))]
        ins += [fq, fk]
    grid_spec = pltpu.PrefetchScalarGridSpec(
        num_scalar_prefetch=4, grid=(H, int(tables[0].shape[0])), in_specs=in_specs,
        out_specs=[qspec(dk), qspec(1)],
        scratch_shapes=[pltpu.VMEM((tq, dk), F32), pltpu.VMEM((tq, 1), F32), pltpu.VMEM((tq, 1), F32)])
    return pl.pallas_call(
        body, name=name, grid_spec=grid_spec,
        out_shape=[jax.ShapeDtypeStruct((H, T, dk), BF16), jax.ShapeDtypeStruct((H, T, 1), F32)],
        compiler_params=_cparams(("parallel", "arbitrary")),
    )(*tables, *ins)


def _attn_bwd_dkv(q, k, v, delta, do, lse, fq, fk, name):
    H, T, dk = q.shape
    dv = v.shape[2]
    tk, tq = _attn_tiles(T)
    decay = fq is not None
    tables = _k_major_tables(T, tq, tk)

    def body(ki_ref, qi_ref, last_ref, kind_ref, *refs):
        if decay:
            q_ref, k_ref, v_ref, dl_ref, do_ref, lse_ref, fq_ref, fk_ref, dk_ref, dv_ref, df_ref, dk_s, dv_s, df_s = refs
        else:
            q_ref, k_ref, v_ref, dl_ref, do_ref, lse_ref, dk_ref, dv_ref, dk_s, dv_s = refs
        t = pl.program_id(1)
        ki, qi = ki_ref[t], qi_ref[t]

        @pl.when(kind_ref[t] > 0)
        def _():
            dk_s[...] = jnp.zeros_like(dk_s)
            dv_s[...] = jnp.zeros_like(dv_s)
            if decay:
                df_s[...] = jnp.zeros_like(df_s)

        def step(kind):
            off = 0 if kind == 0 else tq - kind * tk
            q_v, do_v = q_ref[0, off:, :], do_ref[0, off:, :]
            s = _scores(q_v, k_ref[0], fq_ref[0, off:, :] if decay else None, fk_ref[0] if decay else None,
                        kind > 0, qi * tq + off, ki * tk)
            p = jnp.exp(s - lse_ref[0, off:, :])
            dv_s[...] += lax.dot_general(p.astype(BF16), do_v, (((0,), (0,)), ((), ())), preferred_element_type=F32)
            dp = lax.dot_general(do_v, v_ref[0], (((1,), (1,)), ((), ())), preferred_element_type=F32)
            ds = p * (dp - dl_ref[0, off:, :])
            dk_s[...] += lax.dot_general(ds.astype(BF16), q_v, (((0,), (0,)), ((), ())),
                                         preferred_element_type=F32)
            if decay:
                df_s[...] -= jnp.sum(ds, axis=0, keepdims=True)

        for kind in range(tq // tk + 1):
            pl.when(kind_ref[t] == kind)(functools.partial(step, kind))

        @pl.when(last_ref[t] == 1)
        def _():
            dk_ref[0] = dk_s[...].astype(dk_ref.dtype)
            dv_ref[0] = dv_s[...].astype(dv_ref.dtype)
            if decay:
                df_ref[0] = df_s[...]

    qspec = lambda d: pl.BlockSpec((1, tq, d), lambda h, t, ki, qi, fi, la: (h, qi[t], 0))
    kspec = lambda d: pl.BlockSpec((1, tk, d), lambda h, t, ki, qi, fi, la: (h, ki[t], 0))
    in_specs = [qspec(dk), kspec(dk), kspec(dv), qspec(1), qspec(dv), qspec(1)]
    ins = [q, k, v, delta, do, lse]
    out_specs = [kspec(dk), kspec(dv)]
    out_shape = [jax.ShapeDtypeStruct((H, T, dk), BF16), jax.ShapeDtypeStruct((H, T, dv), BF16)]
    scratch = [pltpu.VMEM((tk, dk), F32), pltpu.VMEM((tk, dv), F32)]
    if decay:
        fkspec = pl.BlockSpec((1, 1, tk), lambda h, t, ki, qi, fi, la: (h, 0, ki[t]))
        in_specs += [qspec(1), fkspec]
        ins += [fq, fk]
        out_specs.append(fkspec)
        out_shape.append(jax.ShapeDtypeStruct((H, 1, T), F32))
        scratch.append(pltpu.VMEM((1, tk), F32))
    grid_spec = pltpu.PrefetchScalarGridSpec(
        num_scalar_prefetch=4, grid=(H, int(tables[0].shape[0])), in_specs=in_specs, out_specs=out_specs,
        scratch_shapes=scratch)
    return pl.pallas_call(
        body, name=name, grid_spec=grid_spec, out_shape=out_shape, compiler_params=_cparams(("parallel", "arbitrary")),
    )(*tables, *ins)


def _attn_bwd_fused(q, k, v, o, do, lse, name):
    H, T, dk = q.shape
    dv = v.shape[2]
    tk, tq = _attn_tiles(T)
    tables = _k_major_tables(T, tq, tk)
    npairs = int(tables[0].shape[0])

    def body(ki_ref, qi_ref, last_ref, kind_ref, q_ref, k_ref, v_ref, o_ref, do_ref, lse_ref,
             dq_ref, dk_ref, dv_ref, dq_s, dk_s, dv_s):
        t = pl.program_id(1)
        ki, qi = ki_ref[t], qi_ref[t]

        @pl.when(t == 0)
        def _():
            dq_s[...] = jnp.zeros_like(dq_s)

        @pl.when(kind_ref[t] > 0)
        def _():
            dk_s[...] = jnp.zeros_like(dk_s)
            dv_s[...] = jnp.zeros_like(dv_s)

        def step(kind):
            off = 0 if kind == 0 else tq - kind * tk
            q_v, do_v, k_v = q_ref[0, off:, :], do_ref[0, off:, :], k_ref[0]
            s = _scores(q_v, k_v, None, None, kind > 0, qi * tq + off, ki * tk)
            p = jnp.exp(s - lse_ref[0, off:, :])
            delta = jnp.sum(do_v.astype(F32) * o_ref[0, off:, :].astype(F32), axis=-1, keepdims=True)
            dv_s[...] += lax.dot_general(p.astype(BF16), do_v, (((0,), (0,)), ((), ())), preferred_element_type=F32)
            dp = lax.dot_general(do_v, v_ref[0], (((1,), (1,)), ((), ())), preferred_element_type=F32)
            ds = (p * (dp - delta)).astype(BF16)
            dk_s[...] += lax.dot_general(ds, q_v, (((0,), (0,)), ((), ())), preferred_element_type=F32)
            rows = pl.ds(pl.multiple_of(qi * tq + off, tk), tq - off)
            dq_s[rows, :] += jnp.dot(ds, k_v, preferred_element_type=F32)

        for kind in range(tq // tk + 1):
            pl.when(kind_ref[t] == kind)(functools.partial(step, kind))

        @pl.when(last_ref[t] == 1)
        def _():
            dk_ref[0] = dk_s[...].astype(dk_ref.dtype)
            dv_ref[0] = dv_s[...].astype(dv_ref.dtype)

        @pl.when(t == npairs - 1)
        def _():
            dq_ref[0] = dq_s[...].astype(dq_ref.dtype)

    qspec = lambda d: pl.BlockSpec((1, tq, d), lambda h, t, ki, qi, la, kd: (h, qi[t], 0))
    kspec = lambda d: pl.BlockSpec((1, tk, d), lambda h, t, ki, qi, la, kd: (h, ki[t], 0))
    grid_spec = pltpu.PrefetchScalarGridSpec(
        num_scalar_prefetch=4, grid=(H, npairs),
        in_specs=[qspec(dk), kspec(dk), kspec(dv), qspec(dv), qspec(dv), qspec(1)],
        out_specs=[pl.BlockSpec((1, T, dk), lambda h, t, ki, qi, la, kd: (h, 0, 0)), kspec(dk), kspec(dv)],
        scratch_shapes=[pltpu.VMEM((T, dk), F32), pltpu.VMEM((tk, dk), F32), pltpu.VMEM((tk, dv), F32)])
    return pl.pallas_call(
        body, name=name, grid_spec=grid_spec,
        out_shape=[jax.ShapeDtypeStruct((H, T, dk), BF16), jax.ShapeDtypeStruct((H, T, dk), BF16),
                   jax.ShapeDtypeStruct((H, T, dv), BF16)],
        compiler_params=_cparams(("parallel", "arbitrary")),
    )(*tables, q, k, v, o, do, lse)


def _loss_head(y, target, name):
    rows, d = y.shape
    tt = _pick(rows, (512, 256, 128))

    def body(y_ref, t_ref, dy_ref, part_ref):
        @pl.when(pl.program_id(0) == 0)
        def _():
            part_ref[...] = jnp.zeros_like(part_ref)

        err = y_ref[...] - t_ref[...]
        dy_ref[...] = err * (1.0 / d)
        sq = jnp.sum(err * err, axis=0, keepdims=True)
        folded = sq[:, :LANES]
        for j in range(1, d // LANES):
            folded = folded + sq[:, j * LANES:(j + 1) * LANES]
        part_ref[...] += folded

    spec = pl.BlockSpec((tt, d), lambda i: (i, 0))
    return pl.pallas_call(
        body, name=name, grid=(rows // tt,), in_specs=[spec, spec],
        out_specs=[spec, pl.BlockSpec((1, LANES), lambda i: (0, 0))],
        out_shape=[jax.ShapeDtypeStruct((rows, d), F32), jax.ShapeDtypeStruct((1, LANES), F32)],
        compiler_params=_cparams(("arbitrary",)),
    )(y, target)


def _adamw(w, g, m, v, name):
    shape = w.shape
    if len(shape) != 3:
        view = (1, shape[0], shape[1]) if len(shape) == 2 else (shape[0], math.prod(shape[1:-1]), shape[-1])
        d, nm, nv = _adamw(*(a.reshape(view) for a in (w, g, m, v)), name)
        return d.reshape(shape), nm.reshape(shape), nv.reshape(shape)
    layers, rows, cols = shape
    tr = _pick(rows, (512, 256, 128, 64, 32, 16, 8))
    c1 = 1.0 / (1.0 - ADAM_B1 ** ADAM_STEP)
    c2 = 1.0 / (1.0 - ADAM_B2 ** ADAM_STEP)

    def body(w_ref, g_ref, m_ref, v_ref, d_ref, nm_ref, nv_ref):
        gv = g_ref[...]
        nm = ADAM_B1 * m_ref[...] + (1.0 - ADAM_B1) * gv
        nv = ADAM_B2 * v_ref[...] + (1.0 - ADAM_B2) * gv * gv
        d_ref[...] = -ADAM_LR * ((nm * c1) / (jnp.sqrt(nv * c2) + ADAM_EPS) + ADAM_WD * w_ref[...])
        nm_ref[...] = nm
        nv_ref[...] = nv

    spec = pl.BlockSpec((1, tr, cols), lambda l, i: (l, i, 0))
    sds = jax.ShapeDtypeStruct(shape, F32)
    return pl.pallas_call(
        body, name=name, grid=(layers, rows // tr), in_specs=[spec] * 4, out_specs=[spec] * 3, out_shape=[sds] * 3,
        compiler_params=_cparams(("parallel", "parallel")),
    )(w, g, m, v)


_ANY = pl.BlockSpec(memory_space=pl.ANY)


def _my_xyc():
    return lax.axis_index("x"), lax.axis_index("y"), lax.axis_index("c")


def _chip_allgather_core_half(shards, name):
    n = len(shards)
    half = [s.shape[0] // 2 for s in shards]

    def body(*refs):
        full_refs, whole_refs = refs[:n], refs[n:2 * n]
        send_sems, recv_sems, loc_sems = refs[2 * n:]
        x, y, c = _my_xyc()
        myq = 2 * x + y
        x_refs = [full_refs[i].at[pl.ds(c * half[i], half[i])] for i in range(n)]
        out_refs = [whole_refs[i].at[c] for i in range(n)]
        chips = [(1 - x, y), (x, 1 - y), (1 - x, 1 - y)]
        started = []
        for i in range(n):
            local = pltpu.make_async_copy(x_refs[i], out_refs[i].at[myq], loc_sems.at[i])
            local.start()
            started.append(local)
        sends = []
        for j, (px, py) in enumerate(chips):
            for i in range(n):
                cp = pltpu.make_async_remote_copy(src_ref=x_refs[i], dst_ref=out_refs[i].at[myq],
                                                  send_sem=send_sems.at[3 * i + j], recv_sem=recv_sems.at[3 * i + j],
                                                  device_id=(px, py, c), device_id_type=MESH_ID)
                cp.start()
                sends.append(cp)
        for j, (px, py) in enumerate(chips):
            for i in range(n):
                pltpu.make_async_remote_copy(src_ref=x_refs[i], dst_ref=out_refs[i].at[2 * px + py],
                                             send_sem=send_sems.at[3 * i + j], recv_sem=recv_sems.at[3 * i + j],
                                             device_id=(px, py, c), device_id_type=MESH_ID).wait_recv()
        for cp in sends:
            cp.wait_send()
        for local in started:
            local.wait()

    return pl.pallas_call(
        body, name=name, in_specs=[_ANY] * n, out_specs=[_ANY] * n,
        out_shape=[jax.ShapeDtypeStruct((2, 4, h) + s.shape[1:], s.dtype) for s, h in zip(shards, half)],
        scratch_shapes=[pltpu.SemaphoreType.DMA((3 * n,)), pltpu.SemaphoreType.DMA((3 * n,)),
                        pltpu.SemaphoreType.DMA((n,))],
    )(*shards)


def _sibling_swap_halves(g, name):
    q, rows, cols = g.shape
    rh = rows // 2

    def body(g_ref, out_ref, send_sem, recv_sem):
        x, y, c = _my_xyc()
        src = g_ref.at[:, pl.ds((1 - c) * rh, rh), :]
        cp = pltpu.make_async_remote_copy(src_ref=src, dst_ref=out_ref, send_sem=send_sem, recv_sem=recv_sem,
                                          device_id=(x, y, 1 - c), device_id_type=MESH_ID)
        cp.start()
        cp.wait()

    return pl.pallas_call(
        body, name=name, in_specs=[_ANY], out_specs=_ANY,
        out_shape=jax.ShapeDtypeStruct((q, rh, cols), g.dtype),
        scratch_shapes=[pltpu.SemaphoreType.DMA, pltpu.SemaphoreType.DMA],
    )(g)


def _add_own_half(g, other, name):
    q, rows, cols = g.shape
    rh = rows // 2
    tr = _pick(rh, (512, 256, 128, 64, 32, 16, 8))
    nb = rh // tr
    cidx = lax.axis_index("c").astype(jnp.int32).reshape(1)

    def body(c_ref, g_ref, o_ref, out_ref):
        out_ref[...] = (g_ref[...] + o_ref[...]).astype(out_ref.dtype)

    grid_spec = pltpu.PrefetchScalarGridSpec(
        num_scalar_prefetch=1, grid=(q, nb),
        in_specs=[pl.BlockSpec((1, tr, cols), lambda a, i, c_ref: (a, c_ref[0] * nb + i, 0)),
                  pl.BlockSpec((1, tr, cols), lambda a, i, c_ref: (a, i, 0))],
        out_specs=pl.BlockSpec((1, tr, cols), lambda a, i, c_ref: (a, i, 0)))
    return pl.pallas_call(
        body, name=name, grid_spec=grid_spec, out_shape=jax.ShapeDtypeStruct((q, rh, cols), BF16),
        compiler_params=_cparams(("parallel", "parallel")),
    )(cidx, g, other)


def _chip_exchange(s, name):
    def body(s_ref, out_ref, send_sems, recv_sems, loc_sem):
        x, y, c = _my_xyc()
        myq = 2 * x + y
        local = pltpu.make_async_copy(s_ref.at[myq], out_ref.at[myq], loc_sem)
        local.start()
        chips = [(1 - x, y), (x, 1 - y), (1 - x, 1 - y)]
        sends = []
        for j, (px, py) in enumerate(chips):
            cp = pltpu.make_async_remote_copy(src_ref=s_ref.at[2 * px + py], dst_ref=out_ref.at[myq],
                                              send_sem=send_sems.at[j], recv_sem=recv_sems.at[j],
                                              device_id=(px, py, c), device_id_type=MESH_ID)
            cp.start()
            sends.append(cp)
        for j, (px, py) in enumerate(chips):
            pltpu.make_async_remote_copy(src_ref=s_ref.at[myq], dst_ref=out_ref.at[2 * px + py],
                                         send_sem=send_sems.at[j], recv_sem=recv_sems.at[j],
                                         device_id=(px, py, c), device_id_type=MESH_ID).wait_recv()
        for cp in sends:
            cp.wait_send()
        local.wait()

    return pl.pallas_call(
        body, name=name, in_specs=[_ANY], out_specs=_ANY, out_shape=jax.ShapeDtypeStruct(s.shape, s.dtype),
        scratch_shapes=[pltpu.SemaphoreType.DMA((3,)), pltpu.SemaphoreType.DMA((3,)), pltpu.SemaphoreType.DMA],
    )(s)


def _sum_leading_into_core_slot(a, name):
    q, rows, cols = a.shape
    tr = _pick(rows, (512, 256, 128, 64, 32, 16, 8))
    cidx = lax.axis_index("c").astype(jnp.int32).reshape(1)

    def body(c_ref, a_ref, out_ref):
        acc = a_ref[0].astype(F32)
        for j in range(1, q):
            acc = acc + a_ref[j].astype(F32)
        out_ref[0] = acc

    grid_spec = pltpu.PrefetchScalarGridSpec(
        num_scalar_prefetch=1, grid=(rows // tr,),
        in_specs=[pl.BlockSpec((q, tr, cols), lambda i, c_ref: (0, i, 0))],
        out_specs=pl.BlockSpec((1, tr, cols), lambda i, c_ref: (c_ref[0], i, 0)))
    return pl.pallas_call(
        body, name=name, grid_spec=grid_spec, out_shape=jax.ShapeDtypeStruct((2, rows, cols), F32),
        compiler_params=_cparams(("parallel",)),
    )(cidx, a)


def _sibling_fill(arrays, name):
    n = len(arrays)

    def body(*refs):
        buf_refs = refs[n:2 * n]
        send_sems, recv_sems = refs[2 * n:]
        x, y, c = _my_xyc()
        sends = []
        for i in range(n):
            cp = pltpu.make_async_remote_copy(src_ref=buf_refs[i].at[c], dst_ref=buf_refs[i].at[c],
                                              send_sem=send_sems.at[i], recv_sem=recv_sems.at[i],
                                              device_id=(x, y, 1 - c), device_id_type=MESH_ID)
            cp.start()
            sends.append(cp)
        for i in range(n):
            pltpu.make_async_remote_copy(src_ref=buf_refs[i].at[c], dst_ref=buf_refs[i].at[1 - c],
                                         send_sem=send_sems.at[i], recv_sem=recv_sems.at[i],
                                         device_id=(x, y, 1 - c), device_id_type=MESH_ID).wait_recv()
        for cp in sends:
            cp.wait_send()

    return pl.pallas_call(
        body, name=name, in_specs=[_ANY] * n, out_specs=[_ANY] * n,
        out_shape=[jax.ShapeDtypeStruct(a.shape, a.dtype) for a in arrays],
        input_output_aliases={i: i for i in range(n)},
        scratch_shapes=[pltpu.SemaphoreType.DMA((n,)), pltpu.SemaphoreType.DMA((n,))],
    )(*arrays)


def _all_reduce_small(vec, name):
    rows, cols = vec.shape

    def body(v_ref, out_ref, buf, send_sems, recv_sems):
        x, y, c = _my_xyc()
        me = 4 * x + 2 * y + c
        buf[me] = v_ref[...]
        sends = []
        for kk in range(1, 8):
            peer = (1 - x if kk & 4 else x, 1 - y if kk & 2 else y, 1 - c if kk & 1 else c)
            cp = pltpu.make_async_remote_copy(src_ref=v_ref, dst_ref=buf.at[me], send_sem=send_sems.at[kk - 1],
                                              recv_sem=recv_sems.at[kk - 1], device_id=peer, device_id_type=MESH_ID)
            cp.start()
            sends.append(cp)
        for kk in range(1, 8):
            px, py, pc = (1 - x if kk & 4 else x, 1 - y if kk & 2 else y, 1 - c if kk & 1 else c)
            pltpu.make_async_remote_copy(src_ref=v_ref, dst_ref=buf.at[4 * px + 2 * py + pc],
                                         send_sem=send_sems.at[kk - 1], recv_sem=recv_sems.at[kk - 1],
                                         device_id=(px, py, pc), device_id_type=MESH_ID).wait_recv()
        for cp in sends:
            cp.wait_send()
        acc = buf[0]
        for j in range(1, 8):
            acc = acc + buf[j]
        out_ref[...] = acc

    vm = pl.BlockSpec(memory_space=pltpu.VMEM)
    return pl.pallas_call(
        body, name=name, in_specs=[vm], out_specs=vm, out_shape=jax.ShapeDtypeStruct((rows, cols), F32),
        scratch_shapes=[pltpu.VMEM((8, rows, cols), F32), pltpu.SemaphoreType.DMA((7,)), pltpu.SemaphoreType.DMA((7,))],
        compiler_params=pltpu.CompilerParams(vmem_limit_bytes=VMEM_LIMIT_BYTES),
    )(vec)


def _reduce_scatter(g):
    other = _sibling_swap_halves(g, "rs_swap_halves")
    s = _add_own_half(g, other, "rs_add_halves")
    recv = _chip_exchange(s, "rs_chip_exchange")
    r = _sum_leading_into_core_slot(recv, "rs_sum_chips")
    (full,) = _sibling_fill([r], "rs_join_halves")
    return full.reshape(g.shape[1], g.shape[2])


PACK_COLS = 1024
PACK_ROW_MULTIPLE = 1024
SHARDED = (("even_w_in", 2), ("conv_w", 2), ("even_w_out", 1), ("odd_w_in", 2), ("q_norm_w", 1), ("w_uq", 2),
           ("kv_norm_w", 1), ("w_ukv", 2), ("odd_w_out", 1), ("ffn_w_gate", 2), ("ffn_w_up", 2), ("ffn_w_down", 1))
KEEP_F32 = ("conv_w", "q_norm_w", "kv_norm_w")
REPLICATED = ("pool_w", "pool_scale", "conv_b", "dt_bias", "a_log", "d_skip", "ssm_norm_w", "fgate_b",
              "ln_mix_g", "ln_mix_b", "ln_ffn_g", "ln_ffn_b")


def _gather_weights(shards):
    sent = [shards[name] if name in KEEP_F32 else shards[name].astype(BF16) for name, _ in SHARDED]
    gathered = _sibling_fill(_chip_allgather_core_half(sent, "weights_allgather"), "weights_sibling_fill")
    out = {}
    for (name, axis), halves in zip(SHARDED, gathered):
        out[name] = jnp.concatenate([jnp.concatenate([halves[h, q] for q in range(4)], axis=axis) for h in range(2)],
                                    axis=0)
    return out


def _pack_grads(grads, shards):
    pieces = []
    for name, axis in SHARDED:
        width = shards[name].shape[axis]
        for g in grads[name]:
            ax = axis - 1
            split = g.reshape(g.shape[:ax] + (4, width) + g.shape[ax + 1:])
            pieces.append(jnp.moveaxis(split, ax, 0).reshape(4, -1))
    n = sum(p.shape[1] for p in pieces)
    per = PACK_COLS * PACK_ROW_MULTIPLE
    total = -(-n // per) * per
    pieces.append(jnp.zeros((4, total - n), F32))
    return jnp.concatenate(pieces, axis=1).reshape(4, total // PACK_COLS, PACK_COLS)


def _unpack_reduced(reduced, shards):
    flat = reduced.reshape(-1)
    out, off = {}, 0
    for name, _ in SHARDED:
        shp = shards[name].shape
        size = math.prod(shp)
        out[name] = flat[off:off + size].reshape(shp)
        off += size
    return out


def _to_heads(t, nh):
    rows, width = t.shape
    return t.reshape(rows, nh, width // nh).transpose(1, 0, 2)


def _from_heads(t):
    nh, rows, d = t.shape
    return t.transpose(1, 0, 2).reshape(rows, nh * d)


def _rope_tables(rows):
    half = MLA_ROPE // 2
    freqs = jnp.power(ROPE_THETA, -jnp.arange(half, dtype=F32) / half)
    ang = jnp.arange(rows, dtype=F32)[:, None] * freqs[None, :]
    cos, sin = jnp.cos(ang), jnp.sin(ang)
    cos_q, sin_q = jnp.tile(cos, (1, MLA_HEADS)), jnp.tile(sin, (1, MLA_HEADS))
    zeros = jnp.zeros((rows, LANES - MLA_ROPE), F32)
    cos_k = jnp.concatenate([cos, cos, zeros], axis=1)
    sin_k = jnp.concatenate([sin, sin, zeros], axis=1)
    r = lax.broadcasted_iota(jnp.int32, (LANES, LANES), 0)
    c = lax.broadcasted_iota(jnp.int32, (LANES, LANES), 1)
    rot = (jnp.where((r == c + half) & (c < half), -1.0, 0.0)
           + jnp.where((c == r + half) & (r < half), 1.0, 0.0)).astype(F32)
    return cos_q, sin_q, cos_k, sin_k, rot


def _pad_cols(a, width):
    return jnp.pad(a, ((0, 0), (0, width - a.shape[1])))


_EVEN_CUT = (512, 1536, 3072, 3088)


def _prep_even_w_in(w):
    c0, c1, c2, c3 = _EVEN_CUT
    return jnp.concatenate([w[:, c1:c2], w[:, :c0], w[:, c0:c1], _pad_cols(w[:, c2:c3], LANES)], axis=1)


def _unprep_even_w_in(g):
    c0, c1, c2, c3 = _EVEN_CUT
    n_xbc, n_u = c2 - c1, c0
    return jnp.concatenate([g[:, n_xbc:n_xbc + n_u], g[:, n_xbc + n_u:c2], g[:, :n_xbc], g[:, c2:c3]], axis=1)


_ODD_CUT = (1536, 1544, 2312, 2344)


def _prep_odd_w_in(w):
    c0, c1, c2, c3 = _ODD_CUT
    return jnp.concatenate([w[:, :c0], w[:, c1:c2], _pad_cols(w[:, c2:c3], LANES), _pad_cols(w[:, c0:c1], LANES)],
                           axis=1)


def _unprep_odd_w_in(g):
    c0, c1, c2, c3 = _ODD_CUT
    n1 = c0 + (c2 - c1)
    return jnp.concatenate([g[:, :c0], g[:, n1 + LANES:n1 + LANES + (c1 - c0)], g[:, c0:n1],
                            g[:, n1:n1 + (c3 - c2)]], axis=1)


def _prep_w_uq(w):
    r = w.reshape(w.shape[0], MLA_HEADS, MLA_NOPE + MLA_ROPE)
    half = MLA_ROPE // 2
    return jnp.concatenate([r[:, :, :MLA_NOPE].reshape(w.shape[0], -1),
                            r[:, :, MLA_NOPE:MLA_NOPE + half].reshape(w.shape[0], -1),
                            r[:, :, MLA_NOPE + half:].reshape(w.shape[0], -1)], axis=1)


def _unprep_w_uq(g):
    rows = g.shape[0]
    half = MLA_ROPE // 2
    nope = MLA_HEADS * MLA_NOPE
    return jnp.concatenate([g[:, :nope].reshape(rows, MLA_HEADS, MLA_NOPE),
                            g[:, nope:nope + LANES].reshape(rows, MLA_HEADS, half),
                            g[:, nope + LANES:].reshape(rows, MLA_HEADS, half)], axis=2).reshape(rows, -1)


def _row(v, width=None):
    v = v.reshape(1, -1)
    return v if width is None else _pad_cols(v, width)


def _even_forward(x_bf, w, i):
    rows = x_bf.shape[0]
    proj = _mm(x_bf, w["even_w_in"][i], "nn", F32, "even_proj")
    xbc, u, z, dtr = _Cols(proj, 0, 1536), _Cols(proj, 3, 512), _Cols(proj, 2, 1024), _Cols(proj, 24, LANES)
    n_row = rows // ROW_TILE
    pool_p = (w["pool_w"][i], _row(w["pool_scale"][i]))
    (y_pool,), pool_c = _scan_fwd(_f_pool, "pool_fwd", n_row, [u], [], pool_p, [], [(16, POOL_WIDTH)],
                                  [(POOL_WIDTH, BF16)])
    conv_p = (w["conv_w"][i], _row(w["conv_b"][i]))
    (xa,), conv_c = _scan_fwd(_f_conv, "conv_fwd", n_row, [xbc], [], conv_p, [], [(8, SSM_CONV_DIM)],
                              [(SSM_CONV_DIM, F32)])
    ssd_p = (_row(w["dt_bias"][i], LANES), _row(w["a_log"][i], LANES), _row(w["d_skip"][i], LANES),
             _row(w["ssm_norm_w"][i]))
    (y_ssm,), ssd_c = _scan_fwd(_f_ssd, "ssd_fwd", rows // SSM_CHUNK, [xa, dtr, z], [], ssd_p, [],
                                [(SSM_STATE, SSM_D_INNER)], [(SSM_D_INNER, BF16)])
    mix = jnp.concatenate([y_pool, y_ssm], axis=1)
    saved = dict(u=u, z=z, xbc=xbc, dtr=dtr, xa=xa, mix=mix, pool_p=pool_p, pool_c=pool_c, conv_p=conv_p,
                 conv_c=conv_c, ssd_p=ssd_p, ssd_c=ssd_c)
    return mix, saved


def _even_backward(dmix, x_bf, sv, w, i, d_r1):
    rows = x_bf.shape[0]
    n_row = rows // ROW_TILE
    dy_pool, dy_ssm = dmix[:, :POOL_WIDTH], dmix[:, POOL_WIDTH:]
    (du,), (g_pool_w, g_pool_scale) = _scan_bwd(_f_pool, "pool_bwd", n_row, [sv["u"]], [], sv["pool_p"], [],
                                                sv["pool_c"], [dy_pool])
    (dxa, ddtr, dz), (g_dt_bias, g_a_log, g_d_skip, g_norm_w) = _scan_bwd(
        _f_ssd, "ssd_bwd", rows // SSM_CHUNK, [sv["xa"], sv["dtr"], sv["z"]], [], sv["ssd_p"], [], sv["ssd_c"],
        [dy_ssm])
    (dxbc,), (g_conv_w, g_conv_b) = _scan_bwd(_f_conv, "conv_bwd", n_row, [sv["xbc"]], [], sv["conv_p"], [],
                                              sv["conv_c"], [dxa])
    dproj = jnp.concatenate([dxbc, du, dz, ddtr], axis=1).astype(BF16)
    g_w_in = _unprep_even_w_in(_mm(x_bf, dproj, "tn", F32, "even_dw_in"))
    dx = _mm(dproj, w["even_w_in"][i], "nt", F32, "even_dx", extra=d_r1, alpha=ALPHA)
    grads = dict(even_w_in=g_w_in, pool_w=g_pool_w, pool_scale=g_pool_scale[0], conv_w=g_conv_w, conv_b=g_conv_b[0],
                 dt_bias=g_dt_bias[0, :SSM_HEADS], a_log=g_a_log[0, :SSM_HEADS], d_skip=g_d_skip[0, :SSM_HEADS],
                 ssm_norm_w=g_norm_w[0])
    return dx, grads


def _odd_forward(x_bf, w, i, tables):
    rows = x_bf.shape[0]
    cos_q, sin_q, cos_k, sin_k, rot = tables
    proj = _mm(x_bf, w["odd_w_in"][i], "nn", F32, "odd_proj")
    qf, kf, vf = (_to_heads((proj[:, j * 512:(j + 1) * 512] * sc).astype(BF16), FOX_HEADS)
                  for j, sc in enumerate((FOX_SCALE, 1.0, 1.0)))
    cq, ckv = _Cols(proj, 3, MLA_Q_RANK), _Cols(proj, 8, MLA_KV_RANK)
    kr, fl = _Cols(proj, 18, LANES), _Cols(proj, 19, LANES)
    n_row = rows // ROW_TILE
    fox_p = (_row(w["fgate_b"][i], LANES),)
    n_gate = rows // min(ATTN_TILE, rows)
    (fcum,), fox_c = _scan_fwd(_f_fox_gate, "fox_gate_fwd", n_gate, [fl], [], fox_p, [], [(8, LANES)], [(LANES, F32)])
    fc_heads = fcum[:, :FOX_HEADS].T
    fq, fk = fc_heads[:, :, None], fc_heads[:, None, :]
    o_fox, lse_fox = _attn_fwd(qf, kf, vf, fq, fk, "fox_attn_fwd")

    prep_p = (_row(w["q_norm_w"][i]), _row(w["kv_norm_w"][i]))
    (cqn, ckvn, krr), _ = _scan_fwd(_f_mla_prep, "mla_prep_fwd", n_row, [cq, ckv, kr], [cos_k, sin_k], prep_p,
                                    [rot], [], [(MLA_Q_RANK, BF16), (MLA_KV_RANK, BF16), (LANES, BF16)])
    q_flat = _mm(cqn, w["w_uq"][i], "nn", F32, "mla_q_up")
    (q_rope,), _ = _scan_fwd(_f_rope_q, "rope_q_fwd", n_row, [q_flat], [cos_q, sin_q], [], [], [],
                             [(q_flat.shape[1], BF16)])
    kv = _mm(ckvn, w["w_ukv"][i], "nn", BF16, "mla_kv_up")
    nope = MLA_HEADS * MLA_NOPE
    half = MLA_ROPE // 2
    q_m = jnp.concatenate([_to_heads(q_rope[:, :nope], MLA_HEADS), _to_heads(q_rope[:, nope:nope + LANES], MLA_HEADS),
                           _to_heads(q_rope[:, nope + LANES:], MLA_HEADS)], axis=2)
    kv_h = _to_heads(kv, MLA_HEADS)
    k_rope = jnp.broadcast_to(krr[None, :, :MLA_ROPE], (MLA_HEADS, rows, MLA_ROPE))
    k_m = jnp.concatenate([kv_h[:, :, :MLA_NOPE], k_rope], axis=2)
    v_m = kv_h[:, :, MLA_NOPE:]
    o_mla, lse_mla = _attn_fwd(q_m, k_m, v_m, None, None, "mla_attn_fwd")
    mix = jnp.concatenate([_from_heads(o_fox), _from_heads(o_mla)], axis=1)
    saved = dict(qf=qf, kf=kf, vf=vf, fq=fq, fk=fk, o_fox=o_fox, lse_fox=lse_fox, fl=fl, fox_p=fox_p, fox_c=fox_c,
                 cq=cq, ckv=ckv, kr=kr, prep_p=prep_p, cqn=cqn, ckvn=ckvn, q_flat=q_flat, q_m=q_m, k_m=k_m, v_m=v_m,
                 o_mla=o_mla, lse_mla=lse_mla, mix=mix)
    return mix, saved


def _odd_backward(dmix, x_bf, sv, w, i, d_r1, tables):
    rows = x_bf.shape[0]
    cos_q, sin_q, cos_k, sin_k, rot = tables
    n_row = rows // ROW_TILE
    nope = MLA_HEADS * MLA_NOPE
    half = MLA_ROPE // 2
    do_fox = _to_heads(dmix[:, :FOX_WIDTH], FOX_HEADS)
    do_mla = _to_heads(dmix[:, FOX_WIDTH:], MLA_HEADS)
    fox_args = (sv["qf"], sv["kf"], sv["vf"], sv["o_fox"], do_fox, sv["lse_fox"], sv["fq"], sv["fk"])
    dqf, delta_fox = _attn_bwd_dq(*fox_args, "fox_attn_dq")
    dkf, dvf, dfk = _attn_bwd_dkv(*fox_args[:3], delta_fox, *fox_args[4:], "fox_attn_dkv")
    dfcum = _pad_cols(dfk[:, 0, :].T, LANES)
    n_gate = rows // min(ATTN_TILE, rows)
    (dfl,), (g_fb,) = _scan_bwd(_f_fox_gate, "fox_gate_bwd", n_gate, [sv["fl"]], [], sv["fox_p"], [], sv["fox_c"],
                                [dfcum])
    mla_args = (sv["q_m"], sv["k_m"], sv["v_m"], sv["o_mla"], do_mla, sv["lse_mla"], None, None)
    dq_m, dk_m, dv_m = _attn_bwd_fused(*mla_args[:6], "mla_attn_bwd")
    dq_rope = jnp.concatenate([_from_heads(dq_m[:, :, :MLA_NOPE]), _from_heads(dq_m[:, :, MLA_NOPE:MLA_NOPE + half]),
                               _from_heads(dq_m[:, :, MLA_NOPE + half:])], axis=1)
    (dq_flat,), _ = _scan_bwd(_f_rope_q, "rope_q_bwd", n_row, [sv["q_flat"]], [cos_q, sin_q], [], [], [], [dq_rope])
    g_w_uq = _mm(sv["cqn"], dq_flat, "tn", F32, "mla_dw_uq")
    dcqn = _mm(dq_flat, w["w_uq"][i], "nt", BF16, "mla_dcqn")
    dkv = _from_heads(jnp.concatenate([dk_m[:, :, :MLA_NOPE], dv_m], axis=2))
    g_w_ukv = _mm(sv["ckvn"], dkv, "tn", F32, "mla_dw_ukv")
    dckvn = _mm(dkv, w["w_ukv"][i], "nt", BF16, "mla_dckvn")
    dkrr = _head_sum(dk_m, "mla_dk_rope_sum")
    (dcq, dckv, dkr), (g_qw, g_kvw) = _scan_bwd(_f_mla_prep, "mla_prep_bwd", n_row, [sv["cq"], sv["ckv"], sv["kr"]],
                                                [cos_k, sin_k], sv["prep_p"], [rot], [], [dcqn, dckvn, dkrr])
    dproj = jnp.concatenate([_from_heads(dqf).astype(F32) * FOX_SCALE, _from_heads(dkf).astype(F32), _from_heads(dvf).astype(F32),
                             dcq, dckv, dkr, dfl], axis=1).astype(BF16)
    g_w_in = _mm(x_bf, dproj, "tn", F32, "odd_dw_in")
    dx = _mm(dproj, w["odd_w_in"][i], "nt", F32, "odd_dx", extra=d_r1, alpha=ALPHA)
    grads = dict(odd_w_in=_unprep_odd_w_in(g_w_in), fgate_b=g_fb[0, :FOX_HEADS], q_norm_w=g_qw[0], kv_norm_w=g_kvw[0],
                 w_uq=_unprep_w_uq(g_w_uq), w_ukv=g_w_ukv)
    return dx, grads


def _head_sum(dk_m, name):
    H, T, dk = dk_m.shape
    tt = _pick(T, (512, 256, 128))

    def body(d_ref, o_ref):
        acc = d_ref[0].astype(F32)
        for h in range(1, H):
            acc = acc + d_ref[h].astype(F32)
        o_ref[...] = jnp.concatenate([acc[:, MLA_NOPE:], jnp.zeros((tt, LANES - MLA_ROPE), F32)], axis=1).astype(BF16)

    return pl.pallas_call(
        body, name=name, grid=(T // tt,), in_specs=[pl.BlockSpec((H, tt, dk), lambda i: (0, i, 0))],
        out_specs=pl.BlockSpec((tt, LANES), lambda i: (i, 0)), out_shape=jax.ShapeDtypeStruct((T, LANES), BF16),
        compiler_params=_cparams(("parallel",)),
    )(dk_m)


def _local_step(x, target, w, small):
    rows = x.shape[0]
    n_row = rows // ROW_TILE
    tables = _rope_tables(rows)
    saved = []
    x_f32 = x
    x_bf = x.astype(BF16)
    for l in range(DEPTH):
        i = l // 2
        if l % 2 == 0:
            mix, sv = _even_forward(x_bf, w, i)
            w_out = w["even_w_out"][i]
        else:
            mix, sv = _odd_forward(x_bf, w, i, tables)
            w_out = w["odd_w_out"][i]
        r1 = _mm(mix, w_out, "nn", F32, "mix_out_even" if l % 2 == 0 else "mix_out_odd", extra=x_f32, alpha=ALPHA)
        ln1_p = (_row(small["ln_mix_g"][l]), _row(small["ln_mix_b"][l]))
        ln_outs = [(D_MODEL, F32), (D_MODEL, BF16)]
        (x_mid, x_mid_bf), _ = _scan_fwd(_f_ln_with_bf16, "ln_fwd", n_row, [r1], [], ln1_p, [], [], ln_outs)
        gu = _mm(x_mid_bf, w["ffn_w_gu"][l], "nn", BF16, "ffn_gu")
        (act,), _ = _scan_fwd(_f_act, "ffn_act_fwd", n_row, [gu], [], [], [], [], [(D_FF, BF16)])
        r2 = _mm(act, w["ffn_w_down"][l], "nn", F32, "ffn_down", extra=x_mid, alpha=ALPHA)
        ln2_p = (_row(small["ln_ffn_g"][l]), _row(small["ln_ffn_b"][l]))
        (x_out, x_out_bf), _ = _scan_fwd(_f_ln_with_bf16, "ln_fwd", n_row, [r2], [], ln2_p, [], [], ln_outs)
        saved.append(dict(sv=sv, x_bf=x_bf, r1=r1, ln1_p=ln1_p, x_mid_bf=x_mid_bf, gu=gu, act=act, r2=r2, ln2_p=ln2_p,
                          w_out=w_out))
        x_f32, x_bf = x_out, x_out_bf

    dy, loss_part = _loss_head(x_f32, target, "loss_head")
    loss = 0.5 * jnp.sum(loss_part) / D_MODEL

    layer_grads = []
    for l in reversed(range(DEPTH)):
        i = l // 2
        s = saved[l]
        (d_r2,), (g_ln2_g, g_ln2_b) = _scan_bwd(_f_ln, "ln_bwd", n_row, [s["r2"]], [], s["ln2_p"], [], [], [dy])
        g_down = _mm(s["act"], d_r2, "tn", F32, "ffn_dw_down")
        dact = _mm(d_r2, w["ffn_w_down"][l], "nt", BF16, "ffn_dact")
        (dgu,), _ = _scan_bwd(_f_act, "ffn_act_bwd", n_row, [s["gu"]], [], [], [], [], [dact])
        g_gu = _mm(s["x_mid_bf"], dgu, "tn", F32, "ffn_dw_gu")
        dx_mid = _mm(dgu, w["ffn_w_gu"][l], "nt", F32, "ffn_dx", extra=d_r2, alpha=ALPHA)
        (d_r1,), (g_ln1_g, g_ln1_b) = _scan_bwd(_f_ln, "ln_bwd", n_row, [s["r1"]], [], s["ln1_p"], [], [], [dx_mid])
        g_w_out = _mm(s["sv"]["mix"], d_r1, "tn", F32, "even_dw_out" if l % 2 == 0 else "odd_dw_out")
        dmix = _mm(d_r1, s["w_out"], "nt", BF16, "even_dmix" if l % 2 == 0 else "odd_dmix")
        if l % 2 == 0:
            dy, g = _even_backward(dmix, s["x_bf"], s["sv"], w, i, d_r1)
            g["even_w_out"] = g_w_out
        else:
            dy, g = _odd_backward(dmix, s["x_bf"], s["sv"], w, i, d_r1, tables)
            g["odd_w_out"] = g_w_out
        g.update(ffn_w_gate=g_gu[:, :D_FF], ffn_w_up=g_gu[:, D_FF:], ffn_w_down=g_down, ln_mix_g=g_ln1_g[0],
                 ln_mix_b=g_ln1_b[0], ln_ffn_g=g_ln2_g[0], ln_ffn_b=g_ln2_b[0])
        layer_grads.append((l, g))
    return loss, dy, layer_grads


EVEN_NAMES = ("even_w_in", "pool_w", "pool_scale", "conv_w", "conv_b", "dt_bias", "a_log", "d_skip", "ssm_norm_w",
              "even_w_out")
ODD_NAMES = ("odd_w_in", "fgate_b", "q_norm_w", "w_uq", "kv_norm_w", "w_ukv", "odd_w_out")
PER_LAYER_NAMES = ("ffn_w_gate", "ffn_w_up", "ffn_w_down", "ln_mix_g", "ln_mix_b", "ln_ffn_g", "ln_ffn_b")
WEIGHT_NAMES = EVEN_NAMES + ODD_NAMES + PER_LAYER_NAMES


def _grads_by_name(layer_grads):
    by_layer = dict(layer_grads)
    out = {}
    for n in EVEN_NAMES:
        out[n] = [by_layer[l][n] for l in range(0, DEPTH, 2)]
    for n in ODD_NAMES:
        out[n] = [by_layer[l][n] for l in range(1, DEPTH, 2)]
    for n in PER_LAYER_NAMES:
        out[n] = [by_layer[l][n] for l in range(DEPTH)]
    return out


def _prepare_weights(full):
    w = {}
    w["even_w_in"] = [_prep_even_w_in(full["even_w_in"][i]) for i in range(2)]
    w["even_w_out"] = [full["even_w_out"][i] for i in range(2)]
    w["odd_w_in"] = [_prep_odd_w_in(full["odd_w_in"][i]) for i in range(2)]
    w["w_uq"] = [_prep_w_uq(full["w_uq"][i]) for i in range(2)]
    w["w_ukv"] = [full["w_ukv"][i] for i in range(2)]
    w["odd_w_out"] = [full["odd_w_out"][i] for i in range(2)]
    w["ffn_w_gu"] = [jnp.concatenate([full["ffn_w_gate"][l], full["ffn_w_up"][l]], axis=1) for l in range(DEPTH)]
    w["ffn_w_down"] = [full["ffn_w_down"][l] for l in range(DEPTH)]
    for n in ("conv_w", "q_norm_w", "kv_norm_w"):
        w[n] = full[n]
    return w


def _flatten_small(grads):
    flat = jnp.concatenate([g.reshape(-1) for n in REPLICATED for g in grads[n]])
    n = flat.shape[0]
    per = LANES * 8
    total = -(-n // per) * per
    return jnp.pad(flat, (0, total - n)).reshape(total // LANES, LANES)


def _unflatten_small(mat, like):
    flat = mat.reshape(-1)
    out, off = {}, 0
    for n in REPLICATED:
        size = math.prod(like[n].shape)
        out[n] = flat[off:off + size].reshape(like[n].shape)
        off += size
    return out


def kernel(x, even_w_in, pool_w, pool_scale, conv_w, conv_b, dt_bias, a_log, d_skip, ssm_norm_w, even_w_out, odd_w_in, fgate_b, q_norm_w, w_uq, kv_norm_w, w_ukv, odd_w_out, ffn_w_gate, ffn_w_up, ffn_w_down, ln_mix_g, ln_mix_b, ln_ffn_g, ln_ffn_b, loss_target, m_even_w_in, m_pool_w, m_pool_scale, m_conv_w, m_conv_b, m_dt_bias, m_a_log, m_d_skip, m_ssm_norm_w, m_even_w_out, m_odd_w_in, m_fgate_b, m_q_norm_w, m_w_uq, m_kv_norm_w, m_w_ukv, m_odd_w_out, m_ffn_w_gate, m_ffn_w_up, m_ffn_w_down, m_ln_mix_g, m_ln_mix_b, m_ln_ffn_g, m_ln_ffn_b, v_even_w_in, v_pool_w, v_pool_scale, v_conv_w, v_conv_b, v_dt_bias, v_a_log, v_d_skip, v_ssm_norm_w, v_even_w_out, v_odd_w_in, v_fgate_b, v_q_norm_w, v_w_uq, v_kv_norm_w, v_w_ukv, v_odd_w_out, v_ffn_w_gate, v_ffn_w_up, v_ffn_w_down, v_ln_mix_g, v_ln_mix_b, v_ln_ffn_g, v_ln_ffn_b):
    weights = dict(even_w_in=even_w_in, pool_w=pool_w, pool_scale=pool_scale, conv_w=conv_w, conv_b=conv_b,
                   dt_bias=dt_bias, a_log=a_log, d_skip=d_skip, ssm_norm_w=ssm_norm_w, even_w_out=even_w_out,
                   odd_w_in=odd_w_in, fgate_b=fgate_b, q_norm_w=q_norm_w, w_uq=w_uq, kv_norm_w=kv_norm_w, w_ukv=w_ukv,
                   odd_w_out=odd_w_out, ffn_w_gate=ffn_w_gate, ffn_w_up=ffn_w_up, ffn_w_down=ffn_w_down,
                   ln_mix_g=ln_mix_g, ln_mix_b=ln_mix_b, ln_ffn_g=ln_ffn_g, ln_ffn_b=ln_ffn_b)
    m_in = dict(even_w_in=m_even_w_in, pool_w=m_pool_w, pool_scale=m_pool_scale, conv_w=m_conv_w, conv_b=m_conv_b,
                dt_bias=m_dt_bias, a_log=m_a_log, d_skip=m_d_skip, ssm_norm_w=m_ssm_norm_w, even_w_out=m_even_w_out,
                odd_w_in=m_odd_w_in, fgate_b=m_fgate_b, q_norm_w=m_q_norm_w, w_uq=m_w_uq, kv_norm_w=m_kv_norm_w,
                w_ukv=m_w_ukv, odd_w_out=m_odd_w_out, ffn_w_gate=m_ffn_w_gate, ffn_w_up=m_ffn_w_up,
                ffn_w_down=m_ffn_w_down, ln_mix_g=m_ln_mix_g, ln_mix_b=m_ln_mix_b, ln_ffn_g=m_ln_ffn_g,
                ln_ffn_b=m_ln_ffn_b)
    v_in = dict(even_w_in=v_even_w_in, pool_w=v_pool_w, pool_scale=v_pool_scale, conv_w=v_conv_w, conv_b=v_conv_b,
                dt_bias=v_dt_bias, a_log=v_a_log, d_skip=v_d_skip, ssm_norm_w=v_ssm_norm_w, even_w_out=v_even_w_out,
                odd_w_in=v_odd_w_in, fgate_b=v_fgate_b, q_norm_w=v_q_norm_w, w_uq=v_w_uq, kv_norm_w=v_kv_norm_w,
                w_ukv=v_w_ukv, odd_w_out=v_odd_w_out, ffn_w_gate=v_ffn_w_gate, ffn_w_up=v_ffn_w_up,
                ffn_w_down=v_ffn_w_down, ln_mix_g=v_ln_mix_g, ln_mix_b=v_ln_mix_b, ln_ffn_g=v_ln_ffn_g,
                ln_ffn_b=v_ln_ffn_b)
    shards = {n: weights[n] for n, _ in SHARDED}

    w = _prepare_weights(_gather_weights(shards))
    small = {n: weights[n] for n in REPLICATED}
    w.update(small)

    loss_local, dx, layer_grads = _local_step(x[0], loss_target[0], w, small)
    grads_full = _grads_by_name(layer_grads)

    reduced = _reduce_scatter(_pack_grads(grads_full, shards))
    grads = _unpack_reduced(reduced, shards)
    small_sum = _all_reduce_small(_flatten_small(grads_full), "small_grads_allreduce")
    grads.update(_unflatten_small(small_sum, small))
    loss = lax.psum(loss_local, ("x", "y", "c"))

    deltas, new_m, new_v = {}, {}, {}
    for n in WEIGHT_NAMES:
        deltas[n], new_m[n], new_v[n] = _adamw(weights[n], grads[n], m_in[n], v_in[n], "adamw_" + n)
    return (loss, dx[None], *[grads[n] for n in WEIGHT_NAMES], *[deltas[n] for n in WEIGHT_NAMES],
            *[new_m[n] for n in WEIGHT_NAMES], *[new_v[n] for n in WEIGHT_NAMES])
```

```python
import functools
import math

import numpy as np
import jax
import jax.numpy as jnp
from jax import lax
from jax.experimental import pallas as pl
from jax.experimental.pallas import tpu as pltpu

F32 = jnp.float32
BF16 = jnp.bfloat16
HI = lax.Precision.HIGHEST
MESH_ID = pl.DeviceIdType.MESH

VMEM_LIMIT_BYTES = 56 * 1024 * 1024
LANES = 128

D_MODEL = 1024
DEPTH = 4
POOL_WINDOWS = (2, 4, 8, 16)
POOL_GROUP = 128
POOL_WIDTH = 512
SSM_D_INNER = 1024
SSM_HEAD_DIM = 64
SSM_HEADS = 16
SSM_GROUPS = 2
SSM_STATE = 128
SSM_CONV = 4
SSM_CHUNK = 128
SSM_CONV_DIM = 1536
EVEN_IN = 3088
EVEN_IN_PAD = 3200
FOX_HEADS = 8
FOX_WIDTH = 512
MLA_HEADS = 8
MLA_NOPE = 64
MLA_ROPE = 32
MLA_V = 64
MLA_Q_RANK = 512
MLA_KV_RANK = 256
ROPE_THETA = 10000.0
ODD_IN = 2344
ODD_IN_PAD = 2560
D_FF = 2816
ALPHA = (2 * DEPTH) ** 0.25
LN_EPS = 1e-5
RMS_EPS = 1e-6
ADAM_LR = 0.001
ADAM_B1 = 0.9
ADAM_B2 = 0.999
ADAM_EPS = 1e-08
ADAM_WD = 0.01
ADAM_STEP = 10
NEG_BIG = -1e30

ATTN_TILE = 512
ATTN_WIDE = 2048
FOX_SCALE = 0.125
MLA_SCALE = (MLA_NOPE + MLA_ROPE) ** -0.5
ROW_TILE = 256


def _cparams(sem=None):
    return pltpu.CompilerParams(dimension_semantics=sem, vmem_limit_bytes=VMEM_LIMIT_BYTES)


def _pick(d, prefs):
    for p in prefs:
        if d % p == 0:
            return p
    return d


_M_PREFS = (2048, 1024, 512, 640, 1408, 768, 384, 256, 128)
_N_PREFS = (1024, 1408, 512, 640, 768, 384, 256, 128)
_K_PREFS = (1024, 1408, 512, 640, 768, 256, 128)
MM_MAX_ACC_ELEMS = 1408 * 1024


def _mm(a, b, mode, out_dtype, name, extra=None, alpha=1.0):
    if mode == "nn":
        (m, k), (k2, n) = a.shape, b.shape
    elif mode == "nt":
        (m, k), (n, k2) = a.shape, b.shape
    else:
        (k, m), (k2, n) = a.shape, b.shape
    assert k == k2, (a.shape, b.shape, mode)
    tn, tk = _pick(n, _N_PREFS), _pick(k, _K_PREFS)
    tm = _pick(m, tuple(p for p in _M_PREFS if p * tn <= MM_MAX_ACC_ELEMS))
    nk = k // tk
    if mode == "nn":
        a_spec = pl.BlockSpec((tm, tk), lambda i, j, kk: (i, kk))
        b_spec = pl.BlockSpec((tk, tn), lambda i, j, kk: (kk, j))
        dims = (((1,), (0,)), ((), ()))
    elif mode == "nt":
        a_spec = pl.BlockSpec((tm, tk), lambda i, j, kk: (i, kk))
        b_spec = pl.BlockSpec((tn, tk), lambda i, j, kk: (j, kk))
        dims = (((1,), (1,)), ((), ()))
    else:
        a_spec = pl.BlockSpec((tk, tm), lambda i, j, kk: (kk, i))
        b_spec = pl.BlockSpec((tk, tn), lambda i, j, kk: (kk, j))
        dims = (((0,), (0,)), ((), ()))
    o_spec = pl.BlockSpec((tm, tn), lambda i, j, kk: (i, j))
    has_extra = extra is not None

    def body(*refs):
        if has_extra:
            a_ref, b_ref, e_ref, o_ref, acc = refs
        else:
            a_ref, b_ref, o_ref, acc = refs
        kk = pl.program_id(2)

        @pl.when(kk == 0)
        def _():
            acc[...] = jnp.zeros_like(acc)

        acc[...] += lax.dot_general(a_ref[...].astype(BF16), b_ref[...].astype(BF16), dims,
                                    preferred_element_type=F32)

        @pl.when(kk == nk - 1)
        def _():
            r = acc[...]
            if has_extra:
                r = r + alpha * e_ref[...].astype(F32)
            o_ref[...] = r.astype(o_ref.dtype)

    ins = [a, b] + ([extra] if has_extra else [])
    specs = [a_spec, b_spec] + ([o_spec] if has_extra else [])
    return pl.pallas_call(
        body, name=name, grid=(m // tm, n // tn, nk), in_specs=specs, out_specs=o_spec,
        out_shape=jax.ShapeDtypeStruct((m, n), out_dtype),
        scratch_shapes=[pltpu.VMEM((tm, tn), F32)],
        compiler_params=_cparams(("parallel", "parallel", "arbitrary")),
    )(*ins)


def _full_spec(shape):
    nd = len(shape)
    return pl.BlockSpec(tuple(shape), lambda i, _nd=nd: (0,) * _nd)


class _Cols:
    def __init__(self, arr, block, width):
        self.arr, self.block, self.width = arr, block, width
        self.shape, self.dtype = (arr.shape[0], width), arr.dtype


def _tile_arr(t):
    return t.arr if isinstance(t, _Cols) else t


def _tile_spec(t, tt, row_index):
    block = t.block if isinstance(t, _Cols) else 0
    return pl.BlockSpec((tt, t.shape[1]), lambda i: (row_index(i), block))


def _scan_fwd(f, name, n, tiles, ctiles, params, cparams, carry_shapes, out_defs):
    rows = tiles[0].shape[0]
    tt = rows // n
    nt, nct, npar, ncp, ncar, nout = len(tiles), len(ctiles), len(params), len(cparams), len(carry_shapes), len(out_defs)

    def body(*refs):
        pos = 0
        t_refs = refs[pos:pos + nt]; pos += nt
        ct_refs = refs[pos:pos + nct]; pos += nct
        p_refs = refs[pos:pos + npar]; pos += npar
        cp_refs = refs[pos:pos + ncp]; pos += ncp
        o_refs = refs[pos:pos + nout]; pos += nout
        cs_refs = refs[pos:pos + ncar]; pos += ncar
        c_scr = refs[pos:pos + ncar]
        i = pl.program_id(0)

        @pl.when(i == 0)
        def _():
            for c in c_scr:
                c[...] = jnp.zeros_like(c)

        carry = tuple(c[...] for c in c_scr)
        for s, c in zip(cs_refs, carry):
            s[0] = c
        new_carry, outs = f(carry, tuple(r[...] for r in t_refs), tuple(r[...] for r in ct_refs),
                            tuple(r[...] for r in p_refs), tuple(r[...] for r in cp_refs), i)
        for o_ref, o in zip(o_refs, outs):
            o_ref[...] = o
        for c, v in zip(c_scr, new_carry):
            c[...] = v

    tile_spec = lambda t: _tile_spec(t, tt, lambda i: i)
    in_specs = ([tile_spec(t) for t in tiles] + [tile_spec(t) for t in ctiles]
                + [_full_spec(p.shape) for p in params] + [_full_spec(p.shape) for p in cparams])
    out_specs = ([pl.BlockSpec((tt, c), lambda i: (i, 0)) for c, _ in out_defs]
                 + [pl.BlockSpec((1,) + tuple(s), lambda i: (i, 0, 0)) for s in carry_shapes])
    out_shape = ([jax.ShapeDtypeStruct((rows, c), dt) for c, dt in out_defs]
                 + [jax.ShapeDtypeStruct((n,) + tuple(s), F32) for s in carry_shapes])
    res = pl.pallas_call(
        body, name=name, grid=(n,), in_specs=in_specs, out_specs=out_specs, out_shape=out_shape,
        scratch_shapes=[pltpu.VMEM(tuple(s), F32) for s in carry_shapes],
        compiler_params=_cparams(("arbitrary",)),
    )(*map(_tile_arr, tiles), *map(_tile_arr, ctiles), *params, *cparams)
    return list(res[:nout]), list(res[nout:])


def _scan_bwd(f, name, n, tiles, ctiles, params, cparams, carries, douts):
    rows = tiles[0].shape[0]
    tt = rows // n
    nt, nct, npar, ncp, ncar, nout = len(tiles), len(ctiles), len(params), len(cparams), len(carries), len(douts)

    def body(*refs):
        pos = 0
        t_refs = refs[pos:pos + nt]; pos += nt
        ct_refs = refs[pos:pos + nct]; pos += nct
        p_refs = refs[pos:pos + npar]; pos += npar
        cp_refs = refs[pos:pos + ncp]; pos += ncp
        cs_refs = refs[pos:pos + ncar]; pos += ncar
        do_refs = refs[pos:pos + nout]; pos += nout
        dt_refs = refs[pos:pos + nt]; pos += nt
        dp_refs = refs[pos:pos + npar]; pos += npar
        dc_scr = refs[pos:pos + ncar]
        i = pl.program_id(0)

        @pl.when(i == 0)
        def _():
            for c in dc_scr:
                c[...] = jnp.zeros_like(c)
            for d in dp_refs:
                d[...] = jnp.zeros_like(d)

        ctv = tuple(r[...] for r in ct_refs)
        cpv = tuple(r[...] for r in cp_refs)

        def g(c, t, p):
            return f(c, t, ctv, p, cpv, n - 1 - i)

        _, vjp = jax.vjp(g, tuple(s[0] for s in cs_refs), tuple(r[...] for r in t_refs),
                         tuple(r[...] for r in p_refs))
        dc, dt, dp = vjp((tuple(c[...] for c in dc_scr), tuple(r[...] for r in do_refs)))
        for r, v in zip(dt_refs, dt):
            r[...] = v
        for r, v in zip(dp_refs, dp):
            r[...] += v
        for c, v in zip(dc_scr, dc):
            c[...] = v

    rev_tile = lambda t: _tile_spec(t, tt, lambda i: n - 1 - i)
    rev_out = lambda t: pl.BlockSpec((tt, t.shape[1]), lambda i: (n - 1 - i, 0))
    in_specs = ([rev_tile(t) for t in tiles] + [rev_tile(t) for t in ctiles]
                + [_full_spec(p.shape) for p in params] + [_full_spec(p.shape) for p in cparams]
                + [pl.BlockSpec((1,) + tuple(c.shape[1:]), lambda i: (n - 1 - i, 0, 0)) for c in carries]
                + [rev_tile(d) for d in douts])
    out_specs = [rev_out(t) for t in tiles] + [_full_spec(p.shape) for p in params]
    out_shape = ([jax.ShapeDtypeStruct(t.shape, t.dtype) for t in tiles]
                 + [jax.ShapeDtypeStruct(p.shape, F32) for p in params])
    res = pl.pallas_call(
        body, name=name, grid=(n,), in_specs=in_specs, out_specs=out_specs, out_shape=out_shape,
        scratch_shapes=[pltpu.VMEM(tuple(c.shape[1:]), F32) for c in carries],
        compiler_params=_cparams(("arbitrary",)),
    )(*map(_tile_arr, tiles), *map(_tile_arr, ctiles), *params, *cparams, *carries, *map(_tile_arr, douts))
    return list(res[:nt]), list(res[nt:])


def _silu(x):
    return x * jax.nn.sigmoid(x)


def _softplus(x):
    return jnp.maximum(x, 0.0) + jnp.log(1.0 + jnp.exp(-jnp.abs(x)))


def _f_ln(carry, tiles, ctiles, params, cparams, idx):
    (r,), (g, b) = tiles, params
    mu = jnp.mean(r, axis=-1, keepdims=True)
    xc = r - mu
    var = jnp.mean(xc * xc, axis=-1, keepdims=True)
    return (), (xc * lax.rsqrt(var + LN_EPS) * g + b,)


def _f_ln_with_bf16(carry, tiles, ctiles, params, cparams, idx):
    _, (y,) = _f_ln(carry, tiles, ctiles, params, cparams, idx)
    return (), (y, y.astype(BF16))


def _f_act(carry, tiles, ctiles, params, cparams, idx):
    (gu,) = tiles
    g = gu[:, :D_FF].astype(F32)
    u = gu[:, D_FF:].astype(F32)
    return (), ((_silu(g) * u).astype(BF16),)


def _f_pool(carry, tiles, ctiles, params, cparams, idx):
    (prev,), (u,), (pool_w, pool_scale) = carry, tiles, params
    tt = u.shape[0]
    halo = prev.shape[0]
    ext = jnp.concatenate([prev, u], axis=0)
    pos = idx * tt + lax.broadcasted_iota(jnp.int32, (tt, 1), 0)
    ys = []
    for g, w in enumerate(POOL_WINDOWS):
        lo, hi = g * POOL_GROUP, (g + 1) * POOL_GROUP
        eg = ext[:, lo:hi]
        s = eg[halo:halo + tt]
        for j in range(1, w):
            s = s + eg[halo - j:halo - j + tt]
        count = jnp.minimum(pos + 1, w).astype(F32)
        diff = s / count - u[:, lo:hi]
        ys.append(jnp.dot(diff.astype(BF16), pool_w[g].astype(BF16), preferred_element_type=F32))
    y = jnp.concatenate(ys, axis=1) * pool_scale
    return (u[tt - halo:, :],), (y.astype(BF16),)


def _f_conv(carry, tiles, ctiles, params, cparams, idx):
    (prev,), (xbc,), (conv_w, conv_b) = carry, tiles, params
    tt = xbc.shape[0]
    halo = prev.shape[0]
    ext = jnp.concatenate([prev, xbc], axis=0)
    acc = jnp.zeros_like(xbc) + conv_b
    for k in range(SSM_CONV):
        off = halo - (SSM_CONV - 1) + k
        acc = acc + ext[off:off + tt] * conv_w[k:k + 1, :]
    return (xbc[tt - halo:, :],), (_silu(acc),)


def _head_expand_matrix():
    r = lax.broadcasted_iota(jnp.int32, (LANES, SSM_D_INNER), 0)
    c = lax.broadcasted_iota(jnp.int32, (LANES, SSM_D_INNER), 1)
    return ((c >= r * SSM_HEAD_DIM) & (c < (r + 1) * SSM_HEAD_DIM)).astype(BF16)


def _head_expand(v, e):
    hi = v.astype(BF16)
    lo = (v - hi.astype(F32)).astype(BF16)
    return jnp.dot(hi, e, preferred_element_type=F32) + jnp.dot(lo, e, preferred_element_type=F32)


def _f_ssd(carry, tiles, ctiles, params, cparams, idx):
    (state,), (xa, dtr, z), (dt_bias, a_log, d_skip, norm_w) = carry, tiles, params
    L, P, N, E = SSM_CHUNK, SSM_HEAD_DIM, SSM_STATE, SSM_HEADS // SSM_GROUPS
    gw = E * P
    em = _head_expand_matrix()
    dt = _softplus(dtr + dt_bias)
    da = dt * (-jnp.exp(a_log))
    r = lax.broadcasted_iota(jnp.int32, (L, L), 0)
    c = lax.broadcasted_iota(jnp.int32, (L, L), 1)
    tri = c <= r
    acs = jnp.dot(tri.astype(F32), da, precision=HI, preferred_element_type=F32)
    acs_t = acs.T
    last = acs[L - 1:L, :]
    xs = xa[:, :SSM_D_INNER]
    xdt = xs * _head_expand(dt, em)
    xdt_b = xdt.astype(BF16)
    xdec = (xdt * _head_expand(jnp.exp(last - acs), em)).astype(BF16)
    eacs = _head_expand(jnp.exp(acs), em)
    y_parts, st_parts = [], []
    for g in range(SSM_GROUPS):
        bg = xa[:, SSM_D_INNER + g * N:SSM_D_INNER + (g + 1) * N].astype(BF16)
        cg = xa[:, SSM_D_INNER + (SSM_GROUPS + g) * N:SSM_D_INNER + (SSM_GROUPS + g + 1) * N].astype(BF16)
        cb = lax.dot_general(cg, bg, (((1,), (1,)), ((), ())), preferred_element_type=F32)
        prev_g = state[:, g * gw:(g + 1) * gw]
        y_off = jnp.dot(cg, prev_g.astype(BF16), preferred_element_type=F32) * eacs[:, g * gw:(g + 1) * gw]
        st_parts.append(lax.dot_general(bg, xdec[:, g * gw:(g + 1) * gw], (((0,), (0,)), ((), ())),
                                        preferred_element_type=F32))
        diag = []
        for e in range(E):
            h = g * E + e
            lmat = jnp.exp(jnp.where(tri, acs[:, h:h + 1] - acs_t[h:h + 1, :], NEG_BIG))
            diag.append(jnp.dot((cb * lmat).astype(BF16), xdt_b[:, h * P:(h + 1) * P], preferred_element_type=F32))
        y_parts.append(jnp.concatenate(diag, axis=1) + y_off)
    y = jnp.concatenate(y_parts, axis=1) + xs * _head_expand(d_skip, em)
    new_state = state * _head_expand(jnp.exp(last), em) + jnp.concatenate(st_parts, axis=1)
    y = y * _silu(z)
    y = y * lax.rsqrt(jnp.mean(y * y, axis=-1, keepdims=True) + RMS_EPS) * norm_w
    return (new_state,), (y.astype(BF16),)


def _f_fox_gate(carry, tiles, ctiles, params, cparams, idx):
    (run,), (fl,), (fb,) = carry, tiles, params
    tt = fl.shape[0]
    logf = -_softplus(-(fl + fb))
    r = lax.broadcasted_iota(jnp.int32, (tt, tt), 0)
    c = lax.broadcasted_iota(jnp.int32, (tt, tt), 1)
    cum = jnp.dot((c <= r).astype(F32), logf, precision=HI, preferred_element_type=F32) + run[0:1, :]
    return (jnp.broadcast_to(cum[tt - 1:tt, :], run.shape),), (cum,)


def _rms(x, w):
    return x * lax.rsqrt(jnp.mean(x * x, axis=-1, keepdims=True) + RMS_EPS) * w


def _f_mla_prep(carry, tiles, ctiles, params, cparams, idx):
    (cq, ckv, kr), (ck, sk), (qw, kvw), (rk,) = tiles, ctiles, params, cparams
    rot = jnp.dot(kr, rk, precision=HI, preferred_element_type=F32)
    return (), (_rms(cq, qw).astype(BF16), _rms(ckv, kvw).astype(BF16), (kr * ck + rot * sk).astype(BF16))


def _f_rope_q(carry, tiles, ctiles, params, cparams, idx):
    (q,), (cos, sin) = tiles, ctiles
    nope = MLA_HEADS * MLA_NOPE
    x1 = q[:, nope:nope + LANES]
    x2 = q[:, nope + LANES:]
    roped = jnp.concatenate([q[:, :nope], x1 * cos - x2 * sin, x2 * cos + x1 * sin], axis=1)
    return (), ((roped * MLA_SCALE).astype(BF16),)


def _scores(q, k, fq, fk, masked, row0, col0):
    s = lax.dot_general(q, k, (((1,), (1,)), ((), ())), preferred_element_type=F32)
    if fq is not None:
        s = s + fq - fk
    if masked:
        rows = row0 + lax.broadcasted_iota(jnp.int32, s.shape, 0)
        cols = col0 + lax.broadcasted_iota(jnp.int32, s.shape, 1)
        s = jnp.where(cols <= rows, s, NEG_BIG)
    return s


def _q_major_tables(T, tq, tk):
    r = tk // tq
    qi = np.concatenate([np.full(i // r + 1, i, np.int32) for i in range(T // tq)])
    ki = np.concatenate([np.arange(i // r + 1, dtype=np.int32) for i in range(T // tq)])
    kind = np.where(ki == qi // r, qi % r + 1, 0).astype(np.int32)
    return [jnp.asarray(a) for a in (qi, ki, (ki == 0).astype(np.int32), kind)]


def _k_major_tables(T, tq, tk):
    r = tq // tk
    nq = T // tq
    ki = np.concatenate([np.full(nq - i // r, i, np.int32) for i in range(T // tk)])
    qi = np.concatenate([np.arange(i // r, nq, dtype=np.int32) for i in range(T // tk)])
    kind = np.where(qi == ki // r, r - ki % r, 0).astype(np.int32)
    return [jnp.asarray(a) for a in (ki, qi, (qi == nq - 1).astype(np.int32), kind)]


def _attn_tiles(T):
    return min(ATTN_TILE, T), min(ATTN_WIDE, T)


def _attn_fwd(q, k, v, fq, fk, name):
    H, T, dk = q.shape
    dv = v.shape[2]
    tq, tk = _attn_tiles(T)
    decay = fq is not None
    tables = _q_major_tables(T, tq, tk)

    def body(qi_ref, ki_ref, first_ref, kind_ref, *refs):
        if decay:
            q_ref, k_ref, v_ref, fq_ref, fk_ref, o_ref, lse_ref, m_s, l_s, acc = refs
        else:
            q_ref, k_ref, v_ref, o_ref, lse_ref, m_s, l_s, acc = refs
        t = pl.program_id(1)
        qi, ki = qi_ref[t], ki_ref[t]

        @pl.when(first_ref[t] == 1)
        def _():
            m_s[...] = jnp.full_like(m_s, NEG_BIG)
            l_s[...] = jnp.zeros_like(l_s)
            acc[...] = jnp.zeros_like(acc)

        def step(kind):
            w = tk if kind == 0 else kind * tq
            s = _scores(q_ref[0], k_ref[0, :w, :], fq_ref[0] if decay else None,
                        fk_ref[0, :, :w] if decay else None, kind > 0, qi * tq, ki * tk)
            m_new = jnp.maximum(m_s[...], jnp.max(s, axis=-1, keepdims=True))
            p = jnp.exp(s - m_new)
            corr = jnp.exp(m_s[...] - m_new)
            l_s[...] = corr * l_s[...] + jnp.sum(p, axis=-1, keepdims=True)
            acc[...] = corr * acc[...] + jnp.dot(p.astype(BF16), v_ref[0, :w, :], preferred_element_type=F32)
            m_s[...] = m_new
            if kind > 0:
                o_ref[0] = (acc[...] / l_s[...]).astype(o_ref.dtype)
                lse_ref[0] = m_s[...] + jnp.log(l_s[...])

        for kind in range(tk // tq + 1):
            pl.when(kind_ref[t] == kind)(functools.partial(step, kind))

    qspec = lambda d: pl.BlockSpec((1, tq, d), lambda h, t, qi, ki, fi, la: (h, qi[t], 0))
    kspec = lambda d: pl.BlockSpec((1, tk, d), lambda h, t, qi, ki, fi, la: (h, ki[t], 0))
    in_specs = [qspec(dk), kspec(dk), kspec(dv)]
    ins = [q, k, v]
    if decay:
        in_specs += [qspec(1), pl.BlockSpec((1, 1, tk), lambda h, t, qi, ki, fi, la: (h, 0, ki[t]))]
        ins += [fq, fk]
    grid_spec = pltpu.PrefetchScalarGridSpec(
        num_scalar_prefetch=4, grid=(H, int(tables[0].shape[0])), in_specs=in_specs,
        out_specs=[qspec(dv), qspec(1)],
        scratch_shapes=[pltpu.VMEM((tq, 1), F32), pltpu.VMEM((tq, 1), F32), pltpu.VMEM((tq, dv), F32)])
    return pl.pallas_call(
        body, name=name, grid_spec=grid_spec,
        out_shape=[jax.ShapeDtypeStruct((H, T, dv), BF16), jax.ShapeDtypeStruct((H, T, 1), F32)],
        compiler_params=_cparams(("parallel", "arbitrary")),
    )(*tables, *ins)


def _attn_bwd_dq(q, k, v, o, do, lse, fq, fk, name):
    H, T, dk = q.shape
    dv = v.shape[2]
    tq, tk = _attn_tiles(T)
    decay = fq is not None
    tables = _q_major_tables(T, tq, tk)

    def body(qi_ref, ki_ref, first_ref, kind_ref, *refs):
        if decay:
            q_ref, k_ref, v_ref, o_ref, do_ref, lse_ref, fq_ref, fk_ref, dq_ref, dl_ref, acc, dl, leak = refs
        else:
            q_ref, k_ref, v_ref, o_ref, do_ref, lse_ref, dq_ref, dl_ref, acc, dl, leak = refs
        t = pl.program_id(1)
        qi, ki = qi_ref[t], ki_ref[t]

        @pl.when(first_ref[t] == 1)
        def _():
            acc[...] = jnp.zeros_like(acc)
            leak[...] = jnp.zeros_like(leak)
            dl[...] = jnp.sum(do_ref[0].astype(F32) * o_ref[0].astype(F32), axis=-1, keepdims=True)

        def step(kind):
            w = tk if kind == 0 else kind * tq
            k_v = k_ref[0, :w, :]
            s = _scores(q_ref[0], k_v, fq_ref[0] if decay else None, fk_ref[0, :, :w] if decay else None,
                        kind > 0, qi * tq, ki * tk)
            p = jnp.exp(s - lse_ref[0])
            dp = lax.dot_general(do_ref[0], v_ref[0, :w, :], (((1,), (1,)), ((), ())), preferred_element_type=F32)
            ds = p * (dp - dl[...])
            leak[...] += jnp.sum(ds, axis=-1, keepdims=True)
            acc[...] += jnp.dot(ds.astype(BF16), k_v, preferred_element_type=F32)
            if kind > 0:
                dq_ref[0] = acc[...].astype(dq_ref.dtype)
                dl_ref[0] = dl[...] + leak[...]

        for kind in range(tk // tq + 1):
            pl.when(kind_ref[t] == kind)(functools.partial(step, kind))

    qspec = lambda d: pl.BlockSpec((1, tq, d), lambda h, t, qi, ki, fi, la: (h, qi[t], 0))
    kspec = lambda d: pl.BlockSpec((1, tk, d), lambda h, t, qi, ki, fi, la: (h, ki[t], 0))
    in_specs = [qspec(dk), kspec(dk), kspec(dv), qspec(dv), qspec(dv), qspec(1)]
    ins = [q, k, v, o, do, lse]
    if decay:
        in_specs += [qspec(1), pl.BlockSpec((1, 1, tk), lambda h, t, qi, ki, fi, la: (h, 0, ki[t]))]
        ins += [fq, fk]
    grid_spec = pltpu.PrefetchScalarGridSpec(
        num_scalar_prefetch=4, grid=(H, int(tables[0].shape[0])), in_specs=in_specs,
        out_specs=[qspec(dk), qspec(1)],
        scratch_shapes=[pltpu.VMEM((tq, dk), F32), pltpu.VMEM((tq, 1), F32), pltpu.VMEM((tq, 1), F32)])
    return pl.pallas_call(
        body, name=name, grid_spec=grid_spec,
        out_shape=[jax.ShapeDtypeStruct((H, T, dk), BF16), jax.ShapeDtypeStruct((H, T, 1), F32)],
        compiler_params=_cparams(("parallel", "arbitrary")),
    )(*tables, *ins)


def _attn_bwd_dkv(q, k, v, delta, do, lse, fq, fk, name):
    H, T, dk = q.shape
    dv = v.shape[2]
    tk, tq = _attn_tiles(T)
    decay = fq is not None
    tables = _k_major_tables(T, tq, tk)

    def body(ki_ref, qi_ref, last_ref, kind_ref, *refs):
        if decay:
            q_ref, k_ref, v_ref, dl_ref, do_ref, lse_ref, fq_ref, fk_ref, dk_ref, dv_ref, df_ref, dk_s, dv_s, df_s = refs
        else:
            q_ref, k_ref, v_ref, dl_ref, do_ref, lse_ref, dk_ref, dv_ref, dk_s, dv_s = refs
        t = pl.program_id(1)
        ki, qi = ki_ref[t], qi_ref[t]

        @pl.when(kind_ref[t] > 0)
        def _():
            dk_s[...] = jnp.zeros_like(dk_s)
            dv_s[...] = jnp.zeros_like(dv_s)
            if decay:
                df_s[...] = jnp.zeros_like(df_s)

        def step(kind):
            off = 0 if kind == 0 else tq - kind * tk
            q_v, do_v = q_ref[0, off:, :], do_ref[0, off:, :]
            s = _scores(q_v, k_ref[0], fq_ref[0, off:, :] if decay else None, fk_ref[0] if decay else None,
                        kind > 0, qi * tq + off, ki * tk)
            p = jnp.exp(s - lse_ref[0, off:, :])
            dv_s[...] += lax.dot_general(p.astype(BF16), do_v, (((0,), (0,)), ((), ())), preferred_element_type=F32)
            dp = lax.dot_general(do_v, v_ref[0], (((1,), (1,)), ((), ())), preferred_element_type=F32)
            ds = p * (dp - dl_ref[0, off:, :])
            dk_s[...] += lax.dot_general(ds.astype(BF16), q_v, (((0,), (0,)), ((), ())),
                                         preferred_element_type=F32)
            if decay:
                df_s[...] -= jnp.sum(ds, axis=0, keepdims=True)

        for kind in range(tq // tk + 1):
            pl.when(kind_ref[t] == kind)(functools.partial(step, kind))

        @pl.when(last_ref[t] == 1)
        def _():
            dk_ref[0] = dk_s[...].astype(dk_ref.dtype)
            dv_ref[0] = dv_s[...].astype(dv_ref.dtype)
            if decay:
                df_ref[0] = df_s[...]

    qspec = lambda d: pl.BlockSpec((1, tq, d), lambda h, t, ki, qi, fi, la: (h, qi[t], 0))
    kspec = lambda d: pl.BlockSpec((1, tk, d), lambda h, t, ki, qi, fi, la: (h, ki[t], 0))
    in_specs = [qspec(dk), kspec(dk), kspec(dv), qspec(1), qspec(dv), qspec(1)]
    ins = [q, k, v, delta, do, lse]
    out_specs = [kspec(dk), kspec(dv)]
    out_shape = [jax.ShapeDtypeStruct((H, T, dk), BF16), jax.ShapeDtypeStruct((H, T, dv), BF16)]
    scratch = [pltpu.VMEM((tk, dk), F32), pltpu.VMEM((tk, dv), F32)]
    if decay:
        fkspec = pl.BlockSpec((1, 1, tk), lambda h, t, ki, qi, fi, la: (h, 0, ki[t]))
        in_specs += [qspec(1), fkspec]
        ins += [fq, fk]
        out_specs.append(fkspec)
        out_shape.append(jax.ShapeDtypeStruct((H, 1, T), F32))
        scratch.append(pltpu.VMEM((1, tk), F32))
    grid_spec = pltpu.PrefetchScalarGridSpec(
        num_scalar_prefetch=4, grid=(H, int(tables[0].shape[0])), in_specs=in_specs, out_specs=out_specs,
        scratch_shapes=scratch)
    return pl.pallas_call(
        body, name=name, grid_spec=grid_spec, out_shape=out_shape, compiler_params=_cparams(("parallel", "arbitrary")),
    )(*tables, *ins)


def _attn_bwd_fused(q, k, v, o, do, lse, name):
    H, T, dk = q.shape
    dv = v.shape[2]
    tk, tq = _attn_tiles(T)
    tables = _k_major_tables(T, tq, tk)
    npairs = int(tables[0].shape[0])

    def body(ki_ref, qi_ref, last_ref, kind_ref, q_ref, k_ref, v_ref, o_ref, do_ref, lse_ref,
             dq_ref, dk_ref, dv_ref, dq_s, dk_s, dv_s):
        t = pl.program_id(1)
        ki, qi = ki_ref[t], qi_ref[t]

        @pl.when(t == 0)
        def _():
            dq_s[...] = jnp.zeros_like(dq_s)

        @pl.when(kind_ref[t] > 0)
        def _():
            dk_s[...] = jnp.zeros_like(dk_s)
            dv_s[...] = jnp.zeros_like(dv_s)

        def step(kind):
            off = 0 if kind == 0 else tq - kind * tk
            q_v, do_v, k_v = q_ref[0, off:, :], do_ref[0, off:, :], k_ref[0]
            s = _scores(q_v, k_v, None, None, kind > 0, qi * tq + off, ki * tk)
            p = jnp.exp(s - lse_ref[0, off:, :])
            delta = jnp.sum(do_v.astype(F32) * o_ref[0, off:, :].astype(F32), axis=-1, keepdims=True)
            dv_s[...] += lax.dot_general(p.astype(BF16), do_v, (((0,), (0,)), ((), ())), preferred_element_type=F32)
            dp = lax.dot_general(do_v, v_ref[0], (((1,), (1,)), ((), ())), preferred_element_type=F32)
            ds = (p * (dp - delta)).astype(BF16)
            dk_s[...] += lax.dot_general(ds, q_v, (((0,), (0,)), ((), ())), preferred_element_type=F32)
            rows = pl.ds(pl.multiple_of(qi * tq + off, tk), tq - off)
            dq_s[rows, :] += jnp.dot(ds, k_v, preferred_element_type=F32)

        for kind in range(tq // tk + 1):
            pl.when(kind_ref[t] == kind)(functools.partial(step, kind))

        @pl.when(last_ref[t] == 1)
        def _():
            dk_ref[0] = dk_s[...].astype(dk_ref.dtype)
            dv_ref[0] = dv_s[...].astype(dv_ref.dtype)

        @pl.when(t == npairs - 1)
        def _():
            dq_ref[0] = dq_s[...].astype(dq_ref.dtype)

    qspec = lambda d: pl.BlockSpec((1, tq, d), lambda h, t, ki, qi, la, kd: (h, qi[t], 0))
    kspec = lambda d: pl.BlockSpec((1, tk, d), lambda h, t, ki, qi, la, kd: (h, ki[t], 0))
    grid_spec = pltpu.PrefetchScalarGridSpec(
        num_scalar_prefetch=4, grid=(H, npairs),
        in_specs=[qspec(dk), kspec(dk), kspec(dv), qspec(dv), qspec(dv), qspec(1)],
        out_specs=[pl.BlockSpec((1, T, dk), lambda h, t, ki, qi, la, kd: (h, 0, 0)), kspec(dk), kspec(dv)],
        scratch_shapes=[pltpu.VMEM((T, dk), F32), pltpu.VMEM((tk, dk), F32), pltpu.VMEM((tk, dv), F32)])
    return pl.pallas_call(
        body, name=name, grid_spec=grid_spec,
        out_shape=[jax.ShapeDtypeStruct((H, T, dk), BF16), jax.ShapeDtypeStruct((H, T, dk), BF16),
                   jax.ShapeDtypeStruct((H, T, dv), BF16)],
        compiler_params=_cparams(("parallel", "arbitrary")),
    )(*tables, q, k, v, o, do, lse)


def _loss_head(y, target, name):
    rows, d = y.shape
    tt = _pick(rows, (512, 256, 128))

    def body(y_ref, t_ref, dy_ref, part_ref):
        @pl.when(pl.program_id(0) == 0)
        def _():
            part_ref[...] = jnp.zeros_like(part_ref)

        err = y_ref[...] - t_ref[...]
        dy_ref[...] = err * (1.0 / d)
        sq = jnp.sum(err * err, axis=0, keepdims=True)
        folded = sq[:, :LANES]
        for j in range(1, d // LANES):
            folded = folded + sq[:, j * LANES:(j + 1) * LANES]
        part_ref[...] += folded

    spec = pl.BlockSpec((tt, d), lambda i: (i, 0))
    return pl.pallas_call(
        body, name=name, grid=(rows // tt,), in_specs=[spec, spec],
        out_specs=[spec, pl.BlockSpec((1, LANES), lambda i: (0, 0))],
        out_shape=[jax.ShapeDtypeStruct((rows, d), F32), jax.ShapeDtypeStruct((1, LANES), F32)],
        compiler_params=_cparams(("arbitrary",)),
    )(y, target)


def _adamw(w, g, m, v, name):
    shape = w.shape
    if len(shape) != 3:
        view = (1, shape[0], shape[1]) if len(shape) == 2 else (shape[0], math.prod(shape[1:-1]), shape[-1])
        d, nm, nv = _adamw(*(a.reshape(view) for a in (w, g, m, v)), name)
        return d.reshape(shape), nm.reshape(shape), nv.reshape(shape)
    layers, rows, cols = shape
    tr = _pick(rows, (512, 256, 128, 64, 32, 16, 8))
    c1 = 1.0 / (1.0 - ADAM_B1 ** ADAM_STEP)
    c2 = 1.0 / (1.0 - ADAM_B2 ** ADAM_STEP)

    def body(w_ref, g_ref, m_ref, v_ref, d_ref, nm_ref, nv_ref):
        gv = g_ref[...]
        nm = ADAM_B1 * m_ref[...] + (1.0 - ADAM_B1) * gv
        nv = ADAM_B2 * v_ref[...] + (1.0 - ADAM_B2) * gv * gv
        d_ref[...] = -ADAM_LR * ((nm * c1) / (jnp.sqrt(nv * c2) + ADAM_EPS) + ADAM_WD * w_ref[...])
        nm_ref[...] = nm
        nv_ref[...] = nv

    spec = pl.BlockSpec((1, tr, cols), lambda l, i: (l, i, 0))
    sds = jax.ShapeDtypeStruct(shape, F32)
    return pl.pallas_call(
        body, name=name, grid=(layers, rows // tr), in_specs=[spec] * 4, out_specs=[spec] * 3, out_shape=[sds] * 3,
        compiler_params=_cparams(("parallel", "parallel")),
    )(w, g, m, v)


_ANY = pl.BlockSpec(memory_space=pl.ANY)


def _my_xyc():
    return lax.axis_index("x"), lax.axis_index("y"), lax.axis_index("c")


def _chip_allgather_core_half(shards, name):
    n = len(shards)
    half = [s.shape[0] // 2 for s in shards]

    def body(*refs):
        full_refs, whole_refs = refs[:n], refs[n:2 * n]
        send_sems, recv_sems, loc_sems = refs[2 * n:]
        x, y, c = _my_xyc()
        myq = 2 * x + y
        x_refs = [full_refs[i].at[pl.ds(c * half[i], half[i])] for i in range(n)]
        out_refs = [whole_refs[i].at[c] for i in range(n)]
        chips = [(1 - x, y), (x, 1 - y), (1 - x, 1 - y)]
        started = []
        for i in range(n):
            local = pltpu.make_async_copy(x_refs[i], out_refs[i].at[myq], loc_sems.at[i])
            local.start()
            started.append(local)
        sends = []
        for j, (px, py) in enumerate(chips):
            for i in range(n):
                cp = pltpu.make_async_remote_copy(src_ref=x_refs[i], dst_ref=out_refs[i].at[myq],
                                                  send_sem=send_sems.at[3 * i + j], recv_sem=recv_sems.at[3 * i + j],
                                                  device_id=(px, py, c), device_id_type=MESH_ID)
                cp.start()
                sends.append(cp)
        for j, (px, py) in enumerate(chips):
            for i in range(n):
                pltpu.make_async_remote_copy(src_ref=x_refs[i], dst_ref=out_refs[i].at[2 * px + py],
                                             send_sem=send_sems.at[3 * i + j], recv_sem=recv_sems.at[3 * i + j],
                                             device_id=(px, py, c), device_id_type=MESH_ID).wait_recv()
        for cp in sends:
            cp.wait_send()
        for local in started:
            local.wait()

    return pl.pallas_call(
        body, name=name, in_specs=[_ANY] * n, out_specs=[_ANY] * n,
        out_shape=[jax.ShapeDtypeStruct((2, 4, h) + s.shape[1:], s.dtype) for s, h in zip(shards, half)],
        scratch_shapes=[pltpu.SemaphoreType.DMA((3 * n,)), pltpu.SemaphoreType.DMA((3 * n,)),
                        pltpu.SemaphoreType.DMA((n,))],
    )(*shards)


def _sibling_swap_halves(g, name):
    q, rows, cols = g.shape
    rh = rows // 2

    def body(g_ref, out_ref, send_sem, recv_sem):
        x, y, c = _my_xyc()
        src = g_ref.at[:, pl.ds((1 - c) * rh, rh), :]
        cp = pltpu.make_async_remote_copy(src_ref=src, dst_ref=out_ref, send_sem=send_sem, recv_sem=recv_sem,
                                          device_id=(x, y, 1 - c), device_id_type=MESH_ID)
        cp.start()
        cp.wait()

    return pl.pallas_call(
        body, name=name, in_specs=[_ANY], out_specs=_ANY,
        out_shape=jax.ShapeDtypeStruct((q, rh, cols), g.dtype),
        scratch_shapes=[pltpu.SemaphoreType.DMA, pltpu.SemaphoreType.DMA],
    )(g)


def _add_own_half(g, other, name):
    q, rows, cols = g.shape
    rh = rows // 2
    tr = _pick(rh, (512, 256, 128, 64, 32, 16, 8))
    nb = rh // tr
    cidx = lax.axis_index("c").astype(jnp.int32).reshape(1)

    def body(c_ref, g_ref, o_ref, out_ref):
        out_ref[...] = (g_ref[...] + o_ref[...]).astype(out_ref.dtype)

    grid_spec = pltpu.PrefetchScalarGridSpec(
        num_scalar_prefetch=1, grid=(q, nb),
        in_specs=[pl.BlockSpec((1, tr, cols), lambda a, i, c_ref: (a, c_ref[0] * nb + i, 0)),
                  pl.BlockSpec((1, tr, cols), lambda a, i, c_ref: (a, i, 0))],
        out_specs=pl.BlockSpec((1, tr, cols), lambda a, i, c_ref: (a, i, 0)))
    return pl.pallas_call(
        body, name=name, grid_spec=grid_spec, out_shape=jax.ShapeDtypeStruct((q, rh, cols), BF16),
        compiler_params=_cparams(("parallel", "parallel")),
    )(cidx, g, other)


def _chip_exchange(s, name):
    def body(s_ref, out_ref, send_sems, recv_sems, loc_sem):
        x, y, c = _my_xyc()
        myq = 2 * x + y
        local = pltpu.make_async_copy(s_ref.at[myq], out_ref.at[myq], loc_sem)
        local.start()
        chips = [(1 - x, y), (x, 1 - y), (1 - x, 1 - y)]
        sends = []
        for j, (px, py) in enumerate(chips):
            cp = pltpu.make_async_remote_copy(src_ref=s_ref.at[2 * px + py], dst_ref=out_ref.at[myq],
                                              send_sem=send_sems.at[j], recv_sem=recv_sems.at[j],
                                              device_id=(px, py, c), device_id_type=MESH_ID)
            cp.start()
            sends.append(cp)
        for j, (px, py) in enumerate(chips):
            pltpu.make_async_remote_copy(src_ref=s_ref.at[myq], dst_ref=out_ref.at[2 * px + py],
                                         send_sem=send_sems.at[j], recv_sem=recv_sems.at[j],
                                         device_id=(px, py, c), device_id_type=MESH_ID).wait_recv()
        for cp in sends:
            cp.wait_send()
        local.wait()

    return pl.pallas_call(
        body, name=name, in_specs=[_ANY], out_specs=_ANY, out_shape=jax.ShapeDtypeStruct(s.shape, s.dtype),
        scratch_shapes=[pltpu.SemaphoreType.DMA((3,)), pltpu.SemaphoreType.DMA((3,)), pltpu.SemaphoreType.DMA],
    )(s)


def _sum_leading_into_core_slot(a, name):
    q, rows, cols = a.shape
    tr = _pick(rows, (512, 256, 128, 64, 32, 16, 8))
    cidx = lax.axis_index("c").astype(jnp.int32).reshape(1)

    def body(c_ref, a_ref, out_ref):
        acc = a_ref[0].astype(F32)
        for j in range(1, q):
            acc = acc + a_ref[j].astype(F32)
        out_ref[0] = acc

    grid_spec = pltpu.PrefetchScalarGridSpec(
        num_scalar_prefetch=1, grid=(rows // tr,),
        in_specs=[pl.BlockSpec((q, tr, cols), lambda i, c_ref: (0, i, 0))],
        out_specs=pl.BlockSpec((1, tr, cols), lambda i, c_ref: (c_ref[0], i, 0)))
    return pl.pallas_call(
        body, name=name, grid_spec=grid_spec, out_shape=jax.ShapeDtypeStruct((2, rows, cols), F32),
        compiler_params=_cparams(("parallel",)),
    )(cidx, a)


def _sibling_fill(arrays, name):
    n = len(arrays)

    def body(*refs):
        buf_refs = refs[n:2 * n]
        send_sems, recv_sems = refs[2 * n:]
        x, y, c = _my_xyc()
        sends = []
        for i in range(n):
            cp = pltpu.make_async_remote_copy(src_ref=buf_refs[i].at[c], dst_ref=buf_refs[i].at[c],
                                              send_sem=send_sems.at[i], recv_sem=recv_sems.at[i],
                                              device_id=(x, y, 1 - c), device_id_type=MESH_ID)
            cp.start()
            sends.append(cp)
        for i in range(n):
            pltpu.make_async_remote_copy(src_ref=buf_refs[i].at[c], dst_ref=buf_refs[i].at[1 - c],
                                         send_sem=send_sems.at[i], recv_sem=recv_sems.at[i],
                                         device_id=(x, y, 1 - c), device_id_type=MESH_ID).wait_recv()
        for cp in sends:
            cp.wait_send()

    return pl.pallas_call(
        body, name=name, in_specs=[_ANY] * n, out_specs=[_ANY] * n,
        out_shape=[jax.ShapeDtypeStruct(a.shape, a.dtype) for a in arrays],
        input_output_aliases={i: i for i in range(n)},
        scratch_shapes=[pltpu.SemaphoreType.DMA((n,)), pltpu.SemaphoreType.DMA((n,))],
    )(*arrays)


def _all_reduce_small(vec, name):
    rows, cols = vec.shape

    def body(v_ref, out_ref, buf, send_sems, recv_sems):
        x, y, c = _my_xyc()
        me = 4 * x + 2 * y + c
        buf[me] = v_ref[...]
        sends = []
        for kk in range(1, 8):
            peer = (1 - x if kk & 4 else x, 1 - y if kk & 2 else y, 1 - c if kk & 1 else c)
            cp = pltpu.make_async_remote_copy(src_ref=v_ref, dst_ref=buf.at[me], send_sem=send_sems.at[kk - 1],
                                              recv_sem=recv_sems.at[kk - 1], device_id=peer, device_id_type=MESH_ID)
            cp.start()
            sends.append(cp)
        for kk in range(1, 8):
            px, py, pc = (1 - x if kk & 4 else x, 1 - y if kk & 2 else y, 1 - c if kk & 1 else c)
            pltpu.make_async_remote_copy(src_ref=v_ref, dst_ref=buf.at[4 * px + 2 * py + pc],
                                         send_sem=send_sems.at[kk - 1], recv_sem=recv_sems.at[kk - 1],
                                         device_id=(px, py, pc), device_id_type=MESH_ID).wait_recv()
        for cp in sends:
            cp.wait_send()
        acc = buf[0]
        for j in range(1, 8):
            acc = acc + buf[j]
        out_ref[...] = acc

    vm = pl.BlockSpec(memory_space=pltpu.VMEM)
    return pl.pallas_call(
        body, name=name, in_specs=[vm], out_specs=vm, out_shape=jax.ShapeDtypeStruct((rows, cols), F32),
        scratch_shapes=[pltpu.VMEM((8, rows, cols), F32), pltpu.SemaphoreType.DMA((7,)), pltpu.SemaphoreType.DMA((7,))],
        compiler_params=pltpu.CompilerParams(vmem_limit_bytes=VMEM_LIMIT_BYTES),
    )(vec)


def _reduce_scatter(g):
    other = _sibling_swap_halves(g, "rs_swap_halves")
    s = _add_own_half(g, other, "rs_add_halves")
    recv = _chip_exchange(s, "rs_chip_exchange")
    r = _sum_leading_into_core_slot(recv, "rs_sum_chips")
    (full,) = _sibling_fill([r], "rs_join_halves")
    return full.reshape(g.shape[1], g.shape[2])


PACK_COLS = 1024
PACK_ROW_MULTIPLE = 1024
SHARDED = (("even_w_in", 2), ("conv_w", 2), ("even_w_out", 1), ("odd_w_in", 2), ("q_norm_w", 1), ("w_uq", 2),
           ("kv_norm_w", 1), ("w_ukv", 2), ("odd_w_out", 1), ("ffn_w_gate", 2), ("ffn_w_up", 2), ("ffn_w_down", 1))
KEEP_F32 = ("conv_w", "q_norm_w", "kv_norm_w")
REPLICATED = ("pool_w", "pool_scale", "conv_b", "dt_bias", "a_log", "d_skip", "ssm_norm_w", "fgate_b",
              "ln_mix_g", "ln_mix_b", "ln_ffn_g", "ln_ffn_b")


def _gather_weights(shards):
    sent = [shards[name] if name in KEEP_F32 else shards[name].astype(BF16) for name, _ in SHARDED]
    gathered = _sibling_fill(_chip_allgather_core_half(sent, "weights_allgather"), "weights_sibling_fill")
    out = {}
    for (name, axis), halves in zip(SHARDED, gathered):
        shape = list(shards[name].shape)
        shape[axis] *= 4
        out[name] = jnp.moveaxis(halves, 1, axis + 1).reshape(shape)
    return out


def _pack_grads(grads, shards):
    pieces = []
    for name, axis in SHARDED:
        width = shards[name].shape[axis]
        for g in grads[name]:
            ax = axis - 1
            split = g.reshape(g.shape[:ax] + (4, width) + g.shape[ax + 1:])
            pieces.append(jnp.moveaxis(split, ax, 0).reshape(4, -1))
    n = sum(p.shape[1] for p in pieces)
    per = PACK_COLS * PACK_ROW_MULTIPLE
    total = -(-n // per) * per
    pieces.append(jnp.zeros((4, total - n), F32))
    return jnp.concatenate(pieces, axis=1).reshape(4, total // PACK_COLS, PACK_COLS)


def _unpack_reduced(reduced, shards):
    flat = reduced.reshape(-1)
    out, off = {}, 0
    for name, _ in SHARDED:
        shp = shards[name].shape
        size = math.prod(shp)
        out[name] = flat[off:off + size].reshape(shp)
        off += size
    return out


def _to_heads(t, nh):
    rows, width = t.shape
    return t.reshape(rows, nh, width // nh).transpose(1, 0, 2)


def _from_heads(t):
    nh, rows, d = t.shape
    return t.transpose(1, 0, 2).reshape(rows, nh * d)


def _rope_tables(rows):
    half = MLA_ROPE // 2
    freqs = jnp.power(ROPE_THETA, -jnp.arange(half, dtype=F32) / half)
    ang = jnp.arange(rows, dtype=F32)[:, None] * freqs[None, :]
    cos, sin = jnp.cos(ang), jnp.sin(ang)
    cos_q, sin_q = jnp.tile(cos, (1, MLA_HEADS)), jnp.tile(sin, (1, MLA_HEADS))
    zeros = jnp.zeros((rows, LANES - MLA_ROPE), F32)
    cos_k = jnp.concatenate([cos, cos, zeros], axis=1)
    sin_k = jnp.concatenate([sin, sin, zeros], axis=1)
    r = lax.broadcasted_iota(jnp.int32, (LANES, LANES), 0)
    c = lax.broadcasted_iota(jnp.int32, (LANES, LANES), 1)
    rot = (jnp.where((r == c + half) & (c < half), -1.0, 0.0)
           + jnp.where((c == r + half) & (r < half), 1.0, 0.0)).astype(F32)
    return cos_q, sin_q, cos_k, sin_k, rot


def _pad_cols(a, width):
    return jnp.pad(a, ((0, 0), (0, width - a.shape[1])))


_EVEN_CUT = (512, 1536, 3072, 3088)


def _prep_even_w_in(w):
    c0, c1, c2, c3 = _EVEN_CUT
    return jnp.concatenate([w[:, c1:c2], w[:, :c0], w[:, c0:c1], _pad_cols(w[:, c2:c3], LANES)], axis=1)


def _unprep_even_w_in(g):
    c0, c1, c2, c3 = _EVEN_CUT
    n_xbc, n_u = c2 - c1, c0
    return jnp.concatenate([g[:, n_xbc:n_xbc + n_u], g[:, n_xbc + n_u:c2], g[:, :n_xbc], g[:, c2:c3]], axis=1)


_ODD_CUT = (1536, 1544, 2312, 2344)


def _prep_odd_w_in(w):
    c0, c1, c2, c3 = _ODD_CUT
    return jnp.concatenate([w[:, :c0], w[:, c1:c2], _pad_cols(w[:, c2:c3], LANES), _pad_cols(w[:, c0:c1], LANES)],
                           axis=1)


def _unprep_odd_w_in(g):
    c0, c1, c2, c3 = _ODD_CUT
    n1 = c0 + (c2 - c1)
    return jnp.concatenate([g[:, :c0], g[:, n1 + LANES:n1 + LANES + (c1 - c0)], g[:, c0:n1],
                            g[:, n1:n1 + (c3 - c2)]], axis=1)


def _prep_w_uq(w):
    r = w.reshape(w.shape[0], MLA_HEADS, MLA_NOPE + MLA_ROPE)
    half = MLA_ROPE // 2
    return jnp.concatenate([r[:, :, :MLA_NOPE].reshape(w.shape[0], -1),
                            r[:, :, MLA_NOPE:MLA_NOPE + half].reshape(w.shape[0], -1),
                            r[:, :, MLA_NOPE + half:].reshape(w.shape[0], -1)], axis=1)


def _unprep_w_uq(g):
    rows = g.shape[0]
    half = MLA_ROPE // 2
    nope = MLA_HEADS * MLA_NOPE
    return jnp.concatenate([g[:, :nope].reshape(rows, MLA_HEADS, MLA_NOPE),
                            g[:, nope:nope + LANES].reshape(rows, MLA_HEADS, half),
                            g[:, nope + LANES:].reshape(rows, MLA_HEADS, half)], axis=2).reshape(rows, -1)


def _row(v, width=None):
    v = v.reshape(1, -1)
    return v if width is None else _pad_cols(v, width)


def _even_forward(x_bf, w, i):
    rows = x_bf.shape[0]
    proj = _mm(x_bf, w["even_w_in"][i], "nn", F32, "even_proj")
    xbc, u, z, dtr = _Cols(proj, 0, 1536), _Cols(proj, 3, 512), _Cols(proj, 2, 1024), _Cols(proj, 24, LANES)
    n_row = rows // ROW_TILE
    pool_p = (w["pool_w"][i], _row(w["pool_scale"][i]))
    (y_pool,), pool_c = _scan_fwd(_f_pool, "pool_fwd", n_row, [u], [], pool_p, [], [(16, POOL_WIDTH)],
                                  [(POOL_WIDTH, BF16)])
    conv_p = (w["conv_w"][i], _row(w["conv_b"][i]))
    (xa,), conv_c = _scan_fwd(_f_conv, "conv_fwd", n_row, [xbc], [], conv_p, [], [(8, SSM_CONV_DIM)],
                              [(SSM_CONV_DIM, F32)])
    ssd_p = (_row(w["dt_bias"][i], LANES), _row(w["a_log"][i], LANES), _row(w["d_skip"][i], LANES),
             _row(w["ssm_norm_w"][i]))
    (y_ssm,), ssd_c = _scan_fwd(_f_ssd, "ssd_fwd", rows // SSM_CHUNK, [xa, dtr, z], [], ssd_p, [],
                                [(SSM_STATE, SSM_D_INNER)], [(SSM_D_INNER, BF16)])
    mix = jnp.concatenate([y_pool, y_ssm], axis=1)
    saved = dict(u=u, z=z, xbc=xbc, dtr=dtr, xa=xa, mix=mix, pool_p=pool_p, pool_c=pool_c, conv_p=conv_p,
                 conv_c=conv_c, ssd_p=ssd_p, ssd_c=ssd_c)
    return mix, saved


def _even_backward(dmix, x_bf, sv, w, i, d_r1):
    rows = x_bf.shape[0]
    n_row = rows // ROW_TILE
    dy_pool, dy_ssm = dmix[:, :POOL_WIDTH], dmix[:, POOL_WIDTH:]
    (du,), (g_pool_w, g_pool_scale) = _scan_bwd(_f_pool, "pool_bwd", n_row, [sv["u"]], [], sv["pool_p"], [],
                                                sv["pool_c"], [dy_pool])
    (dxa, ddtr, dz), (g_dt_bias, g_a_log, g_d_skip, g_norm_w) = _scan_bwd(
        _f_ssd, "ssd_bwd", rows // SSM_CHUNK, [sv["xa"], sv["dtr"], sv["z"]], [], sv["ssd_p"], [], sv["ssd_c"],
        [dy_ssm])
    (dxbc,), (g_conv_w, g_conv_b) = _scan_bwd(_f_conv, "conv_bwd", n_row, [sv["xbc"]], [], sv["conv_p"], [],
                                              sv["conv_c"], [dxa])
    dproj = jnp.concatenate([dxbc, du, dz, ddtr], axis=1).astype(BF16)
    g_w_in = _unprep_even_w_in(_mm(x_bf, dproj, "tn", F32, "even_dw_in"))
    dx = _mm(dproj, w["even_w_in"][i], "nt", F32, "even_dx", extra=d_r1, alpha=ALPHA)
    grads = dict(even_w_in=g_w_in, pool_w=g_pool_w, pool_scale=g_pool_scale[0], conv_w=g_conv_w, conv_b=g_conv_b[0],
                 dt_bias=g_dt_bias[0, :SSM_HEADS], a_log=g_a_log[0, :SSM_HEADS], d_skip=g_d_skip[0, :SSM_HEADS],
                 ssm_norm_w=g_norm_w[0])
    return dx, grads


def _odd_forward(x_bf, w, i, tables):
    rows = x_bf.shape[0]
    cos_q, sin_q, cos_k, sin_k, rot = tables
    proj = _mm(x_bf, w["odd_w_in"][i], "nn", F32, "odd_proj")
    qf, kf, vf = (_to_heads((proj[:, j * 512:(j + 1) * 512] * sc).astype(BF16), FOX_HEADS)
                  for j, sc in enumerate((FOX_SCALE, 1.0, 1.0)))
    cq, ckv = _Cols(proj, 3, MLA_Q_RANK), _Cols(proj, 8, MLA_KV_RANK)
    kr, fl = _Cols(proj, 18, LANES), _Cols(proj, 19, LANES)
    n_row = rows // ROW_TILE
    fox_p = (_row(w["fgate_b"][i], LANES),)
    n_gate = rows // min(ATTN_TILE, rows)
    (fcum,), fox_c = _scan_fwd(_f_fox_gate, "fox_gate_fwd", n_gate, [fl], [], fox_p, [], [(8, LANES)], [(LANES, F32)])
    fc_heads = fcum[:, :FOX_HEADS].T
    fq, fk = fc_heads[:, :, None], fc_heads[:, None, :]
    o_fox, lse_fox = _attn_fwd(qf, kf, vf, fq, fk, "fox_attn_fwd")

    prep_p = (_row(w["q_norm_w"][i]), _row(w["kv_norm_w"][i]))
    (cqn, ckvn, krr), _ = _scan_fwd(_f_mla_prep, "mla_prep_fwd", n_row, [cq, ckv, kr], [cos_k, sin_k], prep_p,
                                    [rot], [], [(MLA_Q_RANK, BF16), (MLA_KV_RANK, BF16), (LANES, BF16)])
    q_flat = _mm(cqn, w["w_uq"][i], "nn", F32, "mla_q_up")
    (q_rope,), _ = _scan_fwd(_f_rope_q, "rope_q_fwd", n_row, [q_flat], [cos_q, sin_q], [], [], [],
                             [(q_flat.shape[1], BF16)])
    kv = _mm(ckvn, w["w_ukv"][i], "nn", BF16, "mla_kv_up")
    nope = MLA_HEADS * MLA_NOPE
    half = MLA_ROPE // 2
    q_m = jnp.concatenate([_to_heads(q_rope[:, :nope], MLA_HEADS), _to_heads(q_rope[:, nope:nope + LANES], MLA_HEADS),
                           _to_heads(q_rope[:, nope + LANES:], MLA_HEADS)], axis=2)
    kv_h = _to_heads(kv, MLA_HEADS)
    k_rope = jnp.broadcast_to(krr[None, :, :MLA_ROPE], (MLA_HEADS, rows, MLA_ROPE))
    k_m = jnp.concatenate([kv_h[:, :, :MLA_NOPE], k_rope], axis=2)
    v_m = kv_h[:, :, MLA_NOPE:]
    o_mla, lse_mla = _attn_fwd(q_m, k_m, v_m, None, None, "mla_attn_fwd")
    mix = jnp.concatenate([_from_heads(o_fox), _from_heads(o_mla)], axis=1)
    saved = dict(qf=qf, kf=kf, vf=vf, fq=fq, fk=fk, o_fox=o_fox, lse_fox=lse_fox, fl=fl, fox_p=fox_p, fox_c=fox_c,
                 cq=cq, ckv=ckv, kr=kr, prep_p=prep_p, cqn=cqn, ckvn=ckvn, q_flat=q_flat, q_m=q_m, k_m=k_m, v_m=v_m,
                 o_mla=o_mla, lse_mla=lse_mla, mix=mix)
    return mix, saved


def _odd_backward(dmix, x_bf, sv, w, i, d_r1, tables):
    rows = x_bf.shape[0]
    cos_q, sin_q, cos_k, sin_k, rot = tables
    n_row = rows // ROW_TILE
    nope = MLA_HEADS * MLA_NOPE
    half = MLA_ROPE // 2
    do_fox = _to_heads(dmix[:, :FOX_WIDTH], FOX_HEADS)
    do_mla = _to_heads(dmix[:, FOX_WIDTH:], MLA_HEADS)
    fox_args = (sv["qf"], sv["kf"], sv["vf"], sv["o_fox"], do_fox, sv["lse_fox"], sv["fq"], sv["fk"])
    dqf, delta_fox = _attn_bwd_dq(*fox_args, "fox_attn_dq")
    dkf, dvf, dfk = _attn_bwd_dkv(*fox_args[:3], delta_fox, *fox_args[4:], "fox_attn_dkv")
    dfcum = _pad_cols(dfk[:, 0, :].T, LANES)
    n_gate = rows // min(ATTN_TILE, rows)
    (dfl,), (g_fb,) = _scan_bwd(_f_fox_gate, "fox_gate_bwd", n_gate, [sv["fl"]], [], sv["fox_p"], [], sv["fox_c"],
                                [dfcum])
    mla_args = (sv["q_m"], sv["k_m"], sv["v_m"], sv["o_mla"], do_mla, sv["lse_mla"], None, None)
    dq_m, dk_m, dv_m = _attn_bwd_fused(*mla_args[:6], "mla_attn_bwd")
    dq_rope = jnp.concatenate([_from_heads(dq_m[:, :, :MLA_NOPE]), _from_heads(dq_m[:, :, MLA_NOPE:MLA_NOPE + half]),
                               _from_heads(dq_m[:, :, MLA_NOPE + half:])], axis=1)
    (dq_flat,), _ = _scan_bwd(_f_rope_q, "rope_q_bwd", n_row, [sv["q_flat"]], [cos_q, sin_q], [], [], [], [dq_rope])
    g_w_uq = _mm(sv["cqn"], dq_flat, "tn", F32, "mla_dw_uq")
    dcqn = _mm(dq_flat, w["w_uq"][i], "nt", BF16, "mla_dcqn")
    dkv = _from_heads(jnp.concatenate([dk_m[:, :, :MLA_NOPE], dv_m], axis=2))
    g_w_ukv = _mm(sv["ckvn"], dkv, "tn", F32, "mla_dw_ukv")
    dckvn = _mm(dkv, w["w_ukv"][i], "nt", BF16, "mla_dckvn")
    dkrr = _head_sum(dk_m, "mla_dk_rope_sum")
    (dcq, dckv, dkr), (g_qw, g_kvw) = _scan_bwd(_f_mla_prep, "mla_prep_bwd", n_row, [sv["cq"], sv["ckv"], sv["kr"]],
                                                [cos_k, sin_k], sv["prep_p"], [rot], [], [dcqn, dckvn, dkrr])
    dproj = jnp.concatenate([_from_heads(dqf).astype(F32) * FOX_SCALE, _from_heads(dkf).astype(F32), _from_heads(dvf).astype(F32),
                             dcq, dckv, dkr, dfl], axis=1).astype(BF16)
    g_w_in = _mm(x_bf, dproj, "tn", F32, "odd_dw_in")
    dx = _mm(dproj, w["odd_w_in"][i], "nt", F32, "odd_dx", extra=d_r1, alpha=ALPHA)
    grads = dict(odd_w_in=_unprep_odd_w_in(g_w_in), fgate_b=g_fb[0, :FOX_HEADS], q_norm_w=g_qw[0], kv_norm_w=g_kvw[0],
                 w_uq=_unprep_w_uq(g_w_uq), w_ukv=g_w_ukv)
    return dx, grads


def _head_sum(dk_m, name):
    H, T, dk = dk_m.shape
    tt = _pick(T, (512, 256, 128))

    def body(d_ref, o_ref):
        acc = d_ref[0].astype(F32)
        for h in range(1, H):
            acc = acc + d_ref[h].astype(F32)
        o_ref[...] = jnp.concatenate([acc[:, MLA_NOPE:], jnp.zeros((tt, LANES - MLA_ROPE), F32)], axis=1).astype(BF16)

    return pl.pallas_call(
        body, name=name, grid=(T // tt,), in_specs=[pl.BlockSpec((H, tt, dk), lambda i: (0, i, 0))],
        out_specs=pl.BlockSpec((tt, LANES), lambda i: (i, 0)), out_shape=jax.ShapeDtypeStruct((T, LANES), BF16),
        compiler_params=_cparams(("parallel",)),
    )(dk_m)


def _local_step(x, target, w, small):
    rows = x.shape[0]
    n_row = rows // ROW_TILE
    tables = _rope_tables(rows)
    saved = []
    x_f32 = x
    x_bf = x.astype(BF16)
    for l in range(DEPTH):
        i = l // 2
        if l % 2 == 0:
            mix, sv = _even_forward(x_bf, w, i)
            w_out = w["even_w_out"][i]
        else:
            mix, sv = _odd_forward(x_bf, w, i, tables)
            w_out = w["odd_w_out"][i]
        r1 = _mm(mix, w_out, "nn", F32, "mix_out_even" if l % 2 == 0 else "mix_out_odd", extra=x_f32, alpha=ALPHA)
        ln1_p = (_row(small["ln_mix_g"][l]), _row(small["ln_mix_b"][l]))
        ln_outs = [(D_MODEL, F32), (D_MODEL, BF16)]
        (x_mid, x_mid_bf), _ = _scan_fwd(_f_ln_with_bf16, "ln_fwd", n_row, [r1], [], ln1_p, [], [], ln_outs)
        gu = _mm(x_mid_bf, w["ffn_w_gu"][l], "nn", BF16, "ffn_gu")
        (act,), _ = _scan_fwd(_f_act, "ffn_act_fwd", n_row, [gu], [], [], [], [], [(D_FF, BF16)])
        r2 = _mm(act, w["ffn_w_down"][l], "nn", F32, "ffn_down", extra=x_mid, alpha=ALPHA)
        ln2_p = (_row(small["ln_ffn_g"][l]), _row(small["ln_ffn_b"][l]))
        (x_out, x_out_bf), _ = _scan_fwd(_f_ln_with_bf16, "ln_fwd", n_row, [r2], [], ln2_p, [], [], ln_outs)
        saved.append(dict(sv=sv, x_bf=x_bf, r1=r1, ln1_p=ln1_p, x_mid_bf=x_mid_bf, gu=gu, act=act, r2=r2, ln2_p=ln2_p,
                          w_out=w_out))
        x_f32, x_bf = x_out, x_out_bf

    dy, loss_part = _loss_head(x_f32, target, "loss_head")
    loss = 0.5 * jnp.sum(loss_part) / D_MODEL

    layer_grads = []
    for l in reversed(range(DEPTH)):
        i = l // 2
        s = saved[l]
        (d_r2,), (g_ln2_g, g_ln2_b) = _scan_bwd(_f_ln, "ln_bwd", n_row, [s["r2"]], [], s["ln2_p"], [], [], [dy])
        g_down = _mm(s["act"], d_r2, "tn", F32, "ffn_dw_down")
        dact = _mm(d_r2, w["ffn_w_down"][l], "nt", BF16, "ffn_dact")
        (dgu,), _ = _scan_bwd(_f_act, "ffn_act_bwd", n_row, [s["gu"]], [], [], [], [], [dact])
        g_gu = _mm(s["x_mid_bf"], dgu, "tn", F32, "ffn_dw_gu")
        dx_mid = _mm(dgu, w["ffn_w_gu"][l], "nt", F32, "ffn_dx", extra=d_r2, alpha=ALPHA)
        (d_r1,), (g_ln1_g, g_ln1_b) = _scan_bwd(_f_ln, "ln_bwd", n_row, [s["r1"]], [], s["ln1_p"], [], [], [dx_mid])
        g_w_out = _mm(s["sv"]["mix"], d_r1, "tn", F32, "even_dw_out" if l % 2 == 0 else "odd_dw_out")
        dmix = _mm(d_r1, s["w_out"], "nt", BF16, "even_dmix" if l % 2 == 0 else "odd_dmix")
        if l % 2 == 0:
            dy, g = _even_backward(dmix, s["x_bf"], s["sv"], w, i, d_r1)
            g["even_w_out"] = g_w_out
        else:
            dy, g = _odd_backward(dmix, s["x_bf"], s["sv"], w, i, d_r1, tables)
            g["odd_w_out"] = g_w_out
        g.update(ffn_w_gate=g_gu[:, :D_FF], ffn_w_up=g_gu[:, D_FF:], ffn_w_down=g_down, ln_mix_g=g_ln1_g[0],
                 ln_mix_b=g_ln1_b[0], ln_ffn_g=g_ln2_g[0], ln_ffn_b=g_ln2_b[0])
        layer_grads.append((l, g))
    return loss, dy, layer_grads


EVEN_NAMES = ("even_w_in", "pool_w", "pool_scale", "conv_w", "conv_b", "dt_bias", "a_log", "d_skip", "ssm_norm_w",
              "even_w_out")
ODD_NAMES = ("odd_w_in", "fgate_b", "q_norm_w", "w_uq", "kv_norm_w", "w_ukv", "odd_w_out")
PER_LAYER_NAMES = ("ffn_w_gate", "ffn_w_up", "ffn_w_down", "ln_mix_g", "ln_mix_b", "ln_ffn_g", "ln_ffn_b")
WEIGHT_NAMES = EVEN_NAMES + ODD_NAMES + PER_LAYER_NAMES


def _grads_by_name(layer_grads):
    by_layer = dict(layer_grads)
    out = {}
    for n in EVEN_NAMES:
        out[n] = [by_layer[l][n] for l in range(0, DEPTH, 2)]
    for n in ODD_NAMES:
        out[n] = [by_layer[l][n] for l in range(1, DEPTH, 2)]
    for n in PER_LAYER_NAMES:
        out[n] = [by_layer[l][n] for l in range(DEPTH)]
    return out


def _prepare_weights(full):
    w = {}
    w["even_w_in"] = [_prep_even_w_in(full["even_w_in"][i]) for i in range(2)]
    w["even_w_out"] = [full["even_w_out"][i] for i in range(2)]
    w["odd_w_in"] = [_prep_odd_w_in(full["odd_w_in"][i]) for i in range(2)]
    w["w_uq"] = [_prep_w_uq(full["w_uq"][i]) for i in range(2)]
    w["w_ukv"] = [full["w_ukv"][i] for i in range(2)]
    w["odd_w_out"] = [full["odd_w_out"][i] for i in range(2)]
    w["ffn_w_gu"] = [jnp.concatenate([full["ffn_w_gate"][l], full["ffn_w_up"][l]], axis=1) for l in range(DEPTH)]
    w["ffn_w_down"] = [full["ffn_w_down"][l] for l in range(DEPTH)]
    for n in ("conv_w", "q_norm_w", "kv_norm_w"):
        w[n] = full[n]
    return w


def _flatten_small(grads):
    flat = jnp.concatenate([g.reshape(-1) for n in REPLICATED for g in grads[n]])
    n = flat.shape[0]
    per = LANES * 8
    total = -(-n // per) * per
    return jnp.pad(flat, (0, total - n)).reshape(total // LANES, LANES)


def _unflatten_small(mat, like):
    flat = mat.reshape(-1)
    out, off = {}, 0
    for n in REPLICATED:
        size = math.prod(like[n].shape)
        out[n] = flat[off:off + size].reshape(like[n].shape)
        off += size
    return out


def kernel(x, even_w_in, pool_w, pool_scale, conv_w, conv_b, dt_bias, a_log, d_skip, ssm_norm_w, even_w_out, odd_w_in, fgate_b, q_norm_w, w_uq, kv_norm_w, w_ukv, odd_w_out, ffn_w_gate, ffn_w_up, ffn_w_down, ln_mix_g, ln_mix_b, ln_ffn_g, ln_ffn_b, loss_target, m_even_w_in, m_pool_w, m_pool_scale, m_conv_w, m_conv_b, m_dt_bias, m_a_log, m_d_skip, m_ssm_norm_w, m_even_w_out, m_odd_w_in, m_fgate_b, m_q_norm_w, m_w_uq, m_kv_norm_w, m_w_ukv, m_odd_w_out, m_ffn_w_gate, m_ffn_w_up, m_ffn_w_down, m_ln_mix_g, m_ln_mix_b, m_ln_ffn_g, m_ln_ffn_b, v_even_w_in, v_pool_w, v_pool_scale, v_conv_w, v_conv_b, v_dt_bias, v_a_log, v_d_skip, v_ssm_norm_w, v_even_w_out, v_odd_w_in, v_fgate_b, v_q_norm_w, v_w_uq, v_kv_norm_w, v_w_ukv, v_odd_w_out, v_ffn_w_gate, v_ffn_w_up, v_ffn_w_down, v_ln_mix_g, v_ln_mix_b, v_ln_ffn_g, v_ln_ffn_b):
    weights = dict(even_w_in=even_w_in, pool_w=pool_w, pool_scale=pool_scale, conv_w=conv_w, conv_b=conv_b,
                   dt_bias=dt_bias, a_log=a_log, d_skip=d_skip, ssm_norm_w=ssm_norm_w, even_w_out=even_w_out,
                   odd_w_in=odd_w_in, fgate_b=fgate_b, q_norm_w=q_norm_w, w_uq=w_uq, kv_norm_w=kv_norm_w, w_ukv=w_ukv,
                   odd_w_out=odd_w_out, ffn_w_gate=ffn_w_gate, ffn_w_up=ffn_w_up, ffn_w_down=ffn_w_down,
                   ln_mix_g=ln_mix_g, ln_mix_b=ln_mix_b, ln_ffn_g=ln_ffn_g, ln_ffn_b=ln_ffn_b)
    m_in = dict(even_w_in=m_even_w_in, pool_w=m_pool_w, pool_scale=m_pool_scale, conv_w=m_conv_w, conv_b=m_conv_b,
                dt_bias=m_dt_bias, a_log=m_a_log, d_skip=m_d_skip, ssm_norm_w=m_ssm_norm_w, even_w_out=m_even_w_out,
                odd_w_in=m_odd_w_in, fgate_b=m_fgate_b, q_norm_w=m_q_norm_w, w_uq=m_w_uq, kv_norm_w=m_kv_norm_w,
                w_ukv=m_w_ukv, odd_w_out=m_odd_w_out, ffn_w_gate=m_ffn_w_gate, ffn_w_up=m_ffn_w_up,
                ffn_w_down=m_ffn_w_down, ln_mix_g=m_ln_mix_g, ln_mix_b=m_ln_mix_b, ln_ffn_g=m_ln_ffn_g,
                ln_ffn_b=m_ln_ffn_b)
    v_in = dict(even_w_in=v_even_w_in, pool_w=v_pool_w, pool_scale=v_pool_scale, conv_w=v_conv_w, conv_b=v_conv_b,
                dt_bias=v_dt_bias, a_log=v_a_log, d_skip=v_d_skip, ssm_norm_w=v_ssm_norm_w, even_w_out=v_even_w_out,
                odd_w_in=v_odd_w_in, fgate_b=v_fgate_b, q_norm_w=v_q_norm_w, w_uq=v_w_uq, kv_norm_w=v_kv_norm_w,
                w_ukv=v_w_ukv, odd_w_out=v_odd_w_out, ffn_w_gate=v_ffn_w_gate, ffn_w_up=v_ffn_w_up,
                ffn_w_down=v_ffn_w_down, ln_mix_g=v_ln_mix_g, ln_mix_b=v_ln_mix_b, ln_ffn_g=v_ln_ffn_g,
                ln_ffn_b=v_ln_ffn_b)
    shards = {n: weights[n] for n, _ in SHARDED}

    w = _prepare_weights(_gather_weights(shards))
    small = {n: weights[n] for n in REPLICATED}
    w.update(small)

    loss_local, dx, layer_grads = _local_step(x[0], loss_target[0], w, small)
    grads_full = _grads_by_name(layer_grads)

    reduced = _reduce_scatter(_pack_grads(grads_full, shards))
    grads = _unpack_reduced(reduced, shards)
    small_sum = _all_reduce_small(_flatten_small(grads_full), "small_grads_allreduce")
    grads.update(_unflatten_small(small_sum, small))
    loss = lax.psum(loss_local, ("x", "y", "c"))

    deltas, new_m, new_v = {}, {}, {}
    for n in WEIGHT_NAMES:
        deltas[n], new_m[n], new_v[n] = _adamw(weights[n], grads[n], m_in[n], v_in[n], "adamw_" + n)
    return (loss, dx[None], *[grads[n] for n in WEIGHT_NAMES], *[deltas[n] for n in WEIGHT_NAMES],
            *[new_m[n] for n in WEIGHT_NAMES], *[new_v[n] for n in WEIGHT_NAMES])
```

```python
import functools
import math

import numpy as np
import jax
import jax.numpy as jnp
from jax import lax
from jax.experimental import pallas as pl
from jax.experimental.pallas import tpu as pltpu

F32 = jnp.float32
BF16 = jnp.bfloat16
HI = lax.Precision.HIGHEST
MESH_ID = pl.DeviceIdType.MESH

VMEM_LIMIT_BYTES = 56 * 1024 * 1024
LANES = 128

D_MODEL = 1024
DEPTH = 4
POOL_WINDOWS = (2, 4, 8, 16)
POOL_GROUP = 128
POOL_WIDTH = 512
SSM_D_INNER = 1024
SSM_HEAD_DIM = 64
SSM_HEADS = 16
SSM_GROUPS = 2
SSM_STATE = 128
SSM_CONV = 4
SSM_CHUNK = 128
SSM_CONV_DIM = 1536
EVEN_IN = 3088
EVEN_IN_PAD = 3200
FOX_HEADS = 8
FOX_WIDTH = 512
MLA_HEADS = 8
MLA_NOPE = 64
MLA_ROPE = 32
MLA_V = 64
MLA_Q_RANK = 512
MLA_KV_RANK = 256
ROPE_THETA = 10000.0
ODD_IN = 2344
ODD_IN_PAD = 2560
D_FF = 2816
ALPHA = (2 * DEPTH) ** 0.25
LN_EPS = 1e-5
RMS_EPS = 1e-6
ADAM_LR = 0.001
ADAM_B1 = 0.9
ADAM_B2 = 0.999
ADAM_EPS = 1e-08
ADAM_WD = 0.01
ADAM_STEP = 10
NEG_BIG = -1e30

ATTN_TILE = 512
ATTN_WIDE = 2048
FOX_SCALE = 0.125
MLA_SCALE = (MLA_NOPE + MLA_ROPE) ** -0.5
ROW_TILE = 256


def _cparams(sem=None):
    return pltpu.CompilerParams(dimension_semantics=sem, vmem_limit_bytes=VMEM_LIMIT_BYTES)


def _pick(d, prefs):
    for p in prefs:
        if d % p == 0:
            return p
    return d


_M_PREFS = (2048, 1024, 512, 640, 1408, 768, 384, 256, 128)
_N_PREFS = (1024, 1408, 512, 640, 768, 384, 256, 128)
_K_PREFS = (1024, 1408, 512, 640, 768, 256, 128)
MM_MAX_ACC_ELEMS = 1408 * 1024


def _mm(a, b, mode, out_dtype, name, extra=None, alpha=1.0):
    if mode == "nn":
        (m, k), (k2, n) = a.shape, b.shape
    elif mode == "nt":
        (m, k), (n, k2) = a.shape, b.shape
    else:
        (k, m), (k2, n) = a.shape, b.shape
    assert k == k2, (a.shape, b.shape, mode)
    tn, tk = _pick(n, _N_PREFS), _pick(k, _K_PREFS)
    tm = _pick(m, tuple(p for p in _M_PREFS if p * tn <= MM_MAX_ACC_ELEMS))
    nk = k // tk
    if mode == "nn":
        a_spec = pl.BlockSpec((tm, tk), lambda i, j, kk: (i, kk))
        b_spec = pl.BlockSpec((tk, tn), lambda i, j, kk: (kk, j))
        dims = (((1,), (0,)), ((), ()))
    elif mode == "nt":
        a_spec = pl.BlockSpec((tm, tk), lambda i, j, kk: (i, kk))
        b_spec = pl.BlockSpec((tn, tk), lambda i, j, kk: (j, kk))
        dims = (((1,), (1,)), ((), ()))
    else:
        a_spec = pl.BlockSpec((tk, tm), lambda i, j, kk: (kk, i))
        b_spec = pl.BlockSpec((tk, tn), lambda i, j, kk: (kk, j))
        dims = (((0,), (0,)), ((), ()))
    o_spec = pl.BlockSpec((tm, tn), lambda i, j, kk: (i, j))
    has_extra = extra is not None

    def body(*refs):
        if has_extra:
            a_ref, b_ref, e_ref, o_ref, acc = refs
        else:
            a_ref, b_ref, o_ref, acc = refs
        kk = pl.program_id(2)

        @pl.when(kk == 0)
        def _():
            acc[...] = jnp.zeros_like(acc)

        acc[...] += lax.dot_general(a_ref[...].astype(BF16), b_ref[...].astype(BF16), dims,
                                    preferred_element_type=F32)

        @pl.when(kk == nk - 1)
        def _():
            r = acc[...]
            if has_extra:
                r = r + alpha * e_ref[...].astype(F32)
            o_ref[...] = r.astype(o_ref.dtype)

    ins = [a, b] + ([extra] if has_extra else [])
    specs = [a_spec, b_spec] + ([o_spec] if has_extra else [])
    return pl.pallas_call(
        body, name=name, grid=(m // tm, n // tn, nk), in_specs=specs, out_specs=o_spec,
        out_shape=jax.ShapeDtypeStruct((m, n), out_dtype),
        scratch_shapes=[pltpu.VMEM((tm, tn), F32)],
        compiler_params=_cparams(("parallel", "parallel", "arbitrary")),
    )(*ins)


def _full_spec(shape):
    nd = len(shape)
    return pl.BlockSpec(tuple(shape), lambda i, _nd=nd: (0,) * _nd)


class _Cols:
    def __init__(self, arr, block, width):
        self.arr, self.block, self.width = arr, block, width
        self.shape, self.dtype = (arr.shape[0], width), arr.dtype


def _tile_arr(t):
    return t.arr if isinstance(t, _Cols) else t


def _tile_spec(t, tt, row_index):
    block = t.block if isinstance(t, _Cols) else 0
    return pl.BlockSpec((tt, t.shape[1]), lambda i: (row_index(i), block))


def _scan_fwd(f, name, n, tiles, ctiles, params, cparams, carry_shapes, out_defs):
    rows = tiles[0].shape[0]
    tt = rows // n
    nt, nct, npar, ncp, ncar, nout = len(tiles), len(ctiles), len(params), len(cparams), len(carry_shapes), len(out_defs)

    def body(*refs):
        pos = 0
        t_refs = refs[pos:pos + nt]; pos += nt
        ct_refs = refs[pos:pos + nct]; pos += nct
        p_refs = refs[pos:pos + npar]; pos += npar
        cp_refs = refs[pos:pos + ncp]; pos += ncp
        o_refs = refs[pos:pos + nout]; pos += nout
        cs_refs = refs[pos:pos + ncar]; pos += ncar
        c_scr = refs[pos:pos + ncar]
        i = pl.program_id(0)

        @pl.when(i == 0)
        def _():
            for c in c_scr:
                c[...] = jnp.zeros_like(c)

        carry = tuple(c[...] for c in c_scr)
        for s, c in zip(cs_refs, carry):
            s[0] = c
        new_carry, outs = f(carry, tuple(r[...] for r in t_refs), tuple(r[...] for r in ct_refs),
                            tuple(r[...] for r in p_refs), tuple(r[...] for r in cp_refs), i)
        for o_ref, o in zip(o_refs, outs):
            o_ref[...] = o
        for c, v in zip(c_scr, new_carry):
            c[...] = v

    tile_spec = lambda t: _tile_spec(t, tt, lambda i: i)
    in_specs = ([tile_spec(t) for t in tiles] + [tile_spec(t) for t in ctiles]
                + [_full_spec(p.shape) for p in params] + [_full_spec(p.shape) for p in cparams])
    out_specs = ([pl.BlockSpec((tt, c), lambda i: (i, 0)) for c, _ in out_defs]
                 + [pl.BlockSpec((1,) + tuple(s), lambda i: (i, 0, 0)) for s in carry_shapes])
    out_shape = ([jax.ShapeDtypeStruct((rows, c), dt) for c, dt in out_defs]
                 + [jax.ShapeDtypeStruct((n,) + tuple(s), F32) for s in carry_shapes])
    res = pl.pallas_call(
        body, name=name, grid=(n,), in_specs=in_specs, out_specs=out_specs, out_shape=out_shape,
        scratch_shapes=[pltpu.VMEM(tuple(s), F32) for s in carry_shapes],
        compiler_params=_cparams(("arbitrary",)),
    )(*map(_tile_arr, tiles), *map(_tile_arr, ctiles), *params, *cparams)
    return list(res[:nout]), list(res[nout:])


def _scan_bwd(f, name, n, tiles, ctiles, params, cparams, carries, douts):
    rows = tiles[0].shape[0]
    tt = rows // n
    nt, nct, npar, ncp, ncar, nout = len(tiles), len(ctiles), len(params), len(cparams), len(carries), len(douts)

    def body(*refs):
        pos = 0
        t_refs = refs[pos:pos + nt]; pos += nt
        ct_refs = refs[pos:pos + nct]; pos += nct
        p_refs = refs[pos:pos + npar]; pos += npar
        cp_refs = refs[pos:pos + ncp]; pos += ncp
        cs_refs = refs[pos:pos + ncar]; pos += ncar
        do_refs = refs[pos:pos + nout]; pos += nout
        dt_refs = refs[pos:pos + nt]; pos += nt
        dp_refs = refs[pos:pos + npar]; pos += npar
        dc_scr = refs[pos:pos + ncar]
        i = pl.program_id(0)

        @pl.when(i == 0)
        def _():
            for c in dc_scr:
                c[...] = jnp.zeros_like(c)
            for d in dp_refs:
                d[...] = jnp.zeros_like(d)

        ctv = tuple(r[...] for r in ct_refs)
        cpv = tuple(r[...] for r in cp_refs)

        def g(c, t, p):
            return f(c, t, ctv, p, cpv, n - 1 - i)

        _, vjp = jax.vjp(g, tuple(s[0] for s in cs_refs), tuple(r[...] for r in t_refs),
                         tuple(r[...] for r in p_refs))
        dc, dt, dp = vjp((tuple(c[...] for c in dc_scr), tuple(r[...] for r in do_refs)))
        for r, v in zip(dt_refs, dt):
            r[...] = v
        for r, v in zip(dp_refs, dp):
            r[...] += v
        for c, v in zip(dc_scr, dc):
            c[...] = v

    rev_tile = lambda t: _tile_spec(t, tt, lambda i: n - 1 - i)
    rev_out = lambda t: pl.BlockSpec((tt, t.shape[1]), lambda i: (n - 1 - i, 0))
    in_specs = ([rev_tile(t) for t in tiles] + [rev_tile(t) for t in ctiles]
                + [_full_spec(p.shape) for p in params] + [_full_spec(p.shape) for p in cparams]
                + [pl.BlockSpec((1,) + tuple(c.shape[1:]), lambda i: (n - 1 - i, 0, 0)) for c in carries]
                + [rev_tile(d) for d in douts])
    out_specs = [rev_out(t) for t in tiles] + [_full_spec(p.shape) for p in params]
    out_shape = ([jax.ShapeDtypeStruct(t.shape, t.dtype) for t in tiles]
                 + [jax.ShapeDtypeStruct(p.shape, F32) for p in params])
    res = pl.pallas_call(
        body, name=name, grid=(n,), in_specs=in_specs, out_specs=out_specs, out_shape=out_shape,
        scratch_shapes=[pltpu.VMEM(tuple(c.shape[1:]), F32) for c in carries],
        compiler_params=_cparams(("arbitrary",)),
    )(*map(_tile_arr, tiles), *map(_tile_arr, ctiles), *params, *cparams, *carries, *map(_tile_arr, douts))
    return list(res[:nt]), list(res[nt:])


def _silu(x):
    return x * jax.nn.sigmoid(x)


def _softplus(x):
    return jnp.maximum(x, 0.0) + jnp.log(1.0 + jnp.exp(-jnp.abs(x)))


def _f_ln(carry, tiles, ctiles, params, cparams, idx):
    (r,), (g, b) = tiles, params
    mu = jnp.mean(r, axis=-1, keepdims=True)
    xc = r - mu
    var = jnp.mean(xc * xc, axis=-1, keepdims=True)
    return (), (xc * lax.rsqrt(var + LN_EPS) * g + b,)


def _f_ln_with_bf16(carry, tiles, ctiles, params, cparams, idx):
    _, (y,) = _f_ln(carry, tiles, ctiles, params, cparams, idx)
    return (), (y, y.astype(BF16))


def _f_act(carry, tiles, ctiles, params, cparams, idx):
    (gu,) = tiles
    g = gu[:, :D_FF].astype(F32)
    u = gu[:, D_FF:].astype(F32)
    return (), ((_silu(g) * u).astype(BF16),)


def _f_pool(carry, tiles, ctiles, params, cparams, idx):
    (prev,), (u,), (pool_w, pool_scale) = carry, tiles, params
    tt = u.shape[0]
    halo = prev.shape[0]
    ext = jnp.concatenate([prev, u], axis=0)
    pos = idx * tt + lax.broadcasted_iota(jnp.int32, (tt, 1), 0)
    ys = []
    for g, w in enumerate(POOL_WINDOWS):
        lo, hi = g * POOL_GROUP, (g + 1) * POOL_GROUP
        eg = ext[:, lo:hi]
        s = eg[halo:halo + tt]
        for j in range(1, w):
            s = s + eg[halo - j:halo - j + tt]
        count = jnp.minimum(pos + 1, w).astype(F32)
        diff = s / count - u[:, lo:hi]
        ys.append(jnp.dot(diff.astype(BF16), pool_w[g].astype(BF16), preferred_element_type=F32))
    y = jnp.concatenate(ys, axis=1) * pool_scale
    return (u[tt - halo:, :],), (y.astype(BF16),)


def _f_conv(carry, tiles, ctiles, params, cparams, idx):
    (prev,), (xbc,), (conv_w, conv_b) = carry, tiles, params
    tt = xbc.shape[0]
    halo = prev.shape[0]
    ext = jnp.concatenate([prev, xbc], axis=0)
    acc = jnp.zeros_like(xbc) + conv_b
    for k in range(SSM_CONV):
        off = halo - (SSM_CONV - 1) + k
        acc = acc + ext[off:off + tt] * conv_w[k:k + 1, :]
    return (xbc[tt - halo:, :],), (_silu(acc),)


def _head_expand_matrix():
    r = lax.broadcasted_iota(jnp.int32, (LANES, SSM_D_INNER), 0)
    c = lax.broadcasted_iota(jnp.int32, (LANES, SSM_D_INNER), 1)
    return ((c >= r * SSM_HEAD_DIM) & (c < (r + 1) * SSM_HEAD_DIM)).astype(BF16)


def _head_expand(v, e):
    hi = v.astype(BF16)
    lo = (v - hi.astype(F32)).astype(BF16)
    return jnp.dot(hi, e, preferred_element_type=F32) + jnp.dot(lo, e, preferred_element_type=F32)


def _f_ssd(carry, tiles, ctiles, params, cparams, idx):
    (state,), (xa, dtr, z), (dt_bias, a_log, d_skip, norm_w) = carry, tiles, params
    L, P, N, E = SSM_CHUNK, SSM_HEAD_DIM, SSM_STATE, SSM_HEADS // SSM_GROUPS
    gw = E * P
    em = _head_expand_matrix()
    dt = _softplus(dtr + dt_bias)
    da = dt * (-jnp.exp(a_log))
    r = lax.broadcasted_iota(jnp.int32, (L, L), 0)
    c = lax.broadcasted_iota(jnp.int32, (L, L), 1)
    tri = c <= r
    acs = jnp.dot(tri.astype(F32), da, precision=HI, preferred_element_type=F32)
    acs_t = acs.T
    last = acs[L - 1:L, :]
    xs = xa[:, :SSM_D_INNER]
    xdt = xs * _head_expand(dt, em)
    xdt_b = xdt.astype(BF16)
    xdec = (xdt * _head_expand(jnp.exp(last - acs), em)).astype(BF16)
    eacs = _head_expand(jnp.exp(acs), em)
    y_parts, st_parts = [], []
    for g in range(SSM_GROUPS):
        bg = xa[:, SSM_D_INNER + g * N:SSM_D_INNER + (g + 1) * N].astype(BF16)
        cg = xa[:, SSM_D_INNER + (SSM_GROUPS + g) * N:SSM_D_INNER + (SSM_GROUPS + g + 1) * N].astype(BF16)
        cb = lax.dot_general(cg, bg, (((1,), (1,)), ((), ())), preferred_element_type=F32)
        prev_g = state[:, g * gw:(g + 1) * gw]
        y_off = jnp.dot(cg, prev_g.astype(BF16), preferred_element_type=F32) * eacs[:, g * gw:(g + 1) * gw]
        st_parts.append(lax.dot_general(bg, xdec[:, g * gw:(g + 1) * gw], (((0,), (0,)), ((), ())),
                                        preferred_element_type=F32))
        diag = []
        for e in range(E):
            h = g * E + e
            lmat = jnp.exp(jnp.where(tri, acs[:, h:h + 1] - acs_t[h:h + 1, :], NEG_BIG))
            diag.append(jnp.dot((cb * lmat).astype(BF16), xdt_b[:, h * P:(h + 1) * P], preferred_element_type=F32))
        y_parts.append(jnp.concatenate(diag, axis=1) + y_off)
    y = jnp.concatenate(y_parts, axis=1) + xs * _head_expand(d_skip, em)
    new_state = state * _head_expand(jnp.exp(last), em) + jnp.concatenate(st_parts, axis=1)
    y = y * _silu(z)
    y = y * lax.rsqrt(jnp.mean(y * y, axis=-1, keepdims=True) + RMS_EPS) * norm_w
    return (new_state,), (y.astype(BF16),)


def _f_fox_gate(carry, tiles, ctiles, params, cparams, idx):
    (run,), (fl,), (fb,) = carry, tiles, params
    tt = fl.shape[0]
    logf = -_softplus(-(fl + fb))
    r = lax.broadcasted_iota(jnp.int32, (tt, tt), 0)
    c = lax.broadcasted_iota(jnp.int32, (tt, tt), 1)
    cum = jnp.dot((c <= r).astype(F32), logf, precision=HI, preferred_element_type=F32) + run[0:1, :]
    return (jnp.broadcast_to(cum[tt - 1:tt, :], run.shape),), (cum,)


def _rms(x, w):
    return x * lax.rsqrt(jnp.mean(x * x, axis=-1, keepdims=True) + RMS_EPS) * w


def _f_mla_prep(carry, tiles, ctiles, params, cparams, idx):
    (cq, ckv, kr), (ck, sk), (qw, kvw), (rk,) = tiles, ctiles, params, cparams
    rot = jnp.dot(kr, rk, precision=HI, preferred_element_type=F32)
    return (), (_rms(cq, qw).astype(BF16), _rms(ckv, kvw).astype(BF16), (kr * ck + rot * sk).astype(BF16))


def _f_rope_q(carry, tiles, ctiles, params, cparams, idx):
    (q,), (cos, sin) = tiles, ctiles
    nope = MLA_HEADS * MLA_NOPE
    x1 = q[:, nope:nope + LANES]
    x2 = q[:, nope + LANES:]
    roped = jnp.concatenate([q[:, :nope], x1 * cos - x2 * sin, x2 * cos + x1 * sin], axis=1)
    return (), ((roped * MLA_SCALE).astype(BF16),)


def _scores(q, k, fq, fk, masked, row0, col0):
    s = lax.dot_general(q, k, (((1,), (1,)), ((), ())), preferred_element_type=F32)
    if fq is not None:
        s = s + fq - fk
    if masked:
        rows = row0 + lax.broadcasted_iota(jnp.int32, s.shape, 0)
        cols = col0 + lax.broadcasted_iota(jnp.int32, s.shape, 1)
        s = jnp.where(cols <= rows, s, NEG_BIG)
    return s


def _q_major_tables(T, tq, tk):
    r = tk // tq
    qi = np.concatenate([np.full(i // r + 1, i, np.int32) for i in range(T // tq)])
    ki = np.concatenate([np.arange(i // r + 1, dtype=np.int32) for i in range(T // tq)])
    kind = np.where(ki == qi // r, qi % r + 1, 0).astype(np.int32)
    return [jnp.asarray(a) for a in (qi, ki, (ki == 0).astype(np.int32), kind)]


def _k_major_tables(T, tq, tk):
    r = tq // tk
    nq = T // tq
    ki = np.concatenate([np.full(nq - i // r, i, np.int32) for i in range(T // tk)])
    qi = np.concatenate([np.arange(i // r, nq, dtype=np.int32) for i in range(T // tk)])
    kind = np.where(qi == ki // r, r - ki % r, 0).astype(np.int32)
    return [jnp.asarray(a) for a in (ki, qi, (qi == nq - 1).astype(np.int32), kind)]


def _attn_tiles(T):
    return min(ATTN_TILE, T), min(ATTN_WIDE, T)


def _attn_fwd(q, k, v, fq, fk, name):
    H, T, dk = q.shape
    dv = v.shape[2]
    tq, tk = _attn_tiles(T)
    decay = fq is not None
    tables = _q_major_tables(T, tq, tk)

    def body(qi_ref, ki_ref, first_ref, kind_ref, *refs):
        if decay:
            q_ref, k_ref, v_ref, fq_ref, fk_ref, o_ref, lse_ref, m_s, l_s, acc = refs
        else:
            q_ref, k_ref, v_ref, o_ref, lse_ref, m_s, l_s, acc = refs
        t = pl.program_id(1)
        qi, ki = qi_ref[t], ki_ref[t]

        @pl.when(first_ref[t] == 1)
        def _():
            m_s[...] = jnp.full_like(m_s, NEG_BIG)
            l_s[...] = jnp.zeros_like(l_s)
            acc[...] = jnp.zeros_like(acc)

        def step(kind):
            w = tk if kind == 0 else kind * tq
            s = _scores(q_ref[0], k_ref[0, :w, :], fq_ref[0] if decay else None,
                        fk_ref[0, :, :w] if decay else None, kind > 0, qi * tq, ki * tk)
            m_new = jnp.maximum(m_s[...], jnp.max(s, axis=-1, keepdims=True))
            p = jnp.exp(s - m_new)
            corr = jnp.exp(m_s[...] - m_new)
            l_s[...] = corr * l_s[...] + jnp.sum(p, axis=-1, keepdims=True)
            acc[...] = corr * acc[...] + jnp.dot(p.astype(BF16), v_ref[0, :w, :], preferred_element_type=F32)
            m_s[...] = m_new
            if kind > 0:
                o_ref[0] = (acc[...] / l_s[...]).astype(o_ref.dtype)
                lse_ref[0] = m_s[...] + jnp.log(l_s[...])

        for kind in range(tk // tq + 1):
            pl.when(kind_ref[t] == kind)(functools.partial(step, kind))

    qspec = lambda d: pl.BlockSpec((1, tq, d), lambda h, t, qi, ki, fi, la: (h, qi[t], 0))
    kspec = lambda d: pl.BlockSpec((1, tk, d), lambda h, t, qi, ki, fi, la: (h, ki[t], 0))
    in_specs = [qspec(dk), kspec(dk), kspec(dv)]
    ins = [q, k, v]
    if decay:
        in_specs += [qspec(1), pl.BlockSpec((1, 1, tk), lambda h, t, qi, ki, fi, la: (h, 0, ki[t]))]
        ins += [fq, fk]
    grid_spec = pltpu.PrefetchScalarGridSpec(
        num_scalar_prefetch=4, grid=(H, int(tables[0].shape[0])), in_specs=in_specs,
        out_specs=[qspec(dv), qspec(1)],
        scratch_shapes=[pltpu.VMEM((tq, 1), F32), pltpu.VMEM((tq, 1), F32), pltpu.VMEM((tq, dv), F32)])
    return pl.pallas_call(
        body, name=name, grid_spec=grid_spec,
        out_shape=[jax.ShapeDtypeStruct((H, T, dv), BF16), jax.ShapeDtypeStruct((H, T, 1), F32)],
        compiler_params=_cparams(("parallel", "arbitrary")),
    )(*tables, *ins)


def _attn_bwd_dq(q, k, v, o, do, lse, fq, fk, name):
    H, T, dk = q.shape
    dv = v.shape[2]
    tq, tk = _attn_tiles(T)
    decay = fq is not None
    tables = _q_major_tables(T, tq, tk)

    def body(qi_ref, ki_ref, first_ref, kind_ref, *refs):
        if decay:
            q_ref, k_ref, v_ref, o_ref, do_ref, lse_ref, fq_ref, fk_ref, dq_ref, dl_ref, acc, dl, leak = refs
        else:
            q_ref, k_ref, v_ref, o_ref, do_ref, lse_ref, dq_ref, dl_ref, acc, dl, leak = refs
        t = pl.program_id(1)
        qi, ki = qi_ref[t], ki_ref[t]

        @pl.when(first_ref[t] == 1)
        def _():
            acc[...] = jnp.zeros_like(acc)
            leak[...] = jnp.zeros_like(leak)
            dl[...] = jnp.sum(do_ref[0].astype(F32) * o_ref[0].astype(F32), axis=-1, keepdims=True)

        def step(kind):
            w = tk if kind == 0 else kind * tq
            k_v = k_ref[0, :w, :]
            s = _scores(q_ref[0], k_v, fq_ref[0] if decay else None, fk_ref[0, :, :w] if decay else None,
                        kind > 0, qi * tq, ki * tk)
            p = jnp.exp(s - lse_ref[0])
            dp = lax.dot_general(do_ref[0], v_ref[0, :w, :], (((1,), (1,)), ((), ())), preferred_element_type=F32)
            ds = p * (dp - dl[...])
            leak[...] += jnp.sum(ds, axis=-1, keepdims=True)
            acc[...] += jnp.dot(ds.astype(BF16), k_v, preferred_element_type=F32)
            if kind > 0:
                dq_ref[0] = acc[...].astype(dq_ref.dtype)
                dl_ref[0] = dl[...] + leak[...]

        for kind in range(tk // tq + 1):
            pl.when(kind_ref[t] == kind)(functools.partial(step, kind))

    qspec = lambda d: pl.BlockSpec((1, tq, d), lambda h, t, qi, ki, fi, la: (h, qi[t], 0))
    kspec = lambda d: pl.BlockSpec((1, tk, d), lambda h, t, qi, ki, fi, la: (h, ki[t], 0))
    in_specs = [qspec(dk), kspec(dk), kspec(dv), qspec(dv), qspec(dv), qspec(1)]
    ins = [q, k, v, o, do, lse]
    if decay:
        in_specs += [qspec(1), pl.BlockSpec((1, 1, tk), lambda h, t, qi, ki, fi, la: (h, 0, ki[t]))]
        ins += [fq, fk]
    grid_spec = pltpu.PrefetchScalarGridSpec(
        num_scalar_prefetch=4, grid=(H, int(tables[0].shape[0])), in_specs=in_specs,
        out_specs=[qspec(dk), qspec(1)],
        scratch_shapes=[pltpu.VMEM((tq, dk), F32), pltpu.VMEM((tq, 1), F32), pltpu.VMEM((tq, 1), F32)])
    return pl.pallas_call(
        body, name=name, grid_spec=grid_spec,
        out_shape=[jax.ShapeDtypeStruct((H, T, dk), BF16), jax.ShapeDtypeStruct((H, T, 1), F32)],
        compiler_params=_cparams(("parallel", "arbitrary")),
    )(*tables, *ins)


def _attn_bwd_dkv(q, k, v, delta, do, lse, fq, fk, name):
    H, T, dk = q.shape
    dv = v.shape[2]
    tk, tq = _attn_tiles(T)
    decay = fq is not None
    tables = _k_major_tables(T, tq, tk)

    def body(ki_ref, qi_ref, last_ref, kind_ref, *refs):
        if decay:
            q_ref, k_ref, v_ref, dl_ref, do_ref, lse_ref, fq_ref, fk_ref, dk_ref, dv_ref, df_ref, dk_s, dv_s, df_s = refs
        else:
            q_ref, k_ref, v_ref, dl_ref, do_ref, lse_ref, dk_ref, dv_ref, dk_s, dv_s = refs
        t = pl.program_id(1)
        ki, qi = ki_ref[t], qi_ref[t]

        @pl.when(kind_ref[t] > 0)
        def _():
            dk_s[...] = jnp.zeros_like(dk_s)
            dv_s[...] = jnp.zeros_like(dv_s)
            if decay:
                df_s[...] = jnp.zeros_like(df_s)

        def step(kind):
            off = 0 if kind == 0 else tq - kind * tk
            q_v, do_v = q_ref[0, off:, :], do_ref[0, off:, :]
            s = _scores(q_v, k_ref[0], fq_ref[0, off:, :] if decay else None, fk_ref[0] if decay else None,
                        kind > 0, qi * tq + off, ki * tk)
            p = jnp.exp(s - lse_ref[0, off:, :])
            dv_s[...] += lax.dot_general(p.astype(BF16), do_v, (((0,), (0,)), ((), ())), preferred_element_type=F32)
            dp = lax.dot_general(do_v, v_ref[0], (((1,), (1,)), ((), ())), preferred_element_type=F32)
            ds = p * (dp - dl_ref[0, off:, :])
            dk_s[...] += lax.dot_general(ds.astype(BF16), q_v, (((0,), (0,)), ((), ())),
                                         preferred_element_type=F32)
            if decay:
                df_s[...] -= jnp.sum(ds, axis=0, keepdims=True)

        for kind in range(tq // tk + 1):
            pl.when(kind_ref[t] == kind)(functools.partial(step, kind))

        @pl.when(last_ref[t] == 1)
        def _():
            dk_ref[0] = dk_s[...].astype(dk_ref.dtype)
            dv_ref[0] = dv_s[...].astype(dv_ref.dtype)
            if decay:
                df_ref[0] = df_s[...]

    qspec = lambda d: pl.BlockSpec((1, tq, d), lambda h, t, ki, qi, fi, la: (h, qi[t], 0))
    kspec = lambda d: pl.BlockSpec((1, tk, d), lambda h, t, ki, qi, fi, la: (h, ki[t], 0))
    in_specs = [qspec(dk), kspec(dk), kspec(dv), qspec(1), qspec(dv), qspec(1)]
    ins = [q, k, v, delta, do, lse]
    out_specs = [kspec(dk), kspec(dv)]
    out_shape = [jax.ShapeDtypeStruct((H, T, dk), BF16), jax.ShapeDtypeStruct((H, T, dv), BF16)]
    scratch = [pltpu.VMEM((tk, dk), F32), pltpu.VMEM((tk, dv), F32)]
    if decay:
        fkspec = pl.BlockSpec((1, 1, tk), lambda h, t, ki, qi, fi, la: (h, 0, ki[t]))
        in_specs += [qspec(1), fkspec]
        ins += [fq, fk]
        out_specs.append(fkspec)
        out_shape.append(jax.ShapeDtypeStruct((H, 1, T), F32))
        scratch.append(pltpu.VMEM((1, tk), F32))
    grid_spec = pltpu.PrefetchScalarGridSpec(
        num_scalar_prefetch=4, grid=(H, int(tables[0].shape[0])), in_specs=in_specs, out_specs=out_specs,
        scratch_shapes=scratch)
    return pl.pallas_call(
        body, name=name, grid_spec=grid_spec, out_shape=out_shape, compiler_params=_cparams(("parallel", "arbitrary")),
    )(*tables, *ins)


def _attn_bwd_fused(q, k, v, o, do, lse, name):
    H, T, dk = q.shape
    dv = v.shape[2]
    tk, tq = _attn_tiles(T)
    tables = _k_major_tables(T, tq, tk)
    npairs = int(tables[0].shape[0])

    def body(ki_ref, qi_ref, last_ref, kind_ref, q_ref, k_ref, v_ref, o_ref, do_ref, lse_ref,
             dq_ref, dk_ref, dv_ref, dq_s, dk_s, dv_s):
        t = pl.program_id(1)
        ki, qi = ki_ref[t], qi_ref[t]

        @pl.when(t == 0)
        def _():
            dq_s[...] = jnp.zeros_like(dq_s)

        @pl.when(kind_ref[t] > 0)
        def _():
            dk_s[...] = jnp.zeros_like(dk_s)
            dv_s[...] = jnp.zeros_like(dv_s)

        def step(kind):
            off = 0 if kind == 0 else tq - kind * tk
            q_v, do_v, k_v = q_ref[0, off:, :], do_ref[0, off:, :], k_ref[0]
            s = _scores(q_v, k_v, None, None, kind > 0, qi * tq + off, ki * tk)
            p = jnp.exp(s - lse_ref[0, off:, :])
            delta = jnp.sum(do_v.astype(F32) * o_ref[0, off:, :].astype(F32), axis=-1, keepdims=True)
            dv_s[...] += lax.dot_general(p.astype(BF16), do_v, (((0,), (0,)), ((), ())), preferred_element_type=F32)
            dp = lax.dot_general(do_v, v_ref[0], (((1,), (1,)), ((), ())), preferred_element_type=F32)
            ds = (p * (dp - delta)).astype(BF16)
            dk_s[...] += lax.dot_general(ds, q_v, (((0,), (0,)), ((), ())), preferred_element_type=F32)
            rows = pl.ds(pl.multiple_of(qi * tq + off, tk), tq - off)
            dq_s[rows, :] += jnp.dot(ds, k_v, preferred_element_type=F32)

        for kind in range(tq // tk + 1):
            pl.when(kind_ref[t] == kind)(functools.partial(step, kind))

        @pl.when(last_ref[t] == 1)
        def _():
            dk_ref[0] = dk_s[...].astype(dk_ref.dtype)
            dv_ref[0] = dv_s[...].astype(dv_ref.dtype)

        @pl.when(t == npairs - 1)
        def _():
            dq_ref[0] = dq_s[...].astype(dq_ref.dtype)

    qspec = lambda d: pl.BlockSpec((1, tq, d), lambda h, t, ki, qi, la, kd: (h, qi[t], 0))
    kspec = lambda d: pl.BlockSpec((1, tk, d), lambda h, t, ki, qi, la, kd: (h, ki[t], 0))
    grid_spec = pltpu.PrefetchScalarGridSpec(
        num_scalar_prefetch=4, grid=(H, npairs),
        in_specs=[qspec(dk), kspec(dk), kspec(dv), qspec(dv), qspec(dv), qspec(1)],
        out_specs=[pl.BlockSpec((1, T, dk), lambda h, t, ki, qi, la, kd: (h, 0, 0)), kspec(dk), kspec(dv)],
        scratch_shapes=[pltpu.VMEM((T, dk), F32), pltpu.VMEM((tk, dk), F32), pltpu.VMEM((tk, dv), F32)])
    return pl.pallas_call(
        body, name=name, grid_spec=grid_spec,
        out_shape=[jax.ShapeDtypeStruct((H, T, dk), BF16), jax.ShapeDtypeStruct((H, T, dk), BF16),
                   jax.ShapeDtypeStruct((H, T, dv), BF16)],
        compiler_params=_cparams(("parallel", "arbitrary")),
    )(*tables, q, k, v, o, do, lse)


def _loss_head(y, target, name):
    rows, d = y.shape
    tt = _pick(rows, (512, 256, 128))

    def body(y_ref, t_ref, dy_ref, part_ref):
        @pl.when(pl.program_id(0) == 0)
        def _():
            part_ref[...] = jnp.zeros_like(part_ref)

        err = y_ref[...] - t_ref[...]
        dy_ref[...] = err * (1.0 / d)
        sq = jnp.sum(err * err, axis=0, keepdims=True)
        folded = sq[:, :LANES]
        for j in range(1, d // LANES):
            folded = folded + sq[:, j * LANES:(j + 1) * LANES]
        part_ref[...] += folded

    spec = pl.BlockSpec((tt, d), lambda i: (i, 0))
    return pl.pallas_call(
        body, name=name, grid=(rows // tt,), in_specs=[spec, spec],
        out_specs=[spec, pl.BlockSpec((1, LANES), lambda i: (0, 0))],
        out_shape=[jax.ShapeDtypeStruct((rows, d), F32), jax.ShapeDtypeStruct((1, LANES), F32)],
        compiler_params=_cparams(("arbitrary",)),
    )(y, target)


def _adamw(w, g, m, v, name):
    shape = w.shape
    cols = shape[-1]
    rows = math.prod(shape[:-1])
    w2, g2, m2, v2 = (a.reshape(rows, cols) for a in (w, g, m, v))
    tr = _pick(rows, (512, 256, 128, 64, 32, 16, 8))
    c1 = 1.0 / (1.0 - ADAM_B1 ** ADAM_STEP)
    c2 = 1.0 / (1.0 - ADAM_B2 ** ADAM_STEP)

    def body(w_ref, g_ref, m_ref, v_ref, d_ref, nm_ref, nv_ref):
        gv = g_ref[...]
        nm = ADAM_B1 * m_ref[...] + (1.0 - ADAM_B1) * gv
        nv = ADAM_B2 * v_ref[...] + (1.0 - ADAM_B2) * gv * gv
        d_ref[...] = -ADAM_LR * ((nm * c1) / (jnp.sqrt(nv * c2) + ADAM_EPS) + ADAM_WD * w_ref[...])
        nm_ref[...] = nm
        nv_ref[...] = nv

    spec = pl.BlockSpec((tr, cols), lambda i: (i, 0))
    sds = jax.ShapeDtypeStruct((rows, cols), F32)
    d, nm, nv = pl.pallas_call(
        body, name=name, grid=(rows // tr,), in_specs=[spec] * 4, out_specs=[spec] * 3, out_shape=[sds] * 3,
        compiler_params=_cparams(("parallel",)),
    )(w2, g2, m2, v2)
    return d.reshape(shape), nm.reshape(shape), nv.reshape(shape)


_ANY = pl.BlockSpec(memory_space=pl.ANY)


def _my_xyc():
    return lax.axis_index("x"), lax.axis_index("y"), lax.axis_index("c")


def _chip_allgather_core_half(shards, name):
    n = len(shards)
    half = [s.shape[0] // 2 for s in shards]

    def body(*refs):
        full_refs, whole_refs = refs[:n], refs[n:2 * n]
        send_sems, recv_sems, loc_sems = refs[2 * n:]
        x, y, c = _my_xyc()
        myq = 2 * x + y
        x_refs = [full_refs[i].at[pl.ds(c * half[i], half[i])] for i in range(n)]
        out_refs = [whole_refs[i].at[c] for i in range(n)]
        chips = [(1 - x, y), (x, 1 - y), (1 - x, 1 - y)]
        started = []
        for i in range(n):
            local = pltpu.make_async_copy(x_refs[i], out_refs[i].at[myq], loc_sems.at[i])
            local.start()
            started.append(local)
        sends = []
        for j, (px, py) in enumerate(chips):
            for i in range(n):
                cp = pltpu.make_async_remote_copy(src_ref=x_refs[i], dst_ref=out_refs[i].at[myq],
                                                  send_sem=send_sems.at[3 * i + j], recv_sem=recv_sems.at[3 * i + j],
                                                  device_id=(px, py, c), device_id_type=MESH_ID)
                cp.start()
                sends.append(cp)
        for j, (px, py) in enumerate(chips):
            for i in range(n):
                pltpu.make_async_remote_copy(src_ref=x_refs[i], dst_ref=out_refs[i].at[2 * px + py],
                                             send_sem=send_sems.at[3 * i + j], recv_sem=recv_sems.at[3 * i + j],
                                             device_id=(px, py, c), device_id_type=MESH_ID).wait_recv()
        for cp in sends:
            cp.wait_send()
        for local in started:
            local.wait()

    return pl.pallas_call(
        body, name=name, in_specs=[_ANY] * n, out_specs=[_ANY] * n,
        out_shape=[jax.ShapeDtypeStruct((2, 4, h) + s.shape[1:], s.dtype) for s, h in zip(shards, half)],
        scratch_shapes=[pltpu.SemaphoreType.DMA((3 * n,)), pltpu.SemaphoreType.DMA((3 * n,)),
                        pltpu.SemaphoreType.DMA((n,))],
    )(*shards)


def _sibling_swap_halves(g, name):
    q, rows, cols = g.shape
    rh = rows // 2

    def body(g_ref, out_ref, send_sem, recv_sem):
        x, y, c = _my_xyc()
        src = g_ref.at[:, pl.ds((1 - c) * rh, rh), :]
        cp = pltpu.make_async_remote_copy(src_ref=src, dst_ref=out_ref, send_sem=send_sem, recv_sem=recv_sem,
                                          device_id=(x, y, 1 - c), device_id_type=MESH_ID)
        cp.start()
        cp.wait()

    return pl.pallas_call(
        body, name=name, in_specs=[_ANY], out_specs=_ANY,
        out_shape=jax.ShapeDtypeStruct((q, rh, cols), g.dtype),
        scratch_shapes=[pltpu.SemaphoreType.DMA, pltpu.SemaphoreType.DMA],
    )(g)


def _add_own_half(g, other, name):
    q, rows, cols = g.shape
    rh = rows // 2
    tr = _pick(rh, (512, 256, 128, 64, 32, 16, 8))
    nb = rh // tr
    cidx = lax.axis_index("c").astype(jnp.int32).reshape(1)

    def body(c_ref, g_ref, o_ref, out_ref):
        out_ref[...] = (g_ref[...] + o_ref[...]).astype(out_ref.dtype)

    grid_spec = pltpu.PrefetchScalarGridSpec(
        num_scalar_prefetch=1, grid=(q, nb),
        in_specs=[pl.BlockSpec((1, tr, cols), lambda a, i, c_ref: (a, c_ref[0] * nb + i, 0)),
                  pl.BlockSpec((1, tr, cols), lambda a, i, c_ref: (a, i, 0))],
        out_specs=pl.BlockSpec((1, tr, cols), lambda a, i, c_ref: (a, i, 0)))
    return pl.pallas_call(
        body, name=name, grid_spec=grid_spec, out_shape=jax.ShapeDtypeStruct((q, rh, cols), BF16),
        compiler_params=_cparams(("parallel", "parallel")),
    )(cidx, g, other)


def _chip_exchange(s, name):
    def body(s_ref, out_ref, send_sems, recv_sems, loc_sem):
        x, y, c = _my_xyc()
        myq = 2 * x + y
        local = pltpu.make_async_copy(s_ref.at[myq], out_ref.at[myq], loc_sem)
        local.start()
        chips = [(1 - x, y), (x, 1 - y), (1 - x, 1 - y)]
        sends = []
        for j, (px, py) in enumerate(chips):
            cp = pltpu.make_async_remote_copy(src_ref=s_ref.at[2 * px + py], dst_ref=out_ref.at[myq],
                                              send_sem=send_sems.at[j], recv_sem=recv_sems.at[j],
                                              device_id=(px, py, c), device_id_type=MESH_ID)
            cp.start()
            sends.append(cp)
        for j, (px, py) in enumerate(chips):
            pltpu.make_async_remote_copy(src_ref=s_ref.at[myq], dst_ref=out_ref.at[2 * px + py],
                                         send_sem=send_sems.at[j], recv_sem=recv_sems.at[j],
                                         device_id=(px, py, c), device_id_type=MESH_ID).wait_recv()
        for cp in sends:
            cp.wait_send()
        local.wait()

    return pl.pallas_call(
        body, name=name, in_specs=[_ANY], out_specs=_ANY, out_shape=jax.ShapeDtypeStruct(s.shape, s.dtype),
        scratch_shapes=[pltpu.SemaphoreType.DMA((3,)), pltpu.SemaphoreType.DMA((3,)), pltpu.SemaphoreType.DMA],
    )(s)


def _sum_leading(a, name):
    q, rows, cols = a.shape
    tr = _pick(rows, (512, 256, 128, 64, 32, 16, 8))

    def body(a_ref, out_ref):
        acc = a_ref[0].astype(F32)
        for j in range(1, q):
            acc = acc + a_ref[j].astype(F32)
        out_ref[...] = acc

    return pl.pallas_call(
        body, name=name, grid=(rows // tr,), in_specs=[pl.BlockSpec((q, tr, cols), lambda i: (0, i, 0))],
        out_specs=pl.BlockSpec((tr, cols), lambda i: (i, 0)), out_shape=jax.ShapeDtypeStruct((rows, cols), F32),
        compiler_params=_cparams(("parallel",)),
    )(a)


def _sibling_swap(arrays, name):
    n = len(arrays)

    def body(*refs):
        a_refs, out_refs = refs[:n], refs[n:2 * n]
        send_sems, recv_sems = refs[2 * n:]
        x, y, c = _my_xyc()
        copies = []
        for i in range(n):
            cp = pltpu.make_async_remote_copy(src_ref=a_refs[i], dst_ref=out_refs[i], send_sem=send_sems.at[i],
                                              recv_sem=recv_sems.at[i], device_id=(x, y, 1 - c),
                                              device_id_type=MESH_ID)
            cp.start()
            copies.append(cp)
        for cp in copies:
            cp.wait()

    return pl.pallas_call(
        body, name=name, in_specs=[_ANY] * n, out_specs=[_ANY] * n,
        out_shape=[jax.ShapeDtypeStruct(a.shape, a.dtype) for a in arrays],
        scratch_shapes=[pltpu.SemaphoreType.DMA((n,)), pltpu.SemaphoreType.DMA((n,))],
    )(*arrays)


def _by_core(mine, other):
    c = lax.axis_index("c")
    return jnp.where(c == 0, jnp.stack([mine, other]), jnp.stack([other, mine]))


def _sibling_fill(arrays, name):
    n = len(arrays)

    def body(*refs):
        buf_refs = refs[n:2 * n]
        send_sems, recv_sems = refs[2 * n:]
        x, y, c = _my_xyc()
        sends = []
        for i in range(n):
            cp = pltpu.make_async_remote_copy(src_ref=buf_refs[i].at[c], dst_ref=buf_refs[i].at[c],
                                              send_sem=send_sems.at[i], recv_sem=recv_sems.at[i],
                                              device_id=(x, y, 1 - c), device_id_type=MESH_ID)
            cp.start()
            sends.append(cp)
        for i in range(n):
            pltpu.make_async_remote_copy(src_ref=buf_refs[i].at[c], dst_ref=buf_refs[i].at[1 - c],
                                         send_sem=send_sems.at[i], recv_sem=recv_sems.at[i],
                                         device_id=(x, y, 1 - c), device_id_type=MESH_ID).wait_recv()
        for cp in sends:
            cp.wait_send()

    return pl.pallas_call(
        body, name=name, in_specs=[_ANY] * n, out_specs=[_ANY] * n,
        out_shape=[jax.ShapeDtypeStruct(a.shape, a.dtype) for a in arrays],
        input_output_aliases={i: i for i in range(n)},
        scratch_shapes=[pltpu.SemaphoreType.DMA((n,)), pltpu.SemaphoreType.DMA((n,))],
    )(*arrays)


def _all_reduce_small(vec, name):
    rows, cols = vec.shape

    def body(v_ref, out_ref, buf, send_sems, recv_sems):
        x, y, c = _my_xyc()
        me = 4 * x + 2 * y + c
        buf[me] = v_ref[...]
        sends = []
        for kk in range(1, 8):
            peer = (1 - x if kk & 4 else x, 1 - y if kk & 2 else y, 1 - c if kk & 1 else c)
            cp = pltpu.make_async_remote_copy(src_ref=v_ref, dst_ref=buf.at[me], send_sem=send_sems.at[kk - 1],
                                              recv_sem=recv_sems.at[kk - 1], device_id=peer, device_id_type=MESH_ID)
            cp.start()
            sends.append(cp)
        for kk in range(1, 8):
            px, py, pc = (1 - x if kk & 4 else x, 1 - y if kk & 2 else y, 1 - c if kk & 1 else c)
            pltpu.make_async_remote_copy(src_ref=v_ref, dst_ref=buf.at[4 * px + 2 * py + pc],
                                         send_sem=send_sems.at[kk - 1], recv_sem=recv_sems.at[kk - 1],
                                         device_id=(px, py, pc), device_id_type=MESH_ID).wait_recv()
        for cp in sends:
            cp.wait_send()
        acc = buf[0]
        for j in range(1, 8):
            acc = acc + buf[j]
        out_ref[...] = acc

    vm = pl.BlockSpec(memory_space=pltpu.VMEM)
    return pl.pallas_call(
        body, name=name, in_specs=[vm], out_specs=vm, out_shape=jax.ShapeDtypeStruct((rows, cols), F32),
        scratch_shapes=[pltpu.VMEM((8, rows, cols), F32), pltpu.SemaphoreType.DMA((7,)), pltpu.SemaphoreType.DMA((7,))],
        compiler_params=pltpu.CompilerParams(vmem_limit_bytes=VMEM_LIMIT_BYTES),
    )(vec)


def _reduce_scatter(g):
    other = _sibling_swap_halves(g, "rs_swap_halves")
    s = _add_own_half(g, other, "rs_add_halves")
    recv = _chip_exchange(s, "rs_chip_exchange")
    r = _sum_leading(recv, "rs_sum_chips")
    (other_r,) = _sibling_swap([r], "rs_join_halves")
    return _by_core(r, other_r).reshape(g.shape[1], g.shape[2])


PACK_COLS = 1024
PACK_ROW_MULTIPLE = 1024
SHARDED = (("even_w_in", 2), ("conv_w", 2), ("even_w_out", 1), ("odd_w_in", 2), ("q_norm_w", 1), ("w_uq", 2),
           ("kv_norm_w", 1), ("w_ukv", 2), ("odd_w_out", 1), ("ffn_w_gate", 2), ("ffn_w_up", 2), ("ffn_w_down", 1))
KEEP_F32 = ("conv_w", "q_norm_w", "kv_norm_w")
REPLICATED = ("pool_w", "pool_scale", "conv_b", "dt_bias", "a_log", "d_skip", "ssm_norm_w", "fgate_b",
              "ln_mix_g", "ln_mix_b", "ln_ffn_g", "ln_ffn_b")


def _gather_weights(shards):
    sent = [shards[name] if name in KEEP_F32 else shards[name].astype(BF16) for name, _ in SHARDED]
    gathered = _sibling_fill(_chip_allgather_core_half(sent, "weights_allgather"), "weights_sibling_fill")
    out = {}
    for (name, axis), halves in zip(SHARDED, gathered):
        out[name] = jnp.concatenate([jnp.concatenate([halves[h, q] for q in range(4)], axis=axis) for h in range(2)],
                                    axis=0)
    return out


def _pack_grads(grads, shards):
    pieces = []
    for name, axis in SHARDED:
        width = shards[name].shape[axis]
        for g in grads[name]:
            ax = axis - 1
            split = g.reshape(g.shape[:ax] + (4, width) + g.shape[ax + 1:])
            pieces.append(jnp.moveaxis(split, ax, 0).reshape(4, -1))
    n = sum(p.shape[1] for p in pieces)
    per = PACK_COLS * PACK_ROW_MULTIPLE
    total = -(-n // per) * per
    pieces.append(jnp.zeros((4, total - n), F32))
    return jnp.concatenate(pieces, axis=1).reshape(4, total // PACK_COLS, PACK_COLS)


def _unpack_reduced(reduced, shards):
    flat = reduced.reshape(-1)
    out, off = {}, 0
    for name, _ in SHARDED:
        shp = shards[name].shape
        size = math.prod(shp)
        out[name] = flat[off:off + size].reshape(shp)
        off += size
    return out


def _to_heads(t, nh):
    rows, width = t.shape
    return t.reshape(rows, nh, width // nh).transpose(1, 0, 2)


def _from_heads(t):
    nh, rows, d = t.shape
    return t.transpose(1, 0, 2).reshape(rows, nh * d)


def _rope_tables(rows):
    half = MLA_ROPE // 2
    freqs = jnp.power(ROPE_THETA, -jnp.arange(half, dtype=F32) / half)
    ang = jnp.arange(rows, dtype=F32)[:, None] * freqs[None, :]
    cos, sin = jnp.cos(ang), jnp.sin(ang)
    cos_q, sin_q = jnp.tile(cos, (1, MLA_HEADS)), jnp.tile(sin, (1, MLA_HEADS))
    zeros = jnp.zeros((rows, LANES - MLA_ROPE), F32)
    cos_k = jnp.concatenate([cos, cos, zeros], axis=1)
    sin_k = jnp.concatenate([sin, sin, zeros], axis=1)
    r = lax.broadcasted_iota(jnp.int32, (LANES, LANES), 0)
    c = lax.broadcasted_iota(jnp.int32, (LANES, LANES), 1)
    rot = (jnp.where((r == c + half) & (c < half), -1.0, 0.0)
           + jnp.where((c == r + half) & (r < half), 1.0, 0.0)).astype(F32)
    return cos_q, sin_q, cos_k, sin_k, rot


def _pad_cols(a, width):
    return jnp.pad(a, ((0, 0), (0, width - a.shape[1])))


_EVEN_CUT = (512, 1536, 3072, 3088)


def _prep_even_w_in(w):
    c0, c1, c2, c3 = _EVEN_CUT
    return jnp.concatenate([w[:, c1:c2], w[:, :c0], w[:, c0:c1], _pad_cols(w[:, c2:c3], LANES)], axis=1)


def _unprep_even_w_in(g):
    c0, c1, c2, c3 = _EVEN_CUT
    n_xbc, n_u = c2 - c1, c0
    return jnp.concatenate([g[:, n_xbc:n_xbc + n_u], g[:, n_xbc + n_u:c2], g[:, :n_xbc], g[:, c2:c3]], axis=1)


_ODD_CUT = (1536, 1544, 2312, 2344)


def _prep_odd_w_in(w):
    c0, c1, c2, c3 = _ODD_CUT
    return jnp.concatenate([w[:, :c0], w[:, c1:c2], _pad_cols(w[:, c2:c3], LANES), _pad_cols(w[:, c0:c1], LANES)],
                           axis=1)


def _unprep_odd_w_in(g):
    c0, c1, c2, c3 = _ODD_CUT
    n1 = c0 + (c2 - c1)
    return jnp.concatenate([g[:, :c0], g[:, n1 + LANES:n1 + LANES + (c1 - c0)], g[:, c0:n1],
                            g[:, n1:n1 + (c3 - c2)]], axis=1)


def _prep_w_uq(w):
    r = w.reshape(w.shape[0], MLA_HEADS, MLA_NOPE + MLA_ROPE)
    half = MLA_ROPE // 2
    return jnp.concatenate([r[:, :, :MLA_NOPE].reshape(w.shape[0], -1),
                            r[:, :, MLA_NOPE:MLA_NOPE + half].reshape(w.shape[0], -1),
                            r[:, :, MLA_NOPE + half:].reshape(w.shape[0], -1)], axis=1)


def _unprep_w_uq(g):
    rows = g.shape[0]
    half = MLA_ROPE // 2
    nope = MLA_HEADS * MLA_NOPE
    return jnp.concatenate([g[:, :nope].reshape(rows, MLA_HEADS, MLA_NOPE),
                            g[:, nope:nope + LANES].reshape(rows, MLA_HEADS, half),
                            g[:, nope + LANES:].reshape(rows, MLA_HEADS, half)], axis=2).reshape(rows, -1)


def _row(v, width=None):
    v = v.reshape(1, -1)
    return v if width is None else _pad_cols(v, width)


def _even_forward(x_bf, w, i):
    rows = x_bf.shape[0]
    proj = _mm(x_bf, w["even_w_in"][i], "nn", F32, "even_proj")
    xbc, u, z, dtr = _Cols(proj, 0, 1536), _Cols(proj, 3, 512), _Cols(proj, 2, 1024), _Cols(proj, 24, LANES)
    n_row = rows // ROW_TILE
    pool_p = (w["pool_w"][i], _row(w["pool_scale"][i]))
    (y_pool,), pool_c = _scan_fwd(_f_pool, "pool_fwd", n_row, [u], [], pool_p, [], [(16, POOL_WIDTH)],
                                  [(POOL_WIDTH, BF16)])
    conv_p = (w["conv_w"][i], _row(w["conv_b"][i]))
    (xa,), conv_c = _scan_fwd(_f_conv, "conv_fwd", n_row, [xbc], [], conv_p, [], [(8, SSM_CONV_DIM)],
                              [(SSM_CONV_DIM, F32)])
    ssd_p = (_row(w["dt_bias"][i], LANES), _row(w["a_log"][i], LANES), _row(w["d_skip"][i], LANES),
             _row(w["ssm_norm_w"][i]))
    (y_ssm,), ssd_c = _scan_fwd(_f_ssd, "ssd_fwd", rows // SSM_CHUNK, [xa, dtr, z], [], ssd_p, [],
                                [(SSM_STATE, SSM_D_INNER)], [(SSM_D_INNER, BF16)])
    mix = jnp.concatenate([y_pool, y_ssm], axis=1)
    saved = dict(u=u, z=z, xbc=xbc, dtr=dtr, xa=xa, mix=mix, pool_p=pool_p, pool_c=pool_c, conv_p=conv_p,
                 conv_c=conv_c, ssd_p=ssd_p, ssd_c=ssd_c)
    return mix, saved


def _even_backward(dmix, x_bf, sv, w, i, d_r1):
    rows = x_bf.shape[0]
    n_row = rows // ROW_TILE
    dy_pool, dy_ssm = dmix[:, :POOL_WIDTH], dmix[:, POOL_WIDTH:]
    (du,), (g_pool_w, g_pool_scale) = _scan_bwd(_f_pool, "pool_bwd", n_row, [sv["u"]], [], sv["pool_p"], [],
                                                sv["pool_c"], [dy_pool])
    (dxa, ddtr, dz), (g_dt_bias, g_a_log, g_d_skip, g_norm_w) = _scan_bwd(
        _f_ssd, "ssd_bwd", rows // SSM_CHUNK, [sv["xa"], sv["dtr"], sv["z"]], [], sv["ssd_p"], [], sv["ssd_c"],
        [dy_ssm])
    (dxbc,), (g_conv_w, g_conv_b) = _scan_bwd(_f_conv, "conv_bwd", n_row, [sv["xbc"]], [], sv["conv_p"], [],
                                              sv["conv_c"], [dxa])
    dproj = jnp.concatenate([dxbc, du, dz, ddtr], axis=1).astype(BF16)
    g_w_in = _unprep_even_w_in(_mm(x_bf, dproj, "tn", F32, "even_dw_in"))
    dx = _mm(dproj, w["even_w_in"][i], "nt", F32, "even_dx", extra=d_r1, alpha=ALPHA)
    grads = dict(even_w_in=g_w_in, pool_w=g_pool_w, pool_scale=g_pool_scale[0], conv_w=g_conv_w, conv_b=g_conv_b[0],
                 dt_bias=g_dt_bias[0, :SSM_HEADS], a_log=g_a_log[0, :SSM_HEADS], d_skip=g_d_skip[0, :SSM_HEADS],
                 ssm_norm_w=g_norm_w[0])
    return dx, grads


def _odd_forward(x_bf, w, i, tables):
    rows = x_bf.shape[0]
    cos_q, sin_q, cos_k, sin_k, rot = tables
    proj = _mm(x_bf, w["odd_w_in"][i], "nn", F32, "odd_proj")
    qf, kf, vf = (_to_heads((proj[:, j * 512:(j + 1) * 512] * sc).astype(BF16), FOX_HEADS)
                  for j, sc in enumerate((FOX_SCALE, 1.0, 1.0)))
    cq, ckv = _Cols(proj, 3, MLA_Q_RANK), _Cols(proj, 8, MLA_KV_RANK)
    kr, fl = _Cols(proj, 18, LANES), _Cols(proj, 19, LANES)
    n_row = rows // ROW_TILE
    fox_p = (_row(w["fgate_b"][i], LANES),)
    n_gate = rows // min(ATTN_TILE, rows)
    (fcum,), fox_c = _scan_fwd(_f_fox_gate, "fox_gate_fwd", n_gate, [fl], [], fox_p, [], [(8, LANES)], [(LANES, F32)])
    fc_heads = fcum[:, :FOX_HEADS].T
    fq, fk = fc_heads[:, :, None], fc_heads[:, None, :]
    o_fox, lse_fox = _attn_fwd(qf, kf, vf, fq, fk, "fox_attn_fwd")

    prep_p = (_row(w["q_norm_w"][i]), _row(w["kv_norm_w"][i]))
    (cqn, ckvn, krr), _ = _scan_fwd(_f_mla_prep, "mla_prep_fwd", n_row, [cq, ckv, kr], [cos_k, sin_k], prep_p,
                                    [rot], [], [(MLA_Q_RANK, BF16), (MLA_KV_RANK, BF16), (LANES, BF16)])
    q_flat = _mm(cqn, w["w_uq"][i], "nn", F32, "mla_q_up")
    (q_rope,), _ = _scan_fwd(_f_rope_q, "rope_q_fwd", n_row, [q_flat], [cos_q, sin_q], [], [], [],
                             [(q_flat.shape[1], BF16)])
    kv = _mm(ckvn, w["w_ukv"][i], "nn", BF16, "mla_kv_up")
    nope = MLA_HEADS * MLA_NOPE
    half = MLA_ROPE // 2
    q_m = jnp.concatenate([_to_heads(q_rope[:, :nope], MLA_HEADS), _to_heads(q_rope[:, nope:nope + LANES], MLA_HEADS),
                           _to_heads(q_rope[:, nope + LANES:], MLA_HEADS)], axis=2)
    kv_h = _to_heads(kv, MLA_HEADS)
    k_rope = jnp.broadcast_to(krr[None, :, :MLA_ROPE], (MLA_HEADS, rows, MLA_ROPE))
    k_m = jnp.concatenate([kv_h[:, :, :MLA_NOPE], k_rope], axis=2)
    v_m = kv_h[:, :, MLA_NOPE:]
    o_mla, lse_mla = _attn_fwd(q_m, k_m, v_m, None, None, "mla_attn_fwd")
    mix = jnp.concatenate([_from_heads(o_fox), _from_heads(o_mla)], axis=1)
    saved = dict(qf=qf, kf=kf, vf=vf, fq=fq, fk=fk, o_fox=o_fox, lse_fox=lse_fox, fl=fl, fox_p=fox_p, fox_c=fox_c,
                 cq=cq, ckv=ckv, kr=kr, prep_p=prep_p, cqn=cqn, ckvn=ckvn, q_flat=q_flat, q_m=q_m, k_m=k_m, v_m=v_m,
                 o_mla=o_mla, lse_mla=lse_mla, mix=mix)
    return mix, saved


def _odd_backward(dmix, x_bf, sv, w, i, d_r1, tables):
    rows = x_bf.shape[0]
    cos_q, sin_q, cos_k, sin_k, rot = tables
    n_row = rows // ROW_TILE
    nope = MLA_HEADS * MLA_NOPE
    half = MLA_ROPE // 2
    do_fox = _to_heads(dmix[:, :FOX_WIDTH], FOX_HEADS)
    do_mla = _to_heads(dmix[:, FOX_WIDTH:], MLA_HEADS)
    fox_args = (sv["qf"], sv["kf"], sv["vf"], sv["o_fox"], do_fox, sv["lse_fox"], sv["fq"], sv["fk"])
    dqf, delta_fox = _attn_bwd_dq(*fox_args, "fox_attn_dq")
    dkf, dvf, dfk = _attn_bwd_dkv(*fox_args[:3], delta_fox, *fox_args[4:], "fox_attn_dkv")
    dfcum = _pad_cols(dfk[:, 0, :].T, LANES)
    n_gate = rows // min(ATTN_TILE, rows)
    (dfl,), (g_fb,) = _scan_bwd(_f_fox_gate, "fox_gate_bwd", n_gate, [sv["fl"]], [], sv["fox_p"], [], sv["fox_c"],
                                [dfcum])
    mla_args = (sv["q_m"], sv["k_m"], sv["v_m"], sv["o_mla"], do_mla, sv["lse_mla"], None, None)
    dq_m, dk_m, dv_m = _attn_bwd_fused(*mla_args[:6], "mla_attn_bwd")
    dq_rope = jnp.concatenate([_from_heads(dq_m[:, :, :MLA_NOPE]), _from_heads(dq_m[:, :, MLA_NOPE:MLA_NOPE + half]),
                               _from_heads(dq_m[:, :, MLA_NOPE + half:])], axis=1)
    (dq_flat,), _ = _scan_bwd(_f_rope_q, "rope_q_bwd", n_row, [sv["q_flat"]], [cos_q, sin_q], [], [], [], [dq_rope])
    g_w_uq = _mm(sv["cqn"], dq_flat, "tn", F32, "mla_dw_uq")
    dcqn = _mm(dq_flat, w["w_uq"][i], "nt", BF16, "mla_dcqn")
    dkv = _from_heads(jnp.concatenate([dk_m[:, :, :MLA_NOPE], dv_m], axis=2))
    g_w_ukv = _mm(sv["ckvn"], dkv, "tn", F32, "mla_dw_ukv")
    dckvn = _mm(dkv, w["w_ukv"][i], "nt", BF16, "mla_dckvn")
    dkrr = _head_sum(dk_m, "mla_dk_rope_sum")
    (dcq, dckv, dkr), (g_qw, g_kvw) = _scan_bwd(_f_mla_prep, "mla_prep_bwd", n_row, [sv["cq"], sv["ckv"], sv["kr"]],
                                                [cos_k, sin_k], sv["prep_p"], [rot], [], [dcqn, dckvn, dkrr])
    dproj = jnp.concatenate([_from_heads(dqf).astype(F32) * FOX_SCALE, _from_heads(dkf).astype(F32), _from_heads(dvf).astype(F32),
                             dcq, dckv, dkr, dfl], axis=1).astype(BF16)
    g_w_in = _mm(x_bf, dproj, "tn", F32, "odd_dw_in")
    dx = _mm(dproj, w["odd_w_in"][i], "nt", F32, "odd_dx", extra=d_r1, alpha=ALPHA)
    grads = dict(odd_w_in=_unprep_odd_w_in(g_w_in), fgate_b=g_fb[0, :FOX_HEADS], q_norm_w=g_qw[0], kv_norm_w=g_kvw[0],
                 w_uq=_unprep_w_uq(g_w_uq), w_ukv=g_w_ukv)
    return dx, grads


def _head_sum(dk_m, name):
    H, T, dk = dk_m.shape
    tt = _pick(T, (512, 256, 128))

    def body(d_ref, o_ref):
        acc = d_ref[0].astype(F32)
        for h in range(1, H):
            acc = acc + d_ref[h].astype(F32)
        o_ref[...] = jnp.concatenate([acc[:, MLA_NOPE:], jnp.zeros((tt, LANES - MLA_ROPE), F32)], axis=1).astype(BF16)

    return pl.pallas_call(
        body, name=name, grid=(T // tt,), in_specs=[pl.BlockSpec((H, tt, dk), lambda i: (0, i, 0))],
        out_specs=pl.BlockSpec((tt, LANES), lambda i: (i, 0)), out_shape=jax.ShapeDtypeStruct((T, LANES), BF16),
        compiler_params=_cparams(("parallel",)),
    )(dk_m)


def _local_step(x, target, w, small):
    rows = x.shape[0]
    n_row = rows // ROW_TILE
    tables = _rope_tables(rows)
    saved = []
    x_f32 = x
    x_bf = x.astype(BF16)
    for l in range(DEPTH):
        i = l // 2
        if l % 2 == 0:
            mix, sv = _even_forward(x_bf, w, i)
            w_out = w["even_w_out"][i]
        else:
            mix, sv = _odd_forward(x_bf, w, i, tables)
            w_out = w["odd_w_out"][i]
        r1 = _mm(mix, w_out, "nn", F32, "mix_out_even" if l % 2 == 0 else "mix_out_odd", extra=x_f32, alpha=ALPHA)
        ln1_p = (_row(small["ln_mix_g"][l]), _row(small["ln_mix_b"][l]))
        ln_outs = [(D_MODEL, F32), (D_MODEL, BF16)]
        (x_mid, x_mid_bf), _ = _scan_fwd(_f_ln_with_bf16, "ln_fwd", n_row, [r1], [], ln1_p, [], [], ln_outs)
        gu = _mm(x_mid_bf, w["ffn_w_gu"][l], "nn", BF16, "ffn_gu")
        (act,), _ = _scan_fwd(_f_act, "ffn_act_fwd", n_row, [gu], [], [], [], [], [(D_FF, BF16)])
        r2 = _mm(act, w["ffn_w_down"][l], "nn", F32, "ffn_down", extra=x_mid, alpha=ALPHA)
        ln2_p = (_row(small["ln_ffn_g"][l]), _row(small["ln_ffn_b"][l]))
        (x_out, x_out_bf), _ = _scan_fwd(_f_ln_with_bf16, "ln_fwd", n_row, [r2], [], ln2_p, [], [], ln_outs)
        saved.append(dict(sv=sv, x_bf=x_bf, r1=r1, ln1_p=ln1_p, x_mid_bf=x_mid_bf, gu=gu, act=act, r2=r2, ln2_p=ln2_p,
                          w_out=w_out))
        x_f32, x_bf = x_out, x_out_bf

    dy, loss_part = _loss_head(x_f32, target, "loss_head")
    loss = 0.5 * jnp.sum(loss_part) / D_MODEL

    layer_grads = []
    for l in reversed(range(DEPTH)):
        i = l // 2
        s = saved[l]
        (d_r2,), (g_ln2_g, g_ln2_b) = _scan_bwd(_f_ln, "ln_bwd", n_row, [s["r2"]], [], s["ln2_p"], [], [], [dy])
        g_down = _mm(s["act"], d_r2, "tn", F32, "ffn_dw_down")
        dact = _mm(d_r2, w["ffn_w_down"][l], "nt", BF16, "ffn_dact")
        (dgu,), _ = _scan_bwd(_f_act, "ffn_act_bwd", n_row, [s["gu"]], [], [], [], [], [dact])
        g_gu = _mm(s["x_mid_bf"], dgu, "tn", F32, "ffn_dw_gu")
        dx_mid = _mm(dgu, w["ffn_w_gu"][l], "nt", F32, "ffn_dx", extra=d_r2, alpha=ALPHA)
        (d_r1,), (g_ln1_g, g_ln1_b) = _scan_bwd(_f_ln, "ln_bwd", n_row, [s["r1"]], [], s["ln1_p"], [], [], [dx_mid])
        g_w_out = _mm(s["sv"]["mix"], d_r1, "tn", F32, "even_dw_out" if l % 2 == 0 else "odd_dw_out")
        dmix = _mm(d_r1, s["w_out"], "nt", BF16, "even_dmix" if l % 2 == 0 else "odd_dmix")
        if l % 2 == 0:
            dy, g = _even_backward(dmix, s["x_bf"], s["sv"], w, i, d_r1)
            g["even_w_out"] = g_w_out
        else:
            dy, g = _odd_backward(dmix, s["x_bf"], s["sv"], w, i, d_r1, tables)
            g["odd_w_out"] = g_w_out
        g.update(ffn_w_gate=g_gu[:, :D_FF], ffn_w_up=g_gu[:, D_FF:], ffn_w_down=g_down, ln_mix_g=g_ln1_g[0],
                 ln_mix_b=g_ln1_b[0], ln_ffn_g=g_ln2_g[0], ln_ffn_b=g_ln2_b[0])
        layer_grads.append((l, g))
    return loss, dy, layer_grads


EVEN_NAMES = ("even_w_in", "pool_w", "pool_scale", "conv_w", "conv_b", "dt_bias", "a_log", "d_skip", "ssm_norm_w",
              "even_w_out")
ODD_NAMES = ("odd_w_in", "fgate_b", "q_norm_w", "w_uq", "kv_norm_w", "w_ukv", "odd_w_out")
PER_LAYER_NAMES = ("ffn_w_gate", "ffn_w_up", "ffn_w_down", "ln_mix_g", "ln_mix_b", "ln_ffn_g", "ln_ffn_b")
WEIGHT_NAMES = EVEN_NAMES + ODD_NAMES + PER_LAYER_NAMES


def _grads_by_name(layer_grads):
    by_layer = dict(layer_grads)
    out = {}
    for n in EVEN_NAMES:
        out[n] = [by_layer[l][n] for l in range(0, DEPTH, 2)]
    for n in ODD_NAMES:
        out[n] = [by_layer[l][n] for l in range(1, DEPTH, 2)]
    for n in PER_LAYER_NAMES:
        out[n] = [by_layer[l][n] for l in range(DEPTH)]
    return out


def _prepare_weights(full):
    w = {}
    w["even_w_in"] = [_prep_even_w_in(full["even_w_in"][i]) for i in range(2)]
    w["even_w_out"] = [full["even_w_out"][i] for i in range(2)]
    w["odd_w_in"] = [_prep_odd_w_in(full["odd_w_in"][i]) for i in range(2)]
    w["w_uq"] = [_prep_w_uq(full["w_uq"][i]) for i in range(2)]
    w["w_ukv"] = [full["w_ukv"][i] for i in range(2)]
    w["odd_w_out"] = [full["odd_w_out"][i] for i in range(2)]
    w["ffn_w_gu"] = [jnp.concatenate([full["ffn_w_gate"][l], full["ffn_w_up"][l]], axis=1) for l in range(DEPTH)]
    w["ffn_w_down"] = [full["ffn_w_down"][l] for l in range(DEPTH)]
    for n in ("conv_w", "q_norm_w", "kv_norm_w"):
        w[n] = full[n]
    return w


def _flatten_small(grads):
    flat = jnp.concatenate([g.reshape(-1) for n in REPLICATED for g in grads[n]])
    n = flat.shape[0]
    per = LANES * 8
    total = -(-n // per) * per
    return jnp.pad(flat, (0, total - n)).reshape(total // LANES, LANES)


def _unflatten_small(mat, like):
    flat = mat.reshape(-1)
    out, off = {}, 0
    for n in REPLICATED:
        size = math.prod(like[n].shape)
        out[n] = flat[off:off + size].reshape(like[n].shape)
        off += size
    return out


def kernel(x, even_w_in, pool_w, pool_scale, conv_w, conv_b, dt_bias, a_log, d_skip, ssm_norm_w, even_w_out, odd_w_in, fgate_b, q_norm_w, w_uq, kv_norm_w, w_ukv, odd_w_out, ffn_w_gate, ffn_w_up, ffn_w_down, ln_mix_g, ln_mix_b, ln_ffn_g, ln_ffn_b, loss_target, m_even_w_in, m_pool_w, m_pool_scale, m_conv_w, m_conv_b, m_dt_bias, m_a_log, m_d_skip, m_ssm_norm_w, m_even_w_out, m_odd_w_in, m_fgate_b, m_q_norm_w, m_w_uq, m_kv_norm_w, m_w_ukv, m_odd_w_out, m_ffn_w_gate, m_ffn_w_up, m_ffn_w_down, m_ln_mix_g, m_ln_mix_b, m_ln_ffn_g, m_ln_ffn_b, v_even_w_in, v_pool_w, v_pool_scale, v_conv_w, v_conv_b, v_dt_bias, v_a_log, v_d_skip, v_ssm_norm_w, v_even_w_out, v_odd_w_in, v_fgate_b, v_q_norm_w, v_w_uq, v_kv_norm_w, v_w_ukv, v_odd_w_out, v_ffn_w_gate, v_ffn_w_up, v_ffn_w_down, v_ln_mix_g, v_ln_mix_b, v_ln_ffn_g, v_ln_ffn_b):
    weights = dict(even_w_in=even_w_in, pool_w=pool_w, pool_scale=pool_scale, conv_w=conv_w, conv_b=conv_b,
                   dt_bias=dt_bias, a_log=a_log, d_skip=d_skip, ssm_norm_w=ssm_norm_w, even_w_out=even_w_out,
                   odd_w_in=odd_w_in, fgate_b=fgate_b, q_norm_w=q_norm_w, w_uq=w_uq, kv_norm_w=kv_norm_w, w_ukv=w_ukv,
                   odd_w_out=odd_w_out, ffn_w_gate=ffn_w_gate, ffn_w_up=ffn_w_up, ffn_w_down=ffn_w_down,
                   ln_mix_g=ln_mix_g, ln_mix_b=ln_mix_b, ln_ffn_g=ln_ffn_g, ln_ffn_b=ln_ffn_b)
    m_in = dict(even_w_in=m_even_w_in, pool_w=m_pool_w, pool_scale=m_pool_scale, conv_w=m_conv_w, conv_b=m_conv_b,
                dt_bias=m_dt_bias, a_log=m_a_log, d_skip=m_d_skip, ssm_norm_w=m_ssm_norm_w, even_w_out=m_even_w_out,
                odd_w_in=m_odd_w_in, fgate_b=m_fgate_b, q_norm_w=m_q_norm_w, w_uq=m_w_uq, kv_norm_w=m_kv_norm_w,
                w_ukv=m_w_ukv, odd_w_out=m_odd_w_out, ffn_w_gate=m_ffn_w_gate, ffn_w_up=m_ffn_w_up,
                ffn_w_down=m_ffn_w_down, ln_mix_g=m_ln_mix_g, ln_mix_b=m_ln_mix_b, ln_ffn_g=m_ln_ffn_g,
                ln_ffn_b=m_ln_ffn_b)
    v_in = dict(even_w_in=v_even_w_in, pool_w=v_pool_w, pool_scale=v_pool_scale, conv_w=v_conv_w, conv_b=v_conv_b,
                dt_bias=v_dt_bias, a_log=v_a_log, d_skip=v_d_skip, ssm_norm_w=v_ssm_norm_w, even_w_out=v_even_w_out,
                odd_w_in=v_odd_w_in, fgate_b=v_fgate_b, q_norm_w=v_q_norm_w, w_uq=v_w_uq, kv_norm_w=v_kv_norm_w,
                w_ukv=v_w_ukv, odd_w_out=v_odd_w_out, ffn_w_gate=v_ffn_w_gate, ffn_w_up=v_ffn_w_up,
                ffn_w_down=v_ffn_w_down, ln_mix_g=v_ln_mix_g, ln_mix_b=v_ln_mix_b, ln_ffn_g=v_ln_ffn_g,
                ln_ffn_b=v_ln_ffn_b)
    shards = {n: weights[n] for n, _ in SHARDED}

    w = _prepare_weights(_gather_weights(shards))
    small = {n: weights[n] for n in REPLICATED}
    w.update(small)

    loss_local, dx, layer_grads = _local_step(x[0], loss_target[0], w, small)
    grads_full = _grads_by_name(layer_grads)

    reduced = _reduce_scatter(_pack_grads(grads_full, shards))
    grads = _unpack_reduced(reduced, shards)
    small_sum = _all_reduce_small(_flatten_small(grads_full), "small_grads_allreduce")
    grads.update(_unflatten_small(small_sum, small))
    loss = lax.psum(loss_local, ("x", "y", "c"))

    deltas, new_m, new_v = {}, {}, {}
    for n in WEIGHT_NAMES:
        deltas[n], new_m[n], new_v[n] = _adamw(weights[n], grads[n], m_in[n], v_in[n], "adamw_" + n)
    return (loss, dx[None], *[grads[n] for n in WEIGHT_NAMES], *[deltas[n] for n in WEIGHT_NAMES],
            *[new_m[n] for n in WEIGHT_NAMES], *[new_v[n] for n in WEIGHT_NAMES])
```

```python
import functools
import math

import numpy as np
import jax
import jax.numpy as jnp
from jax import lax
from jax.experimental import pallas as pl
from jax.experimental.pallas import tpu as pltpu

F32 = jnp.float32
BF16 = jnp.bfloat16
HI = lax.Precision.HIGHEST
MESH_ID = pl.DeviceIdType.MESH

VMEM_LIMIT_BYTES = 56 * 1024 * 1024
LANES = 128

D_MODEL = 1024
DEPTH = 4
POOL_WINDOWS = (2, 4, 8, 16)
POOL_GROUP = 128
POOL_WIDTH = 512
SSM_D_INNER = 1024
SSM_HEAD_DIM = 64
SSM_HEADS = 16
SSM_GROUPS = 2
SSM_STATE = 128
SSM_CONV = 4
SSM_CHUNK = 128
SSM_CONV_DIM = 1536
EVEN_IN = 3088
EVEN_IN_PAD = 3200
FOX_HEADS = 8
FOX_WIDTH = 512
MLA_HEADS = 8
MLA_NOPE = 64
MLA_ROPE = 32
MLA_V = 64
MLA_Q_RANK = 512
MLA_KV_RANK = 256
ROPE_THETA = 10000.0
ODD_IN = 2344
ODD_IN_PAD = 2560
D_FF = 2816
ALPHA = (2 * DEPTH) ** 0.25
LN_EPS = 1e-5
RMS_EPS = 1e-6
ADAM_LR = 0.001
ADAM_B1 = 0.9
ADAM_B2 = 0.999
ADAM_EPS = 1e-08
ADAM_WD = 0.01
ADAM_STEP = 10
NEG_BIG = -1e30

ATTN_TILE = 512
ATTN_WIDE = 2048
FOX_SCALE = 0.125
MLA_SCALE = (MLA_NOPE + MLA_ROPE) ** -0.5
ROW_TILE = 512
ACT_ROW_TILE = 256


def _cparams(sem=None):
    return pltpu.CompilerParams(dimension_semantics=sem, vmem_limit_bytes=VMEM_LIMIT_BYTES)


def _pick(d, prefs):
    for p in prefs:
        if d % p == 0:
            return p
    return d


_M_PREFS = (2048, 1024, 512, 640, 1408, 768, 384, 256, 128)
_N_PREFS = (1024, 1408, 512, 640, 768, 384, 256, 128)
_K_PREFS = (1024, 1408, 512, 640, 768, 256, 128)
MM_MAX_ACC_ELEMS = 1408 * 1024


def _mm(a, b, mode, out_dtype, name, extra=None, alpha=1.0):
    if mode == "nn":
        (m, k), (k2, n) = a.shape, b.shape
    elif mode == "nt":
        (m, k), (n, k2) = a.shape, b.shape
    else:
        (k, m), (k2, n) = a.shape, b.shape
    assert k == k2, (a.shape, b.shape, mode)
    tn, tk = _pick(n, _N_PREFS), _pick(k, _K_PREFS)
    tm = _pick(m, tuple(p for p in _M_PREFS if p * tn <= MM_MAX_ACC_ELEMS))
    nk = k // tk
    if mode == "nn":
        a_spec = pl.BlockSpec((tm, tk), lambda i, j, kk: (i, kk))
        b_spec = pl.BlockSpec((tk, tn), lambda i, j, kk: (kk, j))
        dims = (((1,), (0,)), ((), ()))
    elif mode == "nt":
        a_spec = pl.BlockSpec((tm, tk), lambda i, j, kk: (i, kk))
        b_spec = pl.BlockSpec((tn, tk), lambda i, j, kk: (j, kk))
        dims = (((1,), (1,)), ((), ()))
    else:
        a_spec = pl.BlockSpec((tk, tm), lambda i, j, kk: (kk, i))
        b_spec = pl.BlockSpec((tk, tn), lambda i, j, kk: (kk, j))
        dims = (((0,), (0,)), ((), ()))
    o_spec = pl.BlockSpec((tm, tn), lambda i, j, kk: (i, j))
    has_extra = extra is not None

    def body(*refs):
        if has_extra:
            a_ref, b_ref, e_ref, o_ref, acc = refs
        else:
            a_ref, b_ref, o_ref, acc = refs
        kk = pl.program_id(2)

        @pl.when(kk == 0)
        def _():
            acc[...] = jnp.zeros_like(acc)

        acc[...] += lax.dot_general(a_ref[...].astype(BF16), b_ref[...].astype(BF16), dims,
                                    preferred_element_type=F32)

        @pl.when(kk == nk - 1)
        def _():
            r = acc[...]
            if has_extra:
                r = r + alpha * e_ref[...].astype(F32)
            o_ref[...] = r.astype(o_ref.dtype)

    ins = [a, b] + ([extra] if has_extra else [])
    specs = [a_spec, b_spec] + ([o_spec] if has_extra else [])
    return pl.pallas_call(
        body, name=name, grid=(m // tm, n // tn, nk), in_specs=specs, out_specs=o_spec,
        out_shape=jax.ShapeDtypeStruct((m, n), out_dtype),
        scratch_shapes=[pltpu.VMEM((tm, tn), F32)],
        compiler_params=_cparams(("parallel", "parallel", "arbitrary")),
    )(*ins)


def _full_spec(shape):
    nd = len(shape)
    return pl.BlockSpec(tuple(shape), lambda i, _nd=nd: (0,) * _nd)


class _Cols:
    def __init__(self, arr, block, width):
        self.arr, self.block, self.width = arr, block, width
        self.shape, self.dtype = (arr.shape[0], width), arr.dtype


def _tile_arr(t):
    return t.arr if isinstance(t, _Cols) else t


def _tile_spec(t, tt, row_index):
    block = t.block if isinstance(t, _Cols) else 0
    return pl.BlockSpec((tt, t.shape[1]), lambda i: (row_index(i), block))


def _scan_fwd(f, name, n, tiles, ctiles, params, cparams, carry_shapes, out_defs):
    rows = tiles[0].shape[0]
    tt = rows // n
    nt, nct, npar, ncp, ncar, nout = len(tiles), len(ctiles), len(params), len(cparams), len(carry_shapes), len(out_defs)

    def body(*refs):
        pos = 0
        t_refs = refs[pos:pos + nt]; pos += nt
        ct_refs = refs[pos:pos + nct]; pos += nct
        p_refs = refs[pos:pos + npar]; pos += npar
        cp_refs = refs[pos:pos + ncp]; pos += ncp
        o_refs = refs[pos:pos + nout]; pos += nout
        cs_refs = refs[pos:pos + ncar]; pos += ncar
        c_scr = refs[pos:pos + ncar]
        i = pl.program_id(0)

        @pl.when(i == 0)
        def _():
            for c in c_scr:
                c[...] = jnp.zeros_like(c)

        carry = tuple(c[...] for c in c_scr)
        for s, c in zip(cs_refs, carry):
            s[0] = c
        new_carry, outs = f(carry, tuple(r[...] for r in t_refs), tuple(r[...] for r in ct_refs),
                            tuple(r[...] for r in p_refs), tuple(r[...] for r in cp_refs), i)
        for o_ref, o in zip(o_refs, outs):
            o_ref[...] = o
        for c, v in zip(c_scr, new_carry):
            c[...] = v

    tile_spec = lambda t: _tile_spec(t, tt, lambda i: i)
    in_specs = ([tile_spec(t) for t in tiles] + [tile_spec(t) for t in ctiles]
                + [_full_spec(p.shape) for p in params] + [_full_spec(p.shape) for p in cparams])
    out_specs = ([pl.BlockSpec((tt, c), lambda i: (i, 0)) for c, _ in out_defs]
                 + [pl.BlockSpec((1,) + tuple(s), lambda i: (i, 0, 0)) for s in carry_shapes])
    out_shape = ([jax.ShapeDtypeStruct((rows, c), dt) for c, dt in out_defs]
                 + [jax.ShapeDtypeStruct((n,) + tuple(s), F32) for s in carry_shapes])
    res = pl.pallas_call(
        body, name=name, grid=(n,), in_specs=in_specs, out_specs=out_specs, out_shape=out_shape,
        scratch_shapes=[pltpu.VMEM(tuple(s), F32) for s in carry_shapes],
        compiler_params=_cparams(("arbitrary",)),
    )(*map(_tile_arr, tiles), *map(_tile_arr, ctiles), *params, *cparams)
    return list(res[:nout]), list(res[nout:])


def _scan_bwd(f, name, n, tiles, ctiles, params, cparams, carries, douts):
    rows = tiles[0].shape[0]
    tt = rows // n
    nt, nct, npar, ncp, ncar, nout = len(tiles), len(ctiles), len(params), len(cparams), len(carries), len(douts)

    def body(*refs):
        pos = 0
        t_refs = refs[pos:pos + nt]; pos += nt
        ct_refs = refs[pos:pos + nct]; pos += nct
        p_refs = refs[pos:pos + npar]; pos += npar
        cp_refs = refs[pos:pos + ncp]; pos += ncp
        cs_refs = refs[pos:pos + ncar]; pos += ncar
        do_refs = refs[pos:pos + nout]; pos += nout
        dt_refs = refs[pos:pos + nt]; pos += nt
        dp_refs = refs[pos:pos + npar]; pos += npar
        dc_scr = refs[pos:pos + ncar]
        i = pl.program_id(0)

        @pl.when(i == 0)
        def _():
            for c in dc_scr:
                c[...] = jnp.zeros_like(c)
            for d in dp_refs:
                d[...] = jnp.zeros_like(d)

        ctv = tuple(r[...] for r in ct_refs)
        cpv = tuple(r[...] for r in cp_refs)

        def g(c, t, p):
            return f(c, t, ctv, p, cpv, n - 1 - i)

        _, vjp = jax.vjp(g, tuple(s[0] for s in cs_refs), tuple(r[...] for r in t_refs),
                         tuple(r[...] for r in p_refs))
        dc, dt, dp = vjp((tuple(c[...] for c in dc_scr), tuple(r[...] for r in do_refs)))
        for r, v in zip(dt_refs, dt):
            r[...] = v
        for r, v in zip(dp_refs, dp):
            r[...] += v
        for c, v in zip(dc_scr, dc):
            c[...] = v

    rev_tile = lambda t: _tile_spec(t, tt, lambda i: n - 1 - i)
    rev_out = lambda t: pl.BlockSpec((tt, t.shape[1]), lambda i: (n - 1 - i, 0))
    in_specs = ([rev_tile(t) for t in tiles] + [rev_tile(t) for t in ctiles]
                + [_full_spec(p.shape) for p in params] + [_full_spec(p.shape) for p in cparams]
                + [pl.BlockSpec((1,) + tuple(c.shape[1:]), lambda i: (n - 1 - i, 0, 0)) for c in carries]
                + [rev_tile(d) for d in douts])
    out_specs = [rev_out(t) for t in tiles] + [_full_spec(p.shape) for p in params]
    out_shape = ([jax.ShapeDtypeStruct(t.shape, t.dtype) for t in tiles]
                 + [jax.ShapeDtypeStruct(p.shape, F32) for p in params])
    res = pl.pallas_call(
        body, name=name, grid=(n,), in_specs=in_specs, out_specs=out_specs, out_shape=out_shape,
        scratch_shapes=[pltpu.VMEM(tuple(c.shape[1:]), F32) for c in carries],
        compiler_params=_cparams(("arbitrary",)),
    )(*map(_tile_arr, tiles), *map(_tile_arr, ctiles), *params, *cparams, *carries, *map(_tile_arr, douts))
    return list(res[:nt]), list(res[nt:])


def _silu(x):
    return x * jax.nn.sigmoid(x)


def _softplus(x):
    return jnp.maximum(x, 0.0) + jnp.log(1.0 + jnp.exp(-jnp.abs(x)))


def _f_ln(carry, tiles, ctiles, params, cparams, idx):
    (r,), (g, b) = tiles, params
    mu = jnp.mean(r, axis=-1, keepdims=True)
    xc = r - mu
    var = jnp.mean(xc * xc, axis=-1, keepdims=True)
    return (), (xc * lax.rsqrt(var + LN_EPS) * g + b,)


def _f_ln_with_bf16(carry, tiles, ctiles, params, cparams, idx):
    _, (y,) = _f_ln(carry, tiles, ctiles, params, cparams, idx)
    return (), (y, y.astype(BF16))


def _f_act(carry, tiles, ctiles, params, cparams, idx):
    (gu,) = tiles
    g = gu[:, :D_FF].astype(F32)
    u = gu[:, D_FF:].astype(F32)
    return (), ((_silu(g) * u).astype(BF16),)


def _f_pool(carry, tiles, ctiles, params, cparams, idx):
    (prev,), (u,), (pool_w, pool_scale) = carry, tiles, params
    tt = u.shape[0]
    halo = prev.shape[0]
    ext = jnp.concatenate([prev, u], axis=0)
    pos = idx * tt + lax.broadcasted_iota(jnp.int32, (tt, 1), 0)
    ys = []
    for g, w in enumerate(POOL_WINDOWS):
        lo, hi = g * POOL_GROUP, (g + 1) * POOL_GROUP
        eg = ext[:, lo:hi]
        s = eg[halo:halo + tt]
        for j in range(1, w):
            s = s + eg[halo - j:halo - j + tt]
        count = jnp.minimum(pos + 1, w).astype(F32)
        diff = s / count - u[:, lo:hi]
        ys.append(jnp.dot(diff.astype(BF16), pool_w[g].astype(BF16), preferred_element_type=F32))
    y = jnp.concatenate(ys, axis=1) * pool_scale
    return (u[tt - halo:, :],), (y.astype(BF16),)


def _f_conv(carry, tiles, ctiles, params, cparams, idx):
    (prev,), (xbc,), (conv_w, conv_b) = carry, tiles, params
    tt = xbc.shape[0]
    halo = prev.shape[0]
    ext = jnp.concatenate([prev, xbc], axis=0)
    acc = jnp.zeros_like(xbc) + conv_b
    for k in range(SSM_CONV):
        off = halo - (SSM_CONV - 1) + k
        acc = acc + ext[off:off + tt] * conv_w[k:k + 1, :]
    return (xbc[tt - halo:, :],), (_silu(acc),)


def _head_expand_matrix():
    r = lax.broadcasted_iota(jnp.int32, (LANES, SSM_D_INNER), 0)
    c = lax.broadcasted_iota(jnp.int32, (LANES, SSM_D_INNER), 1)
    return ((c >= r * SSM_HEAD_DIM) & (c < (r + 1) * SSM_HEAD_DIM)).astype(BF16)


def _head_expand(v, e):
    hi = v.astype(BF16)
    lo = (v - hi.astype(F32)).astype(BF16)
    return jnp.dot(hi, e, preferred_element_type=F32) + jnp.dot(lo, e, preferred_element_type=F32)


def _f_ssd(carry, tiles, ctiles, params, cparams, idx):
    (state,), (xa, dtr, z), (dt_bias, a_log, d_skip, norm_w) = carry, tiles, params
    L, P, N, E = SSM_CHUNK, SSM_HEAD_DIM, SSM_STATE, SSM_HEADS // SSM_GROUPS
    gw = E * P
    em = _head_expand_matrix()
    dt = _softplus(dtr + dt_bias)
    da = dt * (-jnp.exp(a_log))
    r = lax.broadcasted_iota(jnp.int32, (L, L), 0)
    c = lax.broadcasted_iota(jnp.int32, (L, L), 1)
    tri = c <= r
    acs = jnp.dot(tri.astype(F32), da, precision=HI, preferred_element_type=F32)
    acs_t = acs.T
    last = acs[L - 1:L, :]
    xs = xa[:, :SSM_D_INNER]
    xdt = xs * _head_expand(dt, em)
    xdt_b = xdt.astype(BF16)
    xdec = (xdt * _head_expand(jnp.exp(last - acs), em)).astype(BF16)
    eacs = _head_expand(jnp.exp(acs), em)
    y_parts, st_parts = [], []
    for g in range(SSM_GROUPS):
        bg = xa[:, SSM_D_INNER + g * N:SSM_D_INNER + (g + 1) * N].astype(BF16)
        cg = xa[:, SSM_D_INNER + (SSM_GROUPS + g) * N:SSM_D_INNER + (SSM_GROUPS + g + 1) * N].astype(BF16)
        cb = lax.dot_general(cg, bg, (((1,), (1,)), ((), ())), preferred_element_type=F32)
        prev_g = state[:, g * gw:(g + 1) * gw]
        y_off = jnp.dot(cg, prev_g.astype(BF16), preferred_element_type=F32) * eacs[:, g * gw:(g + 1) * gw]
        st_parts.append(lax.dot_general(bg, xdec[:, g * gw:(g + 1) * gw], (((0,), (0,)), ((), ())),
                                        preferred_element_type=F32))
        diag = []
        for e in range(E):
            h = g * E + e
            lmat = jnp.exp(jnp.where(tri, acs[:, h:h + 1] - acs_t[h:h + 1, :], NEG_BIG))
            diag.append(jnp.dot((cb * lmat).astype(BF16), xdt_b[:, h * P:(h + 1) * P], preferred_element_type=F32))
        y_parts.append(jnp.concatenate(diag, axis=1) + y_off)
    y = jnp.concatenate(y_parts, axis=1) + xs * _head_expand(d_skip, em)
    new_state = state * _head_expand(jnp.exp(last), em) + jnp.concatenate(st_parts, axis=1)
    y = y * _silu(z)
    y = y * lax.rsqrt(jnp.mean(y * y, axis=-1, keepdims=True) + RMS_EPS) * norm_w
    return (new_state,), (y.astype(BF16),)


def _f_fox_gate(carry, tiles, ctiles, params, cparams, idx):
    (run,), (fl,), (fb,) = carry, tiles, params
    tt = fl.shape[0]
    logf = -_softplus(-(fl + fb))
    r = lax.broadcasted_iota(jnp.int32, (tt, tt), 0)
    c = lax.broadcasted_iota(jnp.int32, (tt, tt), 1)
    cum = jnp.dot((c <= r).astype(F32), logf, precision=HI, preferred_element_type=F32) + run[0:1, :]
    return (jnp.broadcast_to(cum[tt - 1:tt, :], run.shape),), (cum,)


def _rms(x, w):
    return x * lax.rsqrt(jnp.mean(x * x, axis=-1, keepdims=True) + RMS_EPS) * w


def _f_mla_prep(carry, tiles, ctiles, params, cparams, idx):
    (cq, ckv, kr), (ck, sk), (qw, kvw), (rk,) = tiles, ctiles, params, cparams
    rot = jnp.dot(kr, rk, precision=HI, preferred_element_type=F32)
    return (), (_rms(cq, qw).astype(BF16), _rms(ckv, kvw).astype(BF16), (kr * ck + rot * sk).astype(BF16))


def _f_rope_q(carry, tiles, ctiles, params, cparams, idx):
    (q,), (cos, sin) = tiles, ctiles
    nope = MLA_HEADS * MLA_NOPE
    x1 = q[:, nope:nope + LANES]
    x2 = q[:, nope + LANES:]
    roped = jnp.concatenate([q[:, :nope], x1 * cos - x2 * sin, x2 * cos + x1 * sin], axis=1)
    return (), ((roped * MLA_SCALE).astype(BF16),)


def _scores(q, k, fq, fk, masked, row0, col0):
    s = lax.dot_general(q, k, (((1,), (1,)), ((), ())), preferred_element_type=F32)
    if fq is not None:
        s = s + fq - fk
    if masked:
        rows = row0 + lax.broadcasted_iota(jnp.int32, s.shape, 0)
        cols = col0 + lax.broadcasted_iota(jnp.int32, s.shape, 1)
        s = jnp.where(cols <= rows, s, NEG_BIG)
    return s


def _q_major_tables(T, tq, tk):
    r = tk // tq
    qi = np.concatenate([np.full(i // r + 1, i, np.int32) for i in range(T // tq)])
    ki = np.concatenate([np.arange(i // r + 1, dtype=np.int32) for i in range(T // tq)])
    kind = np.where(ki == qi // r, qi % r + 1, 0).astype(np.int32)
    return [jnp.asarray(a) for a in (qi, ki, (ki == 0).astype(np.int32), kind)]


def _k_major_tables(T, tq, tk):
    r = tq // tk
    nq = T // tq
    ki = np.concatenate([np.full(nq - i // r, i, np.int32) for i in range(T // tk)])
    qi = np.concatenate([np.arange(i // r, nq, dtype=np.int32) for i in range(T // tk)])
    kind = np.where(qi == ki // r, r - ki % r, 0).astype(np.int32)
    return [jnp.asarray(a) for a in (ki, qi, (qi == nq - 1).astype(np.int32), kind)]


def _attn_tiles(T):
    return min(ATTN_TILE, T), min(ATTN_WIDE, T)


def _attn_fwd(q, k, v, fq, fk, name):
    H, T, dk = q.shape
    dv = v.shape[2]
    tq, tk = _attn_tiles(T)
    decay = fq is not None
    tables = _q_major_tables(T, tq, tk)

    def body(qi_ref, ki_ref, first_ref, kind_ref, *refs):
        if decay:
            q_ref, k_ref, v_ref, fq_ref, fk_ref, o_ref, lse_ref, m_s, l_s, acc = refs
        else:
            q_ref, k_ref, v_ref, o_ref, lse_ref, m_s, l_s, acc = refs
        t = pl.program_id(1)
        qi, ki = qi_ref[t], ki_ref[t]

        @pl.when(first_ref[t] == 1)
        def _():
            m_s[...] = jnp.full_like(m_s, NEG_BIG)
            l_s[...] = jnp.zeros_like(l_s)
            acc[...] = jnp.zeros_like(acc)

        def step(kind):
            w = tk if kind == 0 else kind * tq
            s = _scores(q_ref[0], k_ref[0, :w, :], fq_ref[0] if decay else None,
                        fk_ref[0, :, :w] if decay else None, kind > 0, qi * tq, ki * tk)
            m_new = jnp.maximum(m_s[...], jnp.max(s, axis=-1, keepdims=True))
            p = jnp.exp(s - m_new)
            corr = jnp.exp(m_s[...] - m_new)
            l_s[...] = corr * l_s[...] + jnp.sum(p, axis=-1, keepdims=True)
            acc[...] = corr * acc[...] + jnp.dot(p.astype(BF16), v_ref[0, :w, :], preferred_element_type=F32)
            m_s[...] = m_new
            if kind > 0:
                o_ref[0] = (acc[...] / l_s[...]).astype(o_ref.dtype)
                lse_ref[0] = m_s[...] + jnp.log(l_s[...])

        for kind in range(tk // tq + 1):
            pl.when(kind_ref[t] == kind)(functools.partial(step, kind))

    qspec = lambda d: pl.BlockSpec((1, tq, d), lambda h, t, qi, ki, fi, la: (h, qi[t], 0))
    kspec = lambda d: pl.BlockSpec((1, tk, d), lambda h, t, qi, ki, fi, la: (h, ki[t], 0))
    in_specs = [qspec(dk), kspec(dk), kspec(dv)]
    ins = [q, k, v]
    if decay:
        in_specs += [qspec(1), pl.BlockSpec((1, 1, tk), lambda h, t, qi, ki, fi, la: (h, 0, ki[t]))]
        ins += [fq, fk]
    grid_spec = pltpu.PrefetchScalarGridSpec(
        num_scalar_prefetch=4, grid=(H, int(tables[0].shape[0])), in_specs=in_specs,
        out_specs=[qspec(dv), qspec(1)],
        scratch_shapes=[pltpu.VMEM((tq, 1), F32), pltpu.VMEM((tq, 1), F32), pltpu.VMEM((tq, dv), F32)])
    return pl.pallas_call(
        body, name=name, grid_spec=grid_spec,
        out_shape=[jax.ShapeDtypeStruct((H, T, dv), BF16), jax.ShapeDtypeStruct((H, T, 1), F32)],
        compiler_params=_cparams(("parallel", "arbitrary")),
    )(*tables, *ins)


def _attn_bwd_dq(q, k, v, o, do, lse, fq, fk, name):
    H, T, dk = q.shape
    dv = v.shape[2]
    tq, tk = _attn_tiles(T)
    decay = fq is not None
    tables = _q_major_tables(T, tq, tk)

    def body(qi_ref, ki_ref, first_ref, kind_ref, *refs):
        if decay:
            q_ref, k_ref, v_ref, o_ref, do_ref, lse_ref, fq_ref, fk_ref, dq_ref, dl_ref, acc, dl, leak = refs
        else:
            q_ref, k_ref, v_ref, o_ref, do_ref, lse_ref, dq_ref, dl_ref, acc, dl, leak = refs
        t = pl.program_id(1)
        qi, ki = qi_ref[t], ki_ref[t]

        @pl.when(first_ref[t] == 1)
        def _():
            acc[...] = jnp.zeros_like(acc)
            leak[...] = jnp.zeros_like(leak)
            dl[...] = jnp.sum(do_ref[0].astype(F32) * o_ref[0].astype(F32), axis=-1, keepdims=True)

        def step(kind):
            w = tk if kind == 0 else kind * tq
            k_v = k_ref[0, :w, :]
            s = _scores(q_ref[0], k_v, fq_ref[0] if decay else None, fk_ref[0, :, :w] if decay else None,
                        kind > 0, qi * tq, ki * tk)
            p = jnp.exp(s - lse_ref[0])
            dp = lax.dot_general(do_ref[0], v_ref[0, :w, :], (((1,), (1,)), ((), ())), preferred_element_type=F32)
            ds = p * (dp - dl[...])
            leak[...] += jnp.sum(ds, axis=-1, keepdims=True)
            acc[...] += jnp.dot(ds.astype(BF16), k_v, preferred_element_type=F32)
            if kind > 0:
                dq_ref[0] = acc[...].astype(dq_ref.dtype)
                dl_ref[0] = dl[...] + leak[...]

        for kind in range(tk // tq + 1):
            pl.when(kind_ref[t] == kind)(functools.partial(step, kind))

    qspec = lambda d: pl.BlockSpec((1, tq, d), lambda h, t, qi, ki, fi, la: (h, qi[t], 0))
    kspec = lambda d: pl.BlockSpec((1, tk, d), lambda h, t, qi, ki, fi, la: (h, ki[t], 0))
    in_specs = [qspec(dk), kspec(dk), kspec(dv), qspec(dv), qspec(dv), qspec(1)]
    ins = [q, k, v, o, do, lse]
    if decay:
        in_specs += [qspec(1), pl.BlockSpec((1, 1, tk), lambda h, t, qi, ki, fi, la: (h, 0, ki[t]))]
        ins += [fq, fk]
    grid_spec = pltpu.PrefetchScalarGridSpec(
        num_scalar_prefetch=4, grid=(H, int(tables[0].shape[0])), in_specs=in_specs,
        out_specs=[qspec(dk), qspec(1)],
        scratch_shapes=[pltpu.VMEM((tq, dk), F32), pltpu.VMEM((tq, 1), F32), pltpu.VMEM((tq, 1), F32)])
    return pl.pallas_call(
        body, name=name, grid_spec=grid_spec,
        out_shape=[jax.ShapeDtypeStruct((H, T, dk), BF16), jax.ShapeDtypeStruct((H, T, 1), F32)],
        compiler_params=_cparams(("parallel", "arbitrary")),
    )(*tables, *ins)


def _attn_bwd_dkv(q, k, v, delta, do, lse, fq, fk, name):
    H, T, dk = q.shape
    dv = v.shape[2]
    tk, tq = _attn_tiles(T)
    decay = fq is not None
    tables = _k_major_tables(T, tq, tk)

    def body(ki_ref, qi_ref, last_ref, kind_ref, *refs):
        if decay:
            q_ref, k_ref, v_ref, dl_ref, do_ref, lse_ref, fq_ref, fk_ref, dk_ref, dv_ref, df_ref, dk_s, dv_s, df_s = refs
        else:
            q_ref, k_ref, v_ref, dl_ref, do_ref, lse_ref, dk_ref, dv_ref, dk_s, dv_s = refs
        t = pl.program_id(1)
        ki, qi = ki_ref[t], qi_ref[t]

        @pl.when(kind_ref[t] > 0)
        def _():
            dk_s[...] = jnp.zeros_like(dk_s)
            dv_s[...] = jnp.zeros_like(dv_s)
            if decay:
                df_s[...] = jnp.zeros_like(df_s)

        def step(kind):
            off = 0 if kind == 0 else tq - kind * tk
            q_v, do_v = q_ref[0, off:, :], do_ref[0, off:, :]
            s = _scores(q_v, k_ref[0], fq_ref[0, off:, :] if decay else None, fk_ref[0] if decay else None,
                        kind > 0, qi * tq + off, ki * tk)
            p = jnp.exp(s - lse_ref[0, off:, :])
            dv_s[...] += lax.dot_general(p.astype(BF16), do_v, (((0,), (0,)), ((), ())), preferred_element_type=F32)
            dp = lax.dot_general(do_v, v_ref[0], (((1,), (1,)), ((), ())), preferred_element_type=F32)
            ds = p * (dp - dl_ref[0, off:, :])
            dk_s[...] += lax.dot_general(ds.astype(BF16), q_v, (((0,), (0,)), ((), ())),
                                         preferred_element_type=F32)
            if decay:
                df_s[...] -= jnp.sum(ds, axis=0, keepdims=True)

        for kind in range(tq // tk + 1):
            pl.when(kind_ref[t] == kind)(functools.partial(step, kind))

        @pl.when(last_ref[t] == 1)
        def _():
            dk_ref[0] = dk_s[...].astype(dk_ref.dtype)
            dv_ref[0] = dv_s[...].astype(dv_ref.dtype)
            if decay:
                df_ref[0] = df_s[...]

    qspec = lambda d: pl.BlockSpec((1, tq, d), lambda h, t, ki, qi, fi, la: (h, qi[t], 0))
    kspec = lambda d: pl.BlockSpec((1, tk, d), lambda h, t, ki, qi, fi, la: (h, ki[t], 0))
    in_specs = [qspec(dk), kspec(dk), kspec(dv), qspec(1), qspec(dv), qspec(1)]
    ins = [q, k, v, delta, do, lse]
    out_specs = [kspec(dk), kspec(dv)]
    out_shape = [jax.ShapeDtypeStruct((H, T, dk), BF16), jax.ShapeDtypeStruct((H, T, dv), BF16)]
    scratch = [pltpu.VMEM((tk, dk), F32), pltpu.VMEM((tk, dv), F32)]
    if decay:
        fkspec = pl.BlockSpec((1, 1, tk), lambda h, t, ki, qi, fi, la: (h, 0, ki[t]))
        in_specs += [qspec(1), fkspec]
        ins += [fq, fk]
        out_specs.append(fkspec)
        out_shape.append(jax.ShapeDtypeStruct((H, 1, T), F32))
        scratch.append(pltpu.VMEM((1, tk), F32))
    grid_spec = pltpu.PrefetchScalarGridSpec(
        num_scalar_prefetch=4, grid=(H, int(tables[0].shape[0])), in_specs=in_specs, out_specs=out_specs,
        scratch_shapes=scratch)
    return pl.pallas_call(
        body, name=name, grid_spec=grid_spec, out_shape=out_shape, compiler_params=_cparams(("parallel", "arbitrary")),
    )(*tables, *ins)


def _attn_bwd_fused(q, k, v, o, do, lse, name):
    H, T, dk = q.shape
    dv = v.shape[2]
    tk, tq = _attn_tiles(T)
    tables = _k_major_tables(T, tq, tk)
    npairs = int(tables[0].shape[0])

    def body(ki_ref, qi_ref, last_ref, kind_ref, q_ref, k_ref, v_ref, o_ref, do_ref, lse_ref,
             dq_ref, dk_ref, dv_ref, dq_s, dk_s, dv_s):
        t = pl.program_id(1)
        ki, qi = ki_ref[t], qi_ref[t]

        @pl.when(t == 0)
        def _():
            dq_s[...] = jnp.zeros_like(dq_s)

        @pl.when(kind_ref[t] > 0)
        def _():
            dk_s[...] = jnp.zeros_like(dk_s)
            dv_s[...] = jnp.zeros_like(dv_s)

        def step(kind):
            off = 0 if kind == 0 else tq - kind * tk
            q_v, do_v, k_v = q_ref[0, off:, :], do_ref[0, off:, :], k_ref[0]
            s = _scores(q_v, k_v, None, None, kind > 0, qi * tq + off, ki * tk)
            p = jnp.exp(s - lse_ref[0, off:, :])
            delta = jnp.sum(do_v.astype(F32) * o_ref[0, off:, :].astype(F32), axis=-1, keepdims=True)
            dv_s[...] += lax.dot_general(p.astype(BF16), do_v, (((0,), (0,)), ((), ())), preferred_element_type=F32)
            dp = lax.dot_general(do_v, v_ref[0], (((1,), (1,)), ((), ())), preferred_element_type=F32)
            ds = (p * (dp - delta)).astype(BF16)
            dk_s[...] += lax.dot_general(ds, q_v, (((0,), (0,)), ((), ())), preferred_element_type=F32)
            rows = pl.ds(pl.multiple_of(qi * tq + off, tk), tq - off)
            dq_s[rows, :] += jnp.dot(ds, k_v, preferred_element_type=F32)

        for kind in range(tq // tk + 1):
            pl.when(kind_ref[t] == kind)(functools.partial(step, kind))

        @pl.when(last_ref[t] == 1)
        def _():
            dk_ref[0] = dk_s[...].astype(dk_ref.dtype)
            dv_ref[0] = dv_s[...].astype(dv_ref.dtype)

        @pl.when(t == npairs - 1)
        def _():
            dq_ref[0] = dq_s[...].astype(dq_ref.dtype)

    qspec = lambda d: pl.BlockSpec((1, tq, d), lambda h, t, ki, qi, la, kd: (h, qi[t], 0))
    kspec = lambda d: pl.BlockSpec((1, tk, d), lambda h, t, ki, qi, la, kd: (h, ki[t], 0))
    grid_spec = pltpu.PrefetchScalarGridSpec(
        num_scalar_prefetch=4, grid=(H, npairs),
        in_specs=[qspec(dk), kspec(dk), kspec(dv), qspec(dv), qspec(dv), qspec(1)],
        out_specs=[pl.BlockSpec((1, T, dk), lambda h, t, ki, qi, la, kd: (h, 0, 0)), kspec(dk), kspec(dv)],
        scratch_shapes=[pltpu.VMEM((T, dk), F32), pltpu.VMEM((tk, dk), F32), pltpu.VMEM((tk, dv), F32)])
    return pl.pallas_call(
        body, name=name, grid_spec=grid_spec,
        out_shape=[jax.ShapeDtypeStruct((H, T, dk), BF16), jax.ShapeDtypeStruct((H, T, dk), BF16),
                   jax.ShapeDtypeStruct((H, T, dv), BF16)],
        compiler_params=_cparams(("parallel", "arbitrary")),
    )(*tables, q, k, v, o, do, lse)


def _loss_head(y, target, name):
    rows, d = y.shape
    tt = _pick(rows, (512, 256, 128))

    def body(y_ref, t_ref, dy_ref, part_ref):
        @pl.when(pl.program_id(0) == 0)
        def _():
            part_ref[...] = jnp.zeros_like(part_ref)

        err = y_ref[...] - t_ref[...]
        dy_ref[...] = err * (1.0 / d)
        sq = jnp.sum(err * err, axis=0, keepdims=True)
        folded = sq[:, :LANES]
        for j in range(1, d // LANES):
            folded = folded + sq[:, j * LANES:(j + 1) * LANES]
        part_ref[...] += folded

    spec = pl.BlockSpec((tt, d), lambda i: (i, 0))
    return pl.pallas_call(
        body, name=name, grid=(rows // tt,), in_specs=[spec, spec],
        out_specs=[spec, pl.BlockSpec((1, LANES), lambda i: (0, 0))],
        out_shape=[jax.ShapeDtypeStruct((rows, d), F32), jax.ShapeDtypeStruct((1, LANES), F32)],
        compiler_params=_cparams(("arbitrary",)),
    )(y, target)


def _adamw(w, g, m, v, name):
    shape = w.shape
    cols = shape[-1]
    rows = math.prod(shape[:-1])
    w2, g2, m2, v2 = (a.reshape(rows, cols) for a in (w, g, m, v))
    tr = _pick(rows, (512, 256, 128, 64, 32, 16, 8))
    c1 = 1.0 / (1.0 - ADAM_B1 ** ADAM_STEP)
    c2 = 1.0 / (1.0 - ADAM_B2 ** ADAM_STEP)

    def body(w_ref, g_ref, m_ref, v_ref, d_ref, nm_ref, nv_ref):
        gv = g_ref[...]
        nm = ADAM_B1 * m_ref[...] + (1.0 - ADAM_B1) * gv
        nv = ADAM_B2 * v_ref[...] + (1.0 - ADAM_B2) * gv * gv
        d_ref[...] = -ADAM_LR * ((nm * c1) / (jnp.sqrt(nv * c2) + ADAM_EPS) + ADAM_WD * w_ref[...])
        nm_ref[...] = nm
        nv_ref[...] = nv

    spec = pl.BlockSpec((tr, cols), lambda i: (i, 0))
    sds = jax.ShapeDtypeStruct((rows, cols), F32)
    d, nm, nv = pl.pallas_call(
        body, name=name, grid=(rows // tr,), in_specs=[spec] * 4, out_specs=[spec] * 3, out_shape=[sds] * 3,
        compiler_params=_cparams(("parallel",)),
    )(w2, g2, m2, v2)
    return d.reshape(shape), nm.reshape(shape), nv.reshape(shape)


_ANY = pl.BlockSpec(memory_space=pl.ANY)


def _my_xyc():
    return lax.axis_index("x"), lax.axis_index("y"), lax.axis_index("c")


def _chip_allgather_core_half(shards, name):
    n = len(shards)
    half = [s.shape[0] // 2 for s in shards]

    def body(*refs):
        full_refs, whole_refs = refs[:n], refs[n:2 * n]
        send_sems, recv_sems, loc_sems = refs[2 * n:]
        x, y, c = _my_xyc()
        myq = 2 * x + y
        x_refs = [full_refs[i].at[pl.ds(c * half[i], half[i])] for i in range(n)]
        out_refs = [whole_refs[i].at[c] for i in range(n)]
        chips = [(1 - x, y), (x, 1 - y), (1 - x, 1 - y)]
        started = []
        for i in range(n):
            local = pltpu.make_async_copy(x_refs[i], out_refs[i].at[myq], loc_sems.at[i])
            local.start()
            started.append(local)
        sends = []
        for j, (px, py) in enumerate(chips):
            for i in range(n):
                cp = pltpu.make_async_remote_copy(src_ref=x_refs[i], dst_ref=out_refs[i].at[myq],
                                                  send_sem=send_sems.at[3 * i + j], recv_sem=recv_sems.at[3 * i + j],
                                                  device_id=(px, py, c), device_id_type=MESH_ID)
                cp.start()
                sends.append(cp)
        for j, (px, py) in enumerate(chips):
            for i in range(n):
                pltpu.make_async_remote_copy(src_ref=x_refs[i], dst_ref=out_refs[i].at[2 * px + py],
                                             send_sem=send_sems.at[3 * i + j], recv_sem=recv_sems.at[3 * i + j],
                                             device_id=(px, py, c), device_id_type=MESH_ID).wait_recv()
        for cp in sends:
            cp.wait_send()
        for local in started:
            local.wait()

    return pl.pallas_call(
        body, name=name, in_specs=[_ANY] * n, out_specs=[_ANY] * n,
        out_shape=[jax.ShapeDtypeStruct((2, 4, h) + s.shape[1:], s.dtype) for s, h in zip(shards, half)],
        scratch_shapes=[pltpu.SemaphoreType.DMA((3 * n,)), pltpu.SemaphoreType.DMA((3 * n,)),
                        pltpu.SemaphoreType.DMA((n,))],
    )(*shards)


def _sibling_swap_halves(g, name):
    q, rows, cols = g.shape
    rh = rows // 2

    def body(g_ref, out_ref, send_sem, recv_sem):
        x, y, c = _my_xyc()
        src = g_ref.at[:, pl.ds((1 - c) * rh, rh), :]
        cp = pltpu.make_async_remote_copy(src_ref=src, dst_ref=out_ref, send_sem=send_sem, recv_sem=recv_sem,
                                          device_id=(x, y, 1 - c), device_id_type=MESH_ID)
        cp.start()
        cp.wait()

    return pl.pallas_call(
        body, name=name, in_specs=[_ANY], out_specs=_ANY,
        out_shape=jax.ShapeDtypeStruct((q, rh, cols), g.dtype),
        scratch_shapes=[pltpu.SemaphoreType.DMA, pltpu.SemaphoreType.DMA],
    )(g)


def _add_own_half(g, other, name):
    q, rows, cols = g.shape
    rh = rows // 2
    tr = _pick(rh, (512, 256, 128, 64, 32, 16, 8))
    nb = rh // tr
    cidx = lax.axis_index("c").astype(jnp.int32).reshape(1)

    def body(c_ref, g_ref, o_ref, out_ref):
        out_ref[...] = (g_ref[...] + o_ref[...]).astype(out_ref.dtype)

    grid_spec = pltpu.PrefetchScalarGridSpec(
        num_scalar_prefetch=1, grid=(q, nb),
        in_specs=[pl.BlockSpec((1, tr, cols), lambda a, i, c_ref: (a, c_ref[0] * nb + i, 0)),
                  pl.BlockSpec((1, tr, cols), lambda a, i, c_ref: (a, i, 0))],
        out_specs=pl.BlockSpec((1, tr, cols), lambda a, i, c_ref: (a, i, 0)))
    return pl.pallas_call(
        body, name=name, grid_spec=grid_spec, out_shape=jax.ShapeDtypeStruct((q, rh, cols), BF16),
        compiler_params=_cparams(("parallel", "parallel")),
    )(cidx, g, other)


def _chip_exchange(s, name):
    def body(s_ref, out_ref, send_sems, recv_sems, loc_sem):
        x, y, c = _my_xyc()
        myq = 2 * x + y
        local = pltpu.make_async_copy(s_ref.at[myq], out_ref.at[myq], loc_sem)
        local.start()
        chips = [(1 - x, y), (x, 1 - y), (1 - x, 1 - y)]
        sends = []
        for j, (px, py) in enumerate(chips):
            cp = pltpu.make_async_remote_copy(src_ref=s_ref.at[2 * px + py], dst_ref=out_ref.at[myq],
                                              send_sem=send_sems.at[j], recv_sem=recv_sems.at[j],
                                              device_id=(px, py, c), device_id_type=MESH_ID)
            cp.start()
            sends.append(cp)
        for j, (px, py) in enumerate(chips):
            pltpu.make_async_remote_copy(src_ref=s_ref.at[myq], dst_ref=out_ref.at[2 * px + py],
                                         send_sem=send_sems.at[j], recv_sem=recv_sems.at[j],
                                         device_id=(px, py, c), device_id_type=MESH_ID).wait_recv()
        for cp in sends:
            cp.wait_send()
        local.wait()

    return pl.pallas_call(
        body, name=name, in_specs=[_ANY], out_specs=_ANY, out_shape=jax.ShapeDtypeStruct(s.shape, s.dtype),
        scratch_shapes=[pltpu.SemaphoreType.DMA((3,)), pltpu.SemaphoreType.DMA((3,)), pltpu.SemaphoreType.DMA],
    )(s)


def _sum_leading(a, name):
    q, rows, cols = a.shape
    tr = _pick(rows, (512, 256, 128, 64, 32, 16, 8))

    def body(a_ref, out_ref):
        acc = a_ref[0].astype(F32)
        for j in range(1, q):
            acc = acc + a_ref[j].astype(F32)
        out_ref[...] = acc

    return pl.pallas_call(
        body, name=name, grid=(rows // tr,), in_specs=[pl.BlockSpec((q, tr, cols), lambda i: (0, i, 0))],
        out_specs=pl.BlockSpec((tr, cols), lambda i: (i, 0)), out_shape=jax.ShapeDtypeStruct((rows, cols), F32),
        compiler_params=_cparams(("parallel",)),
    )(a)


def _sibling_swap(arrays, name):
    n = len(arrays)

    def body(*refs):
        a_refs, out_refs = refs[:n], refs[n:2 * n]
        send_sems, recv_sems = refs[2 * n:]
        x, y, c = _my_xyc()
        copies = []
        for i in range(n):
            cp = pltpu.make_async_remote_copy(src_ref=a_refs[i], dst_ref=out_refs[i], send_sem=send_sems.at[i],
                                              recv_sem=recv_sems.at[i], device_id=(x, y, 1 - c),
                                              device_id_type=MESH_ID)
            cp.start()
            copies.append(cp)
        for cp in copies:
            cp.wait()

    return pl.pallas_call(
        body, name=name, in_specs=[_ANY] * n, out_specs=[_ANY] * n,
        out_shape=[jax.ShapeDtypeStruct(a.shape, a.dtype) for a in arrays],
        scratch_shapes=[pltpu.SemaphoreType.DMA((n,)), pltpu.SemaphoreType.DMA((n,))],
    )(*arrays)


def _by_core(mine, other):
    c = lax.axis_index("c")
    return jnp.where(c == 0, jnp.stack([mine, other]), jnp.stack([other, mine]))


def _sibling_fill(arrays, name):
    n = len(arrays)

    def body(*refs):
        buf_refs = refs[n:2 * n]
        send_sems, recv_sems = refs[2 * n:]
        x, y, c = _my_xyc()
        sends = []
        for i in range(n):
            cp = pltpu.make_async_remote_copy(src_ref=buf_refs[i].at[c], dst_ref=buf_refs[i].at[c],
                                              send_sem=send_sems.at[i], recv_sem=recv_sems.at[i],
                                              device_id=(x, y, 1 - c), device_id_type=MESH_ID)
            cp.start()
            sends.append(cp)
        for i in range(n):
            pltpu.make_async_remote_copy(src_ref=buf_refs[i].at[c], dst_ref=buf_refs[i].at[1 - c],
                                         send_sem=send_sems.at[i], recv_sem=recv_sems.at[i],
                                         device_id=(x, y, 1 - c), device_id_type=MESH_ID).wait_recv()
        for cp in sends:
            cp.wait_send()

    return pl.pallas_call(
        body, name=name, in_specs=[_ANY] * n, out_specs=[_ANY] * n,
        out_shape=[jax.ShapeDtypeStruct(a.shape, a.dtype) for a in arrays],
        input_output_aliases={i: i for i in range(n)},
        scratch_shapes=[pltpu.SemaphoreType.DMA((n,)), pltpu.SemaphoreType.DMA((n,))],
    )(*arrays)


def _all_reduce_small(vec, name):
    rows, cols = vec.shape

    def body(v_ref, out_ref, buf, send_sems, recv_sems):
        x, y, c = _my_xyc()
        me = 4 * x + 2 * y + c
        buf[me] = v_ref[...]
        sends = []
        for kk in range(1, 8):
            peer = (1 - x if kk & 4 else x, 1 - y if kk & 2 else y, 1 - c if kk & 1 else c)
            cp = pltpu.make_async_remote_copy(src_ref=v_ref, dst_ref=buf.at[me], send_sem=send_sems.at[kk - 1],
                                              recv_sem=recv_sems.at[kk - 1], device_id=peer, device_id_type=MESH_ID)
            cp.start()
            sends.append(cp)
        for kk in range(1, 8):
            px, py, pc = (1 - x if kk & 4 else x, 1 - y if kk & 2 else y, 1 - c if kk & 1 else c)
            pltpu.make_async_remote_copy(src_ref=v_ref, dst_ref=buf.at[4 * px + 2 * py + pc],
                                         send_sem=send_sems.at[kk - 1], recv_sem=recv_sems.at[kk - 1],
                                         device_id=(px, py, pc), device_id_type=MESH_ID).wait_recv()
        for cp in sends:
            cp.wait_send()
        acc = buf[0]
        for j in range(1, 8):
            acc = acc + buf[j]
        out_ref[...] = acc

    vm = pl.BlockSpec(memory_space=pltpu.VMEM)
    return pl.pallas_call(
        body, name=name, in_specs=[vm], out_specs=vm, out_shape=jax.ShapeDtypeStruct((rows, cols), F32),
        scratch_shapes=[pltpu.VMEM((8, rows, cols), F32), pltpu.SemaphoreType.DMA((7,)), pltpu.SemaphoreType.DMA((7,))],
        compiler_params=pltpu.CompilerParams(vmem_limit_bytes=VMEM_LIMIT_BYTES),
    )(vec)


def _reduce_scatter(g):
    other = _sibling_swap_halves(g, "rs_swap_halves")
    s = _add_own_half(g, other, "rs_add_halves")
    recv = _chip_exchange(s, "rs_chip_exchange")
    r = _sum_leading(recv, "rs_sum_chips")
    (other_r,) = _sibling_swap([r], "rs_join_halves")
    return _by_core(r, other_r).reshape(g.shape[1], g.shape[2])


PACK_COLS = 1024
PACK_ROW_MULTIPLE = 1024
SHARDED = (("even_w_in", 2), ("conv_w", 2), ("even_w_out", 1), ("odd_w_in", 2), ("q_norm_w", 1), ("w_uq", 2),
           ("kv_norm_w", 1), ("w_ukv", 2), ("odd_w_out", 1), ("ffn_w_gate", 2), ("ffn_w_up", 2), ("ffn_w_down", 1))
KEEP_F32 = ("conv_w", "q_norm_w", "kv_norm_w")
REPLICATED = ("pool_w", "pool_scale", "conv_b", "dt_bias", "a_log", "d_skip", "ssm_norm_w", "fgate_b",
              "ln_mix_g", "ln_mix_b", "ln_ffn_g", "ln_ffn_b")


def _gather_weights(shards):
    sent = [shards[name] if name in KEEP_F32 else shards[name].astype(BF16) for name, _ in SHARDED]
    gathered = _sibling_fill(_chip_allgather_core_half(sent, "weights_allgather"), "weights_sibling_fill")
    out = {}
    for (name, axis), halves in zip(SHARDED, gathered):
        out[name] = jnp.concatenate([jnp.concatenate([halves[h, q] for q in range(4)], axis=axis) for h in range(2)],
                                    axis=0)
    return out


def _pack_grads(grads, shards):
    pieces = []
    for name, axis in SHARDED:
        width = shards[name].shape[axis]
        for g in grads[name]:
            ax = axis - 1
            split = g.reshape(g.shape[:ax] + (4, width) + g.shape[ax + 1:])
            pieces.append(jnp.moveaxis(split, ax, 0).reshape(4, -1))
    n = sum(p.shape[1] for p in pieces)
    per = PACK_COLS * PACK_ROW_MULTIPLE
    total = -(-n // per) * per
    pieces.append(jnp.zeros((4, total - n), F32))
    return jnp.concatenate(pieces, axis=1).reshape(4, total // PACK_COLS, PACK_COLS)


def _unpack_reduced(reduced, shards):
    flat = reduced.reshape(-1)
    out, off = {}, 0
    for name, _ in SHARDED:
        shp = shards[name].shape
        size = math.prod(shp)
        out[name] = flat[off:off + size].reshape(shp)
        off += size
    return out


def _to_heads(t, nh):
    rows, width = t.shape
    return t.reshape(rows, nh, width // nh).transpose(1, 0, 2)


def _from_heads(t):
    nh, rows, d = t.shape
    return t.transpose(1, 0, 2).reshape(rows, nh * d)


def _rope_tables(rows):
    half = MLA_ROPE // 2
    freqs = jnp.power(ROPE_THETA, -jnp.arange(half, dtype=F32) / half)
    ang = jnp.arange(rows, dtype=F32)[:, None] * freqs[None, :]
    cos, sin = jnp.cos(ang), jnp.sin(ang)
    cos_q, sin_q = jnp.tile(cos, (1, MLA_HEADS)), jnp.tile(sin, (1, MLA_HEADS))
    zeros = jnp.zeros((rows, LANES - MLA_ROPE), F32)
    cos_k = jnp.concatenate([cos, cos, zeros], axis=1)
    sin_k = jnp.concatenate([sin, sin, zeros], axis=1)
    r = lax.broadcasted_iota(jnp.int32, (LANES, LANES), 0)
    c = lax.broadcasted_iota(jnp.int32, (LANES, LANES), 1)
    rot = (jnp.where((r == c + half) & (c < half), -1.0, 0.0)
           + jnp.where((c == r + half) & (r < half), 1.0, 0.0)).astype(F32)
    return cos_q, sin_q, cos_k, sin_k, rot


def _pad_cols(a, width):
    return jnp.pad(a, ((0, 0), (0, width - a.shape[1])))


_EVEN_CUT = (512, 1536, 3072, 3088)


def _prep_even_w_in(w):
    c0, c1, c2, c3 = _EVEN_CUT
    return jnp.concatenate([w[:, c1:c2], w[:, :c0], w[:, c0:c1], _pad_cols(w[:, c2:c3], LANES)], axis=1)


def _unprep_even_w_in(g):
    c0, c1, c2, c3 = _EVEN_CUT
    n_xbc, n_u = c2 - c1, c0
    return jnp.concatenate([g[:, n_xbc:n_xbc + n_u], g[:, n_xbc + n_u:c2], g[:, :n_xbc], g[:, c2:c3]], axis=1)


_ODD_CUT = (1536, 1544, 2312, 2344)


def _prep_odd_w_in(w):
    c0, c1, c2, c3 = _ODD_CUT
    return jnp.concatenate([w[:, :c0], w[:, c1:c2], _pad_cols(w[:, c2:c3], LANES), _pad_cols(w[:, c0:c1], LANES)],
                           axis=1)


def _unprep_odd_w_in(g):
    c0, c1, c2, c3 = _ODD_CUT
    n1 = c0 + (c2 - c1)
    return jnp.concatenate([g[:, :c0], g[:, n1 + LANES:n1 + LANES + (c1 - c0)], g[:, c0:n1],
                            g[:, n1:n1 + (c3 - c2)]], axis=1)


def _prep_w_uq(w):
    r = w.reshape(w.shape[0], MLA_HEADS, MLA_NOPE + MLA_ROPE)
    half = MLA_ROPE // 2
    return jnp.concatenate([r[:, :, :MLA_NOPE].reshape(w.shape[0], -1),
                            r[:, :, MLA_NOPE:MLA_NOPE + half].reshape(w.shape[0], -1),
                            r[:, :, MLA_NOPE + half:].reshape(w.shape[0], -1)], axis=1)


def _unprep_w_uq(g):
    rows = g.shape[0]
    half = MLA_ROPE // 2
    nope = MLA_HEADS * MLA_NOPE
    return jnp.concatenate([g[:, :nope].reshape(rows, MLA_HEADS, MLA_NOPE),
                            g[:, nope:nope + LANES].reshape(rows, MLA_HEADS, half),
                            g[:, nope + LANES:].reshape(rows, MLA_HEADS, half)], axis=2).reshape(rows, -1)


def _row(v, width=None):
    v = v.reshape(1, -1)
    return v if width is None else _pad_cols(v, width)


def _even_forward(x_bf, w, i):
    rows = x_bf.shape[0]
    proj = _mm(x_bf, w["even_w_in"][i], "nn", F32, "even_proj")
    xbc, u, z, dtr = _Cols(proj, 0, 1536), _Cols(proj, 3, 512), _Cols(proj, 2, 1024), _Cols(proj, 24, LANES)
    n_row = rows // ROW_TILE
    pool_p = (w["pool_w"][i], _row(w["pool_scale"][i]))
    (y_pool,), pool_c = _scan_fwd(_f_pool, "pool_fwd", n_row, [u], [], pool_p, [], [(16, POOL_WIDTH)],
                                  [(POOL_WIDTH, BF16)])
    conv_p = (w["conv_w"][i], _row(w["conv_b"][i]))
    (xa,), conv_c = _scan_fwd(_f_conv, "conv_fwd", rows // min(ACT_ROW_TILE, rows), [xbc], [], conv_p, [],
                              [(8, SSM_CONV_DIM)],
                              [(SSM_CONV_DIM, F32)])
    ssd_p = (_row(w["dt_bias"][i], LANES), _row(w["a_log"][i], LANES), _row(w["d_skip"][i], LANES),
             _row(w["ssm_norm_w"][i]))
    (y_ssm,), ssd_c = _scan_fwd(_f_ssd, "ssd_fwd", rows // SSM_CHUNK, [xa, dtr, z], [], ssd_p, [],
                                [(SSM_STATE, SSM_D_INNER)], [(SSM_D_INNER, BF16)])
    mix = jnp.concatenate([y_pool, y_ssm], axis=1)
    saved = dict(u=u, z=z, xbc=xbc, dtr=dtr, xa=xa, mix=mix, pool_p=pool_p, pool_c=pool_c, conv_p=conv_p,
                 conv_c=conv_c, ssd_p=ssd_p, ssd_c=ssd_c)
    return mix, saved


def _even_backward(dmix, x_bf, sv, w, i, d_r1):
    rows = x_bf.shape[0]
    n_row = rows // ROW_TILE
    dy_pool, dy_ssm = dmix[:, :POOL_WIDTH], dmix[:, POOL_WIDTH:]
    (du,), (g_pool_w, g_pool_scale) = _scan_bwd(_f_pool, "pool_bwd", n_row, [sv["u"]], [], sv["pool_p"], [],
                                                sv["pool_c"], [dy_pool])
    (dxa, ddtr, dz), (g_dt_bias, g_a_log, g_d_skip, g_norm_w) = _scan_bwd(
        _f_ssd, "ssd_bwd", rows // SSM_CHUNK, [sv["xa"], sv["dtr"], sv["z"]], [], sv["ssd_p"], [], sv["ssd_c"],
        [dy_ssm])
    (dxbc,), (g_conv_w, g_conv_b) = _scan_bwd(_f_conv, "conv_bwd", rows // min(ACT_ROW_TILE, rows), [sv["xbc"]],
                                              [], sv["conv_p"], [],
                                              sv["conv_c"], [dxa])
    dproj = jnp.concatenate([dxbc, du, dz, ddtr], axis=1).astype(BF16)
    g_w_in = _unprep_even_w_in(_mm(x_bf, dproj, "tn", F32, "even_dw_in"))
    dx = _mm(dproj, w["even_w_in"][i], "nt", F32, "even_dx", extra=d_r1, alpha=ALPHA)
    grads = dict(even_w_in=g_w_in, pool_w=g_pool_w, pool_scale=g_pool_scale[0], conv_w=g_conv_w, conv_b=g_conv_b[0],
                 dt_bias=g_dt_bias[0, :SSM_HEADS], a_log=g_a_log[0, :SSM_HEADS], d_skip=g_d_skip[0, :SSM_HEADS],
                 ssm_norm_w=g_norm_w[0])
    return dx, grads


def _odd_forward(x_bf, w, i, tables):
    rows = x_bf.shape[0]
    cos_q, sin_q, cos_k, sin_k, rot = tables
    proj = _mm(x_bf, w["odd_w_in"][i], "nn", F32, "odd_proj")
    qf, kf, vf = (_to_heads((proj[:, j * 512:(j + 1) * 512] * sc).astype(BF16), FOX_HEADS)
                  for j, sc in enumerate((FOX_SCALE, 1.0, 1.0)))
    cq, ckv = _Cols(proj, 3, MLA_Q_RANK), _Cols(proj, 8, MLA_KV_RANK)
    kr, fl = _Cols(proj, 18, LANES), _Cols(proj, 19, LANES)
    n_row = rows // ROW_TILE
    fox_p = (_row(w["fgate_b"][i], LANES),)
    n_gate = rows // min(ATTN_TILE, rows)
    (fcum,), fox_c = _scan_fwd(_f_fox_gate, "fox_gate_fwd", n_gate, [fl], [], fox_p, [], [(8, LANES)], [(LANES, F32)])
    fc_heads = fcum[:, :FOX_HEADS].T
    fq, fk = fc_heads[:, :, None], fc_heads[:, None, :]
    o_fox, lse_fox = _attn_fwd(qf, kf, vf, fq, fk, "fox_attn_fwd")

    prep_p = (_row(w["q_norm_w"][i]), _row(w["kv_norm_w"][i]))
    (cqn, ckvn, krr), _ = _scan_fwd(_f_mla_prep, "mla_prep_fwd", n_row, [cq, ckv, kr], [cos_k, sin_k], prep_p,
                                    [rot], [], [(MLA_Q_RANK, BF16), (MLA_KV_RANK, BF16), (LANES, BF16)])
    q_flat = _mm(cqn, w["w_uq"][i], "nn", F32, "mla_q_up")
    (q_rope,), _ = _scan_fwd(_f_rope_q, "rope_q_fwd", n_row, [q_flat], [cos_q, sin_q], [], [], [],
                             [(q_flat.shape[1], BF16)])
    kv = _mm(ckvn, w["w_ukv"][i], "nn", BF16, "mla_kv_up")
    nope = MLA_HEADS * MLA_NOPE
    half = MLA_ROPE // 2
    q_m = jnp.concatenate([_to_heads(q_rope[:, :nope], MLA_HEADS), _to_heads(q_rope[:, nope:nope + LANES], MLA_HEADS),
                           _to_heads(q_rope[:, nope + LANES:], MLA_HEADS)], axis=2)
    kv_h = _to_heads(kv, MLA_HEADS)
    k_rope = jnp.broadcast_to(krr[None, :, :MLA_ROPE], (MLA_HEADS, rows, MLA_ROPE))
    k_m = jnp.concatenate([kv_h[:, :, :MLA_NOPE], k_rope], axis=2)
    v_m = kv_h[:, :, MLA_NOPE:]
    o_mla, lse_mla = _attn_fwd(q_m, k_m, v_m, None, None, "mla_attn_fwd")
    mix = jnp.concatenate([_from_heads(o_fox), _from_heads(o_mla)], axis=1)
    saved = dict(qf=qf, kf=kf, vf=vf, fq=fq, fk=fk, o_fox=o_fox, lse_fox=lse_fox, fl=fl, fox_p=fox_p, fox_c=fox_c,
                 cq=cq, ckv=ckv, kr=kr, prep_p=prep_p, cqn=cqn, ckvn=ckvn, q_flat=q_flat, q_m=q_m, k_m=k_m, v_m=v_m,
                 o_mla=o_mla, lse_mla=lse_mla, mix=mix)
    return mix, saved


def _odd_backward(dmix, x_bf, sv, w, i, d_r1, tables):
    rows = x_bf.shape[0]
    cos_q, sin_q, cos_k, sin_k, rot = tables
    n_row = rows // ROW_TILE
    nope = MLA_HEADS * MLA_NOPE
    half = MLA_ROPE // 2
    do_fox = _to_heads(dmix[:, :FOX_WIDTH], FOX_HEADS)
    do_mla = _to_heads(dmix[:, FOX_WIDTH:], MLA_HEADS)
    fox_args = (sv["qf"], sv["kf"], sv["vf"], sv["o_fox"], do_fox, sv["lse_fox"], sv["fq"], sv["fk"])
    dqf, delta_fox = _attn_bwd_dq(*fox_args, "fox_attn_dq")
    dkf, dvf, dfk = _attn_bwd_dkv(*fox_args[:3], delta_fox, *fox_args[4:], "fox_attn_dkv")
    dfcum = _pad_cols(dfk[:, 0, :].T, LANES)
    n_gate = rows // min(ATTN_TILE, rows)
    (dfl,), (g_fb,) = _scan_bwd(_f_fox_gate, "fox_gate_bwd", n_gate, [sv["fl"]], [], sv["fox_p"], [], sv["fox_c"],
                                [dfcum])
    mla_args = (sv["q_m"], sv["k_m"], sv["v_m"], sv["o_mla"], do_mla, sv["lse_mla"], None, None)
    dq_m, dk_m, dv_m = _attn_bwd_fused(*mla_args[:6], "mla_attn_bwd")
    dq_rope = jnp.concatenate([_from_heads(dq_m[:, :, :MLA_NOPE]), _from_heads(dq_m[:, :, MLA_NOPE:MLA_NOPE + half]),
                               _from_heads(dq_m[:, :, MLA_NOPE + half:])], axis=1)
    (dq_flat,), _ = _scan_bwd(_f_rope_q, "rope_q_bwd", n_row, [sv["q_flat"]], [cos_q, sin_q], [], [], [], [dq_rope])
    g_w_uq = _mm(sv["cqn"], dq_flat, "tn", F32, "mla_dw_uq")
    dcqn = _mm(dq_flat, w["w_uq"][i], "nt", BF16, "mla_dcqn")
    dkv = _from_heads(jnp.concatenate([dk_m[:, :, :MLA_NOPE], dv_m], axis=2))
    g_w_ukv = _mm(sv["ckvn"], dkv, "tn", F32, "mla_dw_ukv")
    dckvn = _mm(dkv, w["w_ukv"][i], "nt", BF16, "mla_dckvn")
    dkrr = _head_sum(dk_m, "mla_dk_rope_sum")
    (dcq, dckv, dkr), (g_qw, g_kvw) = _scan_bwd(_f_mla_prep, "mla_prep_bwd", n_row, [sv["cq"], sv["ckv"], sv["kr"]],
                                                [cos_k, sin_k], sv["prep_p"], [rot], [], [dcqn, dckvn, dkrr])
    dproj = jnp.concatenate([_from_heads(dqf).astype(F32) * FOX_SCALE, _from_heads(dkf).astype(F32), _from_heads(dvf).astype(F32),
                             dcq, dckv, dkr, dfl], axis=1).astype(BF16)
    g_w_in = _mm(x_bf, dproj, "tn", F32, "odd_dw_in")
    dx = _mm(dproj, w["odd_w_in"][i], "nt", F32, "odd_dx", extra=d_r1, alpha=ALPHA)
    grads = dict(odd_w_in=_unprep_odd_w_in(g_w_in), fgate_b=g_fb[0, :FOX_HEADS], q_norm_w=g_qw[0], kv_norm_w=g_kvw[0],
                 w_uq=_unprep_w_uq(g_w_uq), w_ukv=g_w_ukv)
    return dx, grads


def _head_sum(dk_m, name):
    H, T, dk = dk_m.shape
    tt = _pick(T, (512, 256, 128))

    def body(d_ref, o_ref):
        acc = d_ref[0].astype(F32)
        for h in range(1, H):
            acc = acc + d_ref[h].astype(F32)
        o_ref[...] = jnp.concatenate([acc[:, MLA_NOPE:], jnp.zeros((tt, LANES - MLA_ROPE), F32)], axis=1).astype(BF16)

    return pl.pallas_call(
        body, name=name, grid=(T // tt,), in_specs=[pl.BlockSpec((H, tt, dk), lambda i: (0, i, 0))],
        out_specs=pl.BlockSpec((tt, LANES), lambda i: (i, 0)), out_shape=jax.ShapeDtypeStruct((T, LANES), BF16),
        compiler_params=_cparams(("parallel",)),
    )(dk_m)


def _local_step(x, target, w, small):
    rows = x.shape[0]
    n_row = rows // ROW_TILE
    tables = _rope_tables(rows)
    saved = []
    x_f32 = x
    x_bf = x.astype(BF16)
    for l in range(DEPTH):
        i = l // 2
        if l % 2 == 0:
            mix, sv = _even_forward(x_bf, w, i)
            w_out = w["even_w_out"][i]
        else:
            mix, sv = _odd_forward(x_bf, w, i, tables)
            w_out = w["odd_w_out"][i]
        r1 = _mm(mix, w_out, "nn", F32, "mix_out_even" if l % 2 == 0 else "mix_out_odd", extra=x_f32, alpha=ALPHA)
        ln1_p = (_row(small["ln_mix_g"][l]), _row(small["ln_mix_b"][l]))
        ln_outs = [(D_MODEL, F32), (D_MODEL, BF16)]
        (x_mid, x_mid_bf), _ = _scan_fwd(_f_ln_with_bf16, "ln_fwd", n_row, [r1], [], ln1_p, [], [], ln_outs)
        gu = _mm(x_mid_bf, w["ffn_w_gu"][l], "nn", BF16, "ffn_gu")
        (act,), _ = _scan_fwd(_f_act, "ffn_act_fwd", rows // min(ACT_ROW_TILE, rows), [gu], [], [], [], [],
                              [(D_FF, BF16)])
        r2 = _mm(act, w["ffn_w_down"][l], "nn", F32, "ffn_down", extra=x_mid, alpha=ALPHA)
        ln2_p = (_row(small["ln_ffn_g"][l]), _row(small["ln_ffn_b"][l]))
        (x_out, x_out_bf), _ = _scan_fwd(_f_ln_with_bf16, "ln_fwd", n_row, [r2], [], ln2_p, [], [], ln_outs)
        saved.append(dict(sv=sv, x_bf=x_bf, r1=r1, ln1_p=ln1_p, x_mid_bf=x_mid_bf, gu=gu, act=act, r2=r2, ln2_p=ln2_p,
                          w_out=w_out))
        x_f32, x_bf = x_out, x_out_bf

    dy, loss_part = _loss_head(x_f32, target, "loss_head")
    loss = 0.5 * jnp.sum(loss_part) / D_MODEL

    layer_grads = []
    for l in reversed(range(DEPTH)):
        i = l // 2
        s = saved[l]
        (d_r2,), (g_ln2_g, g_ln2_b) = _scan_bwd(_f_ln, "ln_bwd", n_row, [s["r2"]], [], s["ln2_p"], [], [], [dy])
        g_down = _mm(s["act"], d_r2, "tn", F32, "ffn_dw_down")
        dact = _mm(d_r2, w["ffn_w_down"][l], "nt", BF16, "ffn_dact")
        (dgu,), _ = _scan_bwd(_f_act, "ffn_act_bwd", rows // min(ACT_ROW_TILE, rows), [s["gu"]], [], [], [], [],
                              [dact])
        g_gu = _mm(s["x_mid_bf"], dgu, "tn", F32, "ffn_dw_gu")
        dx_mid = _mm(dgu, w["ffn_w_gu"][l], "nt", F32, "ffn_dx", extra=d_r2, alpha=ALPHA)
        (d_r1,), (g_ln1_g, g_ln1_b) = _scan_bwd(_f_ln, "ln_bwd", n_row, [s["r1"]], [], s["ln1_p"], [], [], [dx_mid])
        g_w_out = _mm(s["sv"]["mix"], d_r1, "tn", F32, "even_dw_out" if l % 2 == 0 else "odd_dw_out")
        dmix = _mm(d_r1, s["w_out"], "nt", BF16, "even_dmix" if l % 2 == 0 else "odd_dmix")
        if l % 2 == 0:
            dy, g = _even_backward(dmix, s["x_bf"], s["sv"], w, i, d_r1)
            g["even_w_out"] = g_w_out
        else:
            dy, g = _odd_backward(dmix, s["x_bf"], s["sv"], w, i, d_r1, tables)
            g["odd_w_out"] = g_w_out
        g.update(ffn_w_gate=g_gu[:, :D_FF], ffn_w_up=g_gu[:, D_FF:], ffn_w_down=g_down, ln_mix_g=g_ln1_g[0],
                 ln_mix_b=g_ln1_b[0], ln_ffn_g=g_ln2_g[0], ln_ffn_b=g_ln2_b[0])
        layer_grads.append((l, g))
    return loss, dy, layer_grads


EVEN_NAMES = ("even_w_in", "pool_w", "pool_scale", "conv_w", "conv_b", "dt_bias", "a_log", "d_skip", "ssm_norm_w",
              "even_w_out")
ODD_NAMES = ("odd_w_in", "fgate_b", "q_norm_w", "w_uq", "kv_norm_w", "w_ukv", "odd_w_out")
PER_LAYER_NAMES = ("ffn_w_gate", "ffn_w_up", "ffn_w_down", "ln_mix_g", "ln_mix_b", "ln_ffn_g", "ln_ffn_b")
WEIGHT_NAMES = EVEN_NAMES + ODD_NAMES + PER_LAYER_NAMES


def _grads_by_name(layer_grads):
    by_layer = dict(layer_grads)
    out = {}
    for n in EVEN_NAMES:
        out[n] = [by_layer[l][n] for l in range(0, DEPTH, 2)]
    for n in ODD_NAMES:
        out[n] = [by_layer[l][n] for l in range(1, DEPTH, 2)]
    for n in PER_LAYER_NAMES:
        out[n] = [by_layer[l][n] for l in range(DEPTH)]
    return out


def _prepare_weights(full):
    w = {}
    w["even_w_in"] = [_prep_even_w_in(full["even_w_in"][i]) for i in range(2)]
    w["even_w_out"] = [full["even_w_out"][i] for i in range(2)]
    w["odd_w_in"] = [_prep_odd_w_in(full["odd_w_in"][i]) for i in range(2)]
    w["w_uq"] = [_prep_w_uq(full["w_uq"][i]) for i in range(2)]
    w["w_ukv"] = [full["w_ukv"][i] for i in range(2)]
    w["odd_w_out"] = [full["odd_w_out"][i] for i in range(2)]
    w["ffn_w_gu"] = [jnp.concatenate([full["ffn_w_gate"][l], full["ffn_w_up"][l]], axis=1) for l in range(DEPTH)]
    w["ffn_w_down"] = [full["ffn_w_down"][l] for l in range(DEPTH)]
    for n in ("conv_w", "q_norm_w", "kv_norm_w"):
        w[n] = full[n]
    return w


def _flatten_small(grads):
    flat = jnp.concatenate([g.reshape(-1) for n in REPLICATED for g in grads[n]])
    n = flat.shape[0]
    per = LANES * 8
    total = -(-n // per) * per
    return jnp.pad(flat, (0, total - n)).reshape(total // LANES, LANES)


def _unflatten_small(mat, like):
    flat = mat.reshape(-1)
    out, off = {}, 0
    for n in REPLICATED:
        size = math.prod(like[n].shape)
        out[n] = flat[off:off + size].reshape(like[n].shape)
        off += size
    return out


def kernel(x, even_w_in, pool_w, pool_scale, conv_w, conv_b, dt_bias, a_log, d_skip, ssm_norm_w, even_w_out, odd_w_in, fgate_b, q_norm_w, w_uq, kv_norm_w, w_ukv, odd_w_out, ffn_w_gate, ffn_w_up, ffn_w_down, ln_mix_g, ln_mix_b, ln_ffn_g, ln_ffn_b, loss_target, m_even_w_in, m_pool_w, m_pool_scale, m_conv_w, m_conv_b, m_dt_bias, m_a_log, m_d_skip, m_ssm_norm_w, m_even_w_out, m_odd_w_in, m_fgate_b, m_q_norm_w, m_w_uq, m_kv_norm_w, m_w_ukv, m_odd_w_out, m_ffn_w_gate, m_ffn_w_up, m_ffn_w_down, m_ln_mix_g, m_ln_mix_b, m_ln_ffn_g, m_ln_ffn_b, v_even_w_in, v_pool_w, v_pool_scale, v_conv_w, v_conv_b, v_dt_bias, v_a_log, v_d_skip, v_ssm_norm_w, v_even_w_out, v_odd_w_in, v_fgate_b, v_q_norm_w, v_w_uq, v_kv_norm_w, v_w_ukv, v_odd_w_out, v_ffn_w_gate, v_ffn_w_up, v_ffn_w_down, v_ln_mix_g, v_ln_mix_b, v_ln_ffn_g, v_ln_ffn_b):
    weights = dict(even_w_in=even_w_in, pool_w=pool_w, pool_scale=pool_scale, conv_w=conv_w, conv_b=conv_b,
                   dt_bias=dt_bias, a_log=a_log, d_skip=d_skip, ssm_norm_w=ssm_norm_w, even_w_out=even_w_out,
                   odd_w_in=odd_w_in, fgate_b=fgate_b, q_norm_w=q_norm_w, w_uq=w_uq, kv_norm_w=kv_norm_w, w_ukv=w_ukv,
                   odd_w_out=odd_w_out, ffn_w_gate=ffn_w_gate, ffn_w_up=ffn_w_up, ffn_w_down=ffn_w_down,
                   ln_mix_g=ln_mix_g, ln_mix_b=ln_mix_b, ln_ffn_g=ln_ffn_g, ln_ffn_b=ln_ffn_b)
    m_in = dict(even_w_in=m_even_w_in, pool_w=m_pool_w, pool_scale=m_pool_scale, conv_w=m_conv_w, conv_b=m_conv_b,
                dt_bias=m_dt_bias, a_log=m_a_log, d_skip=m_d_skip, ssm_norm_w=m_ssm_norm_w, even_w_out=m_even_w_out,
                odd_w_in=m_odd_w_in, fgate_b=m_fgate_b, q_norm_w=m_q_norm_w, w_uq=m_w_uq, kv_norm_w=m_kv_norm_w,
                w_ukv=m_w_ukv, odd_w_out=m_odd_w_out, ffn_w_gate=m_ffn_w_gate, ffn_w_up=m_ffn_w_up,
                ffn_w_down=m_ffn_w_down, ln_mix_g=m_ln_mix_g, ln_mix_b=m_ln_mix_b, ln_ffn_g=m_ln_ffn_g,
                ln_ffn_b=m_ln_ffn_b)
    v_in = dict(even_w_in=v_even_w_in, pool_w=v_pool_w, pool_scale=v_pool_scale, conv_w=v_conv_w, conv_b=v_conv_b,
                dt_bias=v_dt_bias, a_log=v_a_log, d_skip=v_d_skip, ssm_norm_w=v_ssm_norm_w, even_w_out=v_even_w_out,
                odd_w_in=v_odd_w_in, fgate_b=v_fgate_b, q_norm_w=v_q_norm_w, w_uq=v_w_uq, kv_norm_w=v_kv_norm_w,
                w_ukv=v_w_ukv, odd_w_out=v_odd_w_out, ffn_w_gate=v_ffn_w_gate, ffn_w_up=v_ffn_w_up,
                ffn_w_down=v_ffn_w_down, ln_mix_g=v_ln_mix_g, ln_mix_b=v_ln_mix_b, ln_ffn_g=v_ln_ffn_g,
                ln_ffn_b=v_ln_ffn_b)
    shards = {n: weights[n] for n, _ in SHARDED}

    w = _prepare_weights(_gather_weights(shards))
    small = {n: weights[n] for n in REPLICATED}
    w.update(small)

    loss_local, dx, layer_grads = _local_step(x[0], loss_target[0], w, small)
    grads_full = _grads_by_name(layer_grads)

    reduced = _reduce_scatter(_pack_grads(grads_full, shards))
    grads = _unpack_reduced(reduced, shards)
    small_sum = _all_reduce_small(_flatten_small(grads_full), "small_grads_allreduce")
    grads.update(_unflatten_small(small_sum, small))
    loss = lax.psum(loss_local, ("x", "y", "c"))

    deltas, new_m, new_v = {}, {}, {}
    for n in WEIGHT_NAMES:
        deltas[n], new_m[n], new_v[n] = _adamw(weights[n], grads[n], m_in[n], v_in[n], "adamw_" + n)
    return (loss, dx[None], *[grads[n] for n in WEIGHT_NAMES], *[deltas[n] for n in WEIGHT_NAMES],
            *[new_m[n] for n in WEIGHT_NAMES], *[new_v[n] for n in WEIGHT_NAMES])
```

```python
import functools
import math

import numpy as np
import jax
import jax.numpy as jnp
from jax import lax
from jax.experimental import pallas as pl
from jax.experimental.pallas import tpu as pltpu

F32 = jnp.float32
BF16 = jnp.bfloat16
HI = lax.Precision.HIGHEST
MESH_ID = pl.DeviceIdType.MESH

VMEM_LIMIT_BYTES = 56 * 1024 * 1024
LANES = 128

D_MODEL = 1024
DEPTH = 4
POOL_WINDOWS = (2, 4, 8, 16)
POOL_GROUP = 128
POOL_WIDTH = 512
SSM_D_INNER = 1024
SSM_HEAD_DIM = 64
SSM_HEADS = 16
SSM_GROUPS = 2
SSM_STATE = 128
SSM_CONV = 4
SSM_CHUNK = 128
SSM_CONV_DIM = 1536
EVEN_IN = 3088
EVEN_IN_PAD = 3200
FOX_HEADS = 8
FOX_WIDTH = 512
MLA_HEADS = 8
MLA_NOPE = 64
MLA_ROPE = 32
MLA_V = 64
MLA_Q_RANK = 512
MLA_KV_RANK = 256
ROPE_THETA = 10000.0
ODD_IN = 2344
ODD_IN_PAD = 2560
D_FF = 2816
ALPHA = (2 * DEPTH) ** 0.25
LN_EPS = 1e-5
RMS_EPS = 1e-6
ADAM_LR = 0.001
ADAM_B1 = 0.9
ADAM_B2 = 0.999
ADAM_EPS = 1e-08
ADAM_WD = 0.01
ADAM_STEP = 10
NEG_BIG = -1e30

ATTN_TILE = 512
ATTN_WIDE = 2048
FOX_SCALE = 0.125
MLA_SCALE = (MLA_NOPE + MLA_ROPE) ** -0.5
ROW_TILE = 512
LN_ROW_TILE = 1024
ACT_ROW_TILE = 256


def _cparams(sem=None):
    return pltpu.CompilerParams(dimension_semantics=sem, vmem_limit_bytes=VMEM_LIMIT_BYTES)


def _pick(d, prefs):
    for p in prefs:
        if d % p == 0:
            return p
    return d


_M_PREFS = (2048, 1024, 512, 640, 1408, 768, 384, 256, 128)
_N_PREFS = (1024, 1408, 512, 640, 768, 384, 256, 128)
_K_PREFS = (1024, 1408, 512, 640, 768, 256, 128)
MM_MAX_ACC_ELEMS = 1408 * 1024


def _mm(a, b, mode, out_dtype, name, extra=None, alpha=1.0):
    if mode == "nn":
        (m, k), (k2, n) = a.shape, b.shape
    elif mode == "nt":
        (m, k), (n, k2) = a.shape, b.shape
    else:
        (k, m), (k2, n) = a.shape, b.shape
    assert k == k2, (a.shape, b.shape, mode)
    tn, tk = _pick(n, _N_PREFS), _pick(k, _K_PREFS)
    tm = _pick(m, tuple(p for p in _M_PREFS if p * tn <= MM_MAX_ACC_ELEMS))
    nk = k // tk
    if mode == "nn":
        a_spec = pl.BlockSpec((tm, tk), lambda i, j, kk: (i, kk))
        b_spec = pl.BlockSpec((tk, tn), lambda i, j, kk: (kk, j))
        dims = (((1,), (0,)), ((), ()))
    elif mode == "nt":
        a_spec = pl.BlockSpec((tm, tk), lambda i, j, kk: (i, kk))
        b_spec = pl.BlockSpec((tn, tk), lambda i, j, kk: (j, kk))
        dims = (((1,), (1,)), ((), ()))
    else:
        a_spec = pl.BlockSpec((tk, tm), lambda i, j, kk: (kk, i))
        b_spec = pl.BlockSpec((tk, tn), lambda i, j, kk: (kk, j))
        dims = (((0,), (0,)), ((), ()))
    o_spec = pl.BlockSpec((tm, tn), lambda i, j, kk: (i, j))
    has_extra = extra is not None

    def body(*refs):
        if has_extra:
            a_ref, b_ref, e_ref, o_ref, acc = refs
        else:
            a_ref, b_ref, o_ref, acc = refs
        kk = pl.program_id(2)

        @pl.when(kk == 0)
        def _():
            acc[...] = jnp.zeros_like(acc)

        acc[...] += lax.dot_general(a_ref[...].astype(BF16), b_ref[...].astype(BF16), dims,
                                    preferred_element_type=F32)

        @pl.when(kk == nk - 1)
        def _():
            r = acc[...]
            if has_extra:
                r = r + alpha * e_ref[...].astype(F32)
            o_ref[...] = r.astype(o_ref.dtype)

    ins = [a, b] + ([extra] if has_extra else [])
    specs = [a_spec, b_spec] + ([o_spec] if has_extra else [])
    return pl.pallas_call(
        body, name=name, grid=(m // tm, n // tn, nk), in_specs=specs, out_specs=o_spec,
        out_shape=jax.ShapeDtypeStruct((m, n), out_dtype),
        scratch_shapes=[pltpu.VMEM((tm, tn), F32)],
        compiler_params=_cparams(("parallel", "parallel", "arbitrary")),
    )(*ins)


def _full_spec(shape):
    nd = len(shape)
    return pl.BlockSpec(tuple(shape), lambda i, _nd=nd: (0,) * _nd)


class _Cols:
    def __init__(self, arr, block, width):
        self.arr, self.block, self.width = arr, block, width
        self.shape, self.dtype = (arr.shape[0], width), arr.dtype


def _tile_arr(t):
    return t.arr if isinstance(t, _Cols) else t


def _tile_spec(t, tt, row_index):
    block = t.block if isinstance(t, _Cols) else 0
    return pl.BlockSpec((tt, t.shape[1]), lambda i: (row_index(i), block))


def _scan_fwd(f, name, n, tiles, ctiles, params, cparams, carry_shapes, out_defs):
    rows = tiles[0].shape[0]
    tt = rows // n
    nt, nct, npar, ncp, ncar, nout = len(tiles), len(ctiles), len(params), len(cparams), len(carry_shapes), len(out_defs)

    def body(*refs):
        pos = 0
        t_refs = refs[pos:pos + nt]; pos += nt
        ct_refs = refs[pos:pos + nct]; pos += nct
        p_refs = refs[pos:pos + npar]; pos += npar
        cp_refs = refs[pos:pos + ncp]; pos += ncp
        o_refs = refs[pos:pos + nout]; pos += nout
        cs_refs = refs[pos:pos + ncar]; pos += ncar
        c_scr = refs[pos:pos + ncar]
        i = pl.program_id(0)

        @pl.when(i == 0)
        def _():
            for c in c_scr:
                c[...] = jnp.zeros_like(c)

        carry = tuple(c[...] for c in c_scr)
        for s, c in zip(cs_refs, carry):
            s[0] = c
        new_carry, outs = f(carry, tuple(r[...] for r in t_refs), tuple(r[...] for r in ct_refs),
                            tuple(r[...] for r in p_refs), tuple(r[...] for r in cp_refs), i)
        for o_ref, o in zip(o_refs, outs):
            o_ref[...] = o
        for c, v in zip(c_scr, new_carry):
            c[...] = v

    tile_spec = lambda t: _tile_spec(t, tt, lambda i: i)
    in_specs = ([tile_spec(t) for t in tiles] + [tile_spec(t) for t in ctiles]
                + [_full_spec(p.shape) for p in params] + [_full_spec(p.shape) for p in cparams])
    out_specs = ([pl.BlockSpec((tt, c), lambda i: (i, 0)) for c, _ in out_defs]
                 + [pl.BlockSpec((1,) + tuple(s), lambda i: (i, 0, 0)) for s in carry_shapes])
    out_shape = ([jax.ShapeDtypeStruct((rows, c), dt) for c, dt in out_defs]
                 + [jax.ShapeDtypeStruct((n,) + tuple(s), F32) for s in carry_shapes])
    res = pl.pallas_call(
        body, name=name, grid=(n,), in_specs=in_specs, out_specs=out_specs, out_shape=out_shape,
        scratch_shapes=[pltpu.VMEM(tuple(s), F32) for s in carry_shapes],
        compiler_params=_cparams(("arbitrary",)),
    )(*map(_tile_arr, tiles), *map(_tile_arr, ctiles), *params, *cparams)
    return list(res[:nout]), list(res[nout:])


def _scan_bwd(f, name, n, tiles, ctiles, params, cparams, carries, douts):
    rows = tiles[0].shape[0]
    tt = rows // n
    nt, nct, npar, ncp, ncar, nout = len(tiles), len(ctiles), len(params), len(cparams), len(carries), len(douts)

    def body(*refs):
        pos = 0
        t_refs = refs[pos:pos + nt]; pos += nt
        ct_refs = refs[pos:pos + nct]; pos += nct
        p_refs = refs[pos:pos + npar]; pos += npar
        cp_refs = refs[pos:pos + ncp]; pos += ncp
        cs_refs = refs[pos:pos + ncar]; pos += ncar
        do_refs = refs[pos:pos + nout]; pos += nout
        dt_refs = refs[pos:pos + nt]; pos += nt
        dp_refs = refs[pos:pos + npar]; pos += npar
        dc_scr = refs[pos:pos + ncar]
        i = pl.program_id(0)

        @pl.when(i == 0)
        def _():
            for c in dc_scr:
                c[...] = jnp.zeros_like(c)
            for d in dp_refs:
                d[...] = jnp.zeros_like(d)

        ctv = tuple(r[...] for r in ct_refs)
        cpv = tuple(r[...] for r in cp_refs)

        def g(c, t, p):
            return f(c, t, ctv, p, cpv, n - 1 - i)

        _, vjp = jax.vjp(g, tuple(s[0] for s in cs_refs), tuple(r[...] for r in t_refs),
                         tuple(r[...] for r in p_refs))
        dc, dt, dp = vjp((tuple(c[...] for c in dc_scr), tuple(r[...] for r in do_refs)))
        for r, v in zip(dt_refs, dt):
            r[...] = v
        for r, v in zip(dp_refs, dp):
            r[...] += v
        for c, v in zip(dc_scr, dc):
            c[...] = v

    rev_tile = lambda t: _tile_spec(t, tt, lambda i: n - 1 - i)
    rev_out = lambda t: pl.BlockSpec((tt, t.shape[1]), lambda i: (n - 1 - i, 0))
    in_specs = ([rev_tile(t) for t in tiles] + [rev_tile(t) for t in ctiles]
                + [_full_spec(p.shape) for p in params] + [_full_spec(p.shape) for p in cparams]
                + [pl.BlockSpec((1,) + tuple(c.shape[1:]), lambda i: (n - 1 - i, 0, 0)) for c in carries]
                + [rev_tile(d) for d in douts])
    out_specs = [rev_out(t) for t in tiles] + [_full_spec(p.shape) for p in params]
    out_shape = ([jax.ShapeDtypeStruct(t.shape, t.dtype) for t in tiles]
                 + [jax.ShapeDtypeStruct(p.shape, F32) for p in params])
    res = pl.pallas_call(
        body, name=name, grid=(n,), in_specs=in_specs, out_specs=out_specs, out_shape=out_shape,
        scratch_shapes=[pltpu.VMEM(tuple(c.shape[1:]), F32) for c in carries],
        compiler_params=_cparams(("arbitrary",)),
    )(*map(_tile_arr, tiles), *map(_tile_arr, ctiles), *params, *cparams, *carries, *map(_tile_arr, douts))
    return list(res[:nt]), list(res[nt:])


def _silu(x):
    return x * jax.nn.sigmoid(x)


def _softplus(x):
    return jnp.maximum(x, 0.0) + jnp.log(1.0 + jnp.exp(-jnp.abs(x)))


def _f_ln(carry, tiles, ctiles, params, cparams, idx):
    (r,), (g, b) = tiles, params
    mu = jnp.mean(r, axis=-1, keepdims=True)
    xc = r - mu
    var = jnp.mean(xc * xc, axis=-1, keepdims=True)
    return (), (xc * lax.rsqrt(var + LN_EPS) * g + b,)


def _f_ln_with_bf16(carry, tiles, ctiles, params, cparams, idx):
    _, (y,) = _f_ln(carry, tiles, ctiles, params, cparams, idx)
    return (), (y, y.astype(BF16))


def _f_act(carry, tiles, ctiles, params, cparams, idx):
    (gu,) = tiles
    g = gu[:, :D_FF].astype(F32)
    u = gu[:, D_FF:].astype(F32)
    return (), ((_silu(g) * u).astype(BF16),)


def _f_pool(carry, tiles, ctiles, params, cparams, idx):
    (prev,), (u,), (pool_w, pool_scale) = carry, tiles, params
    tt = u.shape[0]
    halo = prev.shape[0]
    ext = jnp.concatenate([prev, u], axis=0)
    pos = idx * tt + lax.broadcasted_iota(jnp.int32, (tt, 1), 0)
    ys = []
    for g, w in enumerate(POOL_WINDOWS):
        lo, hi = g * POOL_GROUP, (g + 1) * POOL_GROUP
        eg = ext[:, lo:hi]
        s = eg[halo:halo + tt]
        for j in range(1, w):
            s = s + eg[halo - j:halo - j + tt]
        count = jnp.minimum(pos + 1, w).astype(F32)
        diff = s / count - u[:, lo:hi]
        ys.append(jnp.dot(diff.astype(BF16), pool_w[g].astype(BF16), preferred_element_type=F32))
    y = jnp.concatenate(ys, axis=1) * pool_scale
    return (u[tt - halo:, :],), (y.astype(BF16),)


def _f_conv(carry, tiles, ctiles, params, cparams, idx):
    (prev,), (xbc,), (conv_w, conv_b) = carry, tiles, params
    tt = xbc.shape[0]
    halo = prev.shape[0]
    ext = jnp.concatenate([prev, xbc], axis=0)
    acc = jnp.zeros_like(xbc) + conv_b
    for k in range(SSM_CONV):
        off = halo - (SSM_CONV - 1) + k
        acc = acc + ext[off:off + tt] * conv_w[k:k + 1, :]
    return (xbc[tt - halo:, :],), (_silu(acc),)


def _head_expand_matrix():
    r = lax.broadcasted_iota(jnp.int32, (LANES, SSM_D_INNER), 0)
    c = lax.broadcasted_iota(jnp.int32, (LANES, SSM_D_INNER), 1)
    return ((c >= r * SSM_HEAD_DIM) & (c < (r + 1) * SSM_HEAD_DIM)).astype(BF16)


def _head_expand(v, e):
    hi = v.astype(BF16)
    lo = (v - hi.astype(F32)).astype(BF16)
    return jnp.dot(hi, e, preferred_element_type=F32) + jnp.dot(lo, e, preferred_element_type=F32)


def _f_ssd(carry, tiles, ctiles, params, cparams, idx):
    (state,), (xa, dtr, z), (dt_bias, a_log, d_skip, norm_w) = carry, tiles, params
    L, P, N, E = SSM_CHUNK, SSM_HEAD_DIM, SSM_STATE, SSM_HEADS // SSM_GROUPS
    gw = E * P
    em = _head_expand_matrix()
    dt = _softplus(dtr + dt_bias)
    da = dt * (-jnp.exp(a_log))
    r = lax.broadcasted_iota(jnp.int32, (L, L), 0)
    c = lax.broadcasted_iota(jnp.int32, (L, L), 1)
    tri = c <= r
    acs = jnp.dot(tri.astype(F32), da, precision=HI, preferred_element_type=F32)
    acs_t = acs.T
    last = acs[L - 1:L, :]
    xs = xa[:, :SSM_D_INNER]
    xdt = xs * _head_expand(dt, em)
    xdt_b = xdt.astype(BF16)
    xdec = (xdt * _head_expand(jnp.exp(last - acs), em)).astype(BF16)
    eacs = _head_expand(jnp.exp(acs), em)
    y_parts, st_parts = [], []
    for g in range(SSM_GROUPS):
        bg = xa[:, SSM_D_INNER + g * N:SSM_D_INNER + (g + 1) * N].astype(BF16)
        cg = xa[:, SSM_D_INNER + (SSM_GROUPS + g) * N:SSM_D_INNER + (SSM_GROUPS + g + 1) * N].astype(BF16)
        cb = lax.dot_general(cg, bg, (((1,), (1,)), ((), ())), preferred_element_type=F32)
        prev_g = state[:, g * gw:(g + 1) * gw]
        y_off = jnp.dot(cg, prev_g.astype(BF16), preferred_element_type=F32) * eacs[:, g * gw:(g + 1) * gw]
        st_parts.append(lax.dot_general(bg, xdec[:, g * gw:(g + 1) * gw], (((0,), (0,)), ((), ())),
                                        preferred_element_type=F32))
        diag = []
        for e in range(E):
            h = g * E + e
            lmat = jnp.exp(jnp.where(tri, acs[:, h:h + 1] - acs_t[h:h + 1, :], NEG_BIG))
            diag.append(jnp.dot((cb * lmat).astype(BF16), xdt_b[:, h * P:(h + 1) * P], preferred_element_type=F32))
        y_parts.append(jnp.concatenate(diag, axis=1) + y_off)
    y = jnp.concatenate(y_parts, axis=1) + xs * _head_expand(d_skip, em)
    new_state = state * _head_expand(jnp.exp(last), em) + jnp.concatenate(st_parts, axis=1)
    y = y * _silu(z)
    y = y * lax.rsqrt(jnp.mean(y * y, axis=-1, keepdims=True) + RMS_EPS) * norm_w
    return (new_state,), (y.astype(BF16),)


def _f_fox_gate(carry, tiles, ctiles, params, cparams, idx):
    (run,), (fl,), (fb,) = carry, tiles, params
    tt = fl.shape[0]
    logf = -_softplus(-(fl + fb))
    r = lax.broadcasted_iota(jnp.int32, (tt, tt), 0)
    c = lax.broadcasted_iota(jnp.int32, (tt, tt), 1)
    cum = jnp.dot((c <= r).astype(F32), logf, precision=HI, preferred_element_type=F32) + run[0:1, :]
    return (jnp.broadcast_to(cum[tt - 1:tt, :], run.shape),), (cum,)


def _rms(x, w):
    return x * lax.rsqrt(jnp.mean(x * x, axis=-1, keepdims=True) + RMS_EPS) * w


def _f_mla_prep(carry, tiles, ctiles, params, cparams, idx):
    (cq, ckv, kr), (ck, sk), (qw, kvw), (rk,) = tiles, ctiles, params, cparams
    rot = jnp.dot(kr, rk, precision=HI, preferred_element_type=F32)
    return (), (_rms(cq, qw).astype(BF16), _rms(ckv, kvw).astype(BF16), (kr * ck + rot * sk).astype(BF16))


def _f_rope_q(carry, tiles, ctiles, params, cparams, idx):
    (q,), (cos, sin) = tiles, ctiles
    nope = MLA_HEADS * MLA_NOPE
    x1 = q[:, nope:nope + LANES]
    x2 = q[:, nope + LANES:]
    roped = jnp.concatenate([q[:, :nope], x1 * cos - x2 * sin, x2 * cos + x1 * sin], axis=1)
    return (), ((roped * MLA_SCALE).astype(BF16),)


def _scores(q, k, fq, fk, masked, row0, col0):
    s = lax.dot_general(q, k, (((1,), (1,)), ((), ())), preferred_element_type=F32)
    if fq is not None:
        s = s + fq - fk
    if masked:
        rows = row0 + lax.broadcasted_iota(jnp.int32, s.shape, 0)
        cols = col0 + lax.broadcasted_iota(jnp.int32, s.shape, 1)
        s = jnp.where(cols <= rows, s, NEG_BIG)
    return s


def _q_major_tables(T, tq, tk):
    r = tk // tq
    qi = np.concatenate([np.full(i // r + 1, i, np.int32) for i in range(T // tq)])
    ki = np.concatenate([np.arange(i // r + 1, dtype=np.int32) for i in range(T // tq)])
    kind = np.where(ki == qi // r, qi % r + 1, 0).astype(np.int32)
    return [jnp.asarray(a) for a in (qi, ki, (ki == 0).astype(np.int32), kind)]


def _k_major_tables(T, tq, tk):
    r = tq // tk
    nq = T // tq
    ki = np.concatenate([np.full(nq - i // r, i, np.int32) for i in range(T // tk)])
    qi = np.concatenate([np.arange(i // r, nq, dtype=np.int32) for i in range(T // tk)])
    kind = np.where(qi == ki // r, r - ki % r, 0).astype(np.int32)
    return [jnp.asarray(a) for a in (ki, qi, (qi == nq - 1).astype(np.int32), kind)]


def _attn_tiles(T):
    return min(ATTN_TILE, T), min(ATTN_WIDE, T)


def _attn_fwd(q, k, v, fq, fk, name):
    H, T, dk = q.shape
    dv = v.shape[2]
    tq, tk = _attn_tiles(T)
    decay = fq is not None
    tables = _q_major_tables(T, tq, tk)

    def body(qi_ref, ki_ref, first_ref, kind_ref, *refs):
        if decay:
            q_ref, k_ref, v_ref, fq_ref, fk_ref, o_ref, lse_ref, m_s, l_s, acc = refs
        else:
            q_ref, k_ref, v_ref, o_ref, lse_ref, m_s, l_s, acc = refs
        t = pl.program_id(1)
        qi, ki = qi_ref[t], ki_ref[t]

        @pl.when(first_ref[t] == 1)
        def _():
            m_s[...] = jnp.full_like(m_s, NEG_BIG)
            l_s[...] = jnp.zeros_like(l_s)
            acc[...] = jnp.zeros_like(acc)

        def step(kind):
            w = tk if kind == 0 else kind * tq
            s = _scores(q_ref[0], k_ref[0, :w, :], fq_ref[0] if decay else None,
                        fk_ref[0, :, :w] if decay else None, kind > 0, qi * tq, ki * tk)
            m_new = jnp.maximum(m_s[...], jnp.max(s, axis=-1, keepdims=True))
            p = jnp.exp(s - m_new)
            corr = jnp.exp(m_s[...] - m_new)
            l_s[...] = corr * l_s[...] + jnp.sum(p, axis=-1, keepdims=True)
            acc[...] = corr * acc[...] + jnp.dot(p.astype(BF16), v_ref[0, :w, :], preferred_element_type=F32)
            m_s[...] = m_new
            if kind > 0:
                o_ref[0] = (acc[...] / l_s[...]).astype(o_ref.dtype)
                lse_ref[0] = m_s[...] + jnp.log(l_s[...])

        for kind in range(tk // tq + 1):
            pl.when(kind_ref[t] == kind)(functools.partial(step, kind))

    qspec = lambda d: pl.BlockSpec((1, tq, d), lambda h, t, qi, ki, fi, la: (h, qi[t], 0))
    kspec = lambda d: pl.BlockSpec((1, tk, d), lambda h, t, qi, ki, fi, la: (h, ki[t], 0))
    in_specs = [qspec(dk), kspec(dk), kspec(dv)]
    ins = [q, k, v]
    if decay:
        in_specs += [qspec(1), pl.BlockSpec((1, 1, tk), lambda h, t, qi, ki, fi, la: (h, 0, ki[t]))]
        ins += [fq, fk]
    grid_spec = pltpu.PrefetchScalarGridSpec(
        num_scalar_prefetch=4, grid=(H, int(tables[0].shape[0])), in_specs=in_specs,
        out_specs=[qspec(dv), qspec(1)],
        scratch_shapes=[pltpu.VMEM((tq, 1), F32), pltpu.VMEM((tq, 1), F32), pltpu.VMEM((tq, dv), F32)])
    return pl.pallas_call(
        body, name=name, grid_spec=grid_spec,
        out_shape=[jax.ShapeDtypeStruct((H, T, dv), BF16), jax.ShapeDtypeStruct((H, T, 1), F32)],
        compiler_params=_cparams(("parallel", "arbitrary")),
    )(*tables, *ins)


def _attn_bwd_dq(q, k, v, o, do, lse, fq, fk, name):
    H, T, dk = q.shape
    dv = v.shape[2]
    tq, tk = _attn_tiles(T)
    decay = fq is not None
    tables = _q_major_tables(T, tq, tk)

    def body(qi_ref, ki_ref, first_ref, kind_ref, *refs):
        if decay:
            q_ref, k_ref, v_ref, o_ref, do_ref, lse_ref, fq_ref, fk_ref, dq_ref, dl_ref, acc, dl, leak = refs
        else:
            q_ref, k_ref, v_ref, o_ref, do_ref, lse_ref, dq_ref, dl_ref, acc, dl, leak = refs
        t = pl.program_id(1)
        qi, ki = qi_ref[t], ki_ref[t]

        @pl.when(first_ref[t] == 1)
        def _():
            acc[...] = jnp.zeros_like(acc)
            leak[...] = jnp.zeros_like(leak)
            dl[...] = jnp.sum(do_ref[0].astype(F32) * o_ref[0].astype(F32), axis=-1, keepdims=True)

        def step(kind):
            w = tk if kind == 0 else kind * tq
            k_v = k_ref[0, :w, :]
            s = _scores(q_ref[0], k_v, fq_ref[0] if decay else None, fk_ref[0, :, :w] if decay else None,
                        kind > 0, qi * tq, ki * tk)
            p = jnp.exp(s - lse_ref[0])
            dp = lax.dot_general(do_ref[0], v_ref[0, :w, :], (((1,), (1,)), ((), ())), preferred_element_type=F32)
            ds = p * (dp - dl[...])
            leak[...] += jnp.sum(ds, axis=-1, keepdims=True)
            acc[...] += jnp.dot(ds.astype(BF16), k_v, preferred_element_type=F32)
            if kind > 0:
                dq_ref[0] = acc[...].astype(dq_ref.dtype)
                dl_ref[0] = dl[...] + leak[...]

        for kind in range(tk // tq + 1):
            pl.when(kind_ref[t] == kind)(functools.partial(step, kind))

    qspec = lambda d: pl.BlockSpec((1, tq, d), lambda h, t, qi, ki, fi, la: (h, qi[t], 0))
    kspec = lambda d: pl.BlockSpec((1, tk, d), lambda h, t, qi, ki, fi, la: (h, ki[t], 0))
    in_specs = [qspec(dk), kspec(dk), kspec(dv), qspec(dv), qspec(dv), qspec(1)]
    ins = [q, k, v, o, do, lse]
    if decay:
        in_specs += [qspec(1), pl.BlockSpec((1, 1, tk), lambda h, t, qi, ki, fi, la: (h, 0, ki[t]))]
        ins += [fq, fk]
    grid_spec = pltpu.PrefetchScalarGridSpec(
        num_scalar_prefetch=4, grid=(H, int(tables[0].shape[0])), in_specs=in_specs,
        out_specs=[qspec(dk), qspec(1)],
        scratch_shapes=[pltpu.VMEM((tq, dk), F32), pltpu.VMEM((tq, 1), F32), pltpu.VMEM((tq, 1), F32)])
    return pl.pallas_call(
        body, name=name, grid_spec=grid_spec,
        out_shape=[jax.ShapeDtypeStruct((H, T, dk), BF16), jax.ShapeDtypeStruct((H, T, 1), F32)],
        compiler_params=_cparams(("parallel", "arbitrary")),
    )(*tables, *ins)


def _attn_bwd_dkv(q, k, v, delta, do, lse, fq, fk, name):
    H, T, dk = q.shape
    dv = v.shape[2]
    tk, tq = _attn_tiles(T)
    decay = fq is not None
    tables = _k_major_tables(T, tq, tk)

    def body(ki_ref, qi_ref, last_ref, kind_ref, *refs):
        if decay:
            q_ref, k_ref, v_ref, dl_ref, do_ref, lse_ref, fq_ref, fk_ref, dk_ref, dv_ref, df_ref, dk_s, dv_s, df_s = refs
        else:
            q_ref, k_ref, v_ref, dl_ref, do_ref, lse_ref, dk_ref, dv_ref, dk_s, dv_s = refs
        t = pl.program_id(1)
        ki, qi = ki_ref[t], qi_ref[t]

        @pl.when(kind_ref[t] > 0)
        def _():
            dk_s[...] = jnp.zeros_like(dk_s)
            dv_s[...] = jnp.zeros_like(dv_s)
            if decay:
                df_s[...] = jnp.zeros_like(df_s)

        def step(kind):
            off = 0 if kind == 0 else tq - kind * tk
            q_v, do_v = q_ref[0, off:, :], do_ref[0, off:, :]
            s = _scores(q_v, k_ref[0], fq_ref[0, off:, :] if decay else None, fk_ref[0] if decay else None,
                        kind > 0, qi * tq + off, ki * tk)
            p = jnp.exp(s - lse_ref[0, off:, :])
            dv_s[...] += lax.dot_general(p.astype(BF16), do_v, (((0,), (0,)), ((), ())), preferred_element_type=F32)
            dp = lax.dot_general(do_v, v_ref[0], (((1,), (1,)), ((), ())), preferred_element_type=F32)
            ds = p * (dp - dl_ref[0, off:, :])
            dk_s[...] += lax.dot_general(ds.astype(BF16), q_v, (((0,), (0,)), ((), ())),
                                         preferred_element_type=F32)
            if decay:
                df_s[...] -= jnp.sum(ds, axis=0, keepdims=True)

        for kind in range(tq // tk + 1):
            pl.when(kind_ref[t] == kind)(functools.partial(step, kind))

        @pl.when(last_ref[t] == 1)
        def _():
            dk_ref[0] = dk_s[...].astype(dk_ref.dtype)
            dv_ref[0] = dv_s[...].astype(dv_ref.dtype)
            if decay:
                df_ref[0] = df_s[...]

    qspec = lambda d: pl.BlockSpec((1, tq, d), lambda h, t, ki, qi, fi, la: (h, qi[t], 0))
    kspec = lambda d: pl.BlockSpec((1, tk, d), lambda h, t, ki, qi, fi, la: (h, ki[t], 0))
    in_specs = [qspec(dk), kspec(dk), kspec(dv), qspec(1), qspec(dv), qspec(1)]
    ins = [q, k, v, delta, do, lse]
    out_specs = [kspec(dk), kspec(dv)]
    out_shape = [jax.ShapeDtypeStruct((H, T, dk), BF16), jax.ShapeDtypeStruct((H, T, dv), BF16)]
    scratch = [pltpu.VMEM((tk, dk), F32), pltpu.VMEM((tk, dv), F32)]
    if decay:
        fkspec = pl.BlockSpec((1, 1, tk), lambda h, t, ki, qi, fi, la: (h, 0, ki[t]))
        in_specs += [qspec(1), fkspec]
        ins += [fq, fk]
        out_specs.append(fkspec)
        out_shape.append(jax.ShapeDtypeStruct((H, 1, T), F32))
        scratch.append(pltpu.VMEM((1, tk), F32))
    grid_spec = pltpu.PrefetchScalarGridSpec(
        num_scalar_prefetch=4, grid=(H, int(tables[0].shape[0])), in_specs=in_specs, out_specs=out_specs,
        scratch_shapes=scratch)
    return pl.pallas_call(
        body, name=name, grid_spec=grid_spec, out_shape=out_shape, compiler_params=_cparams(("parallel", "arbitrary")),
    )(*tables, *ins)


def _attn_bwd_fused(q, k, v, o, do, lse, name):
    H, T, dk = q.shape
    dv = v.shape[2]
    tk, tq = _attn_tiles(T)
    tables = _k_major_tables(T, tq, tk)
    npairs = int(tables[0].shape[0])

    def body(ki_ref, qi_ref, last_ref, kind_ref, q_ref, k_ref, v_ref, o_ref, do_ref, lse_ref,
             dq_ref, dk_ref, dv_ref, dq_s, dk_s, dv_s):
        t = pl.program_id(1)
        ki, qi = ki_ref[t], qi_ref[t]

        @pl.when(t == 0)
        def _():
            dq_s[...] = jnp.zeros_like(dq_s)

        @pl.when(kind_ref[t] > 0)
        def _():
            dk_s[...] = jnp.zeros_like(dk_s)
            dv_s[...] = jnp.zeros_like(dv_s)

        def step(kind):
            off = 0 if kind == 0 else tq - kind * tk
            q_v, do_v, k_v = q_ref[0, off:, :], do_ref[0, off:, :], k_ref[0]
            s = _scores(q_v, k_v, None, None, kind > 0, qi * tq + off, ki * tk)
            p = jnp.exp(s - lse_ref[0, off:, :])
            delta = jnp.sum(do_v.astype(F32) * o_ref[0, off:, :].astype(F32), axis=-1, keepdims=True)
            dv_s[...] += lax.dot_general(p.astype(BF16), do_v, (((0,), (0,)), ((), ())), preferred_element_type=F32)
            dp = lax.dot_general(do_v, v_ref[0], (((1,), (1,)), ((), ())), preferred_element_type=F32)
            ds = (p * (dp - delta)).astype(BF16)
            dk_s[...] += lax.dot_general(ds, q_v, (((0,), (0,)), ((), ())), preferred_element_type=F32)
            rows = pl.ds(pl.multiple_of(qi * tq + off, tk), tq - off)
            dq_s[rows, :] += jnp.dot(ds, k_v, preferred_element_type=F32)

        for kind in range(tq // tk + 1):
            pl.when(kind_ref[t] == kind)(functools.partial(step, kind))

        @pl.when(last_ref[t] == 1)
        def _():
            dk_ref[0] = dk_s[...].astype(dk_ref.dtype)
            dv_ref[0] = dv_s[...].astype(dv_ref.dtype)

        @pl.when(t == npairs - 1)
        def _():
            dq_ref[0] = dq_s[...].astype(dq_ref.dtype)

    qspec = lambda d: pl.BlockSpec((1, tq, d), lambda h, t, ki, qi, la, kd: (h, qi[t], 0))
    kspec = lambda d: pl.BlockSpec((1, tk, d), lambda h, t, ki, qi, la, kd: (h, ki[t], 0))
    grid_spec = pltpu.PrefetchScalarGridSpec(
        num_scalar_prefetch=4, grid=(H, npairs),
        in_specs=[qspec(dk), kspec(dk), kspec(dv), qspec(dv), qspec(dv), qspec(1)],
        out_specs=[pl.BlockSpec((1, T, dk), lambda h, t, ki, qi, la, kd: (h, 0, 0)), kspec(dk), kspec(dv)],
        scratch_shapes=[pltpu.VMEM((T, dk), F32), pltpu.VMEM((tk, dk), F32), pltpu.VMEM((tk, dv), F32)])
    return pl.pallas_call(
        body, name=name, grid_spec=grid_spec,
        out_shape=[jax.ShapeDtypeStruct((H, T, dk), BF16), jax.ShapeDtypeStruct((H, T, dk), BF16),
                   jax.ShapeDtypeStruct((H, T, dv), BF16)],
        compiler_params=_cparams(("parallel", "arbitrary")),
    )(*tables, q, k, v, o, do, lse)


def _loss_head(y, target, name):
    rows, d = y.shape
    tt = _pick(rows, (512, 256, 128))

    def body(y_ref, t_ref, dy_ref, part_ref):
        @pl.when(pl.program_id(0) == 0)
        def _():
            part_ref[...] = jnp.zeros_like(part_ref)

        err = y_ref[...] - t_ref[...]
        dy_ref[...] = err * (1.0 / d)
        sq = jnp.sum(err * err, axis=0, keepdims=True)
        folded = sq[:, :LANES]
        for j in range(1, d // LANES):
            folded = folded + sq[:, j * LANES:(j + 1) * LANES]
        part_ref[...] += folded

    spec = pl.BlockSpec((tt, d), lambda i: (i, 0))
    return pl.pallas_call(
        body, name=name, grid=(rows // tt,), in_specs=[spec, spec],
        out_specs=[spec, pl.BlockSpec((1, LANES), lambda i: (0, 0))],
        out_shape=[jax.ShapeDtypeStruct((rows, d), F32), jax.ShapeDtypeStruct((1, LANES), F32)],
        compiler_params=_cparams(("arbitrary",)),
    )(y, target)


def _adamw(w, g, m, v, name):
    shape = w.shape
    cols = shape[-1]
    rows = math.prod(shape[:-1])
    w2, g2, m2, v2 = (a.reshape(rows, cols) for a in (w, g, m, v))
    tr = _pick(rows, (512, 256, 128, 64, 32, 16, 8))
    c1 = 1.0 / (1.0 - ADAM_B1 ** ADAM_STEP)
    c2 = 1.0 / (1.0 - ADAM_B2 ** ADAM_STEP)

    def body(w_ref, g_ref, m_ref, v_ref, d_ref, nm_ref, nv_ref):
        gv = g_ref[...]
        nm = ADAM_B1 * m_ref[...] + (1.0 - ADAM_B1) * gv
        nv = ADAM_B2 * v_ref[...] + (1.0 - ADAM_B2) * gv * gv
        d_ref[...] = -ADAM_LR * ((nm * c1) / (jnp.sqrt(nv * c2) + ADAM_EPS) + ADAM_WD * w_ref[...])
        nm_ref[...] = nm
        nv_ref[...] = nv

    spec = pl.BlockSpec((tr, cols), lambda i: (i, 0))
    sds = jax.ShapeDtypeStruct((rows, cols), F32)
    d, nm, nv = pl.pallas_call(
        body, name=name, grid=(rows // tr,), in_specs=[spec] * 4, out_specs=[spec] * 3, out_shape=[sds] * 3,
        compiler_params=_cparams(("parallel",)),
    )(w2, g2, m2, v2)
    return d.reshape(shape), nm.reshape(shape), nv.reshape(shape)


_ANY = pl.BlockSpec(memory_space=pl.ANY)


def _my_xyc():
    return lax.axis_index("x"), lax.axis_index("y"), lax.axis_index("c")


def _chip_allgather_core_half(shards, name):
    n = len(shards)
    half = [s.shape[0] // 2 for s in shards]

    def body(*refs):
        full_refs, whole_refs = refs[:n], refs[n:2 * n]
        send_sems, recv_sems, loc_sems = refs[2 * n:]
        x, y, c = _my_xyc()
        myq = 2 * x + y
        x_refs = [full_refs[i].at[pl.ds(c * half[i], half[i])] for i in range(n)]
        out_refs = [whole_refs[i].at[c] for i in range(n)]
        chips = [(1 - x, y), (x, 1 - y), (1 - x, 1 - y)]
        started = []
        for i in range(n):
            local = pltpu.make_async_copy(x_refs[i], out_refs[i].at[myq], loc_sems.at[i])
            local.start()
            started.append(local)
        sends = []
        for j, (px, py) in enumerate(chips):
            for i in range(n):
                cp = pltpu.make_async_remote_copy(src_ref=x_refs[i], dst_ref=out_refs[i].at[myq],
                                                  send_sem=send_sems.at[3 * i + j], recv_sem=recv_sems.at[3 * i + j],
                                                  device_id=(px, py, c), device_id_type=MESH_ID)
                cp.start()
                sends.append(cp)
        for j, (px, py) in enumerate(chips):
            for i in range(n):
                pltpu.make_async_remote_copy(src_ref=x_refs[i], dst_ref=out_refs[i].at[2 * px + py],
                                             send_sem=send_sems.at[3 * i + j], recv_sem=recv_sems.at[3 * i + j],
                                             device_id=(px, py, c), device_id_type=MESH_ID).wait_recv()
        for cp in sends:
            cp.wait_send()
        for local in started:
            local.wait()

    return pl.pallas_call(
        body, name=name, in_specs=[_ANY] * n, out_specs=[_ANY] * n,
        out_shape=[jax.ShapeDtypeStruct((2, 4, h) + s.shape[1:], s.dtype) for s, h in zip(shards, half)],
        scratch_shapes=[pltpu.SemaphoreType.DMA((3 * n,)), pltpu.SemaphoreType.DMA((3 * n,)),
                        pltpu.SemaphoreType.DMA((n,))],
    )(*shards)


def _sibling_swap_halves(g, name):
    q, rows, cols = g.shape
    rh = rows // 2

    def body(g_ref, out_ref, send_sem, recv_sem):
        x, y, c = _my_xyc()
        src = g_ref.at[:, pl.ds((1 - c) * rh, rh), :]
        cp = pltpu.make_async_remote_copy(src_ref=src, dst_ref=out_ref, send_sem=send_sem, recv_sem=recv_sem,
                                          device_id=(x, y, 1 - c), device_id_type=MESH_ID)
        cp.start()
        cp.wait()

    return pl.pallas_call(
        body, name=name, in_specs=[_ANY], out_specs=_ANY,
        out_shape=jax.ShapeDtypeStruct((q, rh, cols), g.dtype),
        scratch_shapes=[pltpu.SemaphoreType.DMA, pltpu.SemaphoreType.DMA],
    )(g)


def _add_own_half(g, other, name):
    q, rows, cols = g.shape
    rh = rows // 2
    tr = _pick(rh, (512, 256, 128, 64, 32, 16, 8))
    nb = rh // tr
    cidx = lax.axis_index("c").astype(jnp.int32).reshape(1)

    def body(c_ref, g_ref, o_ref, out_ref):
        out_ref[...] = (g_ref[...] + o_ref[...]).astype(out_ref.dtype)

    grid_spec = pltpu.PrefetchScalarGridSpec(
        num_scalar_prefetch=1, grid=(q, nb),
        in_specs=[pl.BlockSpec((1, tr, cols), lambda a, i, c_ref: (a, c_ref[0] * nb + i, 0)),
                  pl.BlockSpec((1, tr, cols), lambda a, i, c_ref: (a, i, 0))],
        out_specs=pl.BlockSpec((1, tr, cols), lambda a, i, c_ref: (a, i, 0)))
    return pl.pallas_call(
        body, name=name, grid_spec=grid_spec, out_shape=jax.ShapeDtypeStruct((q, rh, cols), BF16),
        compiler_params=_cparams(("parallel", "parallel")),
    )(cidx, g, other)


def _chip_exchange(s, name):
    def body(s_ref, out_ref, send_sems, recv_sems, loc_sem):
        x, y, c = _my_xyc()
        myq = 2 * x + y
        local = pltpu.make_async_copy(s_ref.at[myq], out_ref.at[myq], loc_sem)
        local.start()
        chips = [(1 - x, y), (x, 1 - y), (1 - x, 1 - y)]
        sends = []
        for j, (px, py) in enumerate(chips):
            cp = pltpu.make_async_remote_copy(src_ref=s_ref.at[2 * px + py], dst_ref=out_ref.at[myq],
                                              send_sem=send_sems.at[j], recv_sem=recv_sems.at[j],
                                              device_id=(px, py, c), device_id_type=MESH_ID)
            cp.start()
            sends.append(cp)
        for j, (px, py) in enumerate(chips):
            pltpu.make_async_remote_copy(src_ref=s_ref.at[myq], dst_ref=out_ref.at[2 * px + py],
                                         send_sem=send_sems.at[j], recv_sem=recv_sems.at[j],
                                         device_id=(px, py, c), device_id_type=MESH_ID).wait_recv()
        for cp in sends:
            cp.wait_send()
        local.wait()

    return pl.pallas_call(
        body, name=name, in_specs=[_ANY], out_specs=_ANY, out_shape=jax.ShapeDtypeStruct(s.shape, s.dtype),
        scratch_shapes=[pltpu.SemaphoreType.DMA((3,)), pltpu.SemaphoreType.DMA((3,)), pltpu.SemaphoreType.DMA],
    )(s)


def _sum_leading(a, name):
    q, rows, cols = a.shape
    tr = _pick(rows, (512, 256, 128, 64, 32, 16, 8))

    def body(a_ref, out_ref):
        acc = a_ref[0].astype(F32)
        for j in range(1, q):
            acc = acc + a_ref[j].astype(F32)
        out_ref[...] = acc

    return pl.pallas_call(
        body, name=name, grid=(rows // tr,), in_specs=[pl.BlockSpec((q, tr, cols), lambda i: (0, i, 0))],
        out_specs=pl.BlockSpec((tr, cols), lambda i: (i, 0)), out_shape=jax.ShapeDtypeStruct((rows, cols), F32),
        compiler_params=_cparams(("parallel",)),
    )(a)


def _sibling_swap(arrays, name):
    n = len(arrays)

    def body(*refs):
        a_refs, out_refs = refs[:n], refs[n:2 * n]
        send_sems, recv_sems = refs[2 * n:]
        x, y, c = _my_xyc()
        copies = []
        for i in range(n):
            cp = pltpu.make_async_remote_copy(src_ref=a_refs[i], dst_ref=out_refs[i], send_sem=send_sems.at[i],
                                              recv_sem=recv_sems.at[i], device_id=(x, y, 1 - c),
                                              device_id_type=MESH_ID)
            cp.start()
            copies.append(cp)
        for cp in copies:
            cp.wait()

    return pl.pallas_call(
        body, name=name, in_specs=[_ANY] * n, out_specs=[_ANY] * n,
        out_shape=[jax.ShapeDtypeStruct(a.shape, a.dtype) for a in arrays],
        scratch_shapes=[pltpu.SemaphoreType.DMA((n,)), pltpu.SemaphoreType.DMA((n,))],
    )(*arrays)


def _by_core(mine, other):
    c = lax.axis_index("c")
    return jnp.where(c == 0, jnp.stack([mine, other]), jnp.stack([other, mine]))


def _sibling_fill(arrays, name):
    n = len(arrays)

    def body(*refs):
        buf_refs = refs[n:2 * n]
        send_sems, recv_sems = refs[2 * n:]
        x, y, c = _my_xyc()
        sends = []
        for i in range(n):
            cp = pltpu.make_async_remote_copy(src_ref=buf_refs[i].at[c], dst_ref=buf_refs[i].at[c],
                                              send_sem=send_sems.at[i], recv_sem=recv_sems.at[i],
                                              device_id=(x, y, 1 - c), device_id_type=MESH_ID)
            cp.start()
            sends.append(cp)
        for i in range(n):
            pltpu.make_async_remote_copy(src_ref=buf_refs[i].at[c], dst_ref=buf_refs[i].at[1 - c],
                                         send_sem=send_sems.at[i], recv_sem=recv_sems.at[i],
                                         device_id=(x, y, 1 - c), device_id_type=MESH_ID).wait_recv()
        for cp in sends:
            cp.wait_send()

    return pl.pallas_call(
        body, name=name, in_specs=[_ANY] * n, out_specs=[_ANY] * n,
        out_shape=[jax.ShapeDtypeStruct(a.shape, a.dtype) for a in arrays],
        input_output_aliases={i: i for i in range(n)},
        scratch_shapes=[pltpu.SemaphoreType.DMA((n,)), pltpu.SemaphoreType.DMA((n,))],
    )(*arrays)


def _all_reduce_small(vec, name):
    rows, cols = vec.shape

    def body(v_ref, out_ref, buf, send_sems, recv_sems):
        x, y, c = _my_xyc()
        me = 4 * x + 2 * y + c
        buf[me] = v_ref[...]
        sends = []
        for kk in range(1, 8):
            peer = (1 - x if kk & 4 else x, 1 - y if kk & 2 else y, 1 - c if kk & 1 else c)
            cp = pltpu.make_async_remote_copy(src_ref=v_ref, dst_ref=buf.at[me], send_sem=send_sems.at[kk - 1],
                                              recv_sem=recv_sems.at[kk - 1], device_id=peer, device_id_type=MESH_ID)
            cp.start()
            sends.append(cp)
        for kk in range(1, 8):
            px, py, pc = (1 - x if kk & 4 else x, 1 - y if kk & 2 else y, 1 - c if kk & 1 else c)
            pltpu.make_async_remote_copy(src_ref=v_ref, dst_ref=buf.at[4 * px + 2 * py + pc],
                                         send_sem=send_sems.at[kk - 1], recv_sem=recv_sems.at[kk - 1],
                                         device_id=(px, py, pc), device_id_type=MESH_ID).wait_recv()
        for cp in sends:
            cp.wait_send()
        acc = buf[0]
        for j in range(1, 8):
            acc = acc + buf[j]
        out_ref[...] = acc

    vm = pl.BlockSpec(memory_space=pltpu.VMEM)
    return pl.pallas_call(
        body, name=name, in_specs=[vm], out_specs=vm, out_shape=jax.ShapeDtypeStruct((rows, cols), F32),
        scratch_shapes=[pltpu.VMEM((8, rows, cols), F32), pltpu.SemaphoreType.DMA((7,)), pltpu.SemaphoreType.DMA((7,))],
        compiler_params=pltpu.CompilerParams(vmem_limit_bytes=VMEM_LIMIT_BYTES),
    )(vec)


def _reduce_scatter(g):
    other = _sibling_swap_halves(g, "rs_swap_halves")
    s = _add_own_half(g, other, "rs_add_halves")
    recv = _chip_exchange(s, "rs_chip_exchange")
    r = _sum_leading(recv, "rs_sum_chips")
    (other_r,) = _sibling_swap([r], "rs_join_halves")
    return _by_core(r, other_r).reshape(g.shape[1], g.shape[2])


PACK_COLS = 1024
PACK_ROW_MULTIPLE = 1024
SHARDED = (("even_w_in", 2), ("conv_w", 2), ("even_w_out", 1), ("odd_w_in", 2), ("q_norm_w", 1), ("w_uq", 2),
           ("kv_norm_w", 1), ("w_ukv", 2), ("odd_w_out", 1), ("ffn_w_gate", 2), ("ffn_w_up", 2), ("ffn_w_down", 1))
KEEP_F32 = ("conv_w", "q_norm_w", "kv_norm_w")
REPLICATED = ("pool_w", "pool_scale", "conv_b", "dt_bias", "a_log", "d_skip", "ssm_norm_w", "fgate_b",
              "ln_mix_g", "ln_mix_b", "ln_ffn_g", "ln_ffn_b")


def _gather_weights(shards):
    sent = [shards[name] if name in KEEP_F32 else shards[name].astype(BF16) for name, _ in SHARDED]
    gathered = _sibling_fill(_chip_allgather_core_half(sent, "weights_allgather"), "weights_sibling_fill")
    out = {}
    for (name, axis), halves in zip(SHARDED, gathered):
        out[name] = jnp.concatenate([jnp.concatenate([halves[h, q] for q in range(4)], axis=axis) for h in range(2)],
                                    axis=0)
    return out


def _pack_grads(grads, shards):
    pieces = []
    for name, axis in SHARDED:
        width = shards[name].shape[axis]
        for g in grads[name]:
            ax = axis - 1
            split = g.reshape(g.shape[:ax] + (4, width) + g.shape[ax + 1:])
            pieces.append(jnp.moveaxis(split, ax, 0).reshape(4, -1))
    n = sum(p.shape[1] for p in pieces)
    per = PACK_COLS * PACK_ROW_MULTIPLE
    total = -(-n // per) * per
    pieces.append(jnp.zeros((4, total - n), F32))
    return jnp.concatenate(pieces, axis=1).reshape(4, total // PACK_COLS, PACK_COLS)


def _unpack_reduced(reduced, shards):
    flat = reduced.reshape(-1)
    out, off = {}, 0
    for name, _ in SHARDED:
        shp = shards[name].shape
        size = math.prod(shp)
        out[name] = flat[off:off + size].reshape(shp)
        off += size
    return out


def _to_heads(t, nh):
    rows, width = t.shape
    return t.reshape(rows, nh, width // nh).transpose(1, 0, 2)


def _from_heads(t):
    nh, rows, d = t.shape
    return t.transpose(1, 0, 2).reshape(rows, nh * d)


def _rope_tables(rows):
    half = MLA_ROPE // 2
    freqs = jnp.power(ROPE_THETA, -jnp.arange(half, dtype=F32) / half)
    ang = jnp.arange(rows, dtype=F32)[:, None] * freqs[None, :]
    cos, sin = jnp.cos(ang), jnp.sin(ang)
    cos_q, sin_q = jnp.tile(cos, (1, MLA_HEADS)), jnp.tile(sin, (1, MLA_HEADS))
    zeros = jnp.zeros((rows, LANES - MLA_ROPE), F32)
    cos_k = jnp.concatenate([cos, cos, zeros], axis=1)
    sin_k = jnp.concatenate([sin, sin, zeros], axis=1)
    r = lax.broadcasted_iota(jnp.int32, (LANES, LANES), 0)
    c = lax.broadcasted_iota(jnp.int32, (LANES, LANES), 1)
    rot = (jnp.where((r == c + half) & (c < half), -1.0, 0.0)
           + jnp.where((c == r + half) & (r < half), 1.0, 0.0)).astype(F32)
    return cos_q, sin_q, cos_k, sin_k, rot


def _pad_cols(a, width):
    return jnp.pad(a, ((0, 0), (0, width - a.shape[1])))


_EVEN_CUT = (512, 1536, 3072, 3088)


def _prep_even_w_in(w):
    c0, c1, c2, c3 = _EVEN_CUT
    return jnp.concatenate([w[:, c1:c2], w[:, :c0], w[:, c0:c1], _pad_cols(w[:, c2:c3], LANES)], axis=1)


def _unprep_even_w_in(g):
    c0, c1, c2, c3 = _EVEN_CUT
    n_xbc, n_u = c2 - c1, c0
    return jnp.concatenate([g[:, n_xbc:n_xbc + n_u], g[:, n_xbc + n_u:c2], g[:, :n_xbc], g[:, c2:c3]], axis=1)


_ODD_CUT = (1536, 1544, 2312, 2344)


def _prep_odd_w_in(w):
    c0, c1, c2, c3 = _ODD_CUT
    return jnp.concatenate([w[:, :c0], w[:, c1:c2], _pad_cols(w[:, c2:c3], LANES), _pad_cols(w[:, c0:c1], LANES)],
                           axis=1)


def _unprep_odd_w_in(g):
    c0, c1, c2, c3 = _ODD_CUT
    n1 = c0 + (c2 - c1)
    return jnp.concatenate([g[:, :c0], g[:, n1 + LANES:n1 + LANES + (c1 - c0)], g[:, c0:n1],
                            g[:, n1:n1 + (c3 - c2)]], axis=1)


def _prep_w_uq(w):
    r = w.reshape(w.shape[0], MLA_HEADS, MLA_NOPE + MLA_ROPE)
    half = MLA_ROPE // 2
    return jnp.concatenate([r[:, :, :MLA_NOPE].reshape(w.shape[0], -1),
                            r[:, :, MLA_NOPE:MLA_NOPE + half].reshape(w.shape[0], -1),
                            r[:, :, MLA_NOPE + half:].reshape(w.shape[0], -1)], axis=1)


def _unprep_w_uq(g):
    rows = g.shape[0]
    half = MLA_ROPE // 2
    nope = MLA_HEADS * MLA_NOPE
    return jnp.concatenate([g[:, :nope].reshape(rows, MLA_HEADS, MLA_NOPE),
                            g[:, nope:nope + LANES].reshape(rows, MLA_HEADS, half),
                            g[:, nope + LANES:].reshape(rows, MLA_HEADS, half)], axis=2).reshape(rows, -1)


def _row(v, width=None):
    v = v.reshape(1, -1)
    return v if width is None else _pad_cols(v, width)


def _even_forward(x_bf, w, i):
    rows = x_bf.shape[0]
    proj = _mm(x_bf, w["even_w_in"][i], "nn", F32, "even_proj")
    xbc, u, z, dtr = _Cols(proj, 0, 1536), _Cols(proj, 3, 512), _Cols(proj, 2, 1024), _Cols(proj, 24, LANES)
    n_row = rows // ROW_TILE
    pool_p = (w["pool_w"][i], _row(w["pool_scale"][i]))
    (y_pool,), pool_c = _scan_fwd(_f_pool, "pool_fwd", n_row, [u], [], pool_p, [], [(16, POOL_WIDTH)],
                                  [(POOL_WIDTH, BF16)])
    conv_p = (w["conv_w"][i], _row(w["conv_b"][i]))
    (xa,), conv_c = _scan_fwd(_f_conv, "conv_fwd", rows // min(ACT_ROW_TILE, rows), [xbc], [], conv_p, [],
                              [(8, SSM_CONV_DIM)],
                              [(SSM_CONV_DIM, F32)])
    ssd_p = (_row(w["dt_bias"][i], LANES), _row(w["a_log"][i], LANES), _row(w["d_skip"][i], LANES),
             _row(w["ssm_norm_w"][i]))
    (y_ssm,), ssd_c = _scan_fwd(_f_ssd, "ssd_fwd", rows // SSM_CHUNK, [xa, dtr, z], [], ssd_p, [],
                                [(SSM_STATE, SSM_D_INNER)], [(SSM_D_INNER, BF16)])
    mix = jnp.concatenate([y_pool, y_ssm], axis=1)
    saved = dict(u=u, z=z, xbc=xbc, dtr=dtr, xa=xa, mix=mix, pool_p=pool_p, pool_c=pool_c, conv_p=conv_p,
                 conv_c=conv_c, ssd_p=ssd_p, ssd_c=ssd_c)
    return mix, saved


def _even_backward(dmix, x_bf, sv, w, i, d_r1):
    rows = x_bf.shape[0]
    n_row = rows // ROW_TILE
    dy_pool, dy_ssm = dmix[:, :POOL_WIDTH], dmix[:, POOL_WIDTH:]
    (du,), (g_pool_w, g_pool_scale) = _scan_bwd(_f_pool, "pool_bwd", n_row, [sv["u"]], [], sv["pool_p"], [],
                                                sv["pool_c"], [dy_pool])
    (dxa, ddtr, dz), (g_dt_bias, g_a_log, g_d_skip, g_norm_w) = _scan_bwd(
        _f_ssd, "ssd_bwd", rows // SSM_CHUNK, [sv["xa"], sv["dtr"], sv["z"]], [], sv["ssd_p"], [], sv["ssd_c"],
        [dy_ssm])
    (dxbc,), (g_conv_w, g_conv_b) = _scan_bwd(_f_conv, "conv_bwd", rows // min(ACT_ROW_TILE, rows), [sv["xbc"]],
                                              [], sv["conv_p"], [],
                                              sv["conv_c"], [dxa])
    dproj = jnp.concatenate([dxbc, du, dz, ddtr], axis=1).astype(BF16)
    g_w_in = _unprep_even_w_in(_mm(x_bf, dproj, "tn", F32, "even_dw_in"))
    dx = _mm(dproj, w["even_w_in"][i], "nt", F32, "even_dx", extra=d_r1, alpha=ALPHA)
    grads = dict(even_w_in=g_w_in, pool_w=g_pool_w, pool_scale=g_pool_scale[0], conv_w=g_conv_w, conv_b=g_conv_b[0],
                 dt_bias=g_dt_bias[0, :SSM_HEADS], a_log=g_a_log[0, :SSM_HEADS], d_skip=g_d_skip[0, :SSM_HEADS],
                 ssm_norm_w=g_norm_w[0])
    return dx, grads


def _odd_forward(x_bf, w, i, tables):
    rows = x_bf.shape[0]
    cos_q, sin_q, cos_k, sin_k, rot = tables
    proj = _mm(x_bf, w["odd_w_in"][i], "nn", F32, "odd_proj")
    qf, kf, vf = (_to_heads((proj[:, j * 512:(j + 1) * 512] * sc).astype(BF16), FOX_HEADS)
                  for j, sc in enumerate((FOX_SCALE, 1.0, 1.0)))
    cq, ckv = _Cols(proj, 3, MLA_Q_RANK), _Cols(proj, 8, MLA_KV_RANK)
    kr, fl = _Cols(proj, 18, LANES), _Cols(proj, 19, LANES)
    n_row = rows // ROW_TILE
    fox_p = (_row(w["fgate_b"][i], LANES),)
    n_gate = rows // min(ATTN_TILE, rows)
    (fcum,), fox_c = _scan_fwd(_f_fox_gate, "fox_gate_fwd", n_gate, [fl], [], fox_p, [], [(8, LANES)], [(LANES, F32)])
    fc_heads = fcum[:, :FOX_HEADS].T
    fq, fk = fc_heads[:, :, None], fc_heads[:, None, :]
    o_fox, lse_fox = _attn_fwd(qf, kf, vf, fq, fk, "fox_attn_fwd")

    prep_p = (_row(w["q_norm_w"][i]), _row(w["kv_norm_w"][i]))
    (cqn, ckvn, krr), _ = _scan_fwd(_f_mla_prep, "mla_prep_fwd", n_row, [cq, ckv, kr], [cos_k, sin_k], prep_p,
                                    [rot], [], [(MLA_Q_RANK, BF16), (MLA_KV_RANK, BF16), (LANES, BF16)])
    q_flat = _mm(cqn, w["w_uq"][i], "nn", F32, "mla_q_up")
    (q_rope,), _ = _scan_fwd(_f_rope_q, "rope_q_fwd", n_row, [q_flat], [cos_q, sin_q], [], [], [],
                             [(q_flat.shape[1], BF16)])
    kv = _mm(ckvn, w["w_ukv"][i], "nn", BF16, "mla_kv_up")
    nope = MLA_HEADS * MLA_NOPE
    half = MLA_ROPE // 2
    q_m = jnp.concatenate([_to_heads(q_rope[:, :nope], MLA_HEADS), _to_heads(q_rope[:, nope:nope + LANES], MLA_HEADS),
                           _to_heads(q_rope[:, nope + LANES:], MLA_HEADS)], axis=2)
    kv_h = _to_heads(kv, MLA_HEADS)
    k_rope = jnp.broadcast_to(krr[None, :, :MLA_ROPE], (MLA_HEADS, rows, MLA_ROPE))
    k_m = jnp.concatenate([kv_h[:, :, :MLA_NOPE], k_rope], axis=2)
    v_m = kv_h[:, :, MLA_NOPE:]
    o_mla, lse_mla = _attn_fwd(q_m, k_m, v_m, None, None, "mla_attn_fwd")
    mix = jnp.concatenate([_from_heads(o_fox), _from_heads(o_mla)], axis=1)
    saved = dict(qf=qf, kf=kf, vf=vf, fq=fq, fk=fk, o_fox=o_fox, lse_fox=lse_fox, fl=fl, fox_p=fox_p, fox_c=fox_c,
                 cq=cq, ckv=ckv, kr=kr, prep_p=prep_p, cqn=cqn, ckvn=ckvn, q_flat=q_flat, q_m=q_m, k_m=k_m, v_m=v_m,
                 o_mla=o_mla, lse_mla=lse_mla, mix=mix)
    return mix, saved


def _odd_backward(dmix, x_bf, sv, w, i, d_r1, tables):
    rows = x_bf.shape[0]
    cos_q, sin_q, cos_k, sin_k, rot = tables
    n_row = rows // ROW_TILE
    nope = MLA_HEADS * MLA_NOPE
    half = MLA_ROPE // 2
    do_fox = _to_heads(dmix[:, :FOX_WIDTH], FOX_HEADS)
    do_mla = _to_heads(dmix[:, FOX_WIDTH:], MLA_HEADS)
    fox_args = (sv["qf"], sv["kf"], sv["vf"], sv["o_fox"], do_fox, sv["lse_fox"], sv["fq"], sv["fk"])
    dqf, delta_fox = _attn_bwd_dq(*fox_args, "fox_attn_dq")
    dkf, dvf, dfk = _attn_bwd_dkv(*fox_args[:3], delta_fox, *fox_args[4:], "fox_attn_dkv")
    dfcum = _pad_cols(dfk[:, 0, :].T, LANES)
    n_gate = rows // min(ATTN_TILE, rows)
    (dfl,), (g_fb,) = _scan_bwd(_f_fox_gate, "fox_gate_bwd", n_gate, [sv["fl"]], [], sv["fox_p"], [], sv["fox_c"],
                                [dfcum])
    mla_args = (sv["q_m"], sv["k_m"], sv["v_m"], sv["o_mla"], do_mla, sv["lse_mla"], None, None)
    dq_m, dk_m, dv_m = _attn_bwd_fused(*mla_args[:6], "mla_attn_bwd")
    dq_rope = jnp.concatenate([_from_heads(dq_m[:, :, :MLA_NOPE]), _from_heads(dq_m[:, :, MLA_NOPE:MLA_NOPE + half]),
                               _from_heads(dq_m[:, :, MLA_NOPE + half:])], axis=1)
    (dq_flat,), _ = _scan_bwd(_f_rope_q, "rope_q_bwd", n_row, [sv["q_flat"]], [cos_q, sin_q], [], [], [], [dq_rope])
    g_w_uq = _mm(sv["cqn"], dq_flat, "tn", F32, "mla_dw_uq")
    dcqn = _mm(dq_flat, w["w_uq"][i], "nt", BF16, "mla_dcqn")
    dkv = _from_heads(jnp.concatenate([dk_m[:, :, :MLA_NOPE], dv_m], axis=2))
    g_w_ukv = _mm(sv["ckvn"], dkv, "tn", F32, "mla_dw_ukv")
    dckvn = _mm(dkv, w["w_ukv"][i], "nt", BF16, "mla_dckvn")
    dkrr = _head_sum(dk_m, "mla_dk_rope_sum")
    (dcq, dckv, dkr), (g_qw, g_kvw) = _scan_bwd(_f_mla_prep, "mla_prep_bwd", n_row, [sv["cq"], sv["ckv"], sv["kr"]],
                                                [cos_k, sin_k], sv["prep_p"], [rot], [], [dcqn, dckvn, dkrr])
    dproj = jnp.concatenate([_from_heads(dqf).astype(F32) * FOX_SCALE, _from_heads(dkf).astype(F32), _from_heads(dvf).astype(F32),
                             dcq, dckv, dkr, dfl], axis=1).astype(BF16)
    g_w_in = _mm(x_bf, dproj, "tn", F32, "odd_dw_in")
    dx = _mm(dproj, w["odd_w_in"][i], "nt", F32, "odd_dx", extra=d_r1, alpha=ALPHA)
    grads = dict(odd_w_in=_unprep_odd_w_in(g_w_in), fgate_b=g_fb[0, :FOX_HEADS], q_norm_w=g_qw[0], kv_norm_w=g_kvw[0],
                 w_uq=_unprep_w_uq(g_w_uq), w_ukv=g_w_ukv)
    return dx, grads


def _head_sum(dk_m, name):
    H, T, dk = dk_m.shape
    tt = _pick(T, (512, 256, 128))

    def body(d_ref, o_ref):
        acc = d_ref[0].astype(F32)
        for h in range(1, H):
            acc = acc + d_ref[h].astype(F32)
        o_ref[...] = jnp.concatenate([acc[:, MLA_NOPE:], jnp.zeros((tt, LANES - MLA_ROPE), F32)], axis=1).astype(BF16)

    return pl.pallas_call(
        body, name=name, grid=(T // tt,), in_specs=[pl.BlockSpec((H, tt, dk), lambda i: (0, i, 0))],
        out_specs=pl.BlockSpec((tt, LANES), lambda i: (i, 0)), out_shape=jax.ShapeDtypeStruct((T, LANES), BF16),
        compiler_params=_cparams(("parallel",)),
    )(dk_m)


def _local_step(x, target, w, small):
    rows = x.shape[0]
    n_row = rows // ROW_TILE
    n_ln = rows // min(LN_ROW_TILE, rows)
    tables = _rope_tables(rows)
    saved = []
    x_f32 = x
    x_bf = x.astype(BF16)
    for l in range(DEPTH):
        i = l // 2
        if l % 2 == 0:
            mix, sv = _even_forward(x_bf, w, i)
            w_out = w["even_w_out"][i]
        else:
            mix, sv = _odd_forward(x_bf, w, i, tables)
            w_out = w["odd_w_out"][i]
        r1 = _mm(mix, w_out, "nn", F32, "mix_out_even" if l % 2 == 0 else "mix_out_odd", extra=x_f32, alpha=ALPHA)
        ln1_p = (_row(small["ln_mix_g"][l]), _row(small["ln_mix_b"][l]))
        ln_outs = [(D_MODEL, F32), (D_MODEL, BF16)]
        (x_mid, x_mid_bf), _ = _scan_fwd(_f_ln_with_bf16, "ln_fwd", n_ln, [r1], [], ln1_p, [], [], ln_outs)
        gu = _mm(x_mid_bf, w["ffn_w_gu"][l], "nn", BF16, "ffn_gu")
        (act,), _ = _scan_fwd(_f_act, "ffn_act_fwd", rows // min(ACT_ROW_TILE, rows), [gu], [], [], [], [],
                              [(D_FF, BF16)])
        r2 = _mm(act, w["ffn_w_down"][l], "nn", F32, "ffn_down", extra=x_mid, alpha=ALPHA)
        ln2_p = (_row(small["ln_ffn_g"][l]), _row(small["ln_ffn_b"][l]))
        (x_out, x_out_bf), _ = _scan_fwd(_f_ln_with_bf16, "ln_fwd", n_ln, [r2], [], ln2_p, [], [], ln_outs)
        saved.append(dict(sv=sv, x_bf=x_bf, r1=r1, ln1_p=ln1_p, x_mid_bf=x_mid_bf, gu=gu, act=act, r2=r2, ln2_p=ln2_p,
                          w_out=w_out))
        x_f32, x_bf = x_out, x_out_bf

    dy, loss_part = _loss_head(x_f32, target, "loss_head")
    loss = 0.5 * jnp.sum(loss_part) / D_MODEL

    layer_grads = []
    for l in reversed(range(DEPTH)):
        i = l // 2
        s = saved[l]
        (d_r2,), (g_ln2_g, g_ln2_b) = _scan_bwd(_f_ln, "ln_bwd", n_ln, [s["r2"]], [], s["ln2_p"], [], [], [dy])
        g_down = _mm(s["act"], d_r2, "tn", F32, "ffn_dw_down")
        dact = _mm(d_r2, w["ffn_w_down"][l], "nt", BF16, "ffn_dact")
        (dgu,), _ = _scan_bwd(_f_act, "ffn_act_bwd", rows // min(ACT_ROW_TILE, rows), [s["gu"]], [], [], [], [],
                              [dact])
        g_gu = _mm(s["x_mid_bf"], dgu, "tn", F32, "ffn_dw_gu")
        dx_mid = _mm(dgu, w["ffn_w_gu"][l], "nt", F32, "ffn_dx", extra=d_r2, alpha=ALPHA)
        (d_r1,), (g_ln1_g, g_ln1_b) = _scan_bwd(_f_ln, "ln_bwd", n_ln, [s["r1"]], [], s["ln1_p"], [], [], [dx_mid])
        g_w_out = _mm(s["sv"]["mix"], d_r1, "tn", F32, "even_dw_out" if l % 2 == 0 else "odd_dw_out")
        dmix = _mm(d_r1, s["w_out"], "nt", BF16, "even_dmix" if l % 2 == 0 else "odd_dmix")
        if l % 2 == 0:
            dy, g = _even_backward(dmix, s["x_bf"], s["sv"], w, i, d_r1)
            g["even_w_out"] = g_w_out
        else:
            dy, g = _odd_backward(dmix, s["x_bf"], s["sv"], w, i, d_r1, tables)
            g["odd_w_out"] = g_w_out
        g.update(ffn_w_gate=g_gu[:, :D_FF], ffn_w_up=g_gu[:, D_FF:], ffn_w_down=g_down, ln_mix_g=g_ln1_g[0],
                 ln_mix_b=g_ln1_b[0], ln_ffn_g=g_ln2_g[0], ln_ffn_b=g_ln2_b[0])
        layer_grads.append((l, g))
    return loss, dy, layer_grads


EVEN_NAMES = ("even_w_in", "pool_w", "pool_scale", "conv_w", "conv_b", "dt_bias", "a_log", "d_skip", "ssm_norm_w",
              "even_w_out")
ODD_NAMES = ("odd_w_in", "fgate_b", "q_norm_w", "w_uq", "kv_norm_w", "w_ukv", "odd_w_out")
PER_LAYER_NAMES = ("ffn_w_gate", "ffn_w_up", "ffn_w_down", "ln_mix_g", "ln_mix_b", "ln_ffn_g", "ln_ffn_b")
WEIGHT_NAMES = EVEN_NAMES + ODD_NAMES + PER_LAYER_NAMES


def _grads_by_name(layer_grads):
    by_layer = dict(layer_grads)
    out = {}
    for n in EVEN_NAMES:
        out[n] = [by_layer[l][n] for l in range(0, DEPTH, 2)]
    for n in ODD_NAMES:
        out[n] = [by_layer[l][n] for l in range(1, DEPTH, 2)]
    for n in PER_LAYER_NAMES:
        out[n] = [by_layer[l][n] for l in range(DEPTH)]
    return out


def _prepare_weights(full):
    w = {}
    w["even_w_in"] = [_prep_even_w_in(full["even_w_in"][i]) for i in range(2)]
    w["even_w_out"] = [full["even_w_out"][i] for i in range(2)]
    w["odd_w_in"] = [_prep_odd_w_in(full["odd_w_in"][i]) for i in range(2)]
    w["w_uq"] = [_prep_w_uq(full["w_uq"][i]) for i in range(2)]
    w["w_ukv"] = [full["w_ukv"][i] for i in range(2)]
    w["odd_w_out"] = [full["odd_w_out"][i] for i in range(2)]
    w["ffn_w_gu"] = [jnp.concatenate([full["ffn_w_gate"][l], full["ffn_w_up"][l]], axis=1) for l in range(DEPTH)]
    w["ffn_w_down"] = [full["ffn_w_down"][l] for l in range(DEPTH)]
    for n in ("conv_w", "q_norm_w", "kv_norm_w"):
        w[n] = full[n]
    return w


def _flatten_small(grads):
    flat = jnp.concatenate([g.reshape(-1) for n in REPLICATED for g in grads[n]])
    n = flat.shape[0]
    per = LANES * 8
    total = -(-n // per) * per
    return jnp.pad(flat, (0, total - n)).reshape(total // LANES, LANES)


def _unflatten_small(mat, like):
    flat = mat.reshape(-1)
    out, off = {}, 0
    for n in REPLICATED:
        size = math.prod(like[n].shape)
        out[n] = flat[off:off + size].reshape(like[n].shape)
        off += size
    return out


def kernel(x, even_w_in, pool_w, pool_scale, conv_w, conv_b, dt_bias, a_log, d_skip, ssm_norm_w, even_w_out, odd_w_in, fgate_b, q_norm_w, w_uq, kv_norm_w, w_ukv, odd_w_out, ffn_w_gate, ffn_w_up, ffn_w_down, ln_mix_g, ln_mix_b, ln_ffn_g, ln_ffn_b, loss_target, m_even_w_in, m_pool_w, m_pool_scale, m_conv_w, m_conv_b, m_dt_bias, m_a_log, m_d_skip, m_ssm_norm_w, m_even_w_out, m_odd_w_in, m_fgate_b, m_q_norm_w, m_w_uq, m_kv_norm_w, m_w_ukv, m_odd_w_out, m_ffn_w_gate, m_ffn_w_up, m_ffn_w_down, m_ln_mix_g, m_ln_mix_b, m_ln_ffn_g, m_ln_ffn_b, v_even_w_in, v_pool_w, v_pool_scale, v_conv_w, v_conv_b, v_dt_bias, v_a_log, v_d_skip, v_ssm_norm_w, v_even_w_out, v_odd_w_in, v_fgate_b, v_q_norm_w, v_w_uq, v_kv_norm_w, v_w_ukv, v_odd_w_out, v_ffn_w_gate, v_ffn_w_up, v_ffn_w_down, v_ln_mix_g, v_ln_mix_b, v_ln_ffn_g, v_ln_ffn_b):
    weights = dict(even_w_in=even_w_in, pool_w=pool_w, pool_scale=pool_scale, conv_w=conv_w, conv_b=conv_b,
                   dt_bias=dt_bias, a_log=a_log, d_skip=d_skip, ssm_norm_w=ssm_norm_w, even_w_out=even_w_out,
                   odd_w_in=odd_w_in, fgate_b=fgate_b, q_norm_w=q_norm_w, w_uq=w_uq, kv_norm_w=kv_norm_w, w_ukv=w_ukv,
                   odd_w_out=odd_w_out, ffn_w_gate=ffn_w_gate, ffn_w_up=ffn_w_up, ffn_w_down=ffn_w_down,
                   ln_mix_g=ln_mix_g, ln_mix_b=ln_mix_b, ln_ffn_g=ln_ffn_g, ln_ffn_b=ln_ffn_b)
    m_in = dict(even_w_in=m_even_w_in, pool_w=m_pool_w, pool_scale=m_pool_scale, conv_w=m_conv_w, conv_b=m_conv_b,
                dt_bias=m_dt_bias, a_log=m_a_log, d_skip=m_d_skip, ssm_norm_w=m_ssm_norm_w, even_w_out=m_even_w_out,
                odd_w_in=m_odd_w_in, fgate_b=m_fgate_b, q_norm_w=m_q_norm_w, w_uq=m_w_uq, kv_norm_w=m_kv_norm_w,
                w_ukv=m_w_ukv, odd_w_out=m_odd_w_out, ffn_w_gate=m_ffn_w_gate, ffn_w_up=m_ffn_w_up,
                ffn_w_down=m_ffn_w_down, ln_mix_g=m_ln_mix_g, ln_mix_b=m_ln_mix_b, ln_ffn_g=m_ln_ffn_g,
                ln_ffn_b=m_ln_ffn_b)
    v_in = dict(even_w_in=v_even_w_in, pool_w=v_pool_w, pool_scale=v_pool_scale, conv_w=v_conv_w, conv_b=v_conv_b,
                dt_bias=v_dt_bias, a_log=v_a_log, d_skip=v_d_skip, ssm_norm_w=v_ssm_norm_w, even_w_out=v_even_w_out,
                odd_w_in=v_odd_w_in, fgate_b=v_fgate_b, q_norm_w=v_q_norm_w, w_uq=v_w_uq, kv_norm_w=v_kv_norm_w,
                w_ukv=v_w_ukv, odd_w_out=v_odd_w_out, ffn_w_gate=v_ffn_w_gate, ffn_w_up=v_ffn_w_up,
                ffn_w_down=v_ffn_w_down, ln_mix_g=v_ln_mix_g, ln_mix_b=v_ln_mix_b, ln_ffn_g=v_ln_ffn_g,
                ln_ffn_b=v_ln_ffn_b)
    shards = {n: weights[n] for n, _ in SHARDED}

    w = _prepare_weights(_gather_weights(shards))
    small = {n: weights[n] for n in REPLICATED}
    w.update(small)

    loss_local, dx, layer_grads = _local_step(x[0], loss_target[0], w, small)
    grads_full = _grads_by_name(layer_grads)

    reduced = _reduce_scatter(_pack_grads(grads_full, shards))
    grads = _unpack_reduced(reduced, shards)
    small_sum = _all_reduce_small(_flatten_small(grads_full), "small_grads_allreduce")
    grads.update(_unflatten_small(small_sum, small))
    loss = lax.psum(loss_local, ("x", "y", "c"))

    deltas, new_m, new_v = {}, {}, {}
    for n in WEIGHT_NAMES:
        deltas[n], new_m[n], new_v[n] = _adamw(weights[n], grads[n], m_in[n], v_in[n], "adamw_" + n)
    return (loss, dx[None], *[grads[n] for n in WEIGHT_NAMES], *[deltas[n] for n in WEIGHT_NAMES],
            *[new_m[n] for n in WEIGHT_NAMES], *[new_v[n] for n in WEIGHT_NAMES])
```
